```python
import math
import jax
import jax.numpy as jnp
from jax import lax
import numpy as np

D_MODEL = 1024
BATCH = 2
SEQ = 16384
DEPTH = 2
DEC_BATCH = 16
DEC_SEQ = 16
PAST_LEN = 1024

CHUNK = 64
Q_BLOCK = 128
H_A = D_MODEL // 256
HD_A = 64
H_B = D_MODEL // 256
DK_B = 32
DV_B = 64
GLA_RANK = 16
GLA_TAU = 16.0
H_C = D_MODEL // 256
HD_C = 64
W_A = H_A * HD_A
W_B = H_B * DV_B
W_C = H_C * 2 * HD_C
D_MIX = W_A + W_B + W_C
D_FF = ((8 * D_MODEL // 3 + 255) // 256) * 256
N_SUB = 3
PROJ_SIZES = (W_A, W_A, W_A, H_A, H_B * DK_B, H_B * DK_B, W_B, GLA_RANK, W_B, 2 * H_C * HD_C, 2 * H_C * HD_C, W_C)
D_PROJ = sum(PROJ_SIZES)
NEG = -1e30
EPS = 1e-6

kernel_name = 'hybrid_stream_encoder_step'


def rmsnorm(x, g):
    xf = x.astype(jnp.float32)
    y = xf * lax.rsqrt(jnp.mean(xf * xf, axis=-1, keepdims=True) + EPS)
    return (y * g.astype(jnp.float32)).astype(x.dtype)


def swiglu(h, w_in, w_out):
    gate, up = jnp.split(h @ w_in, 2, axis=-1)
    return (jax.nn.silu(gate) * up) @ w_out


def project(h, w_in, b_f, w_gla_up, b_gla_up):
    b, t, _ = h.shape
    offsets = [int(o) for o in np.cumsum(PROJ_SIZES)[:-1]]
    qa, ka, va, fa, qb, kb, vb, gb, rb, qc, kc, vc = jnp.split(h @ w_in, offsets, axis=-1)
    loga = jax.nn.log_sigmoid((gb @ w_gla_up + b_gla_up).astype(jnp.float32)) / GLA_TAU
    return dict(
        q_a=qa.reshape(b, t, H_A, HD_A), k_a=ka.reshape(b, t, H_A, HD_A), v_a=va.reshape(b, t, H_A, HD_A),
        logf=jax.nn.log_sigmoid(fa.astype(jnp.float32) + b_f.astype(jnp.float32)),
        q_b=qb.reshape(b, t, H_B, DK_B), k_b=kb.reshape(b, t, H_B, DK_B), v_b=vb.reshape(b, t, H_B, DV_B),
        loga=loga.reshape(b, t, H_B, DK_B), r_b=rb.reshape(b, t, H_B, DV_B),
        q_c=qc.reshape(b, t, H_C, 2, HD_C), k_c=kc.reshape(b, t, H_C, 2, HD_C), v_c=vc.reshape(b, t, H_C, 2 * HD_C))


def fox_block(q, k, v, cq, ck, pos_q, pos_k):
    s = jnp.einsum('bqhd,bkhd->bhqk', q, k).astype(jnp.float32) * (HD_A ** -0.5)
    s = s + jnp.transpose(cq, (0, 2, 1))[..., :, None] - jnp.transpose(ck, (0, 2, 1))[..., None, :]
    mask = pos_k[None, :] <= pos_q[:, None]
    p = jax.nn.softmax(jnp.where(mask, s, NEG), axis=-1)
    return jnp.einsum('bhqk,bkhd->bqhd', p.astype(v.dtype), v)


def diff_block(q, k, v, pos_q, pos_k, lam, slopes):
    s = jnp.einsum('bqhmd,bkhmd->bmhqk', q, k).astype(jnp.float32) * (HD_C ** -0.5)
    dist = jnp.abs(pos_q[:, None] - pos_k[None, :]).astype(jnp.float32)
    bias = -slopes[:, None, None] * dist
    mask = (pos_k // CHUNK)[None, :] <= (pos_q // CHUNK)[:, None]
    p = jax.nn.softmax(jnp.where(mask, s + bias, NEG), axis=-1)
    w = p[:, 0] - lam * p[:, 1]
    return jnp.einsum('bhqk,bkhe->bqhe', w.astype(v.dtype), v)


def sweep_query_blocks(block_fn, q_arrays, seq):
    nb = seq // Q_BLOCK

    def one(i):
        start = i * Q_BLOCK
        qs = [lax.dynamic_slice_in_dim(a, start, Q_BLOCK, axis=1) for a in q_arrays]
        return block_fn(*qs, start + jnp.arange(Q_BLOCK))

    out = jnp.moveaxis(lax.map(one, jnp.arange(nb)), 0, 1)
    return out.reshape((out.shape[0], seq) + out.shape[3:])


def gla_chunked(q, k, v, loga, s0):
    b, t, h, dk = q.shape
    dv = v.shape[-1]
    L = min(CHUNK, t)
    n = t // L
    rs = lambda a: a.astype(jnp.float32).reshape(b, n, L, h, a.shape[-1])
    q = rs(q) * (dk ** -0.5)
    k = rs(k)
    v = rs(v)
    cb = jnp.cumsum(rs(loga), axis=2)
    qe = q * jnp.exp(cb)
    ke = k * jnp.exp(-cb)
    kl = k * jnp.exp(cb[:, :, -1:] - cb)
    decay = jnp.exp(cb[:, :, -1])
    causal = jnp.tril(jnp.ones((L, L), dtype=bool))
    a = jnp.where(causal, jnp.einsum('bnlhk,bnmhk->bnhlm', qe, ke), 0.0)
    o_intra = jnp.einsum('bnhlm,bnmhv->bnlhv', a, v)

    def step(S, xs):
        qe_c, kl_c, v_c, dec_c = xs
        o = jnp.einsum('blhk,bhkv->blhv', qe_c, S)
        S = dec_c[..., None] * S + jnp.einsum('blhk,blhv->bhkv', kl_c, v_c)
        return S, o

    xs = (jnp.moveaxis(qe, 1, 0), jnp.moveaxis(kl, 1, 0), jnp.moveaxis(v, 1, 0), jnp.moveaxis(decay, 1, 0))
    s_fin, o_inter = lax.scan(step, s0.astype(jnp.float32), xs)
    o = o_intra + jnp.moveaxis(o_inter, 0, 1)
    return o.reshape(b, t, h, dv), s_fin


def merge_heads(o_a, o_b, r_b, o_c, g_gla, g_diff, lam_init, w_out):
    b, t = o_a.shape[:2]
    dt = o_a.dtype
    gla = rmsnorm(o_b, g_gla) * jax.nn.silu(r_b.astype(jnp.float32))
    dif = rmsnorm(o_c.astype(jnp.float32), g_diff) * (1.0 - lam_init)
    cat = jnp.concatenate([o_a.reshape(b, t, W_A), gla.reshape(b, t, W_B).astype(dt), dif.reshape(b, t, W_C).astype(dt)], axis=-1)
    return cat @ w_out


def setup_inputs(seed: int = 0) -> dict:
    key = jax.random.key(seed)
    ks = jax.random.split(key, 24)
    nrm = lambda k, shape, s=1.0: s * jax.random.normal(k, shape, jnp.float32)
    return {
        'x_prompt': nrm(ks[0], (BATCH, SEQ, D_MODEL)),
        'x_sample': nrm(ks[1], (DEC_BATCH, DEC_SEQ, D_MODEL)),
        'c_prompt': nrm(ks[2], (BATCH, D_MODEL)),
        'c_sample': nrm(ks[3], (DEC_BATCH, D_MODEL)),
        'cache_fox_k': nrm(ks[4], (DEPTH, DEC_BATCH, PAST_LEN, H_A, HD_A)),
        'cache_fox_v': nrm(ks[5], (DEPTH, DEC_BATCH, PAST_LEN, H_A, HD_A)),
        'cache_fox_logf': jax.nn.log_sigmoid(nrm(ks[6], (DEPTH, DEC_BATCH, PAST_LEN, H_A))),
        'state_gla': nrm(ks[7], (DEPTH, DEC_BATCH, H_B, DK_B, DV_B)),
        'cache_diff_k': nrm(ks[8], (DEPTH, DEC_BATCH, PAST_LEN, H_C, 2 * HD_C)),
        'cache_diff_v': nrm(ks[9], (DEPTH, DEC_BATCH, PAST_LEN, H_C, 2 * HD_C)),
        'w_ada': nrm(ks[10], (DEPTH, D_MODEL, N_SUB * 3 * D_MODEL), 0.5 * D_MODEL ** -0.5),
        'b_ada': nrm(ks[11], (DEPTH, N_SUB * 3 * D_MODEL), 0.02),
        'g_norm': 1.0 + nrm(ks[12], (DEPTH, 2 * N_SUB, D_MODEL), 0.1),
        'w_ffn_in': nrm(ks[13], (DEPTH, 2, D_MODEL, 2 * D_FF), D_MODEL ** -0.5),
        'w_ffn_out': nrm(ks[14], (DEPTH, 2, D_FF, D_MODEL), D_FF ** -0.5),
        'w_in': nrm(ks[15], (DEPTH, D_MODEL, D_PROJ), D_MODEL ** -0.5),
        'b_f': nrm(ks[16], (DEPTH, H_A), 0.1),
        'w_gla_up': nrm(ks[17], (DEPTH, GLA_RANK, H_B * DK_B), GLA_RANK ** -0.5),
        'b_gla_up': nrm(ks[18], (DEPTH, H_B * DK_B), 0.1),
        'g_gla': 1.0 + nrm(ks[19], (DEPTH, DV_B), 0.1),
        'g_diff': 1.0 + nrm(ks[20], (DEPTH, 2 * HD_C), 0.1),
        'lam_params': nrm(ks[21], (DEPTH, 4, HD_C), 0.1),
        'w_out': nrm(ks[22], (DEPTH, D_MIX, D_MODEL), D_MIX ** -0.5),
    }


def reference(x_prompt, x_sample, c_prompt, c_sample, cache_fox_k, cache_fox_v, cache_fox_logf, state_gla,
              cache_diff_k, cache_diff_v, w_ada, b_ada, g_norm, w_ffn_in, w_ffn_out, w_in, b_f, w_gla_up,
              b_gla_up, g_gla, g_diff, lam_params, w_out):
    slopes = 2.0 ** (-8.0 * jnp.arange(1, H_C + 1, dtype=jnp.float32) / H_C)

    def diff_lambda(l):
        lp = lam_params[l].astype(jnp.float32)
        lam_init = 0.8 - 0.6 * math.exp(-0.3 * l)
        lam = jnp.exp(jnp.sum(lp[0] * lp[1])) - jnp.exp(jnp.sum(lp[2] * lp[3])) + lam_init
        return lam, lam_init

    def prompt_mixer(l, h):
        b, t, _ = h.shape
        p = project(h, w_in[l], b_f[l], w_gla_up[l], b_gla_up[l])
        pos_k = jnp.arange(t)
        c_cum = jnp.cumsum(p['logf'], axis=1)
        o_a = sweep_query_blocks(
            lambda qi, cqi, pos_q: fox_block(qi, p['k_a'], p['v_a'], cqi, c_cum, pos_q, pos_k),
            (p['q_a'], c_cum), t)
        o_b, s_b = gla_chunked(p['q_b'], p['k_b'], p['v_b'], p['loga'], jnp.zeros((b, H_B, DK_B, DV_B), jnp.float32))
        lam, lam_init = diff_lambda(l)
        o_c = sweep_query_blocks(
            lambda qi, pos_q: diff_block(qi, p['k_c'], p['v_c'], pos_q, pos_k, lam, slopes),
            (p['q_c'],), t)
        out = merge_heads(o_a, o_b, p['r_b'], o_c, g_gla[l], g_diff[l], lam_init, w_out[l])
        return out, (p['k_a'], p['v_a'], p['logf'], s_b, p['k_c'].reshape(b, t, H_C, 2 * HD_C), p['v_c'])

    def sample_mixer(l, h):
        b, t, _ = h.shape
        past = cache_fox_k.shape[2]
        p = project(h, w_in[l], b_f[l], w_gla_up[l], b_gla_up[l])
        pos_q = past + jnp.arange(t)
        pos_k = jnp.arange(past + t)
        k_a = jnp.concatenate([cache_fox_k[l].astype(h.dtype), p['k_a']], axis=1)
        v_a = jnp.concatenate([cache_fox_v[l].astype(h.dtype), p['v_a']], axis=1)
        c_all = jnp.cumsum(jnp.concatenate([cache_fox_logf[l].astype(jnp.float32), p['logf']], axis=1), axis=1)
        o_a = fox_block(p['q_a'], k_a, v_a, c_all[:, past:], c_all, pos_q, pos_k)
        o_b, s_b = gla_chunked(p['q_b'], p['k_b'], p['v_b'], p['loga'], state_gla[l])
        lam, lam_init = diff_lambda(l)
        k_c = jnp.concatenate([cache_diff_k[l].astype(h.dtype).reshape(b, past, H_C, 2, HD_C), p['k_c']], axis=1)
        v_c = jnp.concatenate([cache_diff_v[l].astype(h.dtype), p['v_c']], axis=1)
        o_c = diff_block(p['q_c'], k_c, v_c, pos_q, pos_k, lam, slopes)
        out = merge_heads(o_a, o_b, p['r_b'], o_c, g_gla[l], g_diff[l], lam_init, w_out[l])
        return out, (p['k_a'], p['v_a'], p['logf'], s_b, p['k_c'].reshape(b, t, H_C, 2 * HD_C), p['v_c'])

    def trunk(x, c, mixer):
        b = x.shape[0]
        states = []
        for l in range(DEPTH):
            mod = (jax.nn.silu(c) @ w_ada[l] + b_ada[l]).reshape(b, N_SUB, 3, D_MODEL)

            def pre(x, s):
                h = rmsnorm(x, g_norm[l, 2 * s])
                return h * (1.0 + mod[:, s, 1][:, None, :]) + mod[:, s, 0][:, None, :]

            def post(x, s, y, res_w):
                return x + res_w * mod[:, s, 2][:, None, :] * rmsnorm(y, g_norm[l, 2 * s + 1])

            x = post(x, 0, swiglu(pre(x, 0), w_ffn_in[l, 0], w_ffn_out[l, 0]), 0.5)
            y, st = mixer(l, pre(x, 1))
            x = post(x, 1, y, 1.0)
            x = post(x, 2, swiglu(pre(x, 2), w_ffn_in[l, 1], w_ffn_out[l, 1]), 0.5)
            states.append(st)
        stacked = [jnp.stack([st[i] for st in states]) for i in range(6)]
        return x, stacked

    y_prompt, sp = trunk(x_prompt, c_prompt, prompt_mixer)
    y_sample, ss = trunk(x_sample, c_sample, sample_mixer)
    return (y_prompt, y_sample, sp[0], sp[1], sp[2], sp[3], sp[4], sp[5], ss[0], ss[1], ss[2], ss[3], ss[4], ss[5])
```

```python
import functools
import math

import jax
import jax.numpy as jnp
import numpy as np
from jax import lax
from jax.experimental import pallas as pl
from jax.experimental.pallas import tpu as pltpu

F32 = jnp.float32
BF16 = jnp.bfloat16

D_MODEL = 1024
N_HEADS = 4
HD_A = 64
DK_B = 32
DV_B = 64
GLA_RANK = 16
GLA_TAU = 16.0
HD_C = 64
CHUNK = 64
W_A = N_HEADS * HD_A
W_B = N_HEADS * DV_B
W_C = N_HEADS * 2 * HD_C
D_FF = ((8 * D_MODEL // 3 + 255) // 256) * 256
N_SUB = 3
PROJ_SIZES = (W_A, W_A, W_A, N_HEADS, N_HEADS * DK_B, N_HEADS * DK_B, W_B, GLA_RANK, W_B, W_C, W_C, W_C)
NEG = -1e30
EPS = 1e-6

LANES = 128
HEAD_PAD = 128
W_PAD = N_HEADS * HEAD_PAD
VMEM_LIMIT = 56 * 1024 * 1024

_SEG_WIDTHS = (
    ("ka", W_A), ("va", W_A), ("kc", W_C), ("vc", W_C), ("qa", W_A),
    ("qa_p", W_PAD), ("ka_p", W_PAD), ("va_p", W_PAD), ("qc1_p", W_PAD), ("qc2_p", W_PAD),
    ("fa", LANES), ("qb", LANES), ("kb", LANES), ("vb", W_B), ("rb", W_B), ("gb", LANES),
)
_SEG = {}
_off = 0
for _name, _w in _SEG_WIDTHS:
    _SEG[_name] = (_off, _off + _w)
    _off += _w
N_EXT = _off


def _nt_dot(a, b):
    return lax.dot_general(a, b, (((1,), (1,)), ((), ())), preferred_element_type=F32)


def _dot(a, b):
    return jnp.dot(a, b, preferred_element_type=F32)


def _split3(x):
    hi = x.astype(BF16)
    r1 = x - hi.astype(F32)
    mid = r1.astype(BF16)
    lo = (r1 - mid.astype(F32)).astype(BF16)
    return hi, mid, lo


def _dot_exact_rhs(x, m):
    mb = m.astype(BF16)
    hi, mid, lo = _split3(x)
    return _dot(hi, mb) + _dot(mid, mb) + _dot(lo, mb)


def _dot_exact_lhs(m, x):
    mb = m.astype(BF16)
    hi, mid, lo = _split3(x)
    return _dot(mb, hi) + _dot(mb, mid) + _dot(mb, lo)


def _rms(x, g):
    return x * lax.rsqrt(jnp.mean(x * x, axis=-1, keepdims=True) + EPS) * g


def _log_sigmoid(x):
    return jnp.minimum(x, 0.0) - jnp.log1p(jnp.exp(-jnp.abs(x)))


def _const_spec(shape):
    nd = len(shape)
    return pl.BlockSpec(shape, lambda *_: (0,) * nd, pipeline_mode=pl.Buffered(1))


def _params(sem):
    return pltpu.CompilerParams(dimension_semantics=sem, vmem_limit_bytes=VMEM_LIMIT)


def _ada_kernel(c_ref, w_ref, b_ref, o_ref):
    c = c_ref[...]
    a = (c * jax.nn.sigmoid(c)).astype(BF16)
    o_ref[...] = _dot(a, w_ref[...]) + b_ref[...]


def _ada_call(c, w, b):
    m, n = c.shape[0], w.shape[1]
    tn = n // 8
    return pl.pallas_call(
        _ada_kernel,
        grid=(n // tn,),
        in_specs=[pl.BlockSpec((m, D_MODEL), lambda i: (0, 0)),
                  pl.BlockSpec((D_MODEL, tn), lambda i: (0, i)),
                  pl.BlockSpec((1, tn), lambda i: (0, i))],
        out_specs=pl.BlockSpec((m, tn), lambda i: (0, i)),
        out_shape=jax.ShapeDtypeStruct((m, n), F32),
        compiler_params=_params(("parallel",)),
        name="ada",
    )(c, w, b)


FF_CHUNK = 256


def _ffn_kernel(x_ref, shift_ref, scale_ref, gate_ref, g1_ref, g2_ref, win_ref, wout_ref, o_ref, *, res_w):
    x = x_ref[...]
    h = _rms(x, g1_ref[...]) * (1.0 + scale_ref[0]) + shift_ref[0]
    hb = h.astype(BF16)
    y = jnp.zeros(x.shape, F32)
    for c in range(D_FF // FF_CHUNK):
        lo, hi = c * FF_CHUNK, (c + 1) * FF_CHUNK
        g = _dot(hb, win_ref[:, lo:hi])
        u = _dot(hb, win_ref[:, D_FF + lo:D_FF + hi])
        a = (g * jax.nn.sigmoid(g) * u).astype(BF16)
        y = y + _dot(a, wout_ref[lo:hi, :])
    o_ref[...] = x + res_w * gate_ref[0] * _rms(y, g2_ref[...])


def _mod_spec(r, tiles_per_mod):
    return pl.BlockSpec((1, r, D_MODEL), lambda i: (i // tiles_per_mod, 0, 0))


def _ffn_call(x, mod, g1, g2, w_in, w_out, res_w, tm, tiles_per_mod):
    n = x.shape[0]
    shift, scale, gate = mod
    r = shift.shape[1]
    tok = pl.BlockSpec((tm, D_MODEL), lambda i: (i, 0))
    return pl.pallas_call(
        functools.partial(_ffn_kernel, res_w=res_w),
        grid=(n // tm,),
        in_specs=[tok, _mod_spec(r, tiles_per_mod), _mod_spec(r, tiles_per_mod), _mod_spec(r, tiles_per_mod),
                  _const_spec((1, D_MODEL)), _const_spec((1, D_MODEL)),
                  _const_spec((D_MODEL, 2 * D_FF)), _const_spec((D_FF, D_MODEL))],
        out_specs=tok,
        out_shape=jax.ShapeDtypeStruct((n, D_MODEL), F32),
        compiler_params=_params(("parallel",)),
        name="ffn",
    )(x, shift, scale, gate, g1, g2, w_in, w_out)


_PROJ_OUTS = (
    ("ka", W_A, F32), ("va", W_A, F32), ("kc", W_C, F32), ("vc", W_C, F32), ("logf", N_HEADS, F32),
    ("qa", W_A, F32), ("qa_p", W_PAD, BF16), ("ka_p", W_PAD, BF16), ("va_p", W_PAD, BF16),
    ("qc1_p", W_PAD, BF16), ("qc2_p", W_PAD, BF16), ("kc_b", W_C, BF16), ("vc_b", W_C, BF16),
    ("qb", LANES, F32), ("kb", LANES, F32), ("vb", W_B, F32), ("loga", LANES, F32), ("rb", W_B, F32),
)


def _proj_kernel(x_ref, shift_ref, scale_ref, g1_ref, w_ref, bf_ref, wup_ref, bup_ref, vone_ref, *out_refs):
    o = dict(zip([n for n, _, _ in _PROJ_OUTS], out_refs))
    x = x_ref[...]
    hb = (_rms(x, g1_ref[...]) * (1.0 + scale_ref[0]) + shift_ref[0]).astype(BF16)

    def seg(name):
        lo, hi = _SEG[name]
        return _dot(hb, w_ref[:, lo:hi])

    ka = seg("ka")
    o["ka"][...] = ka
    va = seg("va")
    o["va"][...] = va
    kc = seg("kc")
    o["kc"][...] = kc
    o["kc_b"][...] = kc.astype(BF16)
    vc = seg("vc")
    o["vc"][...] = vc
    o["vc_b"][...] = vc.astype(BF16)
    o["qa"][...] = seg("qa") * (HD_A ** -0.5)
    o["qa_p"][...] = (seg("qa_p") * (HD_A ** -0.5)).astype(BF16)
    o["ka_p"][...] = seg("ka_p").astype(BF16)
    o["va_p"][...] = (seg("va_p") + vone_ref[...]).astype(BF16)
    o["qc1_p"][...] = (seg("qc1_p") * (HD_C ** -0.5)).astype(BF16)
    o["qc2_p"][...] = (seg("qc2_p") * (HD_C ** -0.5)).astype(BF16)
    fa = seg("fa") + bf_ref[...]
    o["logf"][...] = _log_sigmoid(fa)[:, :N_HEADS]
    o["qb"][...] = seg("qb")
    o["kb"][...] = seg("kb")
    o["vb"][...] = seg("vb")
    o["rb"][...] = seg("rb")
    gb = seg("gb").astype(BF16)
    o["loga"][...] = _log_sigmoid(_dot(gb, wup_ref[...]) + bup_ref[...]) * (1.0 / GLA_TAU)


def _proj_call(x, mod, g1, w_ext, bf, wup, bup, vone, tm, tiles_per_mod):
    n = x.shape[0]
    shift, scale, _ = mod
    r = shift.shape[1]
    tok = lambda w: pl.BlockSpec((tm, w), lambda i: (i, 0))
    outs = pl.pallas_call(
        _proj_kernel,
        grid=(n // tm,),
        in_specs=[tok(D_MODEL), _mod_spec(r, tiles_per_mod), _mod_spec(r, tiles_per_mod),
                  _const_spec((1, D_MODEL)), _const_spec((D_MODEL, N_EXT)), _const_spec((1, LANES)),
                  _const_spec((LANES, LANES)), _const_spec((1, LANES)), _const_spec((1, W_PAD))],
        out_specs=[tok(w) for _, w, _ in _PROJ_OUTS],
        out_shape=[jax.ShapeDtypeStruct((n, w), dt) for _, w, dt in _PROJ_OUTS],
        compiler_params=_params(("parallel",)),
        name="proj",
    )(x, shift, scale, g1, w_ext, bf, wup, bup, vone)
    return dict(zip([nm for nm, _, _ in _PROJ_OUTS], outs))


def _cumsum_kernel(x_ref, o_ref):
    x = x_ref[0]
    nb = x.shape[0]
    r = lax.broadcasted_iota(jnp.int32, (LANES, LANES), 0)
    c = lax.broadcasted_iota(jnp.int32, (LANES, LANES), 1)
    local = _dot_exact_rhs(x, (r <= c).astype(F32))
    tot = jnp.broadcast_to(local[:, LANES - 1:LANES], (nb, LANES))
    rr = lax.broadcasted_iota(jnp.int32, (nb, nb), 0)
    cc = lax.broadcasted_iota(jnp.int32, (nb, nb), 1)
    o_ref[0] = local + _dot_exact_lhs((cc < rr).astype(F32), tot)


def _cumsum_call(x):
    rows, nb, _ = x.shape
    spec = pl.BlockSpec((1, nb, LANES), lambda i: (i, 0, 0))
    return pl.pallas_call(
        _cumsum_kernel, grid=(rows,), in_specs=[spec], out_specs=spec,
        out_shape=jax.ShapeDtypeStruct(x.shape, F32),
        compiler_params=_params(("parallel",)), name="cumsum",
    )(x)


def _fox_kernel(q_ref, k_ref, v_ref, c_ref, cref_ref, o_ref, m_sc, acc_sc):
    i = pl.program_id(2)
    j = pl.program_id(3)
    tq = q_ref.shape[1]
    tk = k_ref.shape[1]

    @pl.when(j == 0)
    def _():
        m_sc[...] = jnp.full(m_sc.shape, NEG, F32)
        acc_sc[...] = jnp.zeros(acc_sc.shape, F32)

    def step(diag):
        s = _nt_dot(q_ref[0], k_ref[0])
        t = s + (cref_ref[0, 0][:, 0:1] - c_ref[0, 0])
        if diag:
            row = lax.broadcasted_iota(jnp.int32, (tq, tk), 0)
            col = lax.broadcasted_iota(jnp.int32, (tq, tk), 1)
            t = jnp.where(col <= row, t, NEG)
        m_old = m_sc[...]
        m_new = jnp.maximum(m_old, jnp.max(t, axis=1, keepdims=True))
        p = jnp.exp(t - m_new).astype(BF16)
        acc_sc[...] = jnp.exp(m_old - m_new) * acc_sc[...] + _dot(p, v_ref[0])
        m_sc[...] = m_new

    @pl.when(j < i)
    def _():
        step(False)

    @pl.when(j == i)
    def _():
        step(True)
        acc = acc_sc[...]
        o_ref[0] = (acc / acc[:, HD_A:HD_A + 1]).astype(BF16)


def _fox_call(qa_p, ka_p, va_p, c, tq):
    b, t, _ = qa_p.shape
    nq = t // tq
    qspec = pl.BlockSpec((1, tq, HEAD_PAD), lambda b_, h, i, j: (b_, i, h))
    kspec = pl.BlockSpec((1, tq, HEAD_PAD), lambda b_, h, i, j: (b_, jnp.minimum(j, i), h))
    return pl.pallas_call(
        _fox_kernel,
        grid=(b, N_HEADS, nq, nq),
        in_specs=[qspec, kspec, kspec,
                  pl.BlockSpec((1, 1, 1, tq), lambda b_, h, i, j: (b_, h, 0, jnp.minimum(j, i))),
                  pl.BlockSpec((1, 1, 1, LANES), lambda b_, h, i, j: (b_, h, 0, i * (tq // LANES)))],
        out_specs=qspec,
        out_shape=jax.ShapeDtypeStruct((b, t, W_PAD), BF16),
        scratch_shapes=[pltpu.VMEM((tq, 1), F32), pltpu.VMEM((tq, HEAD_PAD), F32)],
        compiler_params=_params(("parallel", "parallel", "parallel", "arbitrary")),
        name="fox",
    )(qa_p, ka_p, va_p, c, c)


def _diff_lambda(lp, lam_init):
    a = jnp.sum(lp[0:1] * lp[1:2], axis=1, keepdims=True)
    b = jnp.sum(lp[2:3] * lp[3:4], axis=1, keepdims=True)
    return jnp.exp(a) - jnp.exp(b) + lam_init


def _diff_kernel(q1_ref, q2_ref, k_ref, v_ref, slope_ref, lam_ref, g_ref, o_ref,
                 m1_sc, l1_sc, a1_sc, m2_sc, l2_sc, a2_sc, *, lam_init):
    i = pl.program_id(2)
    j = pl.program_id(3)
    tq = q1_ref.shape[1]
    tk = k_ref.shape[1]

    @pl.when(j == 0)
    def _():
        for m_sc, l_sc, a_sc in ((m1_sc, l1_sc, a1_sc), (m2_sc, l2_sc, a2_sc)):
            m_sc[...] = jnp.full(m_sc.shape, NEG, F32)
            l_sc[...] = jnp.zeros(l_sc.shape, F32)
            a_sc[...] = jnp.zeros(a_sc.shape, F32)

    def step(diag):
        k = k_ref[0]
        v = v_ref[0]
        slope = slope_ref[0][:, 0:1]
        if diag:
            row = lax.broadcasted_iota(jnp.int32, (tq, tk), 0)
            col = lax.broadcasted_iota(jnp.int32, (tq, tk), 1)
            bias = slope * (2 * jnp.minimum(row, col) - col).astype(F32)
            visible = lax.shift_right_logical(col, 6) <= lax.shift_right_logical(row, 6)
            bias = jnp.where(visible, bias, NEG)
        else:
            col = lax.broadcasted_iota(jnp.int32, (1, tk), 1) + (j - i) * tk
            bias = slope * col.astype(F32)
        for q_ref, m_sc, l_sc, a_sc in ((q1_ref, m1_sc, l1_sc, a1_sc), (q2_ref, m2_sc, l2_sc, a2_sc)):
            t = _nt_dot(q_ref[0], k) + bias
            m_old = m_sc[...]
            m_new = jnp.maximum(m_old, jnp.max(t, axis=1, keepdims=True))
            alpha = jnp.exp(m_old - m_new)
            p = jnp.exp(t - m_new)
            l_sc[...] = alpha * l_sc[...] + jnp.sum(p, axis=1, keepdims=True)
            a_sc[...] = alpha * a_sc[...] + _dot(p.astype(BF16), v)
            m_sc[...] = m_new

    @pl.when(j < i)
    def _():
        step(False)

    @pl.when(j == i)
    def _():
        step(True)
        lam = _diff_lambda(lam_ref[...], lam_init)
        out = a1_sc[...] / l1_sc[...] - lam * (a2_sc[...] / l2_sc[...])
        o_ref[0] = (_rms(out, g_ref[...]) * (1.0 - lam_init)).astype(BF16)


def _diff_call(qc1_p, qc2_p, kc_b, vc_b, slopes, lam_p, g_diff, lam_init, tq):
    assert CHUNK == 64 and tq % CHUNK == 0
    b, t, _ = qc1_p.shape
    nq = t // tq
    qspec = pl.BlockSpec((1, tq, HEAD_PAD), lambda b_, h, i, j: (b_, i, h))
    kspec = pl.BlockSpec((1, tq, HEAD_PAD), lambda b_, h, i, j: (b_, jnp.minimum(j, i), h))
    stat = pltpu.VMEM((tq, 1), F32)
    accs = pltpu.VMEM((tq, HEAD_PAD), F32)
    return pl.pallas_call(
        functools.partial(_diff_kernel, lam_init=lam_init),
        grid=(b, N_HEADS, nq, nq),
        in_specs=[qspec, qspec, kspec, kspec,
                  pl.BlockSpec((1, 1, LANES), lambda b_, h, i, j: (h, 0, 0)),
                  pl.BlockSpec((4, HD_C), lambda b_, h, i, j: (0, 0)),
                  pl.BlockSpec((1, HEAD_PAD), lambda b_, h, i, j: (0, 0))],
        out_specs=qspec,
        out_shape=jax.ShapeDtypeStruct((b, t, W_C), BF16),
        scratch_shapes=[stat, stat, accs, stat, stat, accs],
        compiler_params=_params(("parallel", "parallel", "parallel", "arbitrary")),
        name="diff",
    )(qc1_p, qc2_p, kc_b, vc_b, slopes, lam_p, g_diff)


def _gla_kernel(q_ref, k_ref, v_ref, la_ref, r_ref, s0_ref, g_ref, o_ref, s_ref, *, chunk):
    ti = pl.program_id(1)
    n_chunks = q_ref.shape[1] // chunk

    @pl.when(ti == 0)
    def _():
        s_ref[0] = s0_ref[0]

    tri_r = lax.broadcasted_iota(jnp.int32, (chunk, chunk), 0)
    tri_c = lax.broadcasted_iota(jnp.int32, (chunk, chunk), 1)
    incl = (tri_c <= tri_r).astype(F32)
    srow = lax.broadcasted_iota(jnp.int32, (N_HEADS * chunk, LANES), 0) // chunk
    slane = lax.broadcasted_iota(jnp.int32, (N_HEADS * chunk, LANES), 1) // DK_B
    qmask = srow == slane
    arow = lax.broadcasted_iota(jnp.int32, (N_HEADS * chunk, chunk), 0) % chunk
    acol = lax.broadcasted_iota(jnp.int32, (N_HEADS * chunk, chunk), 1)
    causal = acol <= arow
    vlane = lax.broadcasted_iota(jnp.int32, (chunk, W_B), 1) // DV_B
    st_row = lax.broadcasted_iota(jnp.int32, (W_B, LANES), 0) // DV_B
    st_lane = lax.broadcasted_iota(jnp.int32, (W_B, LANES), 1) // DK_B
    diag_blocks = st_row == st_lane
    gr = lax.broadcasted_iota(jnp.int32, (W_B, W_B), 0) // DV_B
    gc = lax.broadcasted_iota(jnp.int32, (W_B, W_B), 1) // DV_B
    group_mean = jnp.where(gr == gc, 1.0 / DV_B, 0.0)

    for c in range(n_chunks):
        sl = slice(c * chunk, (c + 1) * chunk)
        cb = _dot_exact_lhs(incl, la_ref[0, sl, :])
        cb_last = cb[chunk - 1:chunk, :]
        q = q_ref[0, sl, :] * (DK_B ** -0.5)
        k = k_ref[0, sl, :]
        v = v_ref[0, sl, :]
        vb = v.astype(BF16)
        qe = q * jnp.exp(cb)
        ke = (k * jnp.exp(-cb)).astype(BF16)
        kl = (k * jnp.exp(cb_last - cb)).astype(BF16)
        decay = jnp.exp(cb_last)
        qstack = jnp.where(qmask, jnp.concatenate([qe] * N_HEADS, axis=0), 0.0).astype(BF16)
        a = jnp.where(causal, _nt_dot(qstack, ke), 0.0).astype(BF16)
        oi = _dot(a, vb)
        o = jnp.zeros((chunk, W_B), F32)
        for h in range(N_HEADS):
            o = o + jnp.where(vlane == h, oi[h * chunk:(h + 1) * chunk], 0.0)
        st = s_ref[0]
        o = o + _nt_dot(qe.astype(BF16), st.astype(BF16))
        upd = _dot(v.T.astype(BF16), kl)
        s_ref[0] = decay * st + jnp.where(diag_blocks, upd, 0.0)
        ms = _dot_exact_rhs(o * o, group_mean)
        r = r_ref[0, sl, :]
        o_ref[0, sl, :] = (o * lax.rsqrt(ms + EPS) * g_ref[...] * (r * jax.nn.sigmoid(r))).astype(BF16)


def _gla_call(qb, kb, vb, loga, rb, s0_t, g_gla, tg, chunk):
    b, t, _ = qb.shape
    tok = lambda w: pl.BlockSpec((1, tg, w), lambda b_, i: (b_, i, 0))
    sspec = pl.BlockSpec((1, W_B, LANES), lambda b_, i: (b_, 0, 0))
    return pl.pallas_call(
        functools.partial(_gla_kernel, chunk=chunk),
        grid=(b, t // tg),
        in_specs=[tok(LANES), tok(LANES), tok(W_B), tok(LANES), tok(W_B), sspec,
                  pl.BlockSpec((1, W_B), lambda b_, i: (0, 0))],
        out_specs=[tok(W_B), sspec],
        out_shape=[jax.ShapeDtypeStruct((b, t, W_B), BF16), jax.ShapeDtypeStruct((b, W_B, LANES), F32)],
        compiler_params=_params(("parallel", "arbitrary")),
        name="gla",
    )(qb, kb, vb, loga, rb, s0_t, g_gla)


def _softmax_two(t_past, t_new):
    m = jnp.maximum(jnp.max(t_past, axis=1, keepdims=True), jnp.max(t_new, axis=1, keepdims=True))
    p_past = jnp.exp(t_past - m)
    p_new = jnp.exp(t_new - m)
    l = jnp.sum(p_past, axis=1, keepdims=True) + jnp.sum(p_new, axis=1, keepdims=True)
    return p_past, p_new, l


def _fox_s_kernel(q_ref, kn_ref, vn_ref, kp_ref, vp_ref, c_ref, o_ref):
    t, past = q_ref.shape[1], kp_ref.shape[1]
    q = q_ref[0]
    kn = kn_ref[0].astype(BF16)
    vn = vn_ref[0].astype(BF16)
    kp = kp_ref[0].astype(BF16)
    vp = vp_ref[0].astype(BF16)
    lane_head = lax.broadcasted_iota(jnp.int32, (t, W_A), 1) // HD_A
    row = lax.broadcasted_iota(jnp.int32, (t, t), 0)
    col = lax.broadcasted_iota(jnp.int32, (t, t), 1)
    o = jnp.zeros((t, W_A), F32)
    for h in range(N_HEADS):
        qh = jnp.where(lane_head == h, q, 0.0).astype(BF16)
        ch = c_ref[0, h:h + 1, :]
        cref = ch[:, past:past + 1]
        t_past = _nt_dot(qh, kp) + (cref - ch[:, :past])
        t_new = jnp.where(col <= row, _nt_dot(qh, kn) + (cref - ch[:, past:past + t]), NEG)
        p_past, p_new, l = _softmax_two(t_past, t_new)
        oh = (_dot(p_past.astype(BF16), vp) + _dot(p_new.astype(BF16), vn)) / l
        o = o + jnp.where(lane_head == h, oh, 0.0)
    o_ref[0] = o.astype(BF16)


def _fox_s_call(qa, ka, va, cache_k, cache_v, c):
    b, t, _ = qa.shape
    past = cache_k.shape[1]
    new = lambda: pl.BlockSpec((1, t, W_A), lambda i: (i, 0, 0))
    old = lambda: pl.BlockSpec((1, past, W_A), lambda i: (i, 0, 0))
    return pl.pallas_call(
        _fox_s_kernel,
        grid=(b,),
        in_specs=[new(), new(), new(), old(), old(),
                  pl.BlockSpec((1, N_HEADS, c.shape[2]), lambda i: (i, 0, 0))],
        out_specs=new(),
        out_shape=jax.ShapeDtypeStruct((b, t, W_A), BF16),
        compiler_params=_params(("parallel",)),
        name="fox_s",
    )(qa, ka, va, cache_k, cache_v, c)


def _diff_s_kernel(q1_ref, q2_ref, kn_ref, vn_ref, kp_ref, vp_ref, slope_ref, lam_ref, g_ref, o_ref, *, lam_init):
    t, past = q1_ref.shape[1], kp_ref.shape[1]
    lam = _diff_lambda(lam_ref[...], lam_init)
    row = lax.broadcasted_iota(jnp.int32, (t, t), 0) + past
    col = lax.broadcasted_iota(jnp.int32, (t, t), 1) + past
    visible_new = lax.shift_right_logical(col, 6) <= lax.shift_right_logical(row, 6)
    dist_new = jnp.abs(row - col).astype(F32)
    prow = lax.broadcasted_iota(jnp.int32, (t, past), 0) + past
    pcol = lax.broadcasted_iota(jnp.int32, (t, past), 1)
    visible_past = lax.shift_right_logical(pcol, 6) <= lax.shift_right_logical(prow, 6)
    dist_past = (prow - pcol).astype(F32)
    for h in range(N_HEADS):
        hs = slice(h * HEAD_PAD, (h + 1) * HEAD_PAD)
        slope = slope_ref[h][:, 0:1]
        kn = kn_ref[0, :, hs].astype(BF16)
        vn = vn_ref[0, :, hs].astype(BF16)
        kp = kp_ref[0, :, hs].astype(BF16)
        vp = vp_ref[0, :, hs].astype(BF16)
        outs = []
        for q_ref in (q1_ref, q2_ref):
            q = q_ref[0, :, hs]
            t_past = jnp.where(visible_past, _nt_dot(q, kp) - slope * dist_past, NEG)
            t_new = jnp.where(visible_new, _nt_dot(q, kn) - slope * dist_new, NEG)
            p_past, p_new, l = _softmax_two(t_past, t_new)
            outs.append((_dot(p_past.astype(BF16), vp) + _dot(p_new.astype(BF16), vn)) / l)
        out = outs[0] - lam * outs[1]
        o_ref[0, :, hs] = (_rms(out, g_ref[...]) * (1.0 - lam_init)).astype(BF16)


def _diff_s_call(qc1_p, qc2_p, kc, vc, cache_k, cache_v, slopes, lam_p, g_diff, lam_init):
    assert CHUNK == 64
    b, t, _ = qc1_p.shape
    past = cache_k.shape[1]
    new = lambda: pl.BlockSpec((1, t, W_C), lambda i: (i, 0, 0))
    old = lambda: pl.BlockSpec((1, past, W_C), lambda i: (i, 0, 0))
    return pl.pallas_call(
        functools.partial(_diff_s_kernel, lam_init=lam_init),
        grid=(b,),
        in_specs=[new(), new(), new(), new(), old(), old(),
                  pl.BlockSpec((N_HEADS, 1, LANES), lambda i: (0, 0, 0)),
                  pl.BlockSpec((4, HD_C), lambda i: (0, 0)),
                  pl.BlockSpec((1, HEAD_PAD), lambda i: (0, 0))],
        out_specs=new(),
        out_shape=jax.ShapeDtypeStruct((b, t, W_C), BF16),
        compiler_params=_params(("parallel",)),
        name="diff_s",
    )(qc1_p, qc2_p, kc, vc, cache_k, cache_v, slopes, lam_p, g_diff)


def _out_kernel(x_ref, oa_ref, gla_ref, dif_ref, gate_ref, g2_ref, woa_ref, wog_ref, wod_ref, o_ref):
    y = _dot(oa_ref[...], woa_ref[...]) + _dot(gla_ref[...], wog_ref[...]) + _dot(dif_ref[...], wod_ref[...])
    o_ref[...] = x_ref[...] + gate_ref[0] * _rms(y, g2_ref[...])


def _out_call(x, oa, gla, dif, mod, g2, w_oa, w_og, w_od, tm, tiles_per_mod):
    n = x.shape[0]
    gate = mod[2]
    r = gate.shape[1]
    tok = lambda w: pl.BlockSpec((tm, w), lambda i: (i, 0))
    return pl.pallas_call(
        _out_kernel,
        grid=(n // tm,),
        in_specs=[tok(D_MODEL), tok(oa.shape[1]), tok(W_B), tok(W_C), _mod_spec(r, tiles_per_mod),
                  _const_spec((1, D_MODEL)), _const_spec(w_oa.shape), _const_spec(w_og.shape),
                  _const_spec(w_od.shape)],
        out_specs=tok(D_MODEL),
        out_shape=jax.ShapeDtypeStruct((n, D_MODEL), F32),
        compiler_params=_params(("parallel",)),
        name="out_proj",
    )(x, oa, gla, dif, gate, g2, w_oa, w_og, w_od)


def _pad_heads(w, hd, off=0):
    rows = w.shape[0]
    out = jnp.zeros((rows, N_HEADS, HEAD_PAD), w.dtype)
    out = out.at[:, :, off:off + hd].set(w.reshape(rows, N_HEADS, hd))
    return out.reshape(rows, W_PAD)


def _pad_lanes(w, width=LANES):
    return jnp.pad(w, ((0, 0), (0, width - w.shape[1])))


def _layer_params(l, w_ffn_in, w_ffn_out, w_in, b_f, w_gla_up, b_gla_up, g_gla, g_diff, w_out):
    offs = [int(o) for o in np.cumsum(PROJ_SIZES)[:-1]]
    qa, ka, va, fa, qb, kb, vb, gb, rb, qc, kc, vc = jnp.split(w_in[l], offs, axis=1)
    qc = qc.reshape(D_MODEL, N_HEADS, 2, HD_C)
    segs = {
        "ka": ka, "va": va, "kc": kc, "vc": vc, "qa": qa,
        "qa_p": _pad_heads(qa, HD_A), "ka_p": _pad_heads(ka, HD_A), "va_p": _pad_heads(va, HD_A),
        "qc1_p": _pad_heads(qc[:, :, 0].reshape(D_MODEL, -1), HD_C),
        "qc2_p": _pad_heads(qc[:, :, 1].reshape(D_MODEL, -1), HD_C, off=HD_C),
        "fa": _pad_lanes(fa), "qb": qb, "kb": kb, "vb": vb, "rb": rb, "gb": _pad_lanes(gb),
    }
    w_ext = jnp.concatenate([segs[n] for n, _ in _SEG_WIDTHS], axis=1).astype(BF16)
    wo = w_out[l]
    vone = np.zeros((1, W_PAD), np.float32)
    vone[0, HD_A::HEAD_PAD] = 1.0
    return dict(
        w_ffn_in=w_ffn_in[l].astype(BF16), w_ffn_out=w_ffn_out[l].astype(BF16), w_ext=w_ext,
        bf=_pad_lanes(b_f[l][None, :]),
        wup=jnp.pad(w_gla_up[l], ((0, LANES - GLA_RANK), (0, 0))).astype(BF16),
        bup=b_gla_up[l][None, :],
        vone=jnp.asarray(vone),
        g_gla=jnp.tile(g_gla[l], N_HEADS)[None, :], g_diff=g_diff[l][None, :],
        w_oa=wo[:W_A].astype(BF16),
        w_oa_p=_pad_heads(wo[:W_A].T, HD_A).T.astype(BF16),
        w_og=wo[W_A:W_A + W_B].astype(BF16), w_od=wo[W_A + W_B:].astype(BF16),
    )


def _state_to_blockdiag_t(s):
    eye = jnp.eye(N_HEADS, dtype=s.dtype)
    return jnp.einsum("bhkv,hg->bhvgk", s, eye).reshape(s.shape[0], W_B, N_HEADS * DK_B)


def _blockdiag_t_to_state(st):
    b = st.shape[0]
    s5 = st.reshape(b, N_HEADS, DV_B, N_HEADS, DK_B)
    diag = jnp.stack([s5[:, h, :, h, :] for h in range(N_HEADS)], axis=1)
    return jnp.swapaxes(diag, 2, 3)


def _tile(n, pref):
    t = min(n, pref)
    assert n % t == 0, (n, t)
    return t


def _trunk(x, mods, params, g_norm, lam_params, mixer, tm, tiles_per_mod):
    b, t, _ = x.shape
    n = b * t
    xf = x.reshape(n, D_MODEL)
    states = []
    for l in range(len(params)):
        p = params[l]
        gn = lambda k: g_norm[l, k][None, :]
        xf = _ffn_call(xf, mods[l][0], gn(0), gn(1), p["w_ffn_in"][0], p["w_ffn_out"][0], 0.5, tm, tiles_per_mod)
        pr = _proj_call(xf, mods[l][1], gn(2), p["w_ext"], p["bf"], p["wup"], p["bup"], p["vone"], tm, tiles_per_mod)
        lam_init = 0.8 - 0.6 * math.exp(-0.3 * l)
        oa, gla, dif, s_fin, w_oa = mixer(l, pr, p, lam_params[l], lam_init, b, t)
        xf = _out_call(xf, oa, gla, dif, mods[l][1], gn(3), w_oa, p["w_og"], p["w_od"], tm, tiles_per_mod)
        xf = _ffn_call(xf, mods[l][2], gn(4), gn(5), p["w_ffn_in"][1], p["w_ffn_out"][1], 0.5, tm, tiles_per_mod)
        shp = lambda a, *tail: a.reshape((b, t) + tail)
        states.append((shp(pr["ka"], N_HEADS, HD_A), shp(pr["va"], N_HEADS, HD_A), shp(pr["logf"], N_HEADS), s_fin,
                       shp(pr["kc"], N_HEADS, 2 * HD_C), shp(pr["vc"], N_HEADS, 2 * HD_C)))
    stacked = [jnp.stack([st[i] for st in states]) for i in range(6)]
    return xf.reshape(b, t, D_MODEL), stacked


def kernel(x_prompt, x_sample, c_prompt, c_sample, cache_fox_k, cache_fox_v, cache_fox_logf, state_gla,
           cache_diff_k, cache_diff_v, w_ada, b_ada, g_norm, w_ffn_in, w_ffn_out, w_in, b_f, w_gla_up,
           b_gla_up, g_gla, g_diff, lam_params, w_out):
    depth = w_in.shape[0]
    bp, tp, _ = x_prompt.shape
    bs, ts, _ = x_sample.shape
    past = cache_fox_k.shape[2]

    params = [_layer_params(l, w_ffn_in, w_ffn_out, w_in, b_f, w_gla_up, b_gla_up, g_gla, g_diff, w_out)
              for l in range(depth)]

    slopes = 2.0 ** (-8.0 * jnp.arange(1, N_HEADS + 1, dtype=F32) / N_HEADS)
    slopes = jnp.broadcast_to(slopes[:, None, None], (N_HEADS, 1, LANES))

    rows = bp + bs
    rows_pad = -(-rows // 8) * 8
    c_all = jnp.pad(jnp.concatenate([c_prompt, c_sample], axis=0), ((0, rows_pad - rows), (0, 0)))
    mods_p, mods_s = [], []
    for l in range(depth):
        mod = _ada_call(c_all, w_ada[l].astype(BF16), b_ada[l][None, :]).reshape(rows_pad, N_SUB, 3, D_MODEL)
        mods_p.append([[mod[:bp, s, k][:, None, :] for k in range(3)] for s in range(N_SUB)])
        mods_s.append([[jnp.repeat(mod[bp:rows, s, k], ts, axis=0)[None] for k in range(3)] for s in range(N_SUB)])

    tm_p = _tile(tp, 512)
    tq = _tile(tp, 512)
    tg = _tile(tp, 512)

    def prompt_mixer(l, pr, p, lam_p, lam_init, b, t):
        r3 = lambda a: a.reshape(b, t, a.shape[-1])
        logf_t = jnp.swapaxes(r3(pr["logf"]), 1, 2).reshape(b * N_HEADS, t // LANES, LANES)
        c = _cumsum_call(logf_t).reshape(b, N_HEADS, 1, t)
        oa = _fox_call(r3(pr["qa_p"]), r3(pr["ka_p"]), r3(pr["va_p"]), c, tq)
        s0_t = jnp.zeros((b, W_B, N_HEADS * DK_B), F32)
        gla, s_t = _gla_call(r3(pr["qb"]), r3(pr["kb"]), r3(pr["vb"]), r3(pr["loga"]), r3(pr["rb"]), s0_t,
                             p["g_gla"], tg, min(CHUNK, t))
        dif = _diff_call(r3(pr["qc1_p"]), r3(pr["qc2_p"]), r3(pr["kc_b"]), r3(pr["vc_b"]), slopes, lam_p,
                         p["g_diff"], lam_init, tq)
        flat = lambda a: a.reshape(b * t, a.shape[-1])
        return flat(oa), flat(gla), flat(dif), _blockdiag_t_to_state(s_t), p["w_oa_p"]

    def sample_mixer(l, pr, p, lam_p, lam_init, b, t):
        r3 = lambda a: a.reshape(b, t, a.shape[-1])
        total = past + t
        nb = -(-total // (8 * LANES)) * 8
        lf = jnp.concatenate([cache_fox_logf[l].astype(F32), r3(pr["logf"])], axis=1)
        lf = jnp.pad(jnp.swapaxes(lf, 1, 2), ((0, 0), (0, 0), (0, nb * LANES - total)))
        c = _cumsum_call(lf.reshape(b * N_HEADS, nb, LANES)).reshape(b, N_HEADS, nb * LANES)
        oa = _fox_s_call(r3(pr["qa"]), r3(pr["ka"]), r3(pr["va"]),
                         cache_fox_k[l].reshape(b, past, W_A), cache_fox_v[l].reshape(b, past, W_A), c)
        gla, s_t = _gla_call(r3(pr["qb"]), r3(pr["kb"]), r3(pr["vb"]), r3(pr["loga"]), r3(pr["rb"]),
                             _state_to_blockdiag_t(state_gla[l].astype(F32)), p["g_gla"], t, min(CHUNK, t))
        dif = _diff_s_call(r3(pr["qc1_p"]), r3(pr["qc2_p"]), r3(pr["kc"]), r3(pr["vc"]),
                           cache_diff_k[l].reshape(b, past, W_C), cache_diff_v[l].reshape(b, past, W_C),
                           slopes, lam_p, p["g_diff"], lam_init)
        flat = lambda a: a.reshape(b * t, a.shape[-1])
        return flat(oa), flat(gla), flat(dif), _blockdiag_t_to_state(s_t), p["w_oa"]

    y_p, sp = _trunk(x_prompt, mods_p, params, g_norm, lam_params, prompt_mixer, tm_p, tp // tm_p)
    y_s, ss = _trunk(x_sample, mods_s, params, g_norm, lam_params, sample_mixer, bs * ts, 1)
    return (y_p, y_s, sp[0], sp[1], sp[2], sp[3], sp[4], sp[5], ss[0], ss[1], ss[2], ss[3], ss[4], ss[5])
```

```python
import functools
import math

import jax
import jax.numpy as jnp
import numpy as np
from jax import lax
from jax.experimental import pallas as pl
from jax.experimental.pallas import tpu as pltpu

F32 = jnp.float32
BF16 = jnp.bfloat16

D_MODEL = 1024
N_HEADS = 4
HD_A = 64
DK_B = 32
DV_B = 64
GLA_RANK = 16
GLA_TAU = 16.0
HD_C = 64
CHUNK = 64
W_A = N_HEADS * HD_A
W_B = N_HEADS * DV_B
W_C = N_HEADS * 2 * HD_C
D_FF = ((8 * D_MODEL // 3 + 255) // 256) * 256
N_SUB = 3
PROJ_SIZES = (W_A, W_A, W_A, N_HEADS, N_HEADS * DK_B, N_HEADS * DK_B, W_B, GLA_RANK, W_B, W_C, W_C, W_C)
NEG = -1e30
EPS = 1e-6
LOG2E = math.log2(math.e)

LANES = 128
HEAD_PAD = 128
W_PAD = N_HEADS * HEAD_PAD
VMEM_LIMIT = 56 * 1024 * 1024

_SEG_WIDTHS = (
    ("ka", W_A), ("va", W_A), ("kc", W_C), ("vc", W_C), ("qa", W_A),
    ("qa_p", W_PAD), ("ka_p", W_PAD), ("va_p", W_PAD), ("qc1_p", W_PAD), ("qc2_p", W_PAD),
    ("fa", LANES), ("qb", LANES), ("kb", LANES), ("vb", W_B), ("rb", W_B), ("gb", LANES),
)
_SEG = {}
_off = 0
for _name, _w in _SEG_WIDTHS:
    _SEG[_name] = (_off, _off + _w)
    _off += _w
N_EXT = _off


def _nt_dot(a, b):
    return lax.dot_general(a, b, (((1,), (1,)), ((), ())), preferred_element_type=F32)


def _dot(a, b):
    return jnp.dot(a, b, preferred_element_type=F32)


def _split3(x):
    hi = x.astype(BF16)
    r1 = x - hi.astype(F32)
    mid = r1.astype(BF16)
    lo = (r1 - mid.astype(F32)).astype(BF16)
    return hi, mid, lo


def _dot_exact_rhs(x, m):
    mb = m.astype(BF16)
    hi, mid, lo = _split3(x)
    return _dot(hi, mb) + _dot(mid, mb) + _dot(lo, mb)


def _dot_exact_lhs(m, x):
    mb = m.astype(BF16)
    hi, mid, lo = _split3(x)
    return _dot(mb, hi) + _dot(mb, mid) + _dot(mb, lo)


def _rms(x, g):
    return x * lax.rsqrt(jnp.mean(x * x, axis=-1, keepdims=True) + EPS) * g


def _log_sigmoid(x):
    return jnp.minimum(x, 0.0) - jnp.log1p(jnp.exp(-jnp.abs(x)))


def _const_spec(shape):
    nd = len(shape)
    return pl.BlockSpec(shape, lambda *_: (0,) * nd, pipeline_mode=pl.Buffered(1))


def _params(sem):
    return pltpu.CompilerParams(dimension_semantics=sem, vmem_limit_bytes=VMEM_LIMIT)


def _ada_kernel(c_ref, w_ref, b_ref, o_ref):
    c = c_ref[...]
    a = (c * jax.nn.sigmoid(c)).astype(BF16)
    o_ref[...] = _dot(a, w_ref[...]) + b_ref[...]


def _ada_call(c, w, b):
    m, n = c.shape[0], w.shape[1]
    tn = n // 8
    return pl.pallas_call(
        _ada_kernel,
        grid=(n // tn,),
        in_specs=[pl.BlockSpec((m, D_MODEL), lambda i: (0, 0)),
                  pl.BlockSpec((D_MODEL, tn), lambda i: (0, i)),
                  pl.BlockSpec((1, tn), lambda i: (0, i))],
        out_specs=pl.BlockSpec((m, tn), lambda i: (0, i)),
        out_shape=jax.ShapeDtypeStruct((m, n), F32),
        compiler_params=_params(("parallel",)),
        name="ada",
    )(c, w, b)


FF_CHUNK = 256


def _ffn_kernel(x_ref, shift_ref, scale_ref, gate_ref, g1_ref, g2_ref, win_ref, wout_ref, o_ref, *, res_w):
    x = x_ref[...]
    h = _rms(x, g1_ref[...]) * (1.0 + scale_ref[0]) + shift_ref[0]
    hb = h.astype(BF16)
    y = jnp.zeros(x.shape, F32)
    for c in range(D_FF // FF_CHUNK):
        lo, hi = c * FF_CHUNK, (c + 1) * FF_CHUNK
        g = _dot(hb, win_ref[:, lo:hi])
        u = _dot(hb, win_ref[:, D_FF + lo:D_FF + hi])
        a = (g * jax.nn.sigmoid(g) * u).astype(BF16)
        y = y + _dot(a, wout_ref[lo:hi, :])
    o_ref[...] = x + res_w * gate_ref[0] * _rms(y, g2_ref[...])


def _mod_spec(r, tiles_per_mod):
    return pl.BlockSpec((1, r, D_MODEL), lambda i: (i // tiles_per_mod, 0, 0))


def _ffn_call(x, mod, g1, g2, w_in, w_out, res_w, tm, tiles_per_mod):
    n = x.shape[0]
    shift, scale, gate = mod
    r = shift.shape[1]
    tok = pl.BlockSpec((tm, D_MODEL), lambda i: (i, 0))
    return pl.pallas_call(
        functools.partial(_ffn_kernel, res_w=res_w),
        grid=(n // tm,),
        in_specs=[tok, _mod_spec(r, tiles_per_mod), _mod_spec(r, tiles_per_mod), _mod_spec(r, tiles_per_mod),
                  _const_spec((1, D_MODEL)), _const_spec((1, D_MODEL)),
                  _const_spec((D_MODEL, 2 * D_FF)), _const_spec((D_FF, D_MODEL))],
        out_specs=tok,
        out_shape=jax.ShapeDtypeStruct((n, D_MODEL), F32),
        compiler_params=_params(("parallel",)),
        name="ffn",
    )(x, shift, scale, gate, g1, g2, w_in, w_out)


_PROJ_OUTS = (
    ("ka", W_A, F32), ("va", W_A, F32), ("kc", W_C, F32), ("vc", W_C, F32), ("logf", N_HEADS, F32),
    ("qa", W_A, F32), ("qa_p", W_PAD, BF16), ("ka_p", W_PAD, BF16), ("va_p", W_PAD, BF16),
    ("qc1_p", W_PAD, BF16), ("qc2_p", W_PAD, BF16), ("kc_b", W_C, BF16), ("vc_b", W_C, BF16),
    ("qb", LANES, F32), ("kb", LANES, F32), ("vb", W_B, F32), ("loga", LANES, F32), ("rb", W_B, F32),
)


def _proj_kernel(x_ref, shift_ref, scale_ref, g1_ref, w_ref, bf_ref, wup_ref, bup_ref, vone_ref, *out_refs):
    o = dict(zip([n for n, _, _ in _PROJ_OUTS], out_refs))
    x = x_ref[...]
    hb = (_rms(x, g1_ref[...]) * (1.0 + scale_ref[0]) + shift_ref[0]).astype(BF16)

    def seg(name):
        lo, hi = _SEG[name]
        return _dot(hb, w_ref[:, lo:hi])

    ka = seg("ka")
    o["ka"][...] = ka
    va = seg("va")
    o["va"][...] = va
    kc = seg("kc")
    o["kc"][...] = kc
    o["kc_b"][...] = kc.astype(BF16)
    vc = seg("vc")
    o["vc"][...] = vc
    o["vc_b"][...] = vc.astype(BF16)
    o["qa"][...] = seg("qa") * (HD_A ** -0.5)
    o["qa_p"][...] = (seg("qa_p") * (HD_A ** -0.5 * LOG2E)).astype(BF16)
    o["ka_p"][...] = seg("ka_p").astype(BF16)
    o["va_p"][...] = (seg("va_p") + vone_ref[...]).astype(BF16)
    o["qc1_p"][...] = (seg("qc1_p") * (HD_C ** -0.5 * LOG2E)).astype(BF16)
    o["qc2_p"][...] = (seg("qc2_p") * (HD_C ** -0.5 * LOG2E)).astype(BF16)
    fa = seg("fa") + bf_ref[...]
    o["logf"][...] = _log_sigmoid(fa)[:, :N_HEADS]
    o["qb"][...] = seg("qb")
    o["kb"][...] = seg("kb")
    o["vb"][...] = seg("vb")
    o["rb"][...] = seg("rb")
    gb = seg("gb").astype(BF16)
    o["loga"][...] = _log_sigmoid(_dot(gb, wup_ref[...]) + bup_ref[...]) * (1.0 / GLA_TAU)


def _proj_call(x, mod, g1, w_ext, bf, wup, bup, vone, tm, tiles_per_mod):
    n = x.shape[0]
    shift, scale, _ = mod
    r = shift.shape[1]
    tok = lambda w: pl.BlockSpec((tm, w), lambda i: (i, 0))
    outs = pl.pallas_call(
        _proj_kernel,
        grid=(n // tm,),
        in_specs=[tok(D_MODEL), _mod_spec(r, tiles_per_mod), _mod_spec(r, tiles_per_mod),
                  _const_spec((1, D_MODEL)), _const_spec((D_MODEL, N_EXT)), _const_spec((1, LANES)),
                  _const_spec((LANES, LANES)), _const_spec((1, LANES)), _const_spec((1, W_PAD))],
        out_specs=[tok(w) for _, w, _ in _PROJ_OUTS],
        out_shape=[jax.ShapeDtypeStruct((n, w), dt) for _, w, dt in _PROJ_OUTS],
        compiler_params=_params(("parallel",)),
        name="proj",
    )(x, shift, scale, g1, w_ext, bf, wup, bup, vone)
    return dict(zip([nm for nm, _, _ in _PROJ_OUTS], outs))


def _cumsum_kernel(x_ref, o_ref):
    x = x_ref[0]
    nb = x.shape[0]
    r = lax.broadcasted_iota(jnp.int32, (LANES, LANES), 0)
    c = lax.broadcasted_iota(jnp.int32, (LANES, LANES), 1)
    local = _dot_exact_rhs(x, (r <= c).astype(F32))
    tot = jnp.broadcast_to(local[:, LANES - 1:LANES], (nb, LANES))
    rr = lax.broadcasted_iota(jnp.int32, (nb, nb), 0)
    cc = lax.broadcasted_iota(jnp.int32, (nb, nb), 1)
    o_ref[0] = local + _dot_exact_lhs((cc < rr).astype(F32), tot)


def _cumsum_call(x):
    rows, nb, _ = x.shape
    spec = pl.BlockSpec((1, nb, LANES), lambda i: (i, 0, 0))
    return pl.pallas_call(
        _cumsum_kernel, grid=(rows,), in_specs=[spec], out_specs=spec,
        out_shape=jax.ShapeDtypeStruct(x.shape, F32),
        compiler_params=_params(("parallel",)), name="cumsum",
    )(x)


def _lane_tile(x, width):
    return jnp.concatenate([x] * (width // LANES), axis=1)


def _fox_kernel(q_ref, k_ref, v_ref, c_ref, cref_ref, o_ref, m_sc, acc_sc):
    i = pl.program_id(1)
    j = pl.program_id(2)
    tq = q_ref.shape[1]
    tk = k_ref.shape[1]

    @pl.when(j == 0)
    def _():
        m_sc[...] = jnp.full(m_sc.shape, NEG, F32)
        acc_sc[...] = jnp.zeros(acc_sc.shape, F32)

    def step(diag):
        if diag:
            row = lax.broadcasted_iota(jnp.int32, (tq, tk), 0)
            col = lax.broadcasted_iota(jnp.int32, (tq, tk), 1)
            causal = col <= row
        for h in range(N_HEADS):
            hs = slice(h * HEAD_PAD, (h + 1) * HEAD_PAD)
            t = _nt_dot(q_ref[0, :, hs], k_ref[0, :, hs]) + (cref_ref[0, h][:, 0:1] - c_ref[0, h]) * LOG2E
            if diag:
                t = jnp.where(causal, t, NEG)
            m_old = m_sc[h]
            m_new = jnp.maximum(m_old, jnp.max(t, axis=1, keepdims=True))
            p = jnp.exp2(t - _lane_tile(m_new, tk)).astype(BF16)
            acc_sc[h] = jnp.exp2(m_old - m_new) * acc_sc[h] + _dot(p, v_ref[0, :, hs])
            m_sc[h] = m_new

    @pl.when(j < i)
    def _():
        step(False)

    @pl.when(j == i)
    def _():
        step(True)
        for h in range(N_HEADS):
            acc = acc_sc[h]
            o_ref[0, :, h * HEAD_PAD:(h + 1) * HEAD_PAD] = (acc / acc[:, HD_A:HD_A + 1]).astype(BF16)


def _fox_call(qa_p, ka_p, va_p, c, tq):
    b, t, _ = qa_p.shape
    nq = t // tq
    qspec = pl.BlockSpec((1, tq, W_PAD), lambda b_, i, j: (b_, i, 0))
    kspec = pl.BlockSpec((1, tq, W_PAD), lambda b_, i, j: (b_, jnp.minimum(j, i), 0))
    return pl.pallas_call(
        _fox_kernel,
        grid=(b, nq, nq),
        in_specs=[qspec, kspec, kspec,
                  pl.BlockSpec((1, N_HEADS, 1, tq), lambda b_, i, j: (b_, 0, 0, jnp.minimum(j, i))),
                  pl.BlockSpec((1, N_HEADS, 1, LANES), lambda b_, i, j: (b_, 0, 0, i * (tq // LANES)))],
        out_specs=qspec,
        out_shape=jax.ShapeDtypeStruct((b, t, W_PAD), BF16),
        scratch_shapes=[pltpu.VMEM((N_HEADS, tq, LANES), F32), pltpu.VMEM((N_HEADS, tq, HEAD_PAD), F32)],
        compiler_params=_params(("parallel", "parallel", "arbitrary")),
        name="fox",
    )(qa_p, ka_p, va_p, c, c)


def _diff_lambda(lp, lam_init):
    a = jnp.sum(lp[0:1] * lp[1:2], axis=1, keepdims=True)
    b = jnp.sum(lp[2:3] * lp[3:4], axis=1, keepdims=True)
    return jnp.exp(a) - jnp.exp(b) + lam_init


def _diff_kernel(q1_ref, q2_ref, k_ref, v_ref, slope_ref, lam_ref, g_ref, o_ref, m_sc, l_sc, a_sc, *, lam_init):
    i = pl.program_id(1)
    j = pl.program_id(2)
    tq = q1_ref.shape[1]
    tk = k_ref.shape[1]

    @pl.when(j == 0)
    def _():
        m_sc[...] = jnp.full(m_sc.shape, NEG, F32)
        l_sc[...] = jnp.zeros(l_sc.shape, F32)
        a_sc[...] = jnp.zeros(a_sc.shape, F32)

    def step(diag):
        if diag:
            row = lax.broadcasted_iota(jnp.int32, (tq, tk), 0)
            col = lax.broadcasted_iota(jnp.int32, (tq, tk), 1)
            rel = (2 * jnp.minimum(row, col) - col).astype(F32)
            visible = lax.shift_right_logical(col, 6) <= lax.shift_right_logical(row, 6)
        else:
            rel = (lax.broadcasted_iota(jnp.int32, (1, tk), 1) + (j - i) * tk).astype(F32)
        for h in range(N_HEADS):
            hs = slice(h * HEAD_PAD, (h + 1) * HEAD_PAD)
            bias = (slope_ref[h][:, 0:1] * LOG2E) * rel
            if diag:
                bias = jnp.where(visible, bias, NEG)
            k = k_ref[0, :, hs]
            v = v_ref[0, :, hs]
            for m, q_ref in enumerate((q1_ref, q2_ref)):
                n = 2 * h + m
                t = _nt_dot(q_ref[0, :, hs], k) + bias
                m_old = m_sc[n]
                m_new = jnp.maximum(m_old, jnp.max(t, axis=1, keepdims=True))
                alpha = jnp.exp2(m_old - m_new)
                p = jnp.exp2(t - _lane_tile(m_new, tk))
                l_sc[n] = alpha * l_sc[n] + jnp.sum(p, axis=1, keepdims=True)
                a_sc[n] = alpha * a_sc[n] + _dot(p.astype(BF16), v)
                m_sc[n] = m_new

    @pl.when(j < i)
    def _():
        step(False)

    @pl.when(j == i)
    def _():
        step(True)
        lam = _diff_lambda(lam_ref[...], lam_init)
        for h in range(N_HEADS):
            out = a_sc[2 * h] / l_sc[2 * h] - lam * (a_sc[2 * h + 1] / l_sc[2 * h + 1])
            o_ref[0, :, h * HEAD_PAD:(h + 1) * HEAD_PAD] = (_rms(out, g_ref[...]) * (1.0 - lam_init)).astype(BF16)


def _diff_call(qc1_p, qc2_p, kc_b, vc_b, slopes, lam_p, g_diff, lam_init, tq):
    assert CHUNK == 64 and tq % CHUNK == 0
    b, t, _ = qc1_p.shape
    nq = t // tq
    qspec = pl.BlockSpec((1, tq, W_C), lambda b_, i, j: (b_, i, 0))
    kspec = pl.BlockSpec((1, tq, W_C), lambda b_, i, j: (b_, jnp.minimum(j, i), 0))
    stat = pltpu.VMEM((2 * N_HEADS, tq, LANES), F32)
    return pl.pallas_call(
        functools.partial(_diff_kernel, lam_init=lam_init),
        grid=(b, nq, nq),
        in_specs=[qspec, qspec, kspec, kspec,
                  pl.BlockSpec((N_HEADS, 1, LANES), lambda b_, i, j: (0, 0, 0)),
                  pl.BlockSpec((4, HD_C), lambda b_, i, j: (0, 0)),
                  pl.BlockSpec((1, HEAD_PAD), lambda b_, i, j: (0, 0))],
        out_specs=qspec,
        out_shape=jax.ShapeDtypeStruct((b, t, W_C), BF16),
        scratch_shapes=[stat, stat, pltpu.VMEM((2 * N_HEADS, tq, HEAD_PAD), F32)],
        compiler_params=_params(("parallel", "parallel", "arbitrary")),
        name="diff",
    )(qc1_p, qc2_p, kc_b, vc_b, slopes, lam_p, g_diff)


def _gla_kernel(q_ref, k_ref, v_ref, la_ref, r_ref, s0_ref, g_ref, o_ref, s_ref, *, chunk):
    ti = pl.program_id(1)
    n_chunks = q_ref.shape[1] // chunk

    @pl.when(ti == 0)
    def _():
        s_ref[0] = s0_ref[0]

    tri_r = lax.broadcasted_iota(jnp.int32, (chunk, chunk), 0)
    tri_c = lax.broadcasted_iota(jnp.int32, (chunk, chunk), 1)
    incl = (tri_c <= tri_r).astype(F32)
    srow = lax.broadcasted_iota(jnp.int32, (N_HEADS * chunk, LANES), 0) // chunk
    slane = lax.broadcasted_iota(jnp.int32, (N_HEADS * chunk, LANES), 1) // DK_B
    qmask = srow == slane
    arow = lax.broadcasted_iota(jnp.int32, (N_HEADS * chunk, chunk), 0) % chunk
    acol = lax.broadcasted_iota(jnp.int32, (N_HEADS * chunk, chunk), 1)
    causal = acol <= arow
    vlane = lax.broadcasted_iota(jnp.int32, (chunk, W_B), 1) // DV_B
    st_row = lax.broadcasted_iota(jnp.int32, (W_B, LANES), 0) // DV_B
    st_lane = lax.broadcasted_iota(jnp.int32, (W_B, LANES), 1) // DK_B
    diag_blocks = st_row == st_lane
    gr = lax.broadcasted_iota(jnp.int32, (W_B, W_B), 0) // DV_B
    gc = lax.broadcasted_iota(jnp.int32, (W_B, W_B), 1) // DV_B
    group_mean = jnp.where(gr == gc, 1.0 / DV_B, 0.0)

    for c in range(n_chunks):
        sl = slice(c * chunk, (c + 1) * chunk)
        cb = _dot_exact_lhs(incl, la_ref[0, sl, :])
        cb_last = cb[chunk - 1:chunk, :]
        q = q_ref[0, sl, :] * (DK_B ** -0.5)
        k = k_ref[0, sl, :]
        v = v_ref[0, sl, :]
        vb = v.astype(BF16)
        qe = q * jnp.exp(cb)
        ke = (k * jnp.exp(-cb)).astype(BF16)
        kl = (k * jnp.exp(cb_last - cb)).astype(BF16)
        decay = jnp.exp(cb_last)
        qstack = jnp.where(qmask, jnp.concatenate([qe] * N_HEADS, axis=0), 0.0).astype(BF16)
        a = jnp.where(causal, _nt_dot(qstack, ke), 0.0).astype(BF16)
        oi = _dot(a, vb)
        o = jnp.zeros((chunk, W_B), F32)
        for h in range(N_HEADS):
            o = o + jnp.where(vlane == h, oi[h * chunk:(h + 1) * chunk], 0.0)
        st = s_ref[0]
        o = o + _nt_dot(qe.astype(BF16), st.astype(BF16))
        upd = _dot(v.T.astype(BF16), kl)
        s_ref[0] = decay * st + jnp.where(diag_blocks, upd, 0.0)
        ms = _dot_exact_rhs(o * o, group_mean)
        r = r_ref[0, sl, :]
        o_ref[0, sl, :] = (o * lax.rsqrt(ms + EPS) * g_ref[...] * (r * jax.nn.sigmoid(r))).astype(BF16)


def _gla_call(qb, kb, vb, loga, rb, s0_t, g_gla, tg, chunk):
    b, t, _ = qb.shape
    tok = lambda w: pl.BlockSpec((1, tg, w), lambda b_, i: (b_, i, 0))
    sspec = pl.BlockSpec((1, W_B, LANES), lambda b_, i: (b_, 0, 0))
    return pl.pallas_call(
        functools.partial(_gla_kernel, chunk=chunk),
        grid=(b, t // tg),
        in_specs=[tok(LANES), tok(LANES), tok(W_B), tok(LANES), tok(W_B), sspec,
                  pl.BlockSpec((1, W_B), lambda b_, i: (0, 0))],
        out_specs=[tok(W_B), sspec],
        out_shape=[jax.ShapeDtypeStruct((b, t, W_B), BF16), jax.ShapeDtypeStruct((b, W_B, LANES), F32)],
        compiler_params=_params(("parallel", "arbitrary")),
        name="gla",
    )(qb, kb, vb, loga, rb, s0_t, g_gla)


def _softmax_two(t_past, t_new, exp_fn):
    m = jnp.maximum(jnp.max(t_past, axis=1, keepdims=True), jnp.max(t_new, axis=1, keepdims=True))
    p_past = exp_fn(t_past - m)
    p_new = exp_fn(t_new - m)
    l = jnp.sum(p_past, axis=1, keepdims=True) + jnp.sum(p_new, axis=1, keepdims=True)
    return p_past, p_new, l


def _fox_s_kernel(q_ref, kn_ref, vn_ref, kp_ref, vp_ref, c_ref, o_ref):
    t, past = q_ref.shape[1], kp_ref.shape[1]
    q = q_ref[0]
    kn = kn_ref[0].astype(BF16)
    vn = vn_ref[0].astype(BF16)
    kp = kp_ref[0].astype(BF16)
    vp = vp_ref[0].astype(BF16)
    lane_head = lax.broadcasted_iota(jnp.int32, (t, W_A), 1) // HD_A
    row = lax.broadcasted_iota(jnp.int32, (t, t), 0)
    col = lax.broadcasted_iota(jnp.int32, (t, t), 1)
    o = jnp.zeros((t, W_A), F32)
    for h in range(N_HEADS):
        qh = jnp.where(lane_head == h, q, 0.0).astype(BF16)
        ch = c_ref[0, h:h + 1, :]
        cref = ch[:, past:past + 1]
        t_past = _nt_dot(qh, kp) + (cref - ch[:, :past])
        t_new = jnp.where(col <= row, _nt_dot(qh, kn) + (cref - ch[:, past:past + t]), NEG)
        p_past, p_new, l = _softmax_two(t_past, t_new, jnp.exp)
        oh =(_dot(p_past.astype(BF16), vp) + _dot(p_new.astype(BF16), vn)) / l
        o = o + jnp.where(lane_head == h, oh, 0.0)
    o_ref[0] = o.astype(BF16)


def _fox_s_call(qa, ka, va, cache_k, cache_v, c):
    b, t, _ = qa.shape
    past = cache_k.shape[1]
    new = lambda: pl.BlockSpec((1, t, W_A), lambda i: (i, 0, 0))
    old = lambda: pl.BlockSpec((1, past, W_A), lambda i: (i, 0, 0))
    return pl.pallas_call(
        _fox_s_kernel,
        grid=(b,),
        in_specs=[new(), new(), new(), old(), old(),
                  pl.BlockSpec((1, N_HEADS, c.shape[2]), lambda i: (i, 0, 0))],
        out_specs=new(),
        out_shape=jax.ShapeDtypeStruct((b, t, W_A), BF16),
        compiler_params=_params(("parallel",)),
        name="fox_s",
    )(qa, ka, va, cache_k, cache_v, c)


def _diff_s_kernel(q1_ref, q2_ref, kn_ref, vn_ref, kp_ref, vp_ref, slope_ref, lam_ref, g_ref, o_ref, *, lam_init):
    t, past = q1_ref.shape[1], kp_ref.shape[1]
    lam = _diff_lambda(lam_ref[...], lam_init)
    row = lax.broadcasted_iota(jnp.int32, (t, t), 0) + past
    col = lax.broadcasted_iota(jnp.int32, (t, t), 1) + past
    visible_new = lax.shift_right_logical(col, 6) <= lax.shift_right_logical(row, 6)
    dist_new = jnp.abs(row - col).astype(F32)
    prow = lax.broadcasted_iota(jnp.int32, (t, past), 0) + past
    pcol = lax.broadcasted_iota(jnp.int32, (t, past), 1)
    visible_past = lax.shift_right_logical(pcol, 6) <= lax.shift_right_logical(prow, 6)
    dist_past = (prow - pcol).astype(F32)
    for h in range(N_HEADS):
        hs = slice(h * HEAD_PAD, (h + 1) * HEAD_PAD)
        slope = slope_ref[h][:, 0:1] * LOG2E
        kn = kn_ref[0, :, hs].astype(BF16)
        vn = vn_ref[0, :, hs].astype(BF16)
        kp = kp_ref[0, :, hs].astype(BF16)
        vp = vp_ref[0, :, hs].astype(BF16)
        outs = []
        for q_ref in (q1_ref, q2_ref):
            q = q_ref[0, :, hs]
            t_past = jnp.where(visible_past, _nt_dot(q, kp) - slope * dist_past, NEG)
            t_new = jnp.where(visible_new, _nt_dot(q, kn) - slope * dist_new, NEG)
            p_past, p_new, l = _softmax_two(t_past, t_new, jnp.exp2)
            outs.append((_dot(p_past.astype(BF16), vp) + _dot(p_new.astype(BF16), vn)) / l)
        out = outs[0] - lam * outs[1]
        o_ref[0, :, hs] = (_rms(out, g_ref[...]) * (1.0 - lam_init)).astype(BF16)


def _diff_s_call(qc1_p, qc2_p, kc, vc, cache_k, cache_v, slopes, lam_p, g_diff, lam_init):
    assert CHUNK == 64
    b, t, _ = qc1_p.shape
    past = cache_k.shape[1]
    new = lambda: pl.BlockSpec((1, t, W_C), lambda i: (i, 0, 0))
    old = lambda: pl.BlockSpec((1, past, W_C), lambda i: (i, 0, 0))
    return pl.pallas_call(
        functools.partial(_diff_s_kernel, lam_init=lam_init),
        grid=(b,),
        in_specs=[new(), new(), new(), new(), old(), old(),
                  pl.BlockSpec((N_HEADS, 1, LANES), lambda i: (0, 0, 0)),
                  pl.BlockSpec((4, HD_C), lambda i: (0, 0)),
                  pl.BlockSpec((1, HEAD_PAD), lambda i: (0, 0))],
        out_specs=new(),
        out_shape=jax.ShapeDtypeStruct((b, t, W_C), BF16),
        compiler_params=_params(("parallel",)),
        name="diff_s",
    )(qc1_p, qc2_p, kc, vc, cache_k, cache_v, slopes, lam_p, g_diff)


def _out_kernel(x_ref, oa_ref, gla_ref, dif_ref, gate_ref, g2_ref, woa_ref, wog_ref, wod_ref, o_ref):
    y = _dot(oa_ref[...], woa_ref[...]) + _dot(gla_ref[...], wog_ref[...]) + _dot(dif_ref[...], wod_ref[...])
    o_ref[...] = x_ref[...] + gate_ref[0] * _rms(y, g2_ref[...])


def _out_call(x, oa, gla, dif, mod, g2, w_oa, w_og, w_od, tm, tiles_per_mod):
    n = x.shape[0]
    gate = mod[2]
    r = gate.shape[1]
    tok = lambda w: pl.BlockSpec((tm, w), lambda i: (i, 0))
    return pl.pallas_call(
        _out_kernel,
        grid=(n // tm,),
        in_specs=[tok(D_MODEL), tok(oa.shape[1]), tok(W_B), tok(W_C), _mod_spec(r, tiles_per_mod),
                  _const_spec((1, D_MODEL)), _const_spec(w_oa.shape), _const_spec(w_og.shape),
                  _const_spec(w_od.shape)],
        out_specs=tok(D_MODEL),
        out_shape=jax.ShapeDtypeStruct((n, D_MODEL), F32),
        compiler_params=_params(("parallel",)),
        name="out_proj",
    )(x, oa, gla, dif, gate, g2, w_oa, w_og, w_od)


def _pad_heads(w, hd, off=0):
    rows = w.shape[0]
    out = jnp.zeros((rows, N_HEADS, HEAD_PAD), w.dtype)
    out = out.at[:, :, off:off + hd].set(w.reshape(rows, N_HEADS, hd))
    return out.reshape(rows, W_PAD)


def _pad_lanes(w, width=LANES):
    return jnp.pad(w, ((0, 0), (0, width - w.shape[1])))


def _layer_params(l, w_ffn_in, w_ffn_out, w_in, b_f, w_gla_up, b_gla_up, g_gla, g_diff, w_out):
    offs = [int(o) for o in np.cumsum(PROJ_SIZES)[:-1]]
    qa, ka, va, fa, qb, kb, vb, gb, rb, qc, kc, vc = jnp.split(w_in[l], offs, axis=1)
    qc = qc.reshape(D_MODEL, N_HEADS, 2, HD_C)
    segs = {
        "ka": ka, "va": va, "kc": kc, "vc": vc, "qa": qa,
        "qa_p": _pad_heads(qa, HD_A), "ka_p": _pad_heads(ka, HD_A), "va_p": _pad_heads(va, HD_A),
        "qc1_p": _pad_heads(qc[:, :, 0].reshape(D_MODEL, -1), HD_C),
        "qc2_p": _pad_heads(qc[:, :, 1].reshape(D_MODEL, -1), HD_C, off=HD_C),
        "fa": _pad_lanes(fa), "qb": qb, "kb": kb, "vb": vb, "rb": rb, "gb": _pad_lanes(gb),
    }
    w_ext = jnp.concatenate([segs[n] for n, _ in _SEG_WIDTHS], axis=1).astype(BF16)
    wo = w_out[l]
    vone = np.zeros((1, W_PAD), np.float32)
    vone[0, HD_A::HEAD_PAD] = 1.0
    return dict(
        w_ffn_in=w_ffn_in[l].astype(BF16), w_ffn_out=w_ffn_out[l].astype(BF16), w_ext=w_ext,
        bf=_pad_lanes(b_f[l][None, :]),
        wup=jnp.pad(w_gla_up[l], ((0, LANES - GLA_RANK), (0, 0))).astype(BF16),
        bup=b_gla_up[l][None, :],
        vone=jnp.asarray(vone),
        g_gla=jnp.tile(g_gla[l], N_HEADS)[None, :], g_diff=g_diff[l][None, :],
        w_oa=wo[:W_A].astype(BF16),
        w_oa_p=_pad_heads(wo[:W_A].T, HD_A).T.astype(BF16),
        w_og=wo[W_A:W_A + W_B].astype(BF16), w_od=wo[W_A + W_B:].astype(BF16),
    )


def _state_to_blockdiag_t(s):
    eye = jnp.eye(N_HEADS, dtype=s.dtype)
    return jnp.einsum("bhkv,hg->bhvgk", s, eye).reshape(s.shape[0], W_B, N_HEADS * DK_B)


def _blockdiag_t_to_state(st):
    b = st.shape[0]
    s5 = st.reshape(b, N_HEADS, DV_B, N_HEADS, DK_B)
    diag = jnp.stack([s5[:, h, :, h, :] for h in range(N_HEADS)], axis=1)
    return jnp.swapaxes(diag, 2, 3)


def _tile(n, pref):
    t = min(n, pref)
    assert n % t == 0, (n, t)
    return t


def _trunk(x, mods, params, g_norm, lam_params, mixer, tm, tiles_per_mod):
    b, t, _ = x.shape
    n = b * t
    xf = x.reshape(n, D_MODEL)
    states = []
    for l in range(len(params)):
        p = params[l]
        gn = lambda k: g_norm[l, k][None, :]
        xf = _ffn_call(xf, mods[l][0], gn(0), gn(1), p["w_ffn_in"][0], p["w_ffn_out"][0], 0.5, tm, tiles_per_mod)
        pr = _proj_call(xf, mods[l][1], gn(2), p["w_ext"], p["bf"], p["wup"], p["bup"], p["vone"], tm, tiles_per_mod)
        lam_init = 0.8 - 0.6 * math.exp(-0.3 * l)
        oa, gla, dif, s_fin, w_oa = mixer(l, pr, p, lam_params[l], lam_init, b, t)
        xf = _out_call(xf, oa, gla, dif, mods[l][1], gn(3), w_oa, p["w_og"], p["w_od"], tm, tiles_per_mod)
        xf = _ffn_call(xf, mods[l][2], gn(4), gn(5), p["w_ffn_in"][1], p["w_ffn_out"][1], 0.5, tm, tiles_per_mod)
        shp = lambda a, *tail: a.reshape((b, t) + tail)
        states.append((shp(pr["ka"], N_HEADS, HD_A), shp(pr["va"], N_HEADS, HD_A), shp(pr["logf"], N_HEADS), s_fin,
                       shp(pr["kc"], N_HEADS, 2 * HD_C), shp(pr["vc"], N_HEADS, 2 * HD_C)))
    stacked = [jnp.stack([st[i] for st in states]) for i in range(6)]
    return xf.reshape(b, t, D_MODEL), stacked


def kernel(x_prompt, x_sample, c_prompt, c_sample, cache_fox_k, cache_fox_v, cache_fox_logf, state_gla,
           cache_diff_k, cache_diff_v, w_ada, b_ada, g_norm, w_ffn_in, w_ffn_out, w_in, b_f, w_gla_up,
           b_gla_up, g_gla, g_diff, lam_params, w_out):
    depth = w_in.shape[0]
    bp, tp, _ = x_prompt.shape
    bs, ts, _ = x_sample.shape
    past = cache_fox_k.shape[2]

    params = [_layer_params(l, w_ffn_in, w_ffn_out, w_in, b_f, w_gla_up, b_gla_up, g_gla, g_diff, w_out)
              for l in range(depth)]

    slopes = 2.0 ** (-8.0 * jnp.arange(1, N_HEADS + 1, dtype=F32) / N_HEADS)
    slopes = jnp.broadcast_to(slopes[:, None, None], (N_HEADS, 1, LANES))

    rows = bp + bs
    rows_pad = -(-rows // 8) * 8
    c_all = jnp.pad(jnp.concatenate([c_prompt, c_sample], axis=0), ((0, rows_pad - rows), (0, 0)))
    mods_p, mods_s = [], []
    for l in range(depth):
        mod = _ada_call(c_all, w_ada[l].astype(BF16), b_ada[l][None, :]).reshape(rows_pad, N_SUB, 3, D_MODEL)
        mods_p.append([[mod[:bp, s, k][:, None, :] for k in range(3)] for s in range(N_SUB)])
        mods_s.append([[jnp.repeat(mod[bp:rows, s, k], ts, axis=0)[None] for k in range(3)] for s in range(N_SUB)])

    tm_p = _tile(tp, 512)
    tq = _tile(tp, 512)
    tg = _tile(tp, 512)

    def prompt_mixer(l, pr, p, lam_p, lam_init, b, t):
        r3 = lambda a: a.reshape(b, t, a.shape[-1])
        logf_t = jnp.swapaxes(r3(pr["logf"]), 1, 2).reshape(b * N_HEADS, t // LANES, LANES)
        c = _cumsum_call(logf_t).reshape(b, N_HEADS, 1, t)
        oa = _fox_call(r3(pr["qa_p"]), r3(pr["ka_p"]), r3(pr["va_p"]), c, tq)
        s0_t = jnp.zeros((b, W_B, N_HEADS * DK_B), F32)
        gla, s_t = _gla_call(r3(pr["qb"]), r3(pr["kb"]), r3(pr["vb"]), r3(pr["loga"]), r3(pr["rb"]), s0_t,
                             p["g_gla"], tg, min(CHUNK, t))
        dif = _diff_call(r3(pr["qc1_p"]), r3(pr["qc2_p"]), r3(pr["kc_b"]), r3(pr["vc_b"]), slopes, lam_p,
                         p["g_diff"], lam_init, tq)
        flat = lambda a: a.reshape(b * t, a.shape[-1])
        return flat(oa), flat(gla), flat(dif), _blockdiag_t_to_state(s_t), p["w_oa_p"]

    def sample_mixer(l, pr, p, lam_p, lam_init, b, t):
        r3 = lambda a: a.reshape(b, t, a.shape[-1])
        total = past + t
        nb = -(-total // (8 * LANES)) * 8
        lf = jnp.concatenate([cache_fox_logf[l].astype(F32), r3(pr["logf"])], axis=1)
        lf = jnp.pad(jnp.swapaxes(lf, 1, 2), ((0, 0), (0, 0), (0, nb * LANES - total)))
        c = _cumsum_call(lf.reshape(b * N_HEADS, nb, LANES)).reshape(b, N_HEADS, nb * LANES)
        oa = _fox_s_call(r3(pr["qa"]), r3(pr["ka"]), r3(pr["va"]),
                         cache_fox_k[l].reshape(b, past, W_A), cache_fox_v[l].reshape(b, past, W_A), c)
        gla, s_t = _gla_call(r3(pr["qb"]), r3(pr["kb"]), r3(pr["vb"]), r3(pr["loga"]), r3(pr["rb"]),
                             _state_to_blockdiag_t(state_gla[l].astype(F32)), p["g_gla"], t, min(CHUNK, t))
        dif = _diff_s_call(r3(pr["qc1_p"]), r3(pr["qc2_p"]), r3(pr["kc"]), r3(pr["vc"]),
                           cache_diff_k[l].reshape(b, past, W_C), cache_diff_v[l].reshape(b, past, W_C),
                           slopes, lam_p, p["g_diff"], lam_init)
        flat = lambda a: a.reshape(b * t, a.shape[-1])
        return flat(oa), flat(gla), flat(dif), _blockdiag_t_to_state(s_t), p["w_oa"]

    y_p, sp = _trunk(x_prompt, mods_p, params, g_norm, lam_params, prompt_mixer, tm_p, tp // tm_p)
    y_s, ss = _trunk(x_sample, mods_s, params, g_norm, lam_params, sample_mixer, bs * ts, 1)
    return (y_p, y_s, sp[0], sp[1], sp[2], sp[3], sp[4], sp[5], ss[0], ss[1], ss[2], ss[3], ss[4], ss[5])
```

```python
import functools
import math

import jax
import jax.numpy as jnp
import numpy as np
from jax import lax
from jax.experimental import pallas as pl
from jax.experimental.pallas import tpu as pltpu

F32 = jnp.float32
BF16 = jnp.bfloat16

D_MODEL = 1024
N_HEADS = 4
HD_A = 64
DK_B = 32
DV_B = 64
GLA_RANK = 16
GLA_TAU = 16.0
HD_C = 64
CHUNK = 64
W_A = N_HEADS * HD_A
W_B = N_HEADS * DV_B
W_C = N_HEADS * 2 * HD_C
D_FF = ((8 * D_MODEL // 3 + 255) // 256) * 256
N_SUB = 3
PROJ_SIZES = (W_A, W_A, W_A, N_HEADS, N_HEADS * DK_B, N_HEADS * DK_B, W_B, GLA_RANK, W_B, W_C, W_C, W_C)
NEG = -1e30
EPS = 1e-6
LOG2E = math.log2(math.e)

LANES = 128
HEAD_PAD = 128
W_PAD = N_HEADS * HEAD_PAD
VMEM_LIMIT = 56 * 1024 * 1024

_SEG_WIDTHS = (
    ("ka", W_A), ("va", W_A), ("kc", W_C), ("vc", W_C), ("qa", W_A),
    ("qa_p", W_PAD), ("ka_p", W_PAD), ("va_p", W_PAD), ("qc1_p", W_PAD), ("qc2_p", W_PAD),
    ("fa", LANES), ("qb", LANES), ("kb", LANES), ("vb", W_B), ("rb", W_B), ("gb", LANES),
)
_SEG = {}
_off = 0
for _name, _w in _SEG_WIDTHS:
    _SEG[_name] = (_off, _off + _w)
    _off += _w
N_EXT = _off


def _nt_dot(a, b):
    return lax.dot_general(a, b, (((1,), (1,)), ((), ())), preferred_element_type=F32)


def _dot(a, b):
    return jnp.dot(a, b, preferred_element_type=F32)


def _split3(x):
    hi = x.astype(BF16)
    r1 = x - hi.astype(F32)
    mid = r1.astype(BF16)
    lo = (r1 - mid.astype(F32)).astype(BF16)
    return hi, mid, lo


def _dot_exact_rhs(x, m):
    mb = m.astype(BF16)
    hi, mid, lo = _split3(x)
    return _dot(hi, mb) + _dot(mid, mb) + _dot(lo, mb)


def _dot_exact_lhs(m, x):
    mb = m.astype(BF16)
    hi, mid, lo = _split3(x)
    return _dot(mb, hi) + _dot(mb, mid) + _dot(mb, lo)


def _rms(x, g):
    return x * lax.rsqrt(jnp.mean(x * x, axis=-1, keepdims=True) + EPS) * g


def _log_sigmoid(x):
    return jnp.minimum(x, 0.0) - jnp.log1p(jnp.exp(-jnp.abs(x)))


def _const_spec(shape):
    nd = len(shape)
    return pl.BlockSpec(shape, lambda *_: (0,) * nd, pipeline_mode=pl.Buffered(1))


def _params(sem):
    return pltpu.CompilerParams(dimension_semantics=sem, vmem_limit_bytes=VMEM_LIMIT)


def _ada_kernel(c_ref, w_ref, b_ref, o_ref):
    c = c_ref[...]
    a = (c * jax.nn.sigmoid(c)).astype(BF16)
    o_ref[...] = _dot(a, w_ref[...]) + b_ref[...]


def _ada_call(c, w, b):
    m, n = c.shape[0], w.shape[1]
    tn = n // 8
    return pl.pallas_call(
        _ada_kernel,
        grid=(n // tn,),
        in_specs=[pl.BlockSpec((m, D_MODEL), lambda i: (0, 0)),
                  pl.BlockSpec((D_MODEL, tn), lambda i: (0, i)),
                  pl.BlockSpec((1, tn), lambda i: (0, i))],
        out_specs=pl.BlockSpec((m, tn), lambda i: (0, i)),
        out_shape=jax.ShapeDtypeStruct((m, n), F32),
        compiler_params=_params(("parallel",)),
        name="ada",
    )(c, w, b)


FF_CHUNK = 256


def _ffn_kernel(x_ref, shift_ref, scale_ref, gate_ref, g1_ref, g2_ref, win_ref, wout_ref, o_ref, *, res_w):
    x = x_ref[...]
    h = _rms(x, g1_ref[...]) * (1.0 + scale_ref[0]) + shift_ref[0]
    hb = h.astype(BF16)
    y = jnp.zeros(x.shape, F32)
    for c in range(D_FF // FF_CHUNK):
        lo, hi = c * FF_CHUNK, (c + 1) * FF_CHUNK
        g = _dot(hb, win_ref[:, lo:hi])
        u = _dot(hb, win_ref[:, D_FF + lo:D_FF + hi])
        a = (g * jax.nn.sigmoid(g) * u).astype(BF16)
        y = y + _dot(a, wout_ref[lo:hi, :])
    o_ref[...] = x + res_w * gate_ref[0] * _rms(y, g2_ref[...])


def _mod_spec(r, tiles_per_mod):
    return pl.BlockSpec((1, r, D_MODEL), lambda i: (i // tiles_per_mod, 0, 0))


def _ffn_call(x, mod, g1, g2, w_in, w_out, res_w, tm, tiles_per_mod):
    n = x.shape[0]
    shift, scale, gate = mod
    r = shift.shape[1]
    tok = pl.BlockSpec((tm, D_MODEL), lambda i: (i, 0))
    return pl.pallas_call(
        functools.partial(_ffn_kernel, res_w=res_w),
        grid=(n // tm,),
        in_specs=[tok, _mod_spec(r, tiles_per_mod), _mod_spec(r, tiles_per_mod), _mod_spec(r, tiles_per_mod),
                  _const_spec((1, D_MODEL)), _const_spec((1, D_MODEL)),
                  _const_spec((D_MODEL, 2 * D_FF)), _const_spec((D_FF, D_MODEL))],
        out_specs=tok,
        out_shape=jax.ShapeDtypeStruct((n, D_MODEL), F32),
        compiler_params=_params(("parallel",)),
        name="ffn",
    )(x, shift, scale, gate, g1, g2, w_in, w_out)


_PROJ_OUTS = (
    ("ka", W_A, F32), ("va", W_A, F32), ("kc", W_C, F32), ("vc", W_C, F32), ("logf", N_HEADS, F32),
    ("qa", W_A, F32), ("qa_p", W_PAD, BF16), ("ka_p", W_PAD, BF16), ("va_p", W_PAD, BF16),
    ("qc1_p", W_PAD, BF16), ("qc2_p", W_PAD, BF16), ("kc_b", W_C, BF16), ("vc_b", W_C, BF16),
    ("qb", LANES, F32), ("kb", LANES, F32), ("vb", W_B, F32), ("loga", LANES, F32), ("rb", W_B, F32),
)


def _proj_kernel(x_ref, shift_ref, scale_ref, g1_ref, w_ref, bf_ref, wup_ref, bup_ref, vone_ref, *out_refs):
    o = dict(zip([n for n, _, _ in _PROJ_OUTS], out_refs))
    x = x_ref[...]
    hb = (_rms(x, g1_ref[...]) * (1.0 + scale_ref[0]) + shift_ref[0]).astype(BF16)

    def seg(name):
        lo, hi = _SEG[name]
        return _dot(hb, w_ref[:, lo:hi])

    ka = seg("ka")
    o["ka"][...] = ka
    va = seg("va")
    o["va"][...] = va
    kc = seg("kc")
    o["kc"][...] = kc
    o["kc_b"][...] = kc.astype(BF16)
    vc = seg("vc")
    o["vc"][...] = vc
    o["vc_b"][...] = vc.astype(BF16)
    o["qa"][...] = seg("qa") * (HD_A ** -0.5)
    o["qa_p"][...] = (seg("qa_p") * (HD_A ** -0.5 * LOG2E)).astype(BF16)
    o["ka_p"][...] = seg("ka_p").astype(BF16)
    o["va_p"][...] = (seg("va_p") + vone_ref[...]).astype(BF16)
    o["qc1_p"][...] = (seg("qc1_p") * (HD_C ** -0.5 * LOG2E)).astype(BF16)
    o["qc2_p"][...] = (seg("qc2_p") * (HD_C ** -0.5 * LOG2E)).astype(BF16)
    fa = seg("fa") + bf_ref[...]
    o["logf"][...] = _log_sigmoid(fa)[:, :N_HEADS]
    o["qb"][...] = seg("qb")
    o["kb"][...] = seg("kb")
    o["vb"][...] = seg("vb")
    o["rb"][...] = seg("rb")
    gb = seg("gb").astype(BF16)
    o["loga"][...] = _log_sigmoid(_dot(gb, wup_ref[...]) + bup_ref[...]) * (1.0 / GLA_TAU)


def _proj_call(x, mod, g1, w_ext, bf, wup, bup, vone, tm, tiles_per_mod):
    n = x.shape[0]
    shift, scale, _ = mod
    r = shift.shape[1]
    tok = lambda w: pl.BlockSpec((tm, w), lambda i: (i, 0))
    outs = pl.pallas_call(
        _proj_kernel,
        grid=(n // tm,),
        in_specs=[tok(D_MODEL), _mod_spec(r, tiles_per_mod), _mod_spec(r, tiles_per_mod),
                  _const_spec((1, D_MODEL)), _const_spec((D_MODEL, N_EXT)), _const_spec((1, LANES)),
                  _const_spec((LANES, LANES)), _const_spec((1, LANES)), _const_spec((1, W_PAD))],
        out_specs=[tok(w) for _, w, _ in _PROJ_OUTS],
        out_shape=[jax.ShapeDtypeStruct((n, w), dt) for _, w, dt in _PROJ_OUTS],
        compiler_params=_params(("parallel",)),
        name="proj",
    )(x, shift, scale, g1, w_ext, bf, wup, bup, vone)
    return dict(zip([nm for nm, _, _ in _PROJ_OUTS], outs))


def _cumsum_kernel(x_ref, o_ref):
    x = x_ref[0]
    nb = x.shape[0]
    r = lax.broadcasted_iota(jnp.int32, (LANES, LANES), 0)
    c = lax.broadcasted_iota(jnp.int32, (LANES, LANES), 1)
    local = _dot_exact_rhs(x, (r <= c).astype(F32))
    tot = jnp.broadcast_to(local[:, LANES - 1:LANES], (nb, LANES))
    rr = lax.broadcasted_iota(jnp.int32, (nb, nb), 0)
    cc = lax.broadcasted_iota(jnp.int32, (nb, nb), 1)
    o_ref[0] = local + _dot_exact_lhs((cc < rr).astype(F32), tot)


def _cumsum_call(x):
    rows, nb, _ = x.shape
    spec = pl.BlockSpec((1, nb, LANES), lambda i: (i, 0, 0))
    return pl.pallas_call(
        _cumsum_kernel, grid=(rows,), in_specs=[spec], out_specs=spec,
        out_shape=jax.ShapeDtypeStruct(x.shape, F32),
        compiler_params=_params(("parallel",)), name="cumsum",
    )(x)


def _lane_tile(x, width):
    return jnp.concatenate([x] * (width // LANES), axis=1)


GROUP_HEADS = 2
W_GROUP = GROUP_HEADS * HEAD_PAD


def _causal_sweep(i, tq, tk, tiles_per_iter, tile):
    r = tq // tk

    def body(jj, carry):
        for u in range(tiles_per_iter):
            tile(jj * tiles_per_iter + u, slice(0, tq), None)
        return carry

    assert r % tiles_per_iter == 0
    lax.fori_loop(0, (i * r) // tiles_per_iter, body, 0)
    for d in range(r):
        tile(i * r + d, slice(d * tk, (d + 1) * tk), d * tk)
        if d + 1 < r:
            tile(i * r + d, slice((d + 1) * tk, tq), None)


def _fox_kernel(q_ref, k_ref, v_ref, c_ref, o_ref, m_sc, acc_sc, *, tk, tiles_per_iter):
    i = pl.program_id(2)
    tq = q_ref.shape[1]
    m_sc[...] = jnp.full(m_sc.shape, NEG, F32)
    acc_sc[...] = jnp.zeros(acc_sc.shape, F32)
    row = lax.broadcasted_iota(jnp.int32, (tk, tk), 0)
    col = lax.broadcasted_iota(jnp.int32, (tk, tk), 1)
    causal = col <= row

    def tile(j, rows, local):
        ks = pl.ds(pl.multiple_of(j * tk, tk), tk)
        for h in range(GROUP_HEADS):
            hs = slice(h * HEAD_PAD, (h + 1) * HEAD_PAD)
            c_first = c_ref[0, h, pl.ds(i * (tq // tk), 1), :][:, 0:1]
            t = _nt_dot(q_ref[0, rows, hs], k_ref[0, ks, hs]) + (c_first - c_ref[0, h, pl.ds(j, 1), :]) * LOG2E
            if local is not None:
                t = jnp.where(causal, t, NEG)
            m_old = m_sc[h, rows, :]
            m_new = jnp.maximum(m_old, jnp.max(t, axis=1, keepdims=True))
            p = jnp.exp2(t - _lane_tile(m_new, tk)).astype(BF16)
            acc_sc[h, rows, :] = jnp.exp2(m_old - m_new) * acc_sc[h, rows, :] + _dot(p, v_ref[0, ks, hs])
            m_sc[h, rows, :] = m_new

    _causal_sweep(i, tq, tk, tiles_per_iter, tile)
    for h in range(GROUP_HEADS):
        acc = acc_sc[h]
        o_ref[0, :, h * HEAD_PAD:(h + 1) * HEAD_PAD] = (acc / acc[:, HD_A:HD_A + 1]).astype(BF16)


def _resident_spec(t):
    return pl.BlockSpec((1, t, W_GROUP), lambda b_, g, i: (b_, 0, g), pipeline_mode=pl.Buffered(1))


def _fox_call(qa_p, ka_p, va_p, c, tq, tk):
    b, t, _ = qa_p.shape
    qspec = pl.BlockSpec((1, tq, W_GROUP), lambda b_, g, i: (b_, i, g))
    return pl.pallas_call(
        functools.partial(_fox_kernel, tk=tk, tiles_per_iter=tq // tk),
        grid=(b, N_HEADS // GROUP_HEADS, t // tq),
        in_specs=[qspec, _resident_spec(t), _resident_spec(t),
                  pl.BlockSpec((1, GROUP_HEADS, t // tk, tk), lambda b_, g, i: (b_, g, 0, 0))],
        out_specs=qspec,
        out_shape=jax.ShapeDtypeStruct((b, t, W_PAD), BF16),
        scratch_shapes=[pltpu.VMEM((GROUP_HEADS, tq, LANES), F32), pltpu.VMEM((GROUP_HEADS, tq, HEAD_PAD), F32)],
        compiler_params=_params(("parallel", "parallel", "arbitrary")),
        name="fox",
    )(qa_p, ka_p, va_p, c.reshape(b, N_HEADS, t // tk, tk))


def _diff_lambda(lp, lam_init):
    a = jnp.sum(lp[0:1] * lp[1:2], axis=1, keepdims=True)
    b = jnp.sum(lp[2:3] * lp[3:4], axis=1, keepdims=True)
    return jnp.exp(a) - jnp.exp(b) + lam_init


def _diff_kernel(q1_ref, q2_ref, k_ref, v_ref, slope_ref, lam_ref, g_ref, o_ref, m_sc, l_sc, a_sc, *,
                 tk, tiles_per_iter, lam_init):
    g = pl.program_id(1)
    i = pl.program_id(2)
    tq = q1_ref.shape[1]
    m_sc[...] = jnp.full(m_sc.shape, NEG, F32)
    l_sc[...] = jnp.zeros(l_sc.shape, F32)
    a_sc[...] = jnp.zeros(a_sc.shape, F32)
    row = lax.broadcasted_iota(jnp.int32, (tk, tk), 0)
    col = lax.broadcasted_iota(jnp.int32, (tk, tk), 1)
    rel_diag = (2 * jnp.minimum(row, col) - col).astype(F32)
    visible = lax.shift_right_logical(col, 6) <= lax.shift_right_logical(row, 6)
    lane = lax.broadcasted_iota(jnp.int32, (1, tk), 1)

    def tile(j, rows, local):
        ks = pl.ds(pl.multiple_of(j * tk, tk), tk)
        if local is None:
            rel = (lane + (j * tk - i * tq)).astype(F32)
        else:
            rel = rel_diag + float(local)
        for h in range(GROUP_HEADS):
            hs = slice(h * HEAD_PAD, (h + 1) * HEAD_PAD)
            bias = (slope_ref[g * GROUP_HEADS + h][:, 0:1] * LOG2E) * rel
            if local is not None:
                bias = jnp.where(visible, bias, NEG)
            k = k_ref[0, ks, hs]
            v = v_ref[0, ks, hs]
            for m, q_ref in enumerate((q1_ref, q2_ref)):
                n = 2 * h + m
                t = _nt_dot(q_ref[0, rows, hs], k) + bias
                m_old = m_sc[n, rows, :]
                m_new = jnp.maximum(m_old, jnp.max(t, axis=1, keepdims=True))
                alpha = jnp.exp2(m_old - m_new)
                p = jnp.exp2(t - _lane_tile(m_new, tk))
                l_sc[n, rows, :] = alpha * l_sc[n, rows, :] + jnp.sum(p, axis=1, keepdims=True)
                a_sc[n, rows, :] = alpha * a_sc[n, rows, :] + _dot(p.astype(BF16), v)
                m_sc[n, rows, :] = m_new

    _causal_sweep(i, tq, tk, tiles_per_iter, tile)
    lam = _diff_lambda(lam_ref[...], lam_init)
    for h in range(GROUP_HEADS):
        out = a_sc[2 * h] / l_sc[2 * h] - lam * (a_sc[2 * h + 1] / l_sc[2 * h + 1])
        o_ref[0, :, h * HEAD_PAD:(h + 1) * HEAD_PAD] = (_rms(out, g_ref[...]) * (1.0 - lam_init)).astype(BF16)


def _diff_call(qc1_p, qc2_p, kc_b, vc_b, slopes, lam_p, g_diff, lam_init, tq, tk):
    assert CHUNK == 64 and tk % CHUNK == 0
    b, t, _ = qc1_p.shape
    qspec = pl.BlockSpec((1, tq, W_GROUP), lambda b_, g, i: (b_, i, g))
    stat = pltpu.VMEM((2 * GROUP_HEADS, tq, LANES), F32)
    return pl.pallas_call(
        functools.partial(_diff_kernel, tk=tk, tiles_per_iter=1, lam_init=lam_init),
        grid=(b, N_HEADS // GROUP_HEADS, t // tq),
        in_specs=[qspec, qspec, _resident_spec(t), _resident_spec(t),
                  pl.BlockSpec((N_HEADS, 1, LANES), lambda b_, g, i: (0, 0, 0)),
                  pl.BlockSpec((4, HD_C), lambda b_, g, i: (0, 0)),
                  pl.BlockSpec((1, HEAD_PAD), lambda b_, g, i: (0, 0))],
        out_specs=qspec,
        out_shape=jax.ShapeDtypeStruct((b, t, W_C), BF16),
        scratch_shapes=[stat, stat, pltpu.VMEM((2 * GROUP_HEADS, tq, HEAD_PAD), F32)],
        compiler_params=_params(("parallel", "parallel", "arbitrary")),
        name="diff",
    )(qc1_p, qc2_p, kc_b, vc_b, slopes, lam_p, g_diff)


def _gla_kernel(q_ref, k_ref, v_ref, la_ref, r_ref, s0_ref, g_ref, o_ref, s_ref, *, chunk):
    ti = pl.program_id(1)
    n_chunks = q_ref.shape[1] // chunk

    @pl.when(ti == 0)
    def _():
        s_ref[0] = s0_ref[0]

    tri_r = lax.broadcasted_iota(jnp.int32, (chunk, chunk), 0)
    tri_c = lax.broadcasted_iota(jnp.int32, (chunk, chunk), 1)
    incl = (tri_c <= tri_r).astype(F32)
    srow = lax.broadcasted_iota(jnp.int32, (N_HEADS * chunk, LANES), 0) // chunk
    slane = lax.broadcasted_iota(jnp.int32, (N_HEADS * chunk, LANES), 1) // DK_B
    qmask = srow == slane
    arow = lax.broadcasted_iota(jnp.int32, (N_HEADS * chunk, chunk), 0) % chunk
    acol = lax.broadcasted_iota(jnp.int32, (N_HEADS * chunk, chunk), 1)
    causal = acol <= arow
    vlane = lax.broadcasted_iota(jnp.int32, (chunk, W_B), 1) // DV_B
    st_row = lax.broadcasted_iota(jnp.int32, (W_B, LANES), 0) // DV_B
    st_lane = lax.broadcasted_iota(jnp.int32, (W_B, LANES), 1) // DK_B
    diag_blocks = st_row == st_lane
    gr = lax.broadcasted_iota(jnp.int32, (W_B, W_B), 0) // DV_B
    gc = lax.broadcasted_iota(jnp.int32, (W_B, W_B), 1) // DV_B
    group_mean = jnp.where(gr == gc, 1.0 / DV_B, 0.0)

    for c in range(n_chunks):
        sl = slice(c * chunk, (c + 1) * chunk)
        cb = _dot_exact_lhs(incl, la_ref[0, sl, :])
        cb_last = cb[chunk - 1:chunk, :]
        q = q_ref[0, sl, :] * (DK_B ** -0.5)
        k = k_ref[0, sl, :]
        v = v_ref[0, sl, :]
        vb = v.astype(BF16)
        qe = q * jnp.exp(cb)
        ke = (k * jnp.exp(-cb)).astype(BF16)
        kl = (k * jnp.exp(cb_last - cb)).astype(BF16)
        decay = jnp.exp(cb_last)
        qstack = jnp.where(qmask, jnp.concatenate([qe] * N_HEADS, axis=0), 0.0).astype(BF16)
        a = jnp.where(causal, _nt_dot(qstack, ke), 0.0).astype(BF16)
        oi = _dot(a, vb)
        o = jnp.zeros((chunk, W_B), F32)
        for h in range(N_HEADS):
            o = o + jnp.where(vlane == h, oi[h * chunk:(h + 1) * chunk], 0.0)
        st = s_ref[0]
        o = o + _nt_dot(qe.astype(BF16), st.astype(BF16))
        upd = _dot(v.T.astype(BF16), kl)
        s_ref[0] = decay * st + jnp.where(diag_blocks, upd, 0.0)
        ms = _dot_exact_rhs(o * o, group_mean)
        r = r_ref[0, sl, :]
        o_ref[0, sl, :] = (o * lax.rsqrt(ms + EPS) * g_ref[...] * (r * jax.nn.sigmoid(r))).astype(BF16)


def _gla_call(qb, kb, vb, loga, rb, s0_t, g_gla, tg, chunk):
    b, t, _ = qb.shape
    tok = lambda w: pl.BlockSpec((1, tg, w), lambda b_, i: (b_, i, 0))
    sspec = pl.BlockSpec((1, W_B, LANES), lambda b_, i: (b_, 0, 0))
    return pl.pallas_call(
        functools.partial(_gla_kernel, chunk=chunk),
        grid=(b, t // tg),
        in_specs=[tok(LANES), tok(LANES), tok(W_B), tok(LANES), tok(W_B), sspec,
                  pl.BlockSpec((1, W_B), lambda b_, i: (0, 0))],
        out_specs=[tok(W_B), sspec],
        out_shape=[jax.ShapeDtypeStruct((b, t, W_B), BF16), jax.ShapeDtypeStruct((b, W_B, LANES), F32)],
        compiler_params=_params(("parallel", "arbitrary")),
        name="gla",
    )(qb, kb, vb, loga, rb, s0_t, g_gla)


def _softmax_two(t_past, t_new, exp_fn):
    m = jnp.maximum(jnp.max(t_past, axis=1, keepdims=True), jnp.max(t_new, axis=1, keepdims=True))
    p_past = exp_fn(t_past - m)
    p_new = exp_fn(t_new - m)
    l = jnp.sum(p_past, axis=1, keepdims=True) + jnp.sum(p_new, axis=1, keepdims=True)
    return p_past, p_new, l


def _fox_s_kernel(q_ref, kn_ref, vn_ref, kp_ref, vp_ref, c_ref, o_ref):
    t, past = q_ref.shape[1], kp_ref.shape[1]
    q = q_ref[0]
    kn = kn_ref[0].astype(BF16)
    vn = vn_ref[0].astype(BF16)
    kp = kp_ref[0].astype(BF16)
    vp = vp_ref[0].astype(BF16)
    lane_head = lax.broadcasted_iota(jnp.int32, (t, W_A), 1) // HD_A
    row = lax.broadcasted_iota(jnp.int32, (t, t), 0)
    col = lax.broadcasted_iota(jnp.int32, (t, t), 1)
    o = jnp.zeros((t, W_A), F32)
    for h in range(N_HEADS):
        qh = jnp.where(lane_head == h, q, 0.0).astype(BF16)
        ch = c_ref[0, h:h + 1, :]
        cref = ch[:, past:past + 1]
        t_past = _nt_dot(qh, kp) + (cref - ch[:, :past])
        t_new = jnp.where(col <= row, _nt_dot(qh, kn) + (cref - ch[:, past:past + t]), NEG)
        p_past, p_new, l = _softmax_two(t_past, t_new, jnp.exp)
        oh =(_dot(p_past.astype(BF16), vp) + _dot(p_new.astype(BF16), vn)) / l
        o = o + jnp.where(lane_head == h, oh, 0.0)
    o_ref[0] = o.astype(BF16)


def _fox_s_call(qa, ka, va, cache_k, cache_v, c):
    b, t, _ = qa.shape
    past = cache_k.shape[1]
    new = lambda: pl.BlockSpec((1, t, W_A), lambda i: (i, 0, 0))
    old = lambda: pl.BlockSpec((1, past, W_A), lambda i: (i, 0, 0))
    return pl.pallas_call(
        _fox_s_kernel,
        grid=(b,),
        in_specs=[new(), new(), new(), old(), old(),
                  pl.BlockSpec((1, N_HEADS, c.shape[2]), lambda i: (i, 0, 0))],
        out_specs=new(),
        out_shape=jax.ShapeDtypeStruct((b, t, W_A), BF16),
        compiler_params=_params(("parallel",)),
        name="fox_s",
    )(qa, ka, va, cache_k, cache_v, c)


def _diff_s_kernel(q1_ref, q2_ref, kn_ref, vn_ref, kp_ref, vp_ref, slope_ref, lam_ref, g_ref, o_ref, *, lam_init):
    t, past = q1_ref.shape[1], kp_ref.shape[1]
    lam = _diff_lambda(lam_ref[...], lam_init)
    row = lax.broadcasted_iota(jnp.int32, (t, t), 0) + past
    col = lax.broadcasted_iota(jnp.int32, (t, t), 1) + past
    visible_new = lax.shift_right_logical(col, 6) <= lax.shift_right_logical(row, 6)
    dist_new = jnp.abs(row - col).astype(F32)
    prow = lax.broadcasted_iota(jnp.int32, (t, past), 0) + past
    pcol = lax.broadcasted_iota(jnp.int32, (t, past), 1)
    visible_past = lax.shift_right_logical(pcol, 6) <= lax.shift_right_logical(prow, 6)
    dist_past = (prow - pcol).astype(F32)
    for h in range(N_HEADS):
        hs = slice(h * HEAD_PAD, (h + 1) * HEAD_PAD)
        slope = slope_ref[h][:, 0:1] * LOG2E
        kn = kn_ref[0, :, hs].astype(BF16)
        vn = vn_ref[0, :, hs].astype(BF16)
        kp = kp_ref[0, :, hs].astype(BF16)
        vp = vp_ref[0, :, hs].astype(BF16)
        outs = []
        for q_ref in (q1_ref, q2_ref):
            q = q_ref[0, :, hs]
            t_past = jnp.where(visible_past, _nt_dot(q, kp) - slope * dist_past, NEG)
            t_new = jnp.where(visible_new, _nt_dot(q, kn) - slope * dist_new, NEG)
            p_past, p_new, l = _softmax_two(t_past, t_new, jnp.exp2)
            outs.append((_dot(p_past.astype(BF16), vp) + _dot(p_new.astype(BF16), vn)) / l)
        out = outs[0] - lam * outs[1]
        o_ref[0, :, hs] = (_rms(out, g_ref[...]) * (1.0 - lam_init)).astype(BF16)


def _diff_s_call(qc1_p, qc2_p, kc, vc, cache_k, cache_v, slopes, lam_p, g_diff, lam_init):
    assert CHUNK == 64
    b, t, _ = qc1_p.shape
    past = cache_k.shape[1]
    new = lambda: pl.BlockSpec((1, t, W_C), lambda i: (i, 0, 0))
    old = lambda: pl.BlockSpec((1, past, W_C), lambda i: (i, 0, 0))
    return pl.pallas_call(
        functools.partial(_diff_s_kernel, lam_init=lam_init),
        grid=(b,),
        in_specs=[new(), new(), new(), new(), old(), old(),
                  pl.BlockSpec((N_HEADS, 1, LANES), lambda i: (0, 0, 0)),
                  pl.BlockSpec((4, HD_C), lambda i: (0, 0)),
                  pl.BlockSpec((1, HEAD_PAD), lambda i: (0, 0))],
        out_specs=new(),
        out_shape=jax.ShapeDtypeStruct((b, t, W_C), BF16),
        compiler_params=_params(("parallel",)),
        name="diff_s",
    )(qc1_p, qc2_p, kc, vc, cache_k, cache_v, slopes, lam_p, g_diff)


def _out_kernel(x_ref, oa_ref, gla_ref, dif_ref, gate_ref, g2_ref, woa_ref, wog_ref, wod_ref, o_ref):
    y = _dot(oa_ref[...], woa_ref[...]) + _dot(gla_ref[...], wog_ref[...]) + _dot(dif_ref[...], wod_ref[...])
    o_ref[...] = x_ref[...] + gate_ref[0] * _rms(y, g2_ref[...])


def _out_call(x, oa, gla, dif, mod, g2, w_oa, w_og, w_od, tm, tiles_per_mod):
    n = x.shape[0]
    gate = mod[2]
    r = gate.shape[1]
    tok = lambda w: pl.BlockSpec((tm, w), lambda i: (i, 0))
    return pl.pallas_call(
        _out_kernel,
        grid=(n // tm,),
        in_specs=[tok(D_MODEL), tok(oa.shape[1]), tok(W_B), tok(W_C), _mod_spec(r, tiles_per_mod),
                  _const_spec((1, D_MODEL)), _const_spec(w_oa.shape), _const_spec(w_og.shape),
                  _const_spec(w_od.shape)],
        out_specs=tok(D_MODEL),
        out_shape=jax.ShapeDtypeStruct((n, D_MODEL), F32),
        compiler_params=_params(("parallel",)),
        name="out_proj",
    )(x, oa, gla, dif, gate, g2, w_oa, w_og, w_od)


def _pad_heads(w, hd, off=0):
    rows = w.shape[0]
    out = jnp.zeros((rows, N_HEADS, HEAD_PAD), w.dtype)
    out = out.at[:, :, off:off + hd].set(w.reshape(rows, N_HEADS, hd))
    return out.reshape(rows, W_PAD)


def _pad_lanes(w, width=LANES):
    return jnp.pad(w, ((0, 0), (0, width - w.shape[1])))


def _layer_params(l, w_ffn_in, w_ffn_out, w_in, b_f, w_gla_up, b_gla_up, g_gla, g_diff, w_out):
    offs = [int(o) for o in np.cumsum(PROJ_SIZES)[:-1]]
    qa, ka, va, fa, qb, kb, vb, gb, rb, qc, kc, vc = jnp.split(w_in[l], offs, axis=1)
    qc = qc.reshape(D_MODEL, N_HEADS, 2, HD_C)
    segs = {
        "ka": ka, "va": va, "kc": kc, "vc": vc, "qa": qa,
        "qa_p": _pad_heads(qa, HD_A), "ka_p": _pad_heads(ka, HD_A), "va_p": _pad_heads(va, HD_A),
        "qc1_p": _pad_heads(qc[:, :, 0].reshape(D_MODEL, -1), HD_C),
        "qc2_p": _pad_heads(qc[:, :, 1].reshape(D_MODEL, -1), HD_C, off=HD_C),
        "fa": _pad_lanes(fa), "qb": qb, "kb": kb, "vb": vb, "rb": rb, "gb": _pad_lanes(gb),
    }
    w_ext = jnp.concatenate([segs[n] for n, _ in _SEG_WIDTHS], axis=1).astype(BF16)
    wo = w_out[l]
    vone = np.zeros((1, W_PAD), np.float32)
    vone[0, HD_A::HEAD_PAD] = 1.0
    return dict(
        w_ffn_in=w_ffn_in[l].astype(BF16), w_ffn_out=w_ffn_out[l].astype(BF16), w_ext=w_ext,
        bf=_pad_lanes(b_f[l][None, :]),
        wup=jnp.pad(w_gla_up[l], ((0, LANES - GLA_RANK), (0, 0))).astype(BF16),
        bup=b_gla_up[l][None, :],
        vone=jnp.asarray(vone),
        g_gla=jnp.tile(g_gla[l], N_HEADS)[None, :], g_diff=g_diff[l][None, :],
        w_oa=wo[:W_A].astype(BF16),
        w_oa_p=_pad_heads(wo[:W_A].T, HD_A).T.astype(BF16),
        w_og=wo[W_A:W_A + W_B].astype(BF16), w_od=wo[W_A + W_B:].astype(BF16),
    )


def _state_to_blockdiag_t(s):
    eye = jnp.eye(N_HEADS, dtype=s.dtype)
    return jnp.einsum("bhkv,hg->bhvgk", s, eye).reshape(s.shape[0], W_B, N_HEADS * DK_B)


def _blockdiag_t_to_state(st):
    b = st.shape[0]
    s5 = st.reshape(b, N_HEADS, DV_B, N_HEADS, DK_B)
    diag = jnp.stack([s5[:, h, :, h, :] for h in range(N_HEADS)], axis=1)
    return jnp.swapaxes(diag, 2, 3)


def _tile(n, pref):
    t = min(n, pref)
    assert n % t == 0, (n, t)
    return t


def _trunk(x, mods, params, g_norm, lam_params, mixer, tm, tiles_per_mod):
    b, t, _ = x.shape
    n = b * t
    xf = x.reshape(n, D_MODEL)
    states = []
    for l in range(len(params)):
        p = params[l]
        gn = lambda k: g_norm[l, k][None, :]
        xf = _ffn_call(xf, mods[l][0], gn(0), gn(1), p["w_ffn_in"][0], p["w_ffn_out"][0], 0.5, tm, tiles_per_mod)
        pr = _proj_call(xf, mods[l][1], gn(2), p["w_ext"], p["bf"], p["wup"], p["bup"], p["vone"], tm, tiles_per_mod)
        lam_init = 0.8 - 0.6 * math.exp(-0.3 * l)
        oa, gla, dif, s_fin, w_oa = mixer(l, pr, p, lam_params[l], lam_init, b, t)
        xf = _out_call(xf, oa, gla, dif, mods[l][1], gn(3), w_oa, p["w_og"], p["w_od"], tm, tiles_per_mod)
        xf = _ffn_call(xf, mods[l][2], gn(4), gn(5), p["w_ffn_in"][1], p["w_ffn_out"][1], 0.5, tm, tiles_per_mod)
        shp = lambda a, *tail: a.reshape((b, t) + tail)
        states.append((shp(pr["ka"], N_HEADS, HD_A), shp(pr["va"], N_HEADS, HD_A), shp(pr["logf"], N_HEADS), s_fin,
                       shp(pr["kc"], N_HEADS, 2 * HD_C), shp(pr["vc"], N_HEADS, 2 * HD_C)))
    stacked = [jnp.stack([st[i] for st in states]) for i in range(6)]
    return xf.reshape(b, t, D_MODEL), stacked


def kernel(x_prompt, x_sample, c_prompt, c_sample, cache_fox_k, cache_fox_v, cache_fox_logf, state_gla,
           cache_diff_k, cache_diff_v, w_ada, b_ada, g_norm, w_ffn_in, w_ffn_out, w_in, b_f, w_gla_up,
           b_gla_up, g_gla, g_diff, lam_params, w_out):
    depth = w_in.shape[0]
    bp, tp, _ = x_prompt.shape
    bs, ts, _ = x_sample.shape
    past = cache_fox_k.shape[2]

    params = [_layer_params(l, w_ffn_in, w_ffn_out, w_in, b_f, w_gla_up, b_gla_up, g_gla, g_diff, w_out)
              for l in range(depth)]

    slopes = 2.0 ** (-8.0 * jnp.arange(1, N_HEADS + 1, dtype=F32) / N_HEADS)
    slopes = jnp.broadcast_to(slopes[:, None, None], (N_HEADS, 1, LANES))

    rows = bp + bs
    rows_pad = -(-rows // 8) * 8
    c_all = jnp.pad(jnp.concatenate([c_prompt, c_sample], axis=0), ((0, rows_pad - rows), (0, 0)))
    mods_p, mods_s = [], []
    for l in range(depth):
        mod = _ada_call(c_all, w_ada[l].astype(BF16), b_ada[l][None, :]).reshape(rows_pad, N_SUB, 3, D_MODEL)
        mods_p.append([[mod[:bp, s, k][:, None, :] for k in range(3)] for s in range(N_SUB)])
        mods_s.append([[jnp.repeat(mod[bp:rows, s, k], ts, axis=0)[None] for k in range(3)] for s in range(N_SUB)])

    tm_p = _tile(tp, 512)
    tq = _tile(tp, 1024)
    tk = _tile(tp, 512)
    tg = _tile(tp, 512)

    def prompt_mixer(l, pr, p, lam_p, lam_init, b, t):
        r3 = lambda a: a.reshape(b, t, a.shape[-1])
        logf_t = jnp.swapaxes(r3(pr["logf"]), 1, 2).reshape(b * N_HEADS, t // LANES, LANES)
        c = _cumsum_call(logf_t).reshape(b, N_HEADS, t)
        oa = _fox_call(r3(pr["qa_p"]), r3(pr["ka_p"]), r3(pr["va_p"]), c, tq, tk)
        s0_t = jnp.zeros((b, W_B, N_HEADS * DK_B), F32)
        gla, s_t = _gla_call(r3(pr["qb"]), r3(pr["kb"]), r3(pr["vb"]), r3(pr["loga"]), r3(pr["rb"]), s0_t,
                             p["g_gla"], tg, min(CHUNK, t))
        dif = _diff_call(r3(pr["qc1_p"]), r3(pr["qc2_p"]), r3(pr["kc_b"]), r3(pr["vc_b"]), slopes, lam_p,
                         p["g_diff"], lam_init, tq, tk)
        flat = lambda a: a.reshape(b * t, a.shape[-1])
        return flat(oa), flat(gla), flat(dif), _blockdiag_t_to_state(s_t), p["w_oa_p"]

    def sample_mixer(l, pr, p, lam_p, lam_init, b, t):
        r3 = lambda a: a.reshape(b, t, a.shape[-1])
        total = past + t
        nb = -(-total // (8 * LANES)) * 8
        lf = jnp.concatenate([cache_fox_logf[l].astype(F32), r3(pr["logf"])], axis=1)
        lf = jnp.pad(jnp.swapaxes(lf, 1, 2), ((0, 0), (0, 0), (0, nb * LANES - total)))
        c = _cumsum_call(lf.reshape(b * N_HEADS, nb, LANES)).reshape(b, N_HEADS, nb * LANES)
        oa = _fox_s_call(r3(pr["qa"]), r3(pr["ka"]), r3(pr["va"]),
                         cache_fox_k[l].reshape(b, past, W_A), cache_fox_v[l].reshape(b, past, W_A), c)
        gla, s_t = _gla_call(r3(pr["qb"]), r3(pr["kb"]), r3(pr["vb"]), r3(pr["loga"]), r3(pr["rb"]),
                             _state_to_blockdiag_t(state_gla[l].astype(F32)), p["g_gla"], t, min(CHUNK, t))
        dif = _diff_s_call(r3(pr["qc1_p"]), r3(pr["qc2_p"]), r3(pr["kc"]), r3(pr["vc"]),
                           cache_diff_k[l].reshape(b, past, W_C), cache_diff_v[l].reshape(b, past, W_C),
                           slopes, lam_p, p["g_diff"], lam_init)
        flat = lambda a: a.reshape(b * t, a.shape[-1])
        return flat(oa), flat(gla), flat(dif), _blockdiag_t_to_state(s_t), p["w_oa"]

    y_p, sp = _trunk(x_prompt, mods_p, params, g_norm, lam_params, prompt_mixer, tm_p, tp // tm_p)
    y_s, ss = _trunk(x_sample, mods_s, params, g_norm, lam_params, sample_mixer, bs * ts, 1)
    return (y_p, y_s, sp[0], sp[1], sp[2], sp[3], sp[4], sp[5], ss[0], ss[1], ss[2], ss[3], ss[4], ss[5])
```

```python
import functools
import math

import jax
import jax.numpy as jnp
import numpy as np
from jax import lax
from jax.experimental import pallas as pl
from jax.experimental.pallas import tpu as pltpu

F32 = jnp.float32
BF16 = jnp.bfloat16

D_MODEL = 1024
N_HEADS = 4
HD_A = 64
DK_B = 32
DV_B = 64
GLA_RANK = 16
GLA_TAU = 16.0
HD_C = 64
CHUNK = 64
W_A = N_HEADS * HD_A
W_B = N_HEADS * DV_B
W_C = N_HEADS * 2 * HD_C
D_FF = ((8 * D_MODEL // 3 + 255) // 256) * 256
N_SUB = 3
PROJ_SIZES = (W_A, W_A, W_A, N_HEADS, N_HEADS * DK_B, N_HEADS * DK_B, W_B, GLA_RANK, W_B, W_C, W_C, W_C)
NEG = -1e30
EPS = 1e-6
LOG2E = math.log2(math.e)

LANES = 128
HEAD_PAD = 128
W_PAD = N_HEADS * HEAD_PAD
VMEM_LIMIT = 56 * 1024 * 1024

_SEG_WIDTHS = (
    ("ka", W_A), ("va", W_A), ("kc", W_C), ("vc", W_C), ("qa", W_A),
    ("qa_p", W_PAD), ("ka_p", W_PAD), ("va_p", W_PAD), ("qc1_p", W_PAD), ("qc2_p", W_PAD),
    ("fa", LANES), ("qb", LANES), ("kb", LANES), ("vb", W_B), ("rb", W_B), ("gb", LANES),
)
_SEG = {}
_off = 0
for _name, _w in _SEG_WIDTHS:
    _SEG[_name] = (_off, _off + _w)
    _off += _w
N_EXT = _off


def _nt_dot(a, b):
    return lax.dot_general(a, b, (((1,), (1,)), ((), ())), preferred_element_type=F32)


def _dot(a, b):
    return jnp.dot(a, b, preferred_element_type=F32)


def _split3(x):
    hi = x.astype(BF16)
    r1 = x - hi.astype(F32)
    mid = r1.astype(BF16)
    lo = (r1 - mid.astype(F32)).astype(BF16)
    return hi, mid, lo


def _dot_exact_rhs(x, m):
    mb = m.astype(BF16)
    hi, mid, lo = _split3(x)
    return _dot(hi, mb) + _dot(mid, mb) + _dot(lo, mb)


def _dot_exact_lhs(m, x):
    mb = m.astype(BF16)
    hi, mid, lo = _split3(x)
    return _dot(mb, hi) + _dot(mb, mid) + _dot(mb, lo)


def _rms(x, g):
    return x * lax.rsqrt(jnp.mean(x * x, axis=-1, keepdims=True) + EPS) * g


def _log_sigmoid(x):
    return jnp.minimum(x, 0.0) - jnp.log1p(jnp.exp(-jnp.abs(x)))


def _const_spec(shape):
    nd = len(shape)
    return pl.BlockSpec(shape, lambda *_: (0,) * nd, pipeline_mode=pl.Buffered(1))


def _params(sem):
    return pltpu.CompilerParams(dimension_semantics=sem, vmem_limit_bytes=VMEM_LIMIT)


def _ada_kernel(c_ref, w_ref, b_ref, o_ref):
    c = c_ref[...]
    a = (c * jax.nn.sigmoid(c)).astype(BF16)
    o_ref[...] = _dot(a, w_ref[...]) + b_ref[...]


def _ada_call(c, w, b):
    m, n = c.shape[0], w.shape[1]
    tn = n // 8
    return pl.pallas_call(
        _ada_kernel,
        grid=(n // tn,),
        in_specs=[pl.BlockSpec((m, D_MODEL), lambda i: (0, 0)),
                  pl.BlockSpec((D_MODEL, tn), lambda i: (0, i)),
                  pl.BlockSpec((1, tn), lambda i: (0, i))],
        out_specs=pl.BlockSpec((m, tn), lambda i: (0, i)),
        out_shape=jax.ShapeDtypeStruct((m, n), F32),
        compiler_params=_params(("parallel",)),
        name="ada",
    )(c, w, b)


FF_CHUNK = 256


def _ffn_kernel(x_ref, shift_ref, scale_ref, gate_ref, g1_ref, g2_ref, win_ref, wout_ref, o_ref, *, res_w):
    x = x_ref[...]
    h = _rms(x, g1_ref[...]) * (1.0 + scale_ref[0]) + shift_ref[0]
    hb = h.astype(BF16)
    y = jnp.zeros(x.shape, F32)
    for c in range(D_FF // FF_CHUNK):
        lo, hi = c * FF_CHUNK, (c + 1) * FF_CHUNK
        g = _dot(hb, win_ref[:, lo:hi])
        u = _dot(hb, win_ref[:, D_FF + lo:D_FF + hi])
        a = (g * jax.nn.sigmoid(g) * u).astype(BF16)
        y = y + _dot(a, wout_ref[lo:hi, :])
    o_ref[...] = x + res_w * gate_ref[0] * _rms(y, g2_ref[...])


def _mod_spec(r, tiles_per_mod):
    return pl.BlockSpec((1, r, D_MODEL), lambda i: (i // tiles_per_mod, 0, 0))


def _ffn_call(x, mod, g1, g2, w_in, w_out, res_w, tm, tiles_per_mod):
    n = x.shape[0]
    shift, scale, gate = mod
    r = shift.shape[1]
    tok = pl.BlockSpec((tm, D_MODEL), lambda i: (i, 0))
    return pl.pallas_call(
        functools.partial(_ffn_kernel, res_w=res_w),
        grid=(n // tm,),
        in_specs=[tok, _mod_spec(r, tiles_per_mod), _mod_spec(r, tiles_per_mod), _mod_spec(r, tiles_per_mod),
                  _const_spec((1, D_MODEL)), _const_spec((1, D_MODEL)),
                  _const_spec((D_MODEL, 2 * D_FF)), _const_spec((D_FF, D_MODEL))],
        out_specs=tok,
        out_shape=jax.ShapeDtypeStruct((n, D_MODEL), F32),
        compiler_params=_params(("parallel",)),
        name="ffn",
    )(x, shift, scale, gate, g1, g2, w_in, w_out)


_PROJ_OUTS = (
    ("ka", W_A, F32), ("va", W_A, F32), ("kc", W_C, F32), ("vc", W_C, F32), ("logf", N_HEADS, F32),
    ("qa", W_A, F32), ("qa_p", W_PAD, BF16), ("ka_p", W_PAD, BF16), ("va_p", W_PAD, BF16),
    ("qc1_p", W_PAD, BF16), ("qc2_p", W_PAD, BF16), ("kc_b", W_C, BF16), ("vc_b", W_C, BF16),
    ("qb", LANES, F32), ("kb", LANES, F32), ("vb", W_B, F32), ("loga", LANES, F32), ("rb", W_B, F32),
)


_NORM_ROWS = ("qa_p", "ka_p", "qc1_p", "qc2_p", "kc_b")


def _proj_kernel(x_ref, shift_ref, scale_ref, g1_ref, w_ref, bf_ref, wup_ref, bup_ref, vone_ref, *out_refs):
    o = dict(zip([n for n, _, _ in _PROJ_OUTS], out_refs[:-1]))
    norm_ref = out_refs[-1]
    x = x_ref[...]
    hb = (_rms(x, g1_ref[...]) * (1.0 + scale_ref[0]) + shift_ref[0]).astype(BF16)

    def seg(name):
        lo, hi = _SEG[name]
        return _dot(hb, w_ref[:, lo:hi])

    ka = seg("ka")
    o["ka"][...] = ka
    va = seg("va")
    o["va"][...] = va
    kc = seg("kc")
    o["kc"][...] = kc
    rounded = {"kc_b": kc.astype(BF16)}
    o["kc_b"][...] = rounded["kc_b"]
    vc = seg("vc")
    o["vc"][...] = vc
    o["vc_b"][...] = vc.astype(BF16)
    o["qa"][...] = seg("qa") * (HD_A ** -0.5)
    rounded["qa_p"] = (seg("qa_p") * (HD_A ** -0.5 * LOG2E)).astype(BF16)
    rounded["ka_p"] = seg("ka_p").astype(BF16)
    o["qa_p"][...] = rounded["qa_p"]
    o["ka_p"][...] = rounded["ka_p"]
    o["va_p"][...] = (seg("va_p") + vone_ref[...]).astype(BF16)
    rounded["qc1_p"] = (seg("qc1_p") * (HD_C ** -0.5 * LOG2E)).astype(BF16)
    rounded["qc2_p"] = (seg("qc2_p") * (HD_C ** -0.5 * LOG2E)).astype(BF16)
    o["qc1_p"][...] = rounded["qc1_p"]
    o["qc2_p"][...] = rounded["qc2_p"]
    fa = seg("fa") + bf_ref[...]
    o["logf"][...] = _log_sigmoid(fa)[:, :N_HEADS]
    o["qb"][...] = seg("qb")
    o["kb"][...] = seg("kb")
    o["vb"][...] = seg("vb")
    o["rb"][...] = seg("rb")
    gb = seg("gb").astype(BF16)
    o["loga"][...] = _log_sigmoid(_dot(gb, wup_ref[...]) + bup_ref[...]) * (1.0 / GLA_TAU)
    lane = lax.broadcasted_iota(jnp.int32, (8, LANES), 1)
    tile_norms = jnp.zeros((8, LANES), F32)
    for a, name in enumerate(_NORM_ROWS):
        sq = rounded[name].astype(F32)
        sq = sq * sq
        for h in range(N_HEADS):
            ss = jnp.sum(sq[:, h * HEAD_PAD:(h + 1) * HEAD_PAD], axis=1, keepdims=True)
            tile_norms = jnp.where(lane == N_HEADS * a + h, jnp.max(ss, axis=0, keepdims=True), tile_norms)
    norm_ref[0] = tile_norms


def _proj_call(x, mod, g1, w_ext, bf, wup, bup, vone, tm, tiles_per_mod):
    n = x.shape[0]
    shift, scale, _ = mod
    r = shift.shape[1]
    tok = lambda w: pl.BlockSpec((tm, w), lambda i: (i, 0))
    outs = pl.pallas_call(
        _proj_kernel,
        grid=(n // tm,),
        in_specs=[tok(D_MODEL), _mod_spec(r, tiles_per_mod), _mod_spec(r, tiles_per_mod),
                  _const_spec((1, D_MODEL)), _const_spec((D_MODEL, N_EXT)), _const_spec((1, LANES)),
                  _const_spec((LANES, LANES)), _const_spec((1, LANES)), _const_spec((1, W_PAD))],
        out_specs=[tok(w) for _, w, _ in _PROJ_OUTS] + [pl.BlockSpec((1, 8, LANES), lambda i: (i, 0, 0))],
        out_shape=[jax.ShapeDtypeStruct((n, w), dt) for _, w, dt in _PROJ_OUTS]
        + [jax.ShapeDtypeStruct((n // tm, 8, LANES), F32)],
        compiler_params=_params(("parallel",)),
        name="proj",
    )(x, shift, scale, g1, w_ext, bf, wup, bup, vone)
    return dict(zip([nm for nm, _, _ in _PROJ_OUTS] + ["norms"], outs))


def _cumsum_kernel(x_ref, o_ref):
    x = x_ref[0]
    nb = x.shape[0]
    r = lax.broadcasted_iota(jnp.int32, (LANES, LANES), 0)
    c = lax.broadcasted_iota(jnp.int32, (LANES, LANES), 1)
    local = _dot_exact_rhs(x, (r <= c).astype(F32))
    tot = jnp.broadcast_to(local[:, LANES - 1:LANES], (nb, LANES))
    rr = lax.broadcasted_iota(jnp.int32, (nb, nb), 0)
    cc = lax.broadcasted_iota(jnp.int32, (nb, nb), 1)
    o_ref[0] = local + _dot_exact_lhs((cc < rr).astype(F32), tot)


def _cumsum_call(x):
    rows, nb, _ = x.shape
    spec = pl.BlockSpec((1, nb, LANES), lambda i: (i, 0, 0))
    return pl.pallas_call(
        _cumsum_kernel, grid=(rows,), in_specs=[spec], out_specs=spec,
        out_shape=jax.ShapeDtypeStruct(x.shape, F32),
        compiler_params=_params(("parallel",)), name="cumsum",
    )(x)


def _lane_tile(x, width):
    return jnp.concatenate([x] * (width // LANES), axis=1)


GROUP_HEADS = 2
W_GROUP = GROUP_HEADS * HEAD_PAD


def _causal_sweep(i, first, tq, tk, tiles_per_iter, tile):
    r = tq // tk

    def body(jj, carry):
        for u in range(tiles_per_iter):
            tile(jj * tiles_per_iter + u, slice(0, tq), None)
        return carry

    assert r % tiles_per_iter == 0
    lax.fori_loop(lax.div(first, tiles_per_iter), lax.div(i * r, tiles_per_iter), body, 0)
    for d in range(r):
        tile(i * r + d, slice(d * tk, (d + 1) * tk), d * tk)
        if d + 1 < r:
            tile(i * r + d, slice((d + 1) * tk, tq), None)


def _fox_kernel(first_ref, q_ref, k_ref, v_ref, c_ref, o_ref, m_sc, acc_sc, *, tk, tiles_per_iter):
    i = pl.program_id(2)
    first = first_ref[pl.program_id(0), pl.program_id(1), i]
    tq = q_ref.shape[1]
    m_sc[...] = jnp.full(m_sc.shape, NEG, F32)
    acc_sc[...] = jnp.zeros(acc_sc.shape, F32)
    row = lax.broadcasted_iota(jnp.int32, (tk, tk), 0)
    col = lax.broadcasted_iota(jnp.int32, (tk, tk), 1)
    causal = col <= row

    def tile(j, rows, local):
        ks = pl.ds(pl.multiple_of(j * tk, tk), tk)
        for h in range(GROUP_HEADS):
            hs = slice(h * HEAD_PAD, (h + 1) * HEAD_PAD)
            c_first = c_ref[0, h, pl.ds(i * (tq // tk), 1), :][:, 0:1]
            t = _nt_dot(q_ref[0, rows, hs], k_ref[0, ks, hs]) + (c_first - c_ref[0, h, pl.ds(j, 1), :]) * LOG2E
            if local is not None:
                t = jnp.where(causal, t, NEG)
            m_old = m_sc[h, rows, :]
            m_new = jnp.maximum(m_old, jnp.max(t, axis=1, keepdims=True))
            p = jnp.exp2(t - _lane_tile(m_new, tk)).astype(BF16)
            acc_sc[h, rows, :] = jnp.exp2(m_old - m_new) * acc_sc[h, rows, :] + _dot(p, v_ref[0, ks, hs])
            m_sc[h, rows, :] = m_new

    _causal_sweep(i, first, tq, tk, tiles_per_iter, tile)
    for h in range(GROUP_HEADS):
        acc = acc_sc[h]
        o_ref[0, :, h * HEAD_PAD:(h + 1) * HEAD_PAD] = (acc / acc[:, HD_A:HD_A + 1]).astype(BF16)


def _resident_spec(t):
    return pl.BlockSpec((1, t, W_GROUP), lambda b_, g, i, first: (b_, 0, g), pipeline_mode=pl.Buffered(1))


def _fox_call(first, qa_p, ka_p, va_p, c, tq, tk):
    b, t, _ = qa_p.shape
    qspec = pl.BlockSpec((1, tq, W_GROUP), lambda b_, g, i, first: (b_, i, g))
    grid_spec = pltpu.PrefetchScalarGridSpec(
        num_scalar_prefetch=1,
        grid=(b, N_HEADS // GROUP_HEADS, t // tq),
        in_specs=[qspec, _resident_spec(t), _resident_spec(t),
                  pl.BlockSpec((1, GROUP_HEADS, t // tk, tk), lambda b_, g, i, first: (b_, g, 0, 0))],
        out_specs=qspec,
        scratch_shapes=[pltpu.VMEM((GROUP_HEADS, tq, LANES), F32), pltpu.VMEM((GROUP_HEADS, tq, HEAD_PAD), F32)])
    return pl.pallas_call(
        functools.partial(_fox_kernel, tk=tk, tiles_per_iter=tq // tk),
        grid_spec=grid_spec,
        out_shape=jax.ShapeDtypeStruct((b, t, W_PAD), BF16),
        compiler_params=_params(("parallel", "parallel", "arbitrary")),
        name="fox",
    )(first, qa_p, ka_p, va_p, c.reshape(b, N_HEADS, t // tk, tk))


def _diff_lambda(lp, lam_init):
    a = jnp.sum(lp[0:1] * lp[1:2], axis=1, keepdims=True)
    b = jnp.sum(lp[2:3] * lp[3:4], axis=1, keepdims=True)
    return jnp.exp(a) - jnp.exp(b) + lam_init


def _diff_kernel(first_ref, q1_ref, q2_ref, k_ref, v_ref, slope_ref, lam_ref, g_ref, o_ref, m_sc, l_sc, a_sc, *,
                 tk, tiles_per_iter, lam_init):
    g = pl.program_id(1)
    i = pl.program_id(2)
    first = first_ref[pl.program_id(0), g, i]
    tq = q1_ref.shape[1]
    m_sc[...] = jnp.full(m_sc.shape, NEG, F32)
    l_sc[...] = jnp.zeros(l_sc.shape, F32)
    a_sc[...] = jnp.zeros(a_sc.shape, F32)
    row = lax.broadcasted_iota(jnp.int32, (tk, tk), 0)
    col = lax.broadcasted_iota(jnp.int32, (tk, tk), 1)
    rel_diag = (2 * jnp.minimum(row, col) - col).astype(F32)
    visible = lax.shift_right_logical(col, 6) <= lax.shift_right_logical(row, 6)
    lane = lax.broadcasted_iota(jnp.int32, (1, tk), 1)

    def tile(j, rows, local):
        ks = pl.ds(pl.multiple_of(j * tk, tk), tk)
        if local is None:
            rel = (lane + (j * tk - i * tq)).astype(F32)
        else:
            rel = rel_diag + float(local)
        for h in range(GROUP_HEADS):
            hs = slice(h * HEAD_PAD, (h + 1) * HEAD_PAD)
            bias = (slope_ref[g * GROUP_HEADS + h][:, 0:1] * LOG2E) * rel
            if local is not None:
                bias = jnp.where(visible, bias, NEG)
            k = k_ref[0, ks, hs]
            v = v_ref[0, ks, hs]
            for m, q_ref in enumerate((q1_ref, q2_ref)):
                n = 2 * h + m
                t = _nt_dot(q_ref[0, rows, hs], k) + bias
                m_old = m_sc[n, rows, :]
                m_new = jnp.maximum(m_old, jnp.max(t, axis=1, keepdims=True))
                alpha = jnp.exp2(m_old - m_new)
                p = jnp.exp2(t - _lane_tile(m_new, tk))
                l_sc[n, rows, :] = alpha * l_sc[n, rows, :] + jnp.sum(p, axis=1, keepdims=True)
                a_sc[n, rows, :] = alpha * a_sc[n, rows, :] + _dot(p.astype(BF16), v)
                m_sc[n, rows, :] = m_new

    _causal_sweep(i, first, tq, tk, tiles_per_iter, tile)
    lam = _diff_lambda(lam_ref[...], lam_init)
    for h in range(GROUP_HEADS):
        out = a_sc[2 * h] / l_sc[2 * h] - lam * (a_sc[2 * h + 1] / l_sc[2 * h + 1])
        o_ref[0, :, h * HEAD_PAD:(h + 1) * HEAD_PAD] = (_rms(out, g_ref[...]) * (1.0 - lam_init)).astype(BF16)


def _diff_call(first, qc1_p, qc2_p, kc_b, vc_b, slopes, lam_p, g_diff, lam_init, tq, tk):
    assert CHUNK == 64 and tk % CHUNK == 0
    b, t, _ = qc1_p.shape
    qspec = pl.BlockSpec((1, tq, W_GROUP), lambda b_, g, i, first: (b_, i, g))
    stat = pltpu.VMEM((2 * GROUP_HEADS, tq, LANES), F32)
    grid_spec = pltpu.PrefetchScalarGridSpec(
        num_scalar_prefetch=1,
        grid=(b, N_HEADS // GROUP_HEADS, t // tq),
        in_specs=[qspec, qspec, _resident_spec(t), _resident_spec(t),
                  pl.BlockSpec((N_HEADS, 1, LANES), lambda b_, g, i, first: (0, 0, 0)),
                  pl.BlockSpec((4, HD_C), lambda b_, g, i, first: (0, 0)),
                  pl.BlockSpec((1, HEAD_PAD), lambda b_, g, i, first: (0, 0))],
        out_specs=qspec,
        scratch_shapes=[stat, stat, pltpu.VMEM((2 * GROUP_HEADS, tq, HEAD_PAD), F32)])
    return pl.pallas_call(
        functools.partial(_diff_kernel, tk=tk, tiles_per_iter=1, lam_init=lam_init),
        grid_spec=grid_spec,
        out_shape=jax.ShapeDtypeStruct((b, t, W_C), BF16),
        compiler_params=_params(("parallel", "parallel", "arbitrary")),
        name="diff",
    )(first, qc1_p, qc2_p, kc_b, vc_b, slopes, lam_p, g_diff)


def _gla_kernel(q_ref, k_ref, v_ref, la_ref, r_ref, s0_ref, g_ref, o_ref, s_ref, *, chunk):
    ti = pl.program_id(1)
    n_chunks = q_ref.shape[1] // chunk

    @pl.when(ti == 0)
    def _():
        s_ref[0] = s0_ref[0]

    tri_r = lax.broadcasted_iota(jnp.int32, (chunk, chunk), 0)
    tri_c = lax.broadcasted_iota(jnp.int32, (chunk, chunk), 1)
    incl = (tri_c <= tri_r).astype(F32)
    srow = lax.broadcasted_iota(jnp.int32, (N_HEADS * chunk, LANES), 0) // chunk
    slane = lax.broadcasted_iota(jnp.int32, (N_HEADS * chunk, LANES), 1) // DK_B
    qmask = srow == slane
    arow = lax.broadcasted_iota(jnp.int32, (N_HEADS * chunk, chunk), 0) % chunk
    acol = lax.broadcasted_iota(jnp.int32, (N_HEADS * chunk, chunk), 1)
    causal = acol <= arow
    vlane = lax.broadcasted_iota(jnp.int32, (chunk, W_B), 1) // DV_B
    st_row = lax.broadcasted_iota(jnp.int32, (W_B, LANES), 0) // DV_B
    st_lane = lax.broadcasted_iota(jnp.int32, (W_B, LANES), 1) // DK_B
    diag_blocks = st_row == st_lane
    gr = lax.broadcasted_iota(jnp.int32, (W_B, W_B), 0) // DV_B
    gc = lax.broadcasted_iota(jnp.int32, (W_B, W_B), 1) // DV_B
    group_mean = jnp.where(gr == gc, 1.0 / DV_B, 0.0)

    for c in range(n_chunks):
        sl = slice(c * chunk, (c + 1) * chunk)
        cb = _dot_exact_lhs(incl, la_ref[0, sl, :])
        cb_last = cb[chunk - 1:chunk, :]
        q = q_ref[0, sl, :] * (DK_B ** -0.5)
        k = k_ref[0, sl, :]
        v = v_ref[0, sl, :]
        vb = v.astype(BF16)
        qe = q * jnp.exp(cb)
        ke = (k * jnp.exp(-cb)).astype(BF16)
        kl = (k * jnp.exp(cb_last - cb)).astype(BF16)
        decay = jnp.exp(cb_last)
        qstack = jnp.where(qmask, jnp.concatenate([qe] * N_HEADS, axis=0), 0.0).astype(BF16)
        a = jnp.where(causal, _nt_dot(qstack, ke), 0.0).astype(BF16)
        oi = _dot(a, vb)
        o = jnp.zeros((chunk, W_B), F32)
        for h in range(N_HEADS):
            o = o + jnp.where(vlane == h, oi[h * chunk:(h + 1) * chunk], 0.0)
        st = s_ref[0]
        o = o + _nt_dot(qe.astype(BF16), st.astype(BF16))
        upd = _dot(v.T.astype(BF16), kl)
        s_ref[0] = decay * st + jnp.where(diag_blocks, upd, 0.0)
        ms = _dot_exact_rhs(o * o, group_mean)
        r = r_ref[0, sl, :]
        o_ref[0, sl, :] = (o * lax.rsqrt(ms + EPS) * g_ref[...] * (r * jax.nn.sigmoid(r))).astype(BF16)


def _gla_call(qb, kb, vb, loga, rb, s0_t, g_gla, tg, chunk):
    b, t, _ = qb.shape
    tok = lambda w: pl.BlockSpec((1, tg, w), lambda b_, i: (b_, i, 0))
    sspec = pl.BlockSpec((1, W_B, LANES), lambda b_, i: (b_, 0, 0))
    return pl.pallas_call(
        functools.partial(_gla_kernel, chunk=chunk),
        grid=(b, t // tg),
        in_specs=[tok(LANES), tok(LANES), tok(W_B), tok(LANES), tok(W_B), sspec,
                  pl.BlockSpec((1, W_B), lambda b_, i: (0, 0))],
        out_specs=[tok(W_B), sspec],
        out_shape=[jax.ShapeDtypeStruct((b, t, W_B), BF16), jax.ShapeDtypeStruct((b, W_B, LANES), F32)],
        compiler_params=_params(("parallel", "arbitrary")),
        name="gla",
    )(qb, kb, vb, loga, rb, s0_t, g_gla)


def _softmax_two(t_past, t_new, exp_fn):
    m = jnp.maximum(jnp.max(t_past, axis=1, keepdims=True), jnp.max(t_new, axis=1, keepdims=True))
    p_past = exp_fn(t_past - m)
    p_new = exp_fn(t_new - m)
    l = jnp.sum(p_past, axis=1, keepdims=True) + jnp.sum(p_new, axis=1, keepdims=True)
    return p_past, p_new, l


def _fox_s_kernel(q_ref, kn_ref, vn_ref, kp_ref, vp_ref, c_ref, o_ref):
    t, past = q_ref.shape[1], kp_ref.shape[1]
    q = q_ref[0]
    kn = kn_ref[0].astype(BF16)
    vn = vn_ref[0].astype(BF16)
    kp = kp_ref[0].astype(BF16)
    vp = vp_ref[0].astype(BF16)
    lane_head = lax.broadcasted_iota(jnp.int32, (t, W_A), 1) // HD_A
    row = lax.broadcasted_iota(jnp.int32, (t, t), 0)
    col = lax.broadcasted_iota(jnp.int32, (t, t), 1)
    o = jnp.zeros((t, W_A), F32)
    for h in range(N_HEADS):
        qh = jnp.where(lane_head == h, q, 0.0).astype(BF16)
        ch = c_ref[0, h:h + 1, :]
        cref = ch[:, past:past + 1]
        t_past = _nt_dot(qh, kp) + (cref - ch[:, :past])
        t_new = jnp.where(col <= row, _nt_dot(qh, kn) + (cref - ch[:, past:past + t]), NEG)
        p_past, p_new, l = _softmax_two(t_past, t_new, jnp.exp)
        oh =(_dot(p_past.astype(BF16), vp) + _dot(p_new.astype(BF16), vn)) / l
        o = o + jnp.where(lane_head == h, oh, 0.0)
    o_ref[0] = o.astype(BF16)


def _fox_s_call(qa, ka, va, cache_k, cache_v, c):
    b, t, _ = qa.shape
    past = cache_k.shape[1]
    new = lambda: pl.BlockSpec((1, t, W_A), lambda i: (i, 0, 0))
    old = lambda: pl.BlockSpec((1, past, W_A), lambda i: (i, 0, 0))
    return pl.pallas_call(
        _fox_s_kernel,
        grid=(b,),
        in_specs=[new(), new(), new(), old(), old(),
                  pl.BlockSpec((1, N_HEADS, c.shape[2]), lambda i: (i, 0, 0))],
        out_specs=new(),
        out_shape=jax.ShapeDtypeStruct((b, t, W_A), BF16),
        compiler_params=_params(("parallel",)),
        name="fox_s",
    )(qa, ka, va, cache_k, cache_v, c)


def _diff_s_kernel(q1_ref, q2_ref, kn_ref, vn_ref, kp_ref, vp_ref, slope_ref, lam_ref, g_ref, o_ref, *, lam_init):
    t, past = q1_ref.shape[1], kp_ref.shape[1]
    lam = _diff_lambda(lam_ref[...], lam_init)
    row = lax.broadcasted_iota(jnp.int32, (t, t), 0) + past
    col = lax.broadcasted_iota(jnp.int32, (t, t), 1) + past
    visible_new = lax.shift_right_logical(col, 6) <= lax.shift_right_logical(row, 6)
    dist_new = jnp.abs(row - col).astype(F32)
    prow = lax.broadcasted_iota(jnp.int32, (t, past), 0) + past
    pcol = lax.broadcasted_iota(jnp.int32, (t, past), 1)
    visible_past = lax.shift_right_logical(pcol, 6) <= lax.shift_right_logical(prow, 6)
    dist_past = (prow - pcol).astype(F32)
    for h in range(N_HEADS):
        hs = slice(h * HEAD_PAD, (h + 1) * HEAD_PAD)
        slope = slope_ref[h][:, 0:1] * LOG2E
        kn = kn_ref[0, :, hs].astype(BF16)
        vn = vn_ref[0, :, hs].astype(BF16)
        kp = kp_ref[0, :, hs].astype(BF16)
        vp = vp_ref[0, :, hs].astype(BF16)
        outs = []
        for q_ref in (q1_ref, q2_ref):
            q = q_ref[0, :, hs]
            t_past = jnp.where(visible_past, _nt_dot(q, kp) - slope * dist_past, NEG)
            t_new = jnp.where(visible_new, _nt_dot(q, kn) - slope * dist_new, NEG)
            p_past, p_new, l = _softmax_two(t_past, t_new, jnp.exp2)
            outs.append((_dot(p_past.astype(BF16), vp) + _dot(p_new.astype(BF16), vn)) / l)
        out = outs[0] - lam * outs[1]
        o_ref[0, :, hs] = (_rms(out, g_ref[...]) * (1.0 - lam_init)).astype(BF16)


def _diff_s_call(qc1_p, qc2_p, kc, vc, cache_k, cache_v, slopes, lam_p, g_diff, lam_init):
    assert CHUNK == 64
    b, t, _ = qc1_p.shape
    past = cache_k.shape[1]
    new = lambda: pl.BlockSpec((1, t, W_C), lambda i: (i, 0, 0))
    old = lambda: pl.BlockSpec((1, past, W_C), lambda i: (i, 0, 0))
    return pl.pallas_call(
        functools.partial(_diff_s_kernel, lam_init=lam_init),
        grid=(b,),
        in_specs=[new(), new(), new(), new(), old(), old(),
                  pl.BlockSpec((N_HEADS, 1, LANES), lambda i: (0, 0, 0)),
                  pl.BlockSpec((4, HD_C), lambda i: (0, 0)),
                  pl.BlockSpec((1, HEAD_PAD), lambda i: (0, 0))],
        out_specs=new(),
        out_shape=jax.ShapeDtypeStruct((b, t, W_C), BF16),
        compiler_params=_params(("parallel",)),
        name="diff_s",
    )(qc1_p, qc2_p, kc, vc, cache_k, cache_v, slopes, lam_p, g_diff)


def _out_kernel(x_ref, oa_ref, gla_ref, dif_ref, gate_ref, g2_ref, woa_ref, wog_ref, wod_ref, o_ref):
    y = _dot(oa_ref[...], woa_ref[...]) + _dot(gla_ref[...], wog_ref[...]) + _dot(dif_ref[...], wod_ref[...])
    o_ref[...] = x_ref[...] + gate_ref[0] * _rms(y, g2_ref[...])


def _out_call(x, oa, gla, dif, mod, g2, w_oa, w_og, w_od, tm, tiles_per_mod):
    n = x.shape[0]
    gate = mod[2]
    r = gate.shape[1]
    tok = lambda w: pl.BlockSpec((tm, w), lambda i: (i, 0))
    return pl.pallas_call(
        _out_kernel,
        grid=(n // tm,),
        in_specs=[tok(D_MODEL), tok(oa.shape[1]), tok(W_B), tok(W_C), _mod_spec(r, tiles_per_mod),
                  _const_spec((1, D_MODEL)), _const_spec(w_oa.shape), _const_spec(w_og.shape),
                  _const_spec(w_od.shape)],
        out_specs=tok(D_MODEL),
        out_shape=jax.ShapeDtypeStruct((n, D_MODEL), F32),
        compiler_params=_params(("parallel",)),
        name="out_proj",
    )(x, oa, gla, dif, gate, g2, w_oa, w_og, w_od)


def _pad_heads(w, hd, off=0):
    rows = w.shape[0]
    out = jnp.zeros((rows, N_HEADS, HEAD_PAD), w.dtype)
    out = out.at[:, :, off:off + hd].set(w.reshape(rows, N_HEADS, hd))
    return out.reshape(rows, W_PAD)


def _pad_lanes(w, width=LANES):
    return jnp.pad(w, ((0, 0), (0, width - w.shape[1])))


def _layer_params(l, w_ffn_in, w_ffn_out, w_in, b_f, w_gla_up, b_gla_up, g_gla, g_diff, w_out):
    offs = [int(o) for o in np.cumsum(PROJ_SIZES)[:-1]]
    qa, ka, va, fa, qb, kb, vb, gb, rb, qc, kc, vc = jnp.split(w_in[l], offs, axis=1)
    qc = qc.reshape(D_MODEL, N_HEADS, 2, HD_C)
    segs = {
        "ka": ka, "va": va, "kc": kc, "vc": vc, "qa": qa,
        "qa_p": _pad_heads(qa, HD_A), "ka_p": _pad_heads(ka, HD_A), "va_p": _pad_heads(va, HD_A),
        "qc1_p": _pad_heads(qc[:, :, 0].reshape(D_MODEL, -1), HD_C),
        "qc2_p": _pad_heads(qc[:, :, 1].reshape(D_MODEL, -1), HD_C, off=HD_C),
        "fa": _pad_lanes(fa), "qb": qb, "kb": kb, "vb": vb, "rb": rb, "gb": _pad_lanes(gb),
    }
    w_ext = jnp.concatenate([segs[n] for n, _ in _SEG_WIDTHS], axis=1).astype(BF16)
    wo = w_out[l]
    vone = np.zeros((1, W_PAD), np.float32)
    vone[0, HD_A::HEAD_PAD] = 1.0
    return dict(
        w_ffn_in=w_ffn_in[l].astype(BF16), w_ffn_out=w_ffn_out[l].astype(BF16), w_ext=w_ext,
        bf=_pad_lanes(b_f[l][None, :]),
        wup=jnp.pad(w_gla_up[l], ((0, LANES - GLA_RANK), (0, 0))).astype(BF16),
        bup=b_gla_up[l][None, :],
        vone=jnp.asarray(vone),
        g_gla=jnp.tile(g_gla[l], N_HEADS)[None, :], g_diff=g_diff[l][None, :],
        w_oa=wo[:W_A].astype(BF16),
        w_oa_p=_pad_heads(wo[:W_A].T, HD_A).T.astype(BF16),
        w_og=wo[W_A:W_A + W_B].astype(BF16), w_od=wo[W_A + W_B:].astype(BF16),
    )


def _state_to_blockdiag_t(s):
    eye = jnp.eye(N_HEADS, dtype=s.dtype)
    return jnp.einsum("bhkv,hg->bhvgk", s, eye).reshape(s.shape[0], W_B, N_HEADS * DK_B)


def _blockdiag_t_to_state(st):
    b = st.shape[0]
    s5 = st.reshape(b, N_HEADS, DV_B, N_HEADS, DK_B)
    diag = jnp.stack([s5[:, h, :, h, :] for h in range(N_HEADS)], axis=1)
    return jnp.swapaxes(diag, 2, 3)


SKIP_LOG2 = 160.0
NORM_SLACK = 1.01


def _tile_max(x, factor):
    return jnp.max(x.reshape(x.shape[:-1] + (x.shape[-1] // factor, factor)), axis=-1)


def _first_tiles(qn, kn, gap, tq, tk):
    b, h, nq = qn.shape
    r = tq // tk
    kn_diag = _tile_max(kn, r)
    bound = NORM_SLACK * (qn[..., None] * kn[:, :, None, :] + (qn * kn_diag)[..., None]) + gap
    dead = (bound < -SKIP_LOG2).reshape(b, h // GROUP_HEADS, GROUP_HEADS, nq, kn.shape[-1]).all(axis=2)
    lead = jnp.sum(jnp.cumprod(dead.astype(jnp.int32), axis=-1), axis=-1)
    return jnp.minimum(lead, jnp.arange(nq, dtype=jnp.int32) * r).astype(jnp.int32)


def _tile(n, pref):
    t = min(n, pref)
    assert n % t == 0, (n, t)
    return t


def _trunk(x, mods, params, g_norm, lam_params, mixer, tm, tiles_per_mod):
    b, t, _ = x.shape
    n = b * t
    xf = x.reshape(n, D_MODEL)
    states = []
    for l in range(len(params)):
        p = params[l]
        gn = lambda k: g_norm[l, k][None, :]
        xf = _ffn_call(xf, mods[l][0], gn(0), gn(1), p["w_ffn_in"][0], p["w_ffn_out"][0], 0.5, tm, tiles_per_mod)
        pr = _proj_call(xf, mods[l][1], gn(2), p["w_ext"], p["bf"], p["wup"], p["bup"], p["vone"], tm, tiles_per_mod)
        lam_init = 0.8 - 0.6 * math.exp(-0.3 * l)
        oa, gla, dif, s_fin, w_oa = mixer(l, pr, p, lam_params[l], lam_init, b, t)
        xf = _out_call(xf, oa, gla, dif, mods[l][1], gn(3), w_oa, p["w_og"], p["w_od"], tm, tiles_per_mod)
        xf = _ffn_call(xf, mods[l][2], gn(4), gn(5), p["w_ffn_in"][1], p["w_ffn_out"][1], 0.5, tm, tiles_per_mod)
        shp = lambda a, *tail: a.reshape((b, t) + tail)
        states.append((shp(pr["ka"], N_HEADS, HD_A), shp(pr["va"], N_HEADS, HD_A), shp(pr["logf"], N_HEADS), s_fin,
                       shp(pr["kc"], N_HEADS, 2 * HD_C), shp(pr["vc"], N_HEADS, 2 * HD_C)))
    stacked = [jnp.stack([st[i] for st in states]) for i in range(6)]
    return xf.reshape(b, t, D_MODEL), stacked


def kernel(x_prompt, x_sample, c_prompt, c_sample, cache_fox_k, cache_fox_v, cache_fox_logf, state_gla,
           cache_diff_k, cache_diff_v, w_ada, b_ada, g_norm, w_ffn_in, w_ffn_out, w_in, b_f, w_gla_up,
           b_gla_up, g_gla, g_diff, lam_params, w_out):
    depth = w_in.shape[0]
    bp, tp, _ = x_prompt.shape
    bs, ts, _ = x_sample.shape
    past = cache_fox_k.shape[2]

    params = [_layer_params(l, w_ffn_in, w_ffn_out, w_in, b_f, w_gla_up, b_gla_up, g_gla, g_diff, w_out)
              for l in range(depth)]

    slopes = 2.0 ** (-8.0 * jnp.arange(1, N_HEADS + 1, dtype=F32) / N_HEADS)
    slopes = jnp.broadcast_to(slopes[:, None, None], (N_HEADS, 1, LANES))

    rows = bp + bs
    rows_pad = -(-rows // 8) * 8
    c_all = jnp.pad(jnp.concatenate([c_prompt, c_sample], axis=0), ((0, rows_pad - rows), (0, 0)))
    mods_p, mods_s = [], []
    for l in range(depth):
        mod = _ada_call(c_all, w_ada[l].astype(BF16), b_ada[l][None, :]).reshape(rows_pad, N_SUB, 3, D_MODEL)
        mods_p.append([[mod[:bp, s, k][:, None, :] for k in range(3)] for s in range(N_SUB)])
        mods_s.append([[jnp.repeat(mod[bp:rows, s, k], ts, axis=0)[None] for k in range(3)] for s in range(N_SUB)])

    tm_p = _tile(tp, 512)
    tq = _tile(tp, 1024)
    tk = _tile(tp, 512)
    tg = _tile(tp, 512)

    def prompt_mixer(l, pr, p, lam_p, lam_init, b, t):
        r3 = lambda a: a.reshape(b, t, a.shape[-1])
        logf_t = jnp.swapaxes(r3(pr["logf"]), 1, 2).reshape(b * N_HEADS, t // LANES, LANES)
        c = _cumsum_call(logf_t).reshape(b, N_HEADS, t)
        nrm = jnp.sqrt(pr["norms"][:, 0, :N_HEADS * len(_NORM_ROWS)])
        nrm = dict(zip(_NORM_ROWS, jnp.moveaxis(nrm.reshape(b, t // tm_p, len(_NORM_ROWS), N_HEADS), (2, 3), (0, 2))))
        q_tiles = lambda a: _tile_max(a, tq // tm_p)
        k_tiles = lambda a: _tile_max(a, tk // tm_p)
        gap_a = (c[:, :, ::tq][..., None] - c[:, :, tk - 1::tk][:, :, None, :]) * LOG2E
        first_a = _first_tiles(q_tiles(nrm["qa_p"]), k_tiles(nrm["ka_p"]), gap_a, tq, tk)
        oa = _fox_call(first_a, r3(pr["qa_p"]), r3(pr["ka_p"]), r3(pr["va_p"]), c, tq, tk)
        s0_t = jnp.zeros((b, W_B, N_HEADS * DK_B), F32)
        gla, s_t = _gla_call(r3(pr["qb"]), r3(pr["kb"]), r3(pr["vb"]), r3(pr["loga"]), r3(pr["rb"]), s0_t,
                             p["g_gla"], tg, min(CHUNK, t))
        last_key = jnp.arange(tk - 1, t, tk, dtype=F32)[None, :] - jnp.arange(0, t, tq, dtype=F32)[:, None]
        gap_c = slopes[:, 0, :1, None] * LOG2E * last_key[None]
        first_c = _first_tiles(q_tiles(jnp.maximum(nrm["qc1_p"], nrm["qc2_p"])), k_tiles(nrm["kc_b"]), gap_c, tq, tk)
        dif = _diff_call(first_c, r3(pr["qc1_p"]), r3(pr["qc2_p"]), r3(pr["kc_b"]), r3(pr["vc_b"]), slopes, lam_p,
                         p["g_diff"], lam_init, tq, tk)
        flat = lambda a: a.reshape(b * t, a.shape[-1])
        return flat(oa), flat(gla), flat(dif), _blockdiag_t_to_state(s_t), p["w_oa_p"]

    def sample_mixer(l, pr, p, lam_p, lam_init, b, t):
        r3 = lambda a: a.reshape(b, t, a.shape[-1])
        total = past + t
        nb = -(-total // (8 * LANES)) * 8
        lf = jnp.concatenate([cache_fox_logf[l].astype(F32), r3(pr["logf"])], axis=1)
        lf = jnp.pad(jnp.swapaxes(lf, 1, 2), ((0, 0), (0, 0), (0, nb * LANES - total)))
        c = _cumsum_call(lf.reshape(b * N_HEADS, nb, LANES)).reshape(b, N_HEADS, nb * LANES)
        oa = _fox_s_call(r3(pr["qa"]), r3(pr["ka"]), r3(pr["va"]),
                         cache_fox_k[l].reshape(b, past, W_A), cache_fox_v[l].reshape(b, past, W_A), c)
        gla, s_t = _gla_call(r3(pr["qb"]), r3(pr["kb"]), r3(pr["vb"]), r3(pr["loga"]), r3(pr["rb"]),
                             _state_to_blockdiag_t(state_gla[l].astype(F32)), p["g_gla"], t, min(CHUNK, t))
        dif = _diff_s_call(r3(pr["qc1_p"]), r3(pr["qc2_p"]), r3(pr["kc"]), r3(pr["vc"]),
                           cache_diff_k[l].reshape(b, past, W_C), cache_diff_v[l].reshape(b, past, W_C),
                           slopes, lam_p, p["g_diff"], lam_init)
        flat = lambda a: a.reshape(b * t, a.shape[-1])
        return flat(oa), flat(gla), flat(dif), _blockdiag_t_to_state(s_t), p["w_oa"]

    y_p, sp = _trunk(x_prompt, mods_p, params, g_norm, lam_params, prompt_mixer, tm_p, tp // tm_p)
    y_s, ss = _trunk(x_sample, mods_s, params, g_norm, lam_params, sample_mixer, bs * ts, 1)
    return (y_p, y_s, sp[0], sp[1], sp[2], sp[3], sp[4], sp[5], ss[0], ss[1], ss[2], ss[3], ss[4], ss[5])
```

```python
import functools
import math

import jax
import jax.numpy as jnp
import numpy as np
from jax import lax
from jax.experimental import pallas as pl
from jax.experimental.pallas import tpu as pltpu

F32 = jnp.float32
BF16 = jnp.bfloat16

D_MODEL = 1024
N_HEADS = 4
HD_A = 64
DK_B = 32
DV_B = 64
GLA_RANK = 16
GLA_TAU = 16.0
HD_C = 64
CHUNK = 64
W_A = N_HEADS * HD_A
W_B = N_HEADS * DV_B
W_C = N_HEADS * 2 * HD_C
D_FF = ((8 * D_MODEL // 3 + 255) // 256) * 256
N_SUB = 3
PROJ_SIZES = (W_A, W_A, W_A, N_HEADS, N_HEADS * DK_B, N_HEADS * DK_B, W_B, GLA_RANK, W_B, W_C, W_C, W_C)
NEG = -1e30
EPS = 1e-6
LOG2E = math.log2(math.e)

LANES = 128
HEAD_PAD = 128
W_PAD = N_HEADS * HEAD_PAD
VMEM_LIMIT = 56 * 1024 * 1024

_SEG_WIDTHS = (
    ("ka", W_A), ("va", W_A), ("kc", W_C), ("vc", W_C), ("qa", W_A),
    ("qa_p", W_PAD), ("ka_p", W_PAD), ("va_p", W_PAD), ("qc1_p", W_PAD), ("qc2_p", W_PAD),
    ("fa", LANES), ("qb", LANES), ("kb", LANES), ("vb", W_B), ("rb", W_B), ("gb", LANES),
)
_SEG = {}
_off = 0
for _name, _w in _SEG_WIDTHS:
    _SEG[_name] = (_off, _off + _w)
    _off += _w
N_EXT = _off


def _nt_dot(a, b):
    return lax.dot_general(a, b, (((1,), (1,)), ((), ())), preferred_element_type=F32)


def _dot(a, b):
    return jnp.dot(a, b, preferred_element_type=F32)


def _split3(x):
    hi = x.astype(BF16)
    r1 = x - hi.astype(F32)
    mid = r1.astype(BF16)
    lo = (r1 - mid.astype(F32)).astype(BF16)
    return hi, mid, lo


def _dot_exact_rhs(x, m):
    mb = m.astype(BF16)
    hi, mid, lo = _split3(x)
    return _dot(hi, mb) + _dot(mid, mb) + _dot(lo, mb)


def _dot_exact_lhs(m, x):
    mb = m.astype(BF16)
    hi, mid, lo = _split3(x)
    return _dot(mb, hi) + _dot(mb, mid) + _dot(mb, lo)


def _rms(x, g):
    return x * lax.rsqrt(jnp.mean(x * x, axis=-1, keepdims=True) + EPS) * g


def _log_sigmoid(x):
    return jnp.minimum(x, 0.0) - jnp.log1p(jnp.exp(-jnp.abs(x)))


def _const_spec(shape):
    nd = len(shape)
    return pl.BlockSpec(shape, lambda *_: (0,) * nd, pipeline_mode=pl.Buffered(1))


def _params(sem):
    return pltpu.CompilerParams(dimension_semantics=sem, vmem_limit_bytes=VMEM_LIMIT)


def _ada_kernel(c_ref, w_ref, b_ref, o_ref):
    c = c_ref[...]
    a = (c * jax.nn.sigmoid(c)).astype(BF16)
    o_ref[...] = _dot(a, w_ref[...]) + b_ref[...]


def _ada_call(c, w, b):
    m, n = c.shape[0], w.shape[1]
    tn = n // 8
    return pl.pallas_call(
        _ada_kernel,
        grid=(n // tn,),
        in_specs=[pl.BlockSpec((m, D_MODEL), lambda i: (0, 0)),
                  pl.BlockSpec((D_MODEL, tn), lambda i: (0, i)),
                  pl.BlockSpec((1, tn), lambda i: (0, i))],
        out_specs=pl.BlockSpec((m, tn), lambda i: (0, i)),
        out_shape=jax.ShapeDtypeStruct((m, n), F32),
        compiler_params=_params(("parallel",)),
        name="ada",
    )(c, w, b)


FF_CHUNK = 256


def _ffn_kernel(x_ref, shift_ref, scale_ref, gate_ref, g1_ref, g2_ref, win_ref, wout_ref, o_ref, *, res_w):
    x = x_ref[...]
    h = _rms(x, g1_ref[...]) * (1.0 + scale_ref[0]) + shift_ref[0]
    hb = h.astype(BF16)
    y = jnp.zeros(x.shape, F32)
    for c in range(D_FF // FF_CHUNK):
        lo, hi = c * FF_CHUNK, (c + 1) * FF_CHUNK
        g = _dot(hb, win_ref[:, lo:hi])
        u = _dot(hb, win_ref[:, D_FF + lo:D_FF + hi])
        a = (g * jax.nn.sigmoid(g) * u).astype(BF16)
        y = y + _dot(a, wout_ref[lo:hi, :])
    o_ref[...] = x + res_w * gate_ref[0] * _rms(y, g2_ref[...])


def _mod_spec(r, tiles_per_mod):
    return pl.BlockSpec((1, r, D_MODEL), lambda i: (i // tiles_per_mod, 0, 0))


def _ffn_call(x, mod, g1, g2, w_in, w_out, layer, which, res_w, tm, tiles_per_mod):
    n = x.shape[0]
    shift, scale, gate = mod
    r = shift.shape[1]
    tok = pl.BlockSpec((tm, D_MODEL), lambda i: (i, 0))
    pick = lambda rows, cols: pl.BlockSpec((None, None, rows, cols), lambda i: (layer, which, 0, 0),
                                           pipeline_mode=pl.Buffered(1))
    return pl.pallas_call(
        functools.partial(_ffn_kernel, res_w=res_w),
        grid=(n // tm,),
        in_specs=[tok, _mod_spec(r, tiles_per_mod), _mod_spec(r, tiles_per_mod), _mod_spec(r, tiles_per_mod),
                  _const_spec((1, D_MODEL)), _const_spec((1, D_MODEL)),
                  pick(D_MODEL, 2 * D_FF), pick(D_FF, D_MODEL)],
        out_specs=tok,
        out_shape=jax.ShapeDtypeStruct((n, D_MODEL), F32),
        compiler_params=_params(("parallel",)),
        name="ffn",
    )(x, shift, scale, gate, g1, g2, w_in, w_out)


_STACKED_OUTS = (
    ("ka", 1, W_A), ("va", 1, W_A), ("kc", N_HEADS, HEAD_PAD), ("vc", N_HEADS, HEAD_PAD), ("logf", 1, N_HEADS),
)
_LAYER_OUTS = (
    ("qa_p", W_PAD, BF16), ("ka_p", W_PAD, BF16), ("va_p", W_PAD, BF16),
    ("qc1_p", W_PAD, BF16), ("qc2_p", W_PAD, BF16), ("kc_b", W_C, BF16), ("vc_b", W_C, BF16),
    ("qb", LANES, F32), ("kb", LANES, F32), ("vb", W_B, F32), ("loga", LANES, F32), ("rb", W_B, F32),
)
_NORM_ROWS = ("qa_p", "ka_p", "qc1_p", "qc2_p", "kc_b")


def _proj_kernel(x_ref, shift_ref, scale_ref, g1_ref, w_ref, wkvt_ref, bf_ref, wup_ref, bup_ref, vone_ref, *refs,
                 n_alias, with_qa, transposed_a):
    refs = refs[n_alias:]
    names = [n for n, _, _ in _STACKED_OUTS] + [n for n, _, _ in _LAYER_OUTS] + (["qa"] if with_qa else [])
    o = dict(zip(names + ["norms"], refs))
    tm = x_ref.shape[0]
    x = x_ref[...]
    hb = (_rms(x, g1_ref[...]) * (1.0 + scale_ref[0]) + shift_ref[0]).astype(BF16)

    def seg(name):
        lo, hi = _SEG[name]
        return _dot(hb, w_ref[:, lo:hi])

    def store_heads(ref, val):
        for h in range(N_HEADS):
            ref[0, pl.ds(h, tm, stride=N_HEADS), :] = val[:, h * HEAD_PAD:(h + 1) * HEAD_PAD]

    if transposed_a:
        kvt = _nt_dot(wkvt_ref[...], hb)
        o["ka"][0, 0] = kvt[:W_A]
        o["va"][0, 0] = kvt[W_A:]
    else:
        o["ka"][0] = seg("ka")
        o["va"][0] = seg("va")
    kc = seg("kc")
    store_heads(o["kc"], kc)
    rounded = {"kc_b": kc.astype(BF16)}
    o["kc_b"][...] = rounded["kc_b"]
    vc = seg("vc")
    store_heads(o["vc"], vc)
    o["vc_b"][...] = vc.astype(BF16)
    if with_qa:
        o["qa"][...] = seg("qa") * (HD_A ** -0.5)
    rounded["qa_p"] = (seg("qa_p") * (HD_A ** -0.5 * LOG2E)).astype(BF16)
    rounded["ka_p"] = seg("ka_p").astype(BF16)
    o["qa_p"][...] = rounded["qa_p"]
    o["ka_p"][...] = rounded["ka_p"]
    o["va_p"][...] = (seg("va_p") + vone_ref[...]).astype(BF16)
    rounded["qc1_p"] = (seg("qc1_p") * (HD_C ** -0.5 * LOG2E)).astype(BF16)
    rounded["qc2_p"] = (seg("qc2_p") * (HD_C ** -0.5 * LOG2E)).astype(BF16)
    o["qc1_p"][...] = rounded["qc1_p"]
    o["qc2_p"][...] = rounded["qc2_p"]
    fa = seg("fa") + bf_ref[...]
    o["logf"][0] = _log_sigmoid(fa)[:, :N_HEADS]
    o["qb"][...] = seg("qb")
    o["kb"][...] = seg("kb")
    o["vb"][...] = seg("vb")
    o["rb"][...] = seg("rb")
    gb = seg("gb").astype(BF16)
    o["loga"][...] = _log_sigmoid(_dot(gb, wup_ref[...]) + bup_ref[...]) * (1.0 / GLA_TAU)
    lane = lax.broadcasted_iota(jnp.int32, (8, LANES), 1)
    tile_norms = jnp.zeros((8, LANES), F32)
    for a, name in enumerate(_NORM_ROWS):
        sq = rounded[name].astype(F32)
        sq = sq * sq
        for h in range(N_HEADS):
            ss = jnp.sum(sq[:, h * HEAD_PAD:(h + 1) * HEAD_PAD], axis=1, keepdims=True)
            tile_norms = jnp.where(lane == N_HEADS * a + h, jnp.max(ss, axis=0, keepdims=True), tile_norms)
    o["norms"][0] = tile_norms


def _proj_call(x, mod, g1, w_ext, w_kvt, bf, wup, bup, vone, tm, tiles_per_mod, layer, depth, stacked, with_qa,
               seq_len):
    n = x.shape[0]
    shift, scale, _ = mod
    r = shift.shape[1]
    tok = lambda w: pl.BlockSpec((tm, w), lambda i: (i, 0))
    layer_outs = _LAYER_OUTS + ((("qa", W_A, F32),) if with_qa else ())
    alias_in = [] if stacked is None else [stacked[nm] for nm, _, _ in _STACKED_OUTS]
    transposed_a = seq_len % tm == 0
    tps = max(seq_len // tm, 1)
    stacked_specs, stacked_shapes = [], []
    for nm, rows, w in _STACKED_OUTS:
        if transposed_a and nm in ("ka", "va"):
            stacked_specs.append(pl.BlockSpec((1, 1, w, tm), lambda i: (layer, i // tps, 0, i % tps)))
            stacked_shapes.append(jax.ShapeDtypeStruct((depth, n // seq_len, w, seq_len), F32))
        else:
            stacked_specs.append(pl.BlockSpec((1, tm * rows, w), lambda i: (layer, i, 0)))
            stacked_shapes.append(jax.ShapeDtypeStruct((depth, n * rows, w), F32))
    n_fixed = 10
    outs = pl.pallas_call(
        functools.partial(_proj_kernel, n_alias=len(alias_in), with_qa=with_qa, transposed_a=transposed_a),
        grid=(n // tm,),
        in_specs=[tok(D_MODEL), _mod_spec(r, tiles_per_mod), _mod_spec(r, tiles_per_mod),
                  _const_spec((1, D_MODEL)), _const_spec((D_MODEL, N_EXT)), _const_spec((2 * W_A, D_MODEL)),
                  _const_spec((1, LANES)), _const_spec((LANES, LANES)), _const_spec((1, LANES)),
                  _const_spec((1, W_PAD))]
        + [pl.BlockSpec(memory_space=pl.ANY)] * len(alias_in),
        out_specs=stacked_specs + [tok(w) for _, w, _ in layer_outs]
        + [pl.BlockSpec((1, 8, LANES), lambda i: (i, 0, 0))],
        out_shape=stacked_shapes + [jax.ShapeDtypeStruct((n, w), dt) for _, w, dt in layer_outs]
        + [jax.ShapeDtypeStruct((n // tm, 8, LANES), F32)],
        input_output_aliases={n_fixed + k: k for k in range(len(alias_in))},
        compiler_params=_params(("parallel",)),
        name="proj",
    )(x, shift, scale, g1, w_ext, w_kvt, bf, wup, bup, vone, *alias_in)
    names = [nm for nm, _, _ in _STACKED_OUTS] + [nm for nm, _, _ in layer_outs] + ["norms"]
    return dict(zip(names, outs))


def _cumsum_kernel(x_ref, o_ref):
    x = x_ref[0]
    nb = x.shape[0]
    r = lax.broadcasted_iota(jnp.int32, (LANES, LANES), 0)
    c = lax.broadcasted_iota(jnp.int32, (LANES, LANES), 1)
    local = _dot_exact_rhs(x, (r <= c).astype(F32))
    tot = jnp.broadcast_to(local[:, LANES - 1:LANES], (nb, LANES))
    rr = lax.broadcasted_iota(jnp.int32, (nb, nb), 0)
    cc = lax.broadcasted_iota(jnp.int32, (nb, nb), 1)
    o_ref[0] = local + _dot_exact_lhs((cc < rr).astype(F32), tot)


def _cumsum_call(x):
    rows, nb, _ = x.shape
    spec = pl.BlockSpec((1, nb, LANES), lambda i: (i, 0, 0))
    return pl.pallas_call(
        _cumsum_kernel, grid=(rows,), in_specs=[spec], out_specs=spec,
        out_shape=jax.ShapeDtypeStruct(x.shape, F32),
        compiler_params=_params(("parallel",)), name="cumsum",
    )(x)


def _lane_tile(x, width):
    return jnp.concatenate([x] * (width // LANES), axis=1)


GROUP_HEADS = 2
W_GROUP = GROUP_HEADS * HEAD_PAD


def _causal_sweep(i, first, tq, tk, tiles_per_iter, tile):
    r = tq // tk

    def body(jj, carry):
        for u in range(tiles_per_iter):
            tile(jj * tiles_per_iter + u, slice(0, tq), None)
        return carry

    assert r % tiles_per_iter == 0
    lax.fori_loop(lax.div(first, tiles_per_iter), lax.div(i * r, tiles_per_iter), body, 0)
    for d in range(r):
        tile(i * r + d, slice(d * tk, (d + 1) * tk), d * tk)
        if d + 1 < r:
            tile(i * r + d, slice((d + 1) * tk, tq), None)


def _fox_kernel(first_ref, q_ref, k_ref, v_ref, c_ref, o_ref, m_sc, acc_sc, *, tk, tiles_per_iter):
    i = pl.program_id(2)
    first = first_ref[pl.program_id(0), pl.program_id(1), i]
    tq = q_ref.shape[1]
    m_sc[...] = jnp.full(m_sc.shape, NEG, F32)
    acc_sc[...] = jnp.zeros(acc_sc.shape, F32)
    row = lax.broadcasted_iota(jnp.int32, (tk, tk), 0)
    col = lax.broadcasted_iota(jnp.int32, (tk, tk), 1)
    causal = col <= row

    def tile(j, rows, local):
        ks = pl.ds(pl.multiple_of(j * tk, tk), tk)
        for h in range(GROUP_HEADS):
            hs = slice(h * HEAD_PAD, (h + 1) * HEAD_PAD)
            c_first = c_ref[0, h, pl.ds(i * (tq // tk), 1), :][:, 0:1]
            t = _nt_dot(q_ref[0, rows, hs], k_ref[0, ks, hs]) + (c_first - c_ref[0, h, pl.ds(j, 1), :]) * LOG2E
            if local is not None:
                t = jnp.where(causal, t, NEG)
            m_old = m_sc[h, rows, :]
            m_new = jnp.maximum(m_old, jnp.max(t, axis=1, keepdims=True))
            p = jnp.exp2(t - _lane_tile(m_new, tk)).astype(BF16)
            acc_sc[h, rows, :] = jnp.exp2(m_old - m_new) * acc_sc[h, rows, :] + _dot(p, v_ref[0, ks, hs])
            m_sc[h, rows, :] = m_new

    _causal_sweep(i, first, tq, tk, tiles_per_iter, tile)
    for h in range(GROUP_HEADS):
        acc = acc_sc[h]
        o_ref[0, :, h * HEAD_PAD:(h + 1) * HEAD_PAD] = (acc / acc[:, HD_A:HD_A + 1]).astype(BF16)


def _resident_spec(t):
    return pl.BlockSpec((1, t, W_GROUP), lambda b_, g, i, first: (b_, 0, g), pipeline_mode=pl.Buffered(1))


def _fox_call(first, qa_p, ka_p, va_p, c, tq, tk):
    b, t, _ = qa_p.shape
    qspec = pl.BlockSpec((1, tq, W_GROUP), lambda b_, g, i, first: (b_, i, g))
    grid_spec = pltpu.PrefetchScalarGridSpec(
        num_scalar_prefetch=1,
        grid=(b, N_HEADS // GROUP_HEADS, t // tq),
        in_specs=[qspec, _resident_spec(t), _resident_spec(t),
                  pl.BlockSpec((1, GROUP_HEADS, t // tk, tk), lambda b_, g, i, first: (b_, g, 0, 0))],
        out_specs=qspec,
        scratch_shapes=[pltpu.VMEM((GROUP_HEADS, tq, LANES), F32), pltpu.VMEM((GROUP_HEADS, tq, HEAD_PAD), F32)])
    return pl.pallas_call(
        functools.partial(_fox_kernel, tk=tk, tiles_per_iter=tq // tk),
        grid_spec=grid_spec,
        out_shape=jax.ShapeDtypeStruct((b, t, W_PAD), BF16),
        compiler_params=_params(("parallel", "parallel", "arbitrary")),
        name="fox",
    )(first, qa_p, ka_p, va_p, c.reshape(b, N_HEADS, t // tk, tk))


def _diff_lambda(lp, lam_init):
    a = jnp.sum(lp[0:1] * lp[1:2], axis=1, keepdims=True)
    b = jnp.sum(lp[2:3] * lp[3:4], axis=1, keepdims=True)
    return jnp.exp(a) - jnp.exp(b) + lam_init


def _diff_kernel(first_ref, q1_ref, q2_ref, k_ref, v_ref, slope_ref, lam_ref, g_ref, o_ref, m_sc, l_sc, a_sc, *,
                 tk, tiles_per_iter, lam_init):
    g = pl.program_id(1)
    i = pl.program_id(2)
    first = first_ref[pl.program_id(0), g, i]
    tq = q1_ref.shape[1]
    m_sc[...] = jnp.full(m_sc.shape, NEG, F32)
    l_sc[...] = jnp.zeros(l_sc.shape, F32)
    a_sc[...] = jnp.zeros(a_sc.shape, F32)
    row = lax.broadcasted_iota(jnp.int32, (tk, tk), 0)
    col = lax.broadcasted_iota(jnp.int32, (tk, tk), 1)
    rel_diag = (2 * jnp.minimum(row, col) - col).astype(F32)
    visible = lax.shift_right_logical(col, 6) <= lax.shift_right_logical(row, 6)
    lane = lax.broadcasted_iota(jnp.int32, (1, tk), 1)

    def tile(j, rows, local):
        ks = pl.ds(pl.multiple_of(j * tk, tk), tk)
        if local is None:
            rel = (lane + (j * tk - i * tq)).astype(F32)
        else:
            rel = rel_diag + float(local)
        for h in range(GROUP_HEADS):
            hs = slice(h * HEAD_PAD, (h + 1) * HEAD_PAD)
            bias = (slope_ref[g * GROUP_HEADS + h][:, 0:1] * LOG2E) * rel
            if local is not None:
                bias = jnp.where(visible, bias, NEG)
            k = k_ref[0, ks, hs]
            v = v_ref[0, ks, hs]
            for m, q_ref in enumerate((q1_ref, q2_ref)):
                n = 2 * h + m
                t = _nt_dot(q_ref[0, rows, hs], k) + bias
                m_old = m_sc[n, rows, :]
                m_new = jnp.maximum(m_old, jnp.max(t, axis=1, keepdims=True))
                alpha = jnp.exp2(m_old - m_new)
                p = jnp.exp2(t - _lane_tile(m_new, tk))
                l_sc[n, rows, :] = alpha * l_sc[n, rows, :] + jnp.sum(p, axis=1, keepdims=True)
                a_sc[n, rows, :] = alpha * a_sc[n, rows, :] + _dot(p.astype(BF16), v)
                m_sc[n, rows, :] = m_new

    _causal_sweep(i, first, tq, tk, tiles_per_iter, tile)
    lam = _diff_lambda(lam_ref[...], lam_init)
    for h in range(GROUP_HEADS):
        out = a_sc[2 * h] / l_sc[2 * h] - lam * (a_sc[2 * h + 1] / l_sc[2 * h + 1])
        o_ref[0, :, h * HEAD_PAD:(h + 1) * HEAD_PAD] = (_rms(out, g_ref[...]) * (1.0 - lam_init)).astype(BF16)


def _diff_call(first, qc1_p, qc2_p, kc_b, vc_b, slopes, lam_p, g_diff, lam_init, tq, tk):
    assert CHUNK == 64 and tk % CHUNK == 0
    b, t, _ = qc1_p.shape
    qspec = pl.BlockSpec((1, tq, W_GROUP), lambda b_, g, i, first: (b_, i, g))
    stat = pltpu.VMEM((2 * GROUP_HEADS, tq, LANES), F32)
    grid_spec = pltpu.PrefetchScalarGridSpec(
        num_scalar_prefetch=1,
        grid=(b, N_HEADS // GROUP_HEADS, t // tq),
        in_specs=[qspec, qspec, _resident_spec(t), _resident_spec(t),
                  pl.BlockSpec((N_HEADS, 1, LANES), lambda b_, g, i, first: (0, 0, 0)),
                  pl.BlockSpec((4, HD_C), lambda b_, g, i, first: (0, 0)),
                  pl.BlockSpec((1, HEAD_PAD), lambda b_, g, i, first: (0, 0))],
        out_specs=qspec,
        scratch_shapes=[stat, stat, pltpu.VMEM((2 * GROUP_HEADS, tq, HEAD_PAD), F32)])
    return pl.pallas_call(
        functools.partial(_diff_kernel, tk=tk, tiles_per_iter=1, lam_init=lam_init),
        grid_spec=grid_spec,
        out_shape=jax.ShapeDtypeStruct((b, t, W_C), BF16),
        compiler_params=_params(("parallel", "parallel", "arbitrary")),
        name="diff",
    )(first, qc1_p, qc2_p, kc_b, vc_b, slopes, lam_p, g_diff)


def _gla_kernel(q_ref, k_ref, v_ref, la_ref, r_ref, s0_ref, g_ref, o_ref, s_ref, *, chunk):
    ti = pl.program_id(1)
    n_chunks = q_ref.shape[1] // chunk

    @pl.when(ti == 0)
    def _():
        s_ref[0] = s0_ref[0]

    tri_r = lax.broadcasted_iota(jnp.int32, (chunk, chunk), 0)
    tri_c = lax.broadcasted_iota(jnp.int32, (chunk, chunk), 1)
    incl = (tri_c <= tri_r).astype(F32)
    srow = lax.broadcasted_iota(jnp.int32, (N_HEADS * chunk, LANES), 0) // chunk
    slane = lax.broadcasted_iota(jnp.int32, (N_HEADS * chunk, LANES), 1) // DK_B
    qmask = srow == slane
    arow = lax.broadcasted_iota(jnp.int32, (N_HEADS * chunk, chunk), 0) % chunk
    acol = lax.broadcasted_iota(jnp.int32, (N_HEADS * chunk, chunk), 1)
    causal = acol <= arow
    vlane = lax.broadcasted_iota(jnp.int32, (chunk, W_B), 1) // DV_B
    st_row = lax.broadcasted_iota(jnp.int32, (W_B, LANES), 0) // DV_B
    st_lane = lax.broadcasted_iota(jnp.int32, (W_B, LANES), 1) // DK_B
    diag_blocks = st_row == st_lane
    gr = lax.broadcasted_iota(jnp.int32, (W_B, W_B), 0) // DV_B
    gc = lax.broadcasted_iota(jnp.int32, (W_B, W_B), 1) // DV_B
    group_mean = jnp.where(gr == gc, 1.0 / DV_B, 0.0)

    for c in range(n_chunks):
        sl = slice(c * chunk, (c + 1) * chunk)
        cb = _dot_exact_lhs(incl, la_ref[0, sl, :])
        cb_last = cb[chunk - 1:chunk, :]
        q = q_ref[0, sl, :] * (DK_B ** -0.5)
        k = k_ref[0, sl, :]
        v = v_ref[0, sl, :]
        vb = v.astype(BF16)
        qe = q * jnp.exp(cb)
        ke = (k * jnp.exp(-cb)).astype(BF16)
        kl = (k * jnp.exp(cb_last - cb)).astype(BF16)
        decay = jnp.exp(cb_last)
        qstack = jnp.where(qmask, jnp.concatenate([qe] * N_HEADS, axis=0), 0.0).astype(BF16)
        a = jnp.where(causal, _nt_dot(qstack, ke), 0.0).astype(BF16)
        oi = _dot(a, vb)
        o = jnp.zeros((chunk, W_B), F32)
        for h in range(N_HEADS):
            o = o + jnp.where(vlane == h, oi[h * chunk:(h + 1) * chunk], 0.0)
        st = s_ref[0]
        o = o + _nt_dot(qe.astype(BF16), st.astype(BF16))
        upd = _dot(v.T.astype(BF16), kl)
        s_ref[0] = decay * st + jnp.where(diag_blocks, upd, 0.0)
        ms = _dot_exact_rhs(o * o, group_mean)
        r = r_ref[0, sl, :]
        o_ref[0, sl, :] = (o * lax.rsqrt(ms + EPS) * g_ref[...] * (r * jax.nn.sigmoid(r))).astype(BF16)


def _gla_call(qb, kb, vb, loga, rb, s0_t, g_gla, tg, chunk):
    b, t, _ = qb.shape
    tok = lambda w: pl.BlockSpec((1, tg, w), lambda b_, i: (b_, i, 0))
    sspec = pl.BlockSpec((1, W_B, LANES), lambda b_, i: (b_, 0, 0))
    return pl.pallas_call(
        functools.partial(_gla_kernel, chunk=chunk),
        grid=(b, t // tg),
        in_specs=[tok(LANES), tok(LANES), tok(W_B), tok(LANES), tok(W_B), sspec,
                  pl.BlockSpec((1, W_B), lambda b_, i: (0, 0))],
        out_specs=[tok(W_B), sspec],
        out_shape=[jax.ShapeDtypeStruct((b, t, W_B), BF16), jax.ShapeDtypeStruct((b, W_B, LANES), F32)],
        compiler_params=_params(("parallel", "arbitrary")),
        name="gla",
    )(qb, kb, vb, loga, rb, s0_t, g_gla)


def _softmax_two(t_past, t_new, exp_fn):
    m = jnp.maximum(jnp.max(t_past, axis=1, keepdims=True), jnp.max(t_new, axis=1, keepdims=True))
    p_past = exp_fn(t_past - m)
    p_new = exp_fn(t_new - m)
    l = jnp.sum(p_past, axis=1, keepdims=True) + jnp.sum(p_new, axis=1, keepdims=True)
    return p_past, p_new, l


def _fox_s_kernel(q_ref, kn_ref, vn_ref, kp_ref, vp_ref, c_ref, o_ref):
    t, past = q_ref.shape[1], kp_ref.shape[1]
    q = q_ref[0]
    kn = kn_ref[0].astype(BF16)
    vn = vn_ref[0].astype(BF16)
    kp = kp_ref[0].astype(BF16)
    vp = vp_ref[0].astype(BF16)
    lane_head = lax.broadcasted_iota(jnp.int32, (t, W_A), 1) // HD_A
    row = lax.broadcasted_iota(jnp.int32, (t, t), 0)
    col = lax.broadcasted_iota(jnp.int32, (t, t), 1)
    o = jnp.zeros((t, W_A), F32)
    for h in range(N_HEADS):
        qh = jnp.where(lane_head == h, q, 0.0).astype(BF16)
        ch = c_ref[0, h:h + 1, :]
        cref = ch[:, past:past + 1]
        t_past = _nt_dot(qh, kp) + (cref - ch[:, :past])
        t_new = jnp.where(col <= row, _nt_dot(qh, kn) + (cref - ch[:, past:past + t]), NEG)
        p_past, p_new, l = _softmax_two(t_past, t_new, jnp.exp)
        oh =(_dot(p_past.astype(BF16), vp) + _dot(p_new.astype(BF16), vn)) / l
        o = o + jnp.where(lane_head == h, oh, 0.0)
    o_ref[0] = o.astype(BF16)


def _fox_s_call(qa, ka, va, cache_k, cache_v, c):
    b, t, _ = qa.shape
    past = cache_k.shape[1]
    new = lambda: pl.BlockSpec((1, t, W_A), lambda i: (i, 0, 0))
    old = lambda: pl.BlockSpec((1, past, W_A), lambda i: (i, 0, 0))
    return pl.pallas_call(
        _fox_s_kernel,
        grid=(b,),
        in_specs=[new(), new(), new(), old(), old(),
                  pl.BlockSpec((1, N_HEADS, c.shape[2]), lambda i: (i, 0, 0))],
        out_specs=new(),
        out_shape=jax.ShapeDtypeStruct((b, t, W_A), BF16),
        compiler_params=_params(("parallel",)),
        name="fox_s",
    )(qa, ka, va, cache_k, cache_v, c)


def _diff_s_kernel(q1_ref, q2_ref, kn_ref, vn_ref, kp_ref, vp_ref, slope_ref, lam_ref, g_ref, o_ref, *, lam_init):
    t, past = q1_ref.shape[1], kp_ref.shape[1]
    lam = _diff_lambda(lam_ref[...], lam_init)
    row = lax.broadcasted_iota(jnp.int32, (t, t), 0) + past
    col = lax.broadcasted_iota(jnp.int32, (t, t), 1) + past
    visible_new = lax.shift_right_logical(col, 6) <= lax.shift_right_logical(row, 6)
    dist_new = jnp.abs(row - col).astype(F32)
    prow = lax.broadcasted_iota(jnp.int32, (t, past), 0) + past
    pcol = lax.broadcasted_iota(jnp.int32, (t, past), 1)
    visible_past = lax.shift_right_logical(pcol, 6) <= lax.shift_right_logical(prow, 6)
    dist_past = (prow - pcol).astype(F32)
    for h in range(N_HEADS):
        hs = slice(h * HEAD_PAD, (h + 1) * HEAD_PAD)
        slope = slope_ref[h][:, 0:1] * LOG2E
        kn = kn_ref[0, :, hs].astype(BF16)
        vn = vn_ref[0, :, hs].astype(BF16)
        kp = kp_ref[0, :, hs].astype(BF16)
        vp = vp_ref[0, :, hs].astype(BF16)
        outs = []
        for q_ref in (q1_ref, q2_ref):
            q = q_ref[0, :, hs]
            t_past = jnp.where(visible_past, _nt_dot(q, kp) - slope * dist_past, NEG)
            t_new = jnp.where(visible_new, _nt_dot(q, kn) - slope * dist_new, NEG)
            p_past, p_new, l = _softmax_two(t_past, t_new, jnp.exp2)
            outs.append((_dot(p_past.astype(BF16), vp) + _dot(p_new.astype(BF16), vn)) / l)
        out = outs[0] - lam * outs[1]
        o_ref[0, :, hs] = (_rms(out, g_ref[...]) * (1.0 - lam_init)).astype(BF16)


def _diff_s_call(qc1_p, qc2_p, kc, vc, cache_k, cache_v, slopes, lam_p, g_diff, lam_init):
    assert CHUNK == 64
    b, t, _ = qc1_p.shape
    past = cache_k.shape[1]
    new = lambda: pl.BlockSpec((1, t, W_C), lambda i: (i, 0, 0))
    old = lambda: pl.BlockSpec((1, past, W_C), lambda i: (i, 0, 0))
    return pl.pallas_call(
        functools.partial(_diff_s_kernel, lam_init=lam_init),
        grid=(b,),
        in_specs=[new(), new(), new(), new(), old(), old(),
                  pl.BlockSpec((N_HEADS, 1, LANES), lambda i: (0, 0, 0)),
                  pl.BlockSpec((4, HD_C), lambda i: (0, 0)),
                  pl.BlockSpec((1, HEAD_PAD), lambda i: (0, 0))],
        out_specs=new(),
        out_shape=jax.ShapeDtypeStruct((b, t, W_C), BF16),
        compiler_params=_params(("parallel",)),
        name="diff_s",
    )(qc1_p, qc2_p, kc, vc, cache_k, cache_v, slopes, lam_p, g_diff)


def _out_kernel(x_ref, oa_ref, gla_ref, dif_ref, gate_ref, g2_ref, woa_ref, wog_ref, wod_ref, o_ref):
    y = _dot(oa_ref[...], woa_ref[...]) + _dot(gla_ref[...], wog_ref[...]) + _dot(dif_ref[...], wod_ref[...])
    o_ref[...] = x_ref[...] + gate_ref[0] * _rms(y, g2_ref[...])


def _out_call(x, oa, gla, dif, mod, g2, w_oa, w_og, w_od, tm, tiles_per_mod):
    n = x.shape[0]
    gate = mod[2]
    r = gate.shape[1]
    tok = lambda w: pl.BlockSpec((tm, w), lambda i: (i, 0))
    return pl.pallas_call(
        _out_kernel,
        grid=(n // tm,),
        in_specs=[tok(D_MODEL), tok(oa.shape[1]), tok(W_B), tok(W_C), _mod_spec(r, tiles_per_mod),
                  _const_spec((1, D_MODEL)), _const_spec(w_oa.shape), _const_spec(w_og.shape),
                  _const_spec(w_od.shape)],
        out_specs=tok(D_MODEL),
        out_shape=jax.ShapeDtypeStruct((n, D_MODEL), F32),
        compiler_params=_params(("parallel",)),
        name="out_proj",
    )(x, oa, gla, dif, gate, g2, w_oa, w_og, w_od)


def _pad_heads(w, hd, off=0):
    rows = w.shape[0]
    zeros = lambda width: [jnp.zeros((rows, width), w.dtype)] if width else []
    pieces = []
    for h in range(N_HEADS):
        pieces += zeros(off) + [w[:, h * hd:(h + 1) * hd]] + zeros(HEAD_PAD - off - hd)
    return jnp.concatenate(pieces, axis=1)


def _pad_lanes(w, width=LANES):
    return jnp.pad(w, ((0, 0), (0, width - w.shape[1])))


def _layer_params(l, w_in, b_f, w_gla_up, b_gla_up, g_gla, g_diff, w_out):
    offs = [int(o) for o in np.cumsum(PROJ_SIZES)[:-1]]
    qa, ka, va, fa, qb, kb, vb, gb, rb, qc, kc, vc = jnp.split(w_in[l], offs, axis=1)
    qc = qc.reshape(D_MODEL, N_HEADS, 2, HD_C)
    segs = {
        "ka": ka, "va": va, "kc": kc, "vc": vc, "qa": qa,
        "qa_p": _pad_heads(qa, HD_A), "ka_p": _pad_heads(ka, HD_A), "va_p": _pad_heads(va, HD_A),
        "qc1_p": _pad_heads(qc[:, :, 0].reshape(D_MODEL, -1), HD_C),
        "qc2_p": _pad_heads(qc[:, :, 1].reshape(D_MODEL, -1), HD_C, off=HD_C),
        "fa": _pad_lanes(fa), "qb": qb, "kb": kb, "vb": vb, "rb": rb, "gb": _pad_lanes(gb),
    }
    w_ext = jnp.concatenate([segs[n] for n, _ in _SEG_WIDTHS], axis=1).astype(BF16)
    wo = w_out[l]
    vone = np.zeros((1, W_PAD), np.float32)
    vone[0, HD_A::HEAD_PAD] = 1.0
    return dict(
        w_ext=w_ext, w_kvt=jnp.concatenate([ka, va], axis=1).T.astype(BF16),
        bf=_pad_lanes(b_f[l][None, :]),
        wup=jnp.pad(w_gla_up[l], ((0, LANES - GLA_RANK), (0, 0))).astype(BF16),
        bup=b_gla_up[l][None, :],
        vone=jnp.asarray(vone),
        g_gla=jnp.tile(g_gla[l], N_HEADS)[None, :], g_diff=g_diff[l][None, :],
        w_oa=wo[:W_A].astype(BF16),
        w_oa_p=_pad_heads(wo[:W_A].T, HD_A).T.astype(BF16),
        w_og=wo[W_A:W_A + W_B].astype(BF16), w_od=wo[W_A + W_B:].astype(BF16),
    )


def _state_to_blockdiag_t(s):
    eye = jnp.eye(N_HEADS, dtype=s.dtype)
    return jnp.einsum("bhkv,hg->bhvgk", s, eye).reshape(s.shape[0], W_B, N_HEADS * DK_B)


def _blockdiag_t_to_state(st):
    b = st.shape[0]
    s5 = st.reshape(b, N_HEADS, DV_B, N_HEADS, DK_B)
    diag = jnp.stack([s5[:, h, :, h, :] for h in range(N_HEADS)], axis=1)
    return jnp.swapaxes(diag, 2, 3)


SKIP_LOG2 = 160.0
NORM_SLACK = 1.01


def _tile_max(x, factor):
    return jnp.max(x.reshape(x.shape[:-1] + (x.shape[-1] // factor, factor)), axis=-1)


def _first_tiles(qn, kn, gap, tq, tk):
    b, h, nq = qn.shape
    r = tq // tk
    kn_diag = _tile_max(kn, r)
    bound = NORM_SLACK * (qn[..., None] * kn[:, :, None, :] + (qn * kn_diag)[..., None]) + gap
    dead = (bound < -SKIP_LOG2).reshape(b, h // GROUP_HEADS, GROUP_HEADS, nq, kn.shape[-1]).all(axis=2)
    lead = jnp.sum(jnp.cumprod(dead.astype(jnp.int32), axis=-1), axis=-1)
    return jnp.minimum(lead, jnp.arange(nq, dtype=jnp.int32) * r).astype(jnp.int32)


def _tile(n, pref):
    t = min(n, pref)
    assert n % t == 0, (n, t)
    return t


def _trunk(x, mods, params, ffn_w, g_norm, lam_params, mixer, tm, tiles_per_mod, with_qa):
    b, t, _ = x.shape
    n = b * t
    depth = len(params)
    xf = x.reshape(n, D_MODEL)
    stacked = None
    gla_states = []
    for l in range(depth):
        p = params[l]
        gn = lambda k: g_norm[l, k][None, :]
        xf = _ffn_call(xf, mods[l][0], gn(0), gn(1), *ffn_w, l, 0, 0.5, tm, tiles_per_mod)
        pr = _proj_call(xf, mods[l][1], gn(2), p["w_ext"], p["w_kvt"], p["bf"], p["wup"], p["bup"], p["vone"], tm,
                        tiles_per_mod, l, depth, stacked, with_qa, t)
        stacked = {nm: pr[nm] for nm, _, _ in _STACKED_OUTS}
        lam_init = 0.8 - 0.6 * math.exp(-0.3 * l)
        oa, gla, dif, s_fin, w_oa = mixer(l, pr, p, lam_params[l], lam_init, b, t)
        gla_states.append(s_fin)
        xf = _out_call(xf, oa, gla, dif, mods[l][1], gn(3), w_oa, p["w_og"], p["w_od"], tm, tiles_per_mod)
        xf = _ffn_call(xf, mods[l][2], gn(4), gn(5), *ffn_w, l, 1, 0.5, tm, tiles_per_mod)
    shp = lambda nm, *tail: stacked[nm].reshape((depth, b, t) + tail)
    if stacked["ka"].ndim == 4:
        shp_a = lambda nm: jnp.transpose(stacked[nm].reshape(depth, b, N_HEADS, HD_A, t), (0, 1, 4, 2, 3))
    else:
        shp_a = lambda nm: shp(nm, N_HEADS, HD_A)
    states = [shp_a("ka"), shp_a("va"), shp("logf", N_HEADS), jnp.stack(gla_states),
              shp("kc", N_HEADS, 2 * HD_C), shp("vc", N_HEADS, 2 * HD_C)]
    return xf.reshape(b, t, D_MODEL), states


def kernel(x_prompt, x_sample, c_prompt, c_sample, cache_fox_k, cache_fox_v, cache_fox_logf, state_gla,
           cache_diff_k, cache_diff_v, w_ada, b_ada, g_norm, w_ffn_in, w_ffn_out, w_in, b_f, w_gla_up,
           b_gla_up, g_gla, g_diff, lam_params, w_out):
    depth = w_in.shape[0]
    bp, tp, _ = x_prompt.shape
    bs, ts, _ = x_sample.shape
    past = cache_fox_k.shape[2]

    params = [_layer_params(l, w_in, b_f, w_gla_up, b_gla_up, g_gla, g_diff, w_out) for l in range(depth)]
    ffn_w = (w_ffn_in.astype(BF16), w_ffn_out.astype(BF16))

    slopes = 2.0 ** (-8.0 * jnp.arange(1, N_HEADS + 1, dtype=F32) / N_HEADS)
    slopes = jnp.broadcast_to(slopes[:, None, None], (N_HEADS, 1, LANES))

    rows = bp + bs
    rows_pad = -(-rows // 8) * 8
    c_all = jnp.pad(jnp.concatenate([c_prompt, c_sample], axis=0), ((0, rows_pad - rows), (0, 0)))
    mods_p, mods_s = [], []
    for l in range(depth):
        mod = _ada_call(c_all, w_ada[l].astype(BF16), b_ada[l][None, :]).reshape(rows_pad, N_SUB, 3, D_MODEL)
        mods_p.append([[mod[:bp, s, k][:, None, :] for k in range(3)] for s in range(N_SUB)])
        mods_s.append([[jnp.repeat(mod[bp:rows, s, k], ts, axis=0)[None] for k in range(3)] for s in range(N_SUB)])

    tm_p = _tile(tp, 512)
    tq = _tile(tp, 1024)
    tk = _tile(tp, 512)
    tg = _tile(tp, 512)

    def prompt_mixer(l, pr, p, lam_p, lam_init, b, t):
        r3 = lambda a: a.reshape(b, t, a.shape[-1])
        logf_t = jnp.swapaxes(r3(pr["logf"][l]), 1, 2).reshape(b * N_HEADS, t // LANES, LANES)
        c = _cumsum_call(logf_t).reshape(b, N_HEADS, t)
        nrm = jnp.sqrt(pr["norms"][:, 0, :N_HEADS * len(_NORM_ROWS)])
        nrm = dict(zip(_NORM_ROWS, jnp.moveaxis(nrm.reshape(b, t // tm_p, len(_NORM_ROWS), N_HEADS), (2, 3), (0, 2))))
        q_tiles = lambda a: _tile_max(a, tq // tm_p)
        k_tiles = lambda a: _tile_max(a, tk // tm_p)
        gap_a = (c[:, :, ::tq][..., None] - c[:, :, tk - 1::tk][:, :, None, :]) * LOG2E
        first_a = _first_tiles(q_tiles(nrm["qa_p"]), k_tiles(nrm["ka_p"]), gap_a, tq, tk)
        oa = _fox_call(first_a, r3(pr["qa_p"]), r3(pr["ka_p"]), r3(pr["va_p"]), c, tq, tk)
        s0_t = jnp.zeros((b, W_B, N_HEADS * DK_B), F32)
        gla, s_t = _gla_call(r3(pr["qb"]), r3(pr["kb"]), r3(pr["vb"]), r3(pr["loga"]), r3(pr["rb"]), s0_t,
                             p["g_gla"], tg, min(CHUNK, t))
        last_key = jnp.arange(tk - 1, t, tk, dtype=F32)[None, :] - jnp.arange(0, t, tq, dtype=F32)[:, None]
        gap_c = slopes[:, 0, :1, None] * LOG2E * last_key[None]
        first_c = _first_tiles(q_tiles(jnp.maximum(nrm["qc1_p"], nrm["qc2_p"])), k_tiles(nrm["kc_b"]), gap_c, tq, tk)
        dif = _diff_call(first_c, r3(pr["qc1_p"]), r3(pr["qc2_p"]), r3(pr["kc_b"]), r3(pr["vc_b"]), slopes, lam_p,
                         p["g_diff"], lam_init, tq, tk)
        flat = lambda a: a.reshape(b * t, a.shape[-1])
        return flat(oa), flat(gla), flat(dif), _blockdiag_t_to_state(s_t), p["w_oa_p"]

    def sample_mixer(l, pr, p, lam_p, lam_init, b, t):
        r3 = lambda a: a.reshape(b, t, a.shape[-1])
        total = past + t
        nb = -(-total // (8 * LANES)) * 8
        lf = jnp.concatenate([cache_fox_logf[l].astype(F32), r3(pr["logf"][l])], axis=1)
        lf = jnp.pad(jnp.swapaxes(lf, 1, 2), ((0, 0), (0, 0), (0, nb * LANES - total)))
        c = _cumsum_call(lf.reshape(b * N_HEADS, nb, LANES)).reshape(b, N_HEADS, nb * LANES)
        oa = _fox_s_call(r3(pr["qa"]), r3(pr["ka"][l]), r3(pr["va"][l]),
                         cache_fox_k[l].reshape(b, past, W_A), cache_fox_v[l].reshape(b, past, W_A), c)
        gla, s_t = _gla_call(r3(pr["qb"]), r3(pr["kb"]), r3(pr["vb"]), r3(pr["loga"]), r3(pr["rb"]),
                             _state_to_blockdiag_t(state_gla[l].astype(F32)), p["g_gla"], t, min(CHUNK, t))
        dif = _diff_s_call(r3(pr["qc1_p"]), r3(pr["qc2_p"]), pr["kc"][l].reshape(b, t, W_C), pr["vc"][l].reshape(b, t, W_C),
                           cache_diff_k[l].reshape(b, past, W_C), cache_diff_v[l].reshape(b, past, W_C),
                           slopes, lam_p, p["g_diff"], lam_init)
        flat = lambda a: a.reshape(b * t, a.shape[-1])
        return flat(oa), flat(gla), flat(dif), _blockdiag_t_to_state(s_t), p["w_oa"]

    y_p, sp = _trunk(x_prompt, mods_p, params, ffn_w, g_norm, lam_params, prompt_mixer, tm_p, tp // tm_p, False)
    y_s, ss = _trunk(x_sample, mods_s, params, ffn_w, g_norm, lam_params, sample_mixer, bs * ts, 1, True)
    return (y_p, y_s, sp[0], sp[1], sp[2], sp[3], sp[4], sp[5], ss[0], ss[1], ss[2], ss[3], ss[4], ss[5])
```

```python
import functools
import math

import jax
import jax.numpy as jnp
import numpy as np
from jax import lax
from jax.experimental import pallas as pl
from jax.experimental.pallas import tpu as pltpu

F32 = jnp.float32
BF16 = jnp.bfloat16

D_MODEL = 1024
N_HEADS = 4
HD_A = 64
DK_B = 32
DV_B = 64
GLA_RANK = 16
GLA_TAU = 16.0
HD_C = 64
CHUNK = 64
W_A = N_HEADS * HD_A
W_B = N_HEADS * DV_B
W_C = N_HEADS * 2 * HD_C
D_FF = ((8 * D_MODEL // 3 + 255) // 256) * 256
N_SUB = 3
PROJ_SIZES = (W_A, W_A, W_A, N_HEADS, N_HEADS * DK_B, N_HEADS * DK_B, W_B, GLA_RANK, W_B, W_C, W_C, W_C)
NEG = -1e30
EPS = 1e-6
LOG2E = math.log2(math.e)

LANES = 128
HEAD_PAD = 128
W_PAD = N_HEADS * HEAD_PAD
VMEM_LIMIT = 56 * 1024 * 1024

_SEG_WIDTHS = (
    ("ka", W_A), ("va", W_A), ("kc", W_C), ("vc", W_C), ("qa", W_A),
    ("qa_p", W_PAD), ("ka_p", W_PAD), ("va_p", W_PAD), ("qc", W_C),
    ("fa", LANES), ("qb", LANES), ("kb", LANES), ("vb", W_B), ("rb", W_B), ("gb", LANES),
)
_SEG = {}
_off = 0
for _name, _w in _SEG_WIDTHS:
    _SEG[_name] = (_off, _off + _w)
    _off += _w
N_EXT = _off


def _nt_dot(a, b):
    return lax.dot_general(a, b, (((1,), (1,)), ((), ())), preferred_element_type=F32)


def _dot(a, b):
    return jnp.dot(a, b, preferred_element_type=F32)


def _split3(x):
    hi = x.astype(BF16)
    r1 = x - hi.astype(F32)
    mid = r1.astype(BF16)
    lo = (r1 - mid.astype(F32)).astype(BF16)
    return hi, mid, lo


def _dot_exact_rhs(x, m):
    mb = m.astype(BF16)
    hi, mid, lo = _split3(x)
    return _dot(hi, mb) + _dot(mid, mb) + _dot(lo, mb)


def _dot_exact_lhs(m, x):
    mb = m.astype(BF16)
    hi, mid, lo = _split3(x)
    return _dot(mb, hi) + _dot(mb, mid) + _dot(mb, lo)


def _rms(x, g):
    return x * lax.rsqrt(jnp.mean(x * x, axis=-1, keepdims=True) + EPS) * g


def _log_sigmoid(x):
    return jnp.minimum(x, 0.0) - jnp.log1p(jnp.exp(-jnp.abs(x)))


def _const_spec(shape):
    nd = len(shape)
    return pl.BlockSpec(shape, lambda *_: (0,) * nd, pipeline_mode=pl.Buffered(1))


def _params(sem):
    return pltpu.CompilerParams(dimension_semantics=sem, vmem_limit_bytes=VMEM_LIMIT)


def _ada_kernel(c_ref, w_ref, b_ref, o_ref):
    c = c_ref[...]
    a = (c * jax.nn.sigmoid(c)).astype(BF16)
    o_ref[...] = _dot(a, w_ref[...]) + b_ref[...]


def _ada_call(c, w, b):
    m, n = c.shape[0], w.shape[1]
    tn = n // 8
    return pl.pallas_call(
        _ada_kernel,
        grid=(n // tn,),
        in_specs=[pl.BlockSpec((m, D_MODEL), lambda i: (0, 0)),
                  pl.BlockSpec((D_MODEL, tn), lambda i: (0, i)),
                  pl.BlockSpec((1, tn), lambda i: (0, i))],
        out_specs=pl.BlockSpec((m, tn), lambda i: (0, i)),
        out_shape=jax.ShapeDtypeStruct((m, n), F32),
        compiler_params=_params(("parallel",)),
        name="ada",
    )(c, w, b)


FF_CHUNK = 256


def _ffn_kernel(x_ref, shift_ref, scale_ref, gate_ref, g1_ref, g2_ref, win_ref, wout_ref, o_ref, *, res_w):
    x = x_ref[...]
    h = _rms(x, g1_ref[...]) * (1.0 + scale_ref[0]) + shift_ref[0]
    hb = h.astype(BF16)
    y = jnp.zeros(x.shape, F32)
    for c in range(D_FF // FF_CHUNK):
        lo, hi = c * FF_CHUNK, (c + 1) * FF_CHUNK
        g = _dot(hb, win_ref[:, lo:hi])
        u = _dot(hb, win_ref[:, D_FF + lo:D_FF + hi])
        a = (g * jax.nn.sigmoid(g) * u).astype(BF16)
        y = y + _dot(a, wout_ref[lo:hi, :])
    o_ref[...] = x + res_w * gate_ref[0] * _rms(y, g2_ref[...])


def _mod_spec(r, tiles_per_mod):
    return pl.BlockSpec((1, r, D_MODEL), lambda i: (i // tiles_per_mod, 0, 0))


def _ffn_call(x, mod, g1, g2, w_in, w_out, layer, which, res_w, tm, tiles_per_mod):
    n = x.shape[0]
    shift, scale, gate = mod
    r = shift.shape[1]
    tok = pl.BlockSpec((tm, D_MODEL), lambda i: (i, 0))
    pick = lambda rows, cols: pl.BlockSpec((None, None, rows, cols), lambda i: (layer, which, 0, 0),
                                           pipeline_mode=pl.Buffered(1))
    return pl.pallas_call(
        functools.partial(_ffn_kernel, res_w=res_w),
        grid=(n // tm,),
        in_specs=[tok, _mod_spec(r, tiles_per_mod), _mod_spec(r, tiles_per_mod), _mod_spec(r, tiles_per_mod),
                  _const_spec((1, D_MODEL)), _const_spec((1, D_MODEL)),
                  pick(D_MODEL, 2 * D_FF), pick(D_FF, D_MODEL)],
        out_specs=tok,
        out_shape=jax.ShapeDtypeStruct((n, D_MODEL), F32),
        compiler_params=_params(("parallel",)),
        name="ffn",
    )(x, shift, scale, gate, g1, g2, w_in, w_out)


_STACKED_OUTS = (
    ("ka", 1, W_A), ("va", 1, W_A), ("kc", N_HEADS, HEAD_PAD), ("vc", N_HEADS, HEAD_PAD), ("logf", 1, N_HEADS),
)
_LAYER_OUTS = (
    ("qa_p", W_PAD, BF16), ("ka_p", W_PAD, BF16), ("va_p", W_PAD, BF16),
    ("qc1_p", W_PAD, BF16), ("qc2_p", W_PAD, BF16), ("kc_b", W_C, BF16), ("vc_b", W_C, BF16),
    ("qb", LANES, F32), ("kb", LANES, F32), ("vb", W_B, F32), ("loga", LANES, F32), ("rb", W_B, F32),
)
_NORM_ROWS = ("qa_p", "ka_p", "qc1_p", "qc2_p", "kc_b")


def _proj_kernel(x_ref, shift_ref, scale_ref, g1_ref, w_ref, wkvt_ref, bf_ref, wup_ref, bup_ref, vone_ref, *refs,
                 n_alias, with_qa, transposed_a):
    refs = refs[n_alias:]
    names = [n for n, _, _ in _STACKED_OUTS] + [n for n, _, _ in _LAYER_OUTS] + (["qa"] if with_qa else [])
    o = dict(zip(names + ["norms"], refs))
    tm = x_ref.shape[0]
    x = x_ref[...]
    hb = (_rms(x, g1_ref[...]) * (1.0 + scale_ref[0]) + shift_ref[0]).astype(BF16)

    def seg(name):
        lo, hi = _SEG[name]
        return _dot(hb, w_ref[:, lo:hi])

    def store_heads(ref, val):
        for h in range(N_HEADS):
            ref[0, pl.ds(h, tm, stride=N_HEADS), :] = val[:, h * HEAD_PAD:(h + 1) * HEAD_PAD]

    if transposed_a:
        kvt = _nt_dot(wkvt_ref[...], hb)
        o["ka"][0, 0] = kvt[:W_A]
        o["va"][0, 0] = kvt[W_A:]
    else:
        o["ka"][0] = seg("ka")
        o["va"][0] = seg("va")
    kc = seg("kc")
    store_heads(o["kc"], kc)
    rounded = {"kc_b": kc.astype(BF16)}
    o["kc_b"][...] = rounded["kc_b"]
    vc = seg("vc")
    store_heads(o["vc"], vc)
    o["vc_b"][...] = vc.astype(BF16)
    if with_qa:
        o["qa"][...] = seg("qa") * (HD_A ** -0.5)
    rounded["qa_p"] = (seg("qa_p") * (HD_A ** -0.5 * LOG2E)).astype(BF16)
    rounded["ka_p"] = seg("ka_p").astype(BF16)
    o["qa_p"][...] = rounded["qa_p"]
    o["ka_p"][...] = rounded["ka_p"]
    o["va_p"][...] = (seg("va_p") + vone_ref[...]).astype(BF16)
    qc = seg("qc") * (HD_C ** -0.5 * LOG2E)
    first_map = lax.broadcasted_iota(jnp.int32, qc.shape, 1) % HEAD_PAD < HD_C
    rounded["qc1_p"] = jnp.where(first_map, qc, 0.0).astype(BF16)
    rounded["qc2_p"] = jnp.where(first_map, 0.0, qc).astype(BF16)
    o["qc1_p"][...] = rounded["qc1_p"]
    o["qc2_p"][...] = rounded["qc2_p"]
    fa = seg("fa") + bf_ref[...]
    o["logf"][0] = _log_sigmoid(fa)[:, :N_HEADS]
    o["qb"][...] = seg("qb")
    o["kb"][...] = seg("kb")
    o["vb"][...] = seg("vb")
    o["rb"][...] = seg("rb")
    gb = seg("gb").astype(BF16)
    o["loga"][...] = _log_sigmoid(_dot(gb, wup_ref[...]) + bup_ref[...]) * (1.0 / GLA_TAU)
    lane = lax.broadcasted_iota(jnp.int32, (8, LANES), 1)
    tile_norms = jnp.zeros((8, LANES), F32)
    for a, name in enumerate(_NORM_ROWS):
        sq = rounded[name].astype(F32)
        sq = sq * sq
        for h in range(N_HEADS):
            ss = jnp.sum(sq[:, h * HEAD_PAD:(h + 1) * HEAD_PAD], axis=1, keepdims=True)
            tile_norms = jnp.where(lane == N_HEADS * a + h, jnp.max(ss, axis=0, keepdims=True), tile_norms)
    o["norms"][0] = tile_norms


def _proj_call(x, mod, g1, w_ext, w_kvt, bf, wup, bup, vone, tm, tiles_per_mod, layer, depth, stacked, with_qa,
               seq_len):
    n = x.shape[0]
    shift, scale, _ = mod
    r = shift.shape[1]
    tok = lambda w: pl.BlockSpec((tm, w), lambda i: (i, 0))
    layer_outs = _LAYER_OUTS + ((("qa", W_A, F32),) if with_qa else ())
    alias_in = [] if stacked is None else [stacked[nm] for nm, _, _ in _STACKED_OUTS]
    transposed_a = seq_len % tm == 0
    tps = max(seq_len // tm, 1)
    stacked_specs, stacked_shapes = [], []
    for nm, rows, w in _STACKED_OUTS:
        if transposed_a and nm in ("ka", "va"):
            stacked_specs.append(pl.BlockSpec((1, 1, w, tm), lambda i: (layer, i // tps, 0, i % tps)))
            stacked_shapes.append(jax.ShapeDtypeStruct((depth, n // seq_len, w, seq_len), F32))
        else:
            stacked_specs.append(pl.BlockSpec((1, tm * rows, w), lambda i: (layer, i, 0)))
            stacked_shapes.append(jax.ShapeDtypeStruct((depth, n * rows, w), F32))
    n_fixed = 10
    outs = pl.pallas_call(
        functools.partial(_proj_kernel, n_alias=len(alias_in), with_qa=with_qa, transposed_a=transposed_a),
        grid=(n // tm,),
        in_specs=[tok(D_MODEL), _mod_spec(r, tiles_per_mod), _mod_spec(r, tiles_per_mod),
                  _const_spec((1, D_MODEL)), _const_spec((D_MODEL, N_EXT)), _const_spec((2 * W_A, D_MODEL)),
                  _const_spec((1, LANES)), _const_spec((LANES, LANES)), _const_spec((1, LANES)),
                  _const_spec((1, W_PAD))]
        + [pl.BlockSpec(memory_space=pl.ANY)] * len(alias_in),
        out_specs=stacked_specs + [tok(w) for _, w, _ in layer_outs]
        + [pl.BlockSpec((1, 8, LANES), lambda i: (i, 0, 0))],
        out_shape=stacked_shapes + [jax.ShapeDtypeStruct((n, w), dt) for _, w, dt in layer_outs]
        + [jax.ShapeDtypeStruct((n // tm, 8, LANES), F32)],
        input_output_aliases={n_fixed + k: k for k in range(len(alias_in))},
        compiler_params=_params(("parallel",)),
        name="proj",
    )(x, shift, scale, g1, w_ext, w_kvt, bf, wup, bup, vone, *alias_in)
    names = [nm for nm, _, _ in _STACKED_OUTS] + [nm for nm, _, _ in layer_outs] + ["norms"]
    return dict(zip(names, outs))


def _cumsum_kernel(x_ref, o_ref):
    x = x_ref[0]
    nb = x.shape[0]
    r = lax.broadcasted_iota(jnp.int32, (LANES, LANES), 0)
    c = lax.broadcasted_iota(jnp.int32, (LANES, LANES), 1)
    local = _dot_exact_rhs(x, (r <= c).astype(F32))
    tot = jnp.broadcast_to(local[:, LANES - 1:LANES], (nb, LANES))
    rr = lax.broadcasted_iota(jnp.int32, (nb, nb), 0)
    cc = lax.broadcasted_iota(jnp.int32, (nb, nb), 1)
    o_ref[0] = local + _dot_exact_lhs((cc < rr).astype(F32), tot)


def _cumsum_call(x):
    rows, nb, _ = x.shape
    spec = pl.BlockSpec((1, nb, LANES), lambda i: (i, 0, 0))
    return pl.pallas_call(
        _cumsum_kernel, grid=(rows,), in_specs=[spec], out_specs=spec,
        out_shape=jax.ShapeDtypeStruct(x.shape, F32),
        compiler_params=_params(("parallel",)), name="cumsum",
    )(x)


def _lane_tile(x, width):
    return jnp.concatenate([x] * (width // LANES), axis=1)


GROUP_HEADS = 2
W_GROUP = GROUP_HEADS * HEAD_PAD


def _causal_sweep(i, first, tq, tk, tiles_per_iter, tile):
    r = tq // tk

    def body(jj, carry):
        for u in range(tiles_per_iter):
            tile(jj * tiles_per_iter + u, slice(0, tq), None)
        return carry

    assert r % tiles_per_iter == 0
    lax.fori_loop(lax.div(first, tiles_per_iter), lax.div(i * r, tiles_per_iter), body, 0)
    for d in range(r):
        tile(i * r + d, slice(d * tk, (d + 1) * tk), d * tk)
        if d + 1 < r:
            tile(i * r + d, slice((d + 1) * tk, tq), None)


def _fox_kernel(first_ref, q_ref, k_ref, v_ref, c_ref, o_ref, m_sc, acc_sc, *, tk, tiles_per_iter):
    i = pl.program_id(2)
    first = first_ref[pl.program_id(0), pl.program_id(1), i]
    tq = q_ref.shape[1]
    m_sc[...] = jnp.full(m_sc.shape, NEG, F32)
    acc_sc[...] = jnp.zeros(acc_sc.shape, F32)
    row = lax.broadcasted_iota(jnp.int32, (tk, tk), 0)
    col = lax.broadcasted_iota(jnp.int32, (tk, tk), 1)
    causal = col <= row

    def tile(j, rows, local):
        ks = pl.ds(pl.multiple_of(j * tk, tk), tk)
        for h in range(GROUP_HEADS):
            hs = slice(h * HEAD_PAD, (h + 1) * HEAD_PAD)
            c_first = c_ref[0, h, pl.ds(i * (tq // tk), 1), :][:, 0:1]
            t = _nt_dot(q_ref[0, rows, hs], k_ref[0, ks, hs]) + (c_first - c_ref[0, h, pl.ds(j, 1), :]) * LOG2E
            if local is not None:
                t = jnp.where(causal, t, NEG)
            m_old = m_sc[h, rows, :]
            m_new = jnp.maximum(m_old, jnp.max(t, axis=1, keepdims=True))
            p = jnp.exp2(t - _lane_tile(m_new, tk)).astype(BF16)
            acc_sc[h, rows, :] = jnp.exp2(m_old - m_new) * acc_sc[h, rows, :] + _dot(p, v_ref[0, ks, hs])
            m_sc[h, rows, :] = m_new

    _causal_sweep(i, first, tq, tk, tiles_per_iter, tile)
    for h in range(GROUP_HEADS):
        acc = acc_sc[h]
        o_ref[0, :, h * HEAD_PAD:(h + 1) * HEAD_PAD] = (acc / acc[:, HD_A:HD_A + 1]).astype(BF16)


def _resident_spec(t):
    return pl.BlockSpec((1, t, W_GROUP), lambda b_, g, i, first: (b_, 0, g), pipeline_mode=pl.Buffered(1))


def _fox_call(first, qa_p, ka_p, va_p, c, tq, tk):
    b, t, _ = qa_p.shape
    qspec = pl.BlockSpec((1, tq, W_GROUP), lambda b_, g, i, first: (b_, i, g))
    grid_spec = pltpu.PrefetchScalarGridSpec(
        num_scalar_prefetch=1,
        grid=(b, N_HEADS // GROUP_HEADS, t // tq),
        in_specs=[qspec, _resident_spec(t), _resident_spec(t),
                  pl.BlockSpec((1, GROUP_HEADS, t // tk, tk), lambda b_, g, i, first: (b_, g, 0, 0))],
        out_specs=qspec,
        scratch_shapes=[pltpu.VMEM((GROUP_HEADS, tq, LANES), F32), pltpu.VMEM((GROUP_HEADS, tq, HEAD_PAD), F32)])
    return pl.pallas_call(
        functools.partial(_fox_kernel, tk=tk, tiles_per_iter=tq // tk),
        grid_spec=grid_spec,
        out_shape=jax.ShapeDtypeStruct((b, t, W_PAD), BF16),
        compiler_params=_params(("parallel", "parallel", "arbitrary")),
        name="fox",
    )(first, qa_p, ka_p, va_p, c.reshape(b, N_HEADS, t // tk, tk))


def _diff_lambda(lp, lam_init):
    a = jnp.sum(lp[0:1] * lp[1:2], axis=1, keepdims=True)
    b = jnp.sum(lp[2:3] * lp[3:4], axis=1, keepdims=True)
    return jnp.exp(a) - jnp.exp(b) + lam_init


def _diff_kernel(first_ref, q1_ref, q2_ref, k_ref, v_ref, slope_ref, lam_ref, g_ref, o_ref, m_sc, l_sc, a_sc, *,
                 tk, tiles_per_iter, lam_init):
    g = pl.program_id(1)
    i = pl.program_id(2)
    first = first_ref[pl.program_id(0), g, i]
    tq = q1_ref.shape[1]
    m_sc[...] = jnp.full(m_sc.shape, NEG, F32)
    l_sc[...] = jnp.zeros(l_sc.shape, F32)
    a_sc[...] = jnp.zeros(a_sc.shape, F32)
    row = lax.broadcasted_iota(jnp.int32, (tk, tk), 0)
    col = lax.broadcasted_iota(jnp.int32, (tk, tk), 1)
    rel_diag = (2 * jnp.minimum(row, col) - col).astype(F32)
    visible = lax.shift_right_logical(col, 6) <= lax.shift_right_logical(row, 6)
    lane = lax.broadcasted_iota(jnp.int32, (1, tk), 1)

    def tile(j, rows, local):
        ks = pl.ds(pl.multiple_of(j * tk, tk), tk)
        if local is None:
            rel = (lane + (j * tk - i * tq)).astype(F32)
        else:
            rel = rel_diag + float(local)
        for h in range(GROUP_HEADS):
            hs = slice(h * HEAD_PAD, (h + 1) * HEAD_PAD)
            bias = (slope_ref[g * GROUP_HEADS + h][:, 0:1] * LOG2E) * rel
            if local is not None:
                bias = jnp.where(visible, bias, NEG)
            k = k_ref[0, ks, hs]
            v = v_ref[0, ks, hs]
            nrows = rows.stop - rows.start
            scores = _nt_dot(jnp.concatenate([q1_ref[0, rows, hs], q2_ref[0, rows, hs]], axis=0), k)
            probs, alphas = [], []
            for m in range(2):
                n = 2 * h + m
                t = scores[m * nrows:(m + 1) * nrows] + bias
                m_old = m_sc[n, rows, :]
                m_new = jnp.maximum(m_old, jnp.max(t, axis=1, keepdims=True))
                alpha = jnp.exp2(m_old - m_new)
                p = jnp.exp2(t - _lane_tile(m_new, tk))
                l_sc[n, rows, :] = alpha * l_sc[n, rows, :] + jnp.sum(p, axis=1, keepdims=True)
                m_sc[n, rows, :] = m_new
                probs.append(p.astype(BF16))
                alphas.append(alpha)
            pv = _dot(jnp.concatenate(probs, axis=0), v)
            for m in range(2):
                n = 2 * h + m
                a_sc[n, rows, :] = alphas[m] * a_sc[n, rows, :] + pv[m * nrows:(m + 1) * nrows]

    _causal_sweep(i, first, tq, tk, tiles_per_iter, tile)
    lam = _diff_lambda(lam_ref[...], lam_init)
    for h in range(GROUP_HEADS):
        out = a_sc[2 * h] / l_sc[2 * h] - lam * (a_sc[2 * h + 1] / l_sc[2 * h + 1])
        o_ref[0, :, h * HEAD_PAD:(h + 1) * HEAD_PAD] = (_rms(out, g_ref[...]) * (1.0 - lam_init)).astype(BF16)


def _diff_call(first, qc1_p, qc2_p, kc_b, vc_b, slopes, lam_p, g_diff, lam_init, tq, tk):
    assert CHUNK == 64 and tk % CHUNK == 0
    b, t, _ = qc1_p.shape
    qspec = pl.BlockSpec((1, tq, W_GROUP), lambda b_, g, i, first: (b_, i, g))
    stat = pltpu.VMEM((2 * GROUP_HEADS, tq, LANES), F32)
    grid_spec = pltpu.PrefetchScalarGridSpec(
        num_scalar_prefetch=1,
        grid=(b, N_HEADS // GROUP_HEADS, t // tq),
        in_specs=[qspec, qspec, _resident_spec(t), _resident_spec(t),
                  pl.BlockSpec((N_HEADS, 1, LANES), lambda b_, g, i, first: (0, 0, 0)),
                  pl.BlockSpec((4, HD_C), lambda b_, g, i, first: (0, 0)),
                  pl.BlockSpec((1, HEAD_PAD), lambda b_, g, i, first: (0, 0))],
        out_specs=qspec,
        scratch_shapes=[stat, stat, pltpu.VMEM((2 * GROUP_HEADS, tq, HEAD_PAD), F32)])
    return pl.pallas_call(
        functools.partial(_diff_kernel, tk=tk, tiles_per_iter=1, lam_init=lam_init),
        grid_spec=grid_spec,
        out_shape=jax.ShapeDtypeStruct((b, t, W_C), BF16),
        compiler_params=_params(("parallel", "parallel", "arbitrary")),
        name="diff",
    )(first, qc1_p, qc2_p, kc_b, vc_b, slopes, lam_p, g_diff)


def _gla_kernel(q_ref, k_ref, v_ref, la_ref, r_ref, s0_ref, g_ref, o_ref, s_ref, *, chunk):
    ti = pl.program_id(1)
    n_chunks = q_ref.shape[1] // chunk

    @pl.when(ti == 0)
    def _():
        s_ref[0] = s0_ref[0]

    tri_r = lax.broadcasted_iota(jnp.int32, (chunk, chunk), 0)
    tri_c = lax.broadcasted_iota(jnp.int32, (chunk, chunk), 1)
    incl = (tri_c <= tri_r).astype(F32)
    srow = lax.broadcasted_iota(jnp.int32, (N_HEADS * chunk, LANES), 0) // chunk
    slane = lax.broadcasted_iota(jnp.int32, (N_HEADS * chunk, LANES), 1) // DK_B
    qmask = srow == slane
    arow = lax.broadcasted_iota(jnp.int32, (N_HEADS * chunk, chunk), 0) % chunk
    acol = lax.broadcasted_iota(jnp.int32, (N_HEADS * chunk, chunk), 1)
    causal = acol <= arow
    vlane = lax.broadcasted_iota(jnp.int32, (chunk, W_B), 1) // DV_B
    st_row = lax.broadcasted_iota(jnp.int32, (W_B, LANES), 0) // DV_B
    st_lane = lax.broadcasted_iota(jnp.int32, (W_B, LANES), 1) // DK_B
    diag_blocks = st_row == st_lane
    gr = lax.broadcasted_iota(jnp.int32, (W_B, W_B), 0) // DV_B
    gc = lax.broadcasted_iota(jnp.int32, (W_B, W_B), 1) // DV_B
    group_mean = jnp.where(gr == gc, 1.0 / DV_B, 0.0)

    for c in range(n_chunks):
        sl = slice(c * chunk, (c + 1) * chunk)
        cb = _dot_exact_lhs(incl, la_ref[0, sl, :])
        cb_last = cb[chunk - 1:chunk, :]
        q = q_ref[0, sl, :] * (DK_B ** -0.5)
        k = k_ref[0, sl, :]
        v = v_ref[0, sl, :]
        vb = v.astype(BF16)
        qe = q * jnp.exp(cb)
        ke = (k * jnp.exp(-cb)).astype(BF16)
        kl = (k * jnp.exp(cb_last - cb)).astype(BF16)
        decay = jnp.exp(cb_last)
        qstack = jnp.where(qmask, jnp.concatenate([qe] * N_HEADS, axis=0), 0.0).astype(BF16)
        a = jnp.where(causal, _nt_dot(qstack, ke), 0.0).astype(BF16)
        oi = _dot(a, vb)
        o = jnp.zeros((chunk, W_B), F32)
        for h in range(N_HEADS):
            o = o + jnp.where(vlane == h, oi[h * chunk:(h + 1) * chunk], 0.0)
        st = s_ref[0]
        o = o + _nt_dot(qe.astype(BF16), st.astype(BF16))
        upd = _dot(v.T.astype(BF16), kl)
        s_ref[0] = decay * st + jnp.where(diag_blocks, upd, 0.0)
        ms = _dot_exact_rhs(o * o, group_mean)
        r = r_ref[0, sl, :]
        o_ref[0, sl, :] = (o * lax.rsqrt(ms + EPS) * g_ref[...] * (r * jax.nn.sigmoid(r))).astype(BF16)


def _gla_call(qb, kb, vb, loga, rb, s0_t, g_gla, tg, chunk):
    b, t, _ = qb.shape
    tok = lambda w: pl.BlockSpec((1, tg, w), lambda b_, i: (b_, i, 0))
    sspec = pl.BlockSpec((1, W_B, LANES), lambda b_, i: (b_, 0, 0))
    return pl.pallas_call(
        functools.partial(_gla_kernel, chunk=chunk),
        grid=(b, t // tg),
        in_specs=[tok(LANES), tok(LANES), tok(W_B), tok(LANES), tok(W_B), sspec,
                  pl.BlockSpec((1, W_B), lambda b_, i: (0, 0))],
        out_specs=[tok(W_B), sspec],
        out_shape=[jax.ShapeDtypeStruct((b, t, W_B), BF16), jax.ShapeDtypeStruct((b, W_B, LANES), F32)],
        compiler_params=_params(("parallel", "arbitrary")),
        name="gla",
    )(qb, kb, vb, loga, rb, s0_t, g_gla)


def _softmax_two(t_past, t_new, exp_fn):
    m = jnp.maximum(jnp.max(t_past, axis=1, keepdims=True), jnp.max(t_new, axis=1, keepdims=True))
    p_past = exp_fn(t_past - m)
    p_new = exp_fn(t_new - m)
    l = jnp.sum(p_past, axis=1, keepdims=True) + jnp.sum(p_new, axis=1, keepdims=True)
    return p_past, p_new, l


def _fox_s_kernel(q_ref, kn_ref, vn_ref, kpt_ref, vpt_ref, c_ref, o_ref):
    t, past = q_ref.shape[1], kpt_ref.shape[1]
    q = q_ref[0]
    kn = kn_ref[...].astype(BF16)
    vn = vn_ref[...].astype(BF16)
    kpt = kpt_ref[...].astype(BF16)
    vpt = vpt_ref[...].astype(BF16)
    lane_head = lax.broadcasted_iota(jnp.int32, (t, W_A), 1) // HD_A
    row = lax.broadcasted_iota(jnp.int32, (t, t), 0)
    col = lax.broadcasted_iota(jnp.int32, (t, t), 1)
    o = jnp.zeros((t, W_A), F32)
    for h in range(N_HEADS):
        qh = jnp.where(lane_head == h, q, 0.0).astype(BF16)
        ch = c_ref[0, h:h + 1, :]
        cref = ch[:, past:past + 1]
        t_past = _dot(qh, kpt) + (cref - ch[:, :past])
        t_new = jnp.where(col <= row, _nt_dot(qh, kn) + (cref - ch[:, past:past + t]), NEG)
        p_past, p_new, l = _softmax_two(t_past, t_new, jnp.exp)
        oh = (_nt_dot(p_past.astype(BF16), vpt) + _dot(p_new.astype(BF16), vn)) / l
        o = o + jnp.where(lane_head == h, oh, 0.0)
    o_ref[0] = o.astype(BF16)


def _fox_s_call(layer, qa, ka, va, cache_kt, cache_vt, c):
    b, t, _ = qa.shape
    past = cache_kt.shape[3]
    new = pl.BlockSpec((1, t, W_A), lambda i: (i, 0, 0))
    new_l = lambda: pl.BlockSpec((None, t, W_A), lambda i: (layer, i, 0))
    old_l = lambda: pl.BlockSpec((None, None, W_A, past), lambda i: (layer, i, 0, 0))
    return pl.pallas_call(
        _fox_s_kernel,
        grid=(b,),
        in_specs=[new, new_l(), new_l(), old_l(), old_l(),
                  pl.BlockSpec((1, N_HEADS, c.shape[2]), lambda i: (i, 0, 0))],
        out_specs=new,
        out_shape=jax.ShapeDtypeStruct((b, t, W_A), BF16),
        compiler_params=_params(("parallel",)),
        name="fox_s",
    )(qa, ka, va, cache_kt, cache_vt, c)


def _diff_s_kernel(q1_ref, q2_ref, kn_ref, vn_ref, kp_ref, vp_ref, slope_ref, lam_ref, g_ref, o_ref, *, lam_init):
    t, past = q1_ref.shape[1], kp_ref.shape[0] // N_HEADS
    lam = _diff_lambda(lam_ref[...], lam_init)
    row = lax.broadcasted_iota(jnp.int32, (t, t), 0) + past
    col = lax.broadcasted_iota(jnp.int32, (t, t), 1) + past
    visible_new = lax.shift_right_logical(col, 6) <= lax.shift_right_logical(row, 6)
    dist_new = jnp.abs(row - col).astype(F32)
    prow = lax.broadcasted_iota(jnp.int32, (t, past), 0) + past
    pcol = lax.broadcasted_iota(jnp.int32, (t, past), 1)
    visible_past = lax.shift_right_logical(pcol, 6) <= lax.shift_right_logical(prow, 6)
    dist_past = (prow - pcol).astype(F32)
    for h in range(N_HEADS):
        hs = slice(h * HEAD_PAD, (h + 1) * HEAD_PAD)
        slope = slope_ref[h][:, 0:1] * LOG2E
        kn = kn_ref[pl.ds(h, t, stride=N_HEADS), :].astype(BF16)
        vn = vn_ref[pl.ds(h, t, stride=N_HEADS), :].astype(BF16)
        kp = kp_ref[pl.ds(h, past, stride=N_HEADS), :].astype(BF16)
        vp = vp_ref[pl.ds(h, past, stride=N_HEADS), :].astype(BF16)
        outs = []
        for q_ref in (q1_ref, q2_ref):
            q = q_ref[0, :, hs]
            t_past = jnp.where(visible_past, _nt_dot(q, kp) - slope * dist_past, NEG)
            t_new = jnp.where(visible_new, _nt_dot(q, kn) - slope * dist_new, NEG)
            p_past, p_new, l = _softmax_two(t_past, t_new, jnp.exp2)
            outs.append((_dot(p_past.astype(BF16), vp) + _dot(p_new.astype(BF16), vn)) / l)
        out = outs[0] - lam * outs[1]
        o_ref[0, :, hs] = (_rms(out, g_ref[...]) * (1.0 - lam_init)).astype(BF16)


def _diff_s_call(layer, qc1_p, qc2_p, kc, vc, cache_k, cache_v, slopes, lam_p, g_diff, lam_init):
    assert CHUNK == 64
    b, t, _ = qc1_p.shape
    rows_past = cache_k.shape[2]
    new = lambda: pl.BlockSpec((1, t, W_C), lambda i: (i, 0, 0))
    new_l = lambda: pl.BlockSpec((None, t * N_HEADS, HEAD_PAD), lambda i: (layer, i, 0))
    old_l = lambda: pl.BlockSpec((None, None, rows_past, HEAD_PAD), lambda i: (layer, i, 0, 0))
    return pl.pallas_call(
        functools.partial(_diff_s_kernel, lam_init=lam_init),
        grid=(b,),
        in_specs=[new(), new(), new_l(), new_l(), old_l(), old_l(),
                  pl.BlockSpec((N_HEADS, 1, LANES), lambda i: (0, 0, 0)),
                  pl.BlockSpec((4, HD_C), lambda i: (0, 0)),
                  pl.BlockSpec((1, HEAD_PAD), lambda i: (0, 0))],
        out_specs=new(),
        out_shape=jax.ShapeDtypeStruct((b, t, W_C), BF16),
        compiler_params=_params(("parallel",)),
        name="diff_s",
    )(qc1_p, qc2_p, kc, vc, cache_k, cache_v, slopes, lam_p, g_diff)


def _out_kernel(x_ref, oa_ref, gla_ref, dif_ref, gate_ref, g2_ref, woa_ref, wog_ref, wod_ref, o_ref):
    y = _dot(oa_ref[...], woa_ref[...]) + _dot(gla_ref[...], wog_ref[...]) + _dot(dif_ref[...], wod_ref[...])
    o_ref[...] = x_ref[...] + gate_ref[0] * _rms(y, g2_ref[...])


def _out_call(x, oa, gla, dif, mod, g2, w_oa, w_og, w_od, tm, tiles_per_mod):
    n = x.shape[0]
    gate = mod[2]
    r = gate.shape[1]
    tok = lambda w: pl.BlockSpec((tm, w), lambda i: (i, 0))
    return pl.pallas_call(
        _out_kernel,
        grid=(n // tm,),
        in_specs=[tok(D_MODEL), tok(oa.shape[1]), tok(W_B), tok(W_C), _mod_spec(r, tiles_per_mod),
                  _const_spec((1, D_MODEL)), _const_spec(w_oa.shape), _const_spec(w_og.shape),
                  _const_spec(w_od.shape)],
        out_specs=tok(D_MODEL),
        out_shape=jax.ShapeDtypeStruct((n, D_MODEL), F32),
        compiler_params=_params(("parallel",)),
        name="out_proj",
    )(x, oa, gla, dif, gate, g2, w_oa, w_og, w_od)


def _pad_heads(w, hd, off=0):
    rows = w.shape[0]
    zeros = lambda width: [jnp.zeros((rows, width), w.dtype)] if width else []
    pieces = []
    for h in range(N_HEADS):
        pieces += zeros(off) + [w[:, h * hd:(h + 1) * hd]] + zeros(HEAD_PAD - off - hd)
    return jnp.concatenate(pieces, axis=1)


def _pad_lanes(w, width=LANES):
    return jnp.pad(w, ((0, 0), (0, width - w.shape[1])))


def _layer_params(l, w_in, b_f, w_gla_up, b_gla_up, g_gla, g_diff, w_out):
    offs = [int(o) for o in np.cumsum(PROJ_SIZES)[:-1]]
    qa, ka, va, fa, qb, kb, vb, gb, rb, qc, kc, vc = jnp.split(w_in[l], offs, axis=1)
    segs = {
        "ka": ka, "va": va, "kc": kc, "vc": vc, "qa": qa, "qc": qc,
        "qa_p": _pad_heads(qa, HD_A), "ka_p": _pad_heads(ka, HD_A), "va_p": _pad_heads(va, HD_A),
        "fa": _pad_lanes(fa), "qb": qb, "kb": kb, "vb": vb, "rb": rb, "gb": _pad_lanes(gb),
    }
    w_ext = jnp.concatenate([segs[n] for n, _ in _SEG_WIDTHS], axis=1).astype(BF16)
    wo = w_out[l]
    vone = np.zeros((1, W_PAD), np.float32)
    vone[0, HD_A::HEAD_PAD] = 1.0
    return dict(
        w_ext=w_ext, w_kvt=jnp.concatenate([ka, va], axis=1).T.astype(BF16),
        bf=_pad_lanes(b_f[l][None, :]),
        wup=jnp.pad(w_gla_up[l], ((0, LANES - GLA_RANK), (0, 0))).astype(BF16),
        bup=b_gla_up[l][None, :],
        vone=jnp.asarray(vone),
        g_gla=jnp.tile(g_gla[l], N_HEADS)[None, :], g_diff=g_diff[l][None, :],
        w_oa=wo[:W_A].astype(BF16),
        w_oa_p=_pad_heads(wo[:W_A].T, HD_A).T.astype(BF16),
        w_og=wo[W_A:W_A + W_B].astype(BF16), w_od=wo[W_A + W_B:].astype(BF16),
    )


def _state_to_blockdiag_t(s):
    eye = jnp.eye(N_HEADS, dtype=s.dtype)
    return jnp.einsum("bhkv,hg->bhvgk", s, eye).reshape(s.shape[0], W_B, N_HEADS * DK_B)


def _blockdiag_t_to_state(st):
    b = st.shape[0]
    s5 = st.reshape(b, N_HEADS, DV_B, N_HEADS, DK_B)
    diag = jnp.stack([s5[:, h, :, h, :] for h in range(N_HEADS)], axis=1)
    return jnp.swapaxes(diag, 2, 3)


SKIP_LOG2 = 160.0
NORM_SLACK = 1.01


def _tile_max(x, factor):
    return jnp.max(x.reshape(x.shape[:-1] + (x.shape[-1] // factor, factor)), axis=-1)


def _first_tiles(qn, kn, gap, tq, tk):
    b, h, nq = qn.shape
    r = tq // tk
    kn_diag = _tile_max(kn, r)
    bound = NORM_SLACK * (qn[..., None] * kn[:, :, None, :] + (qn * kn_diag)[..., None]) + gap
    dead = (bound < -SKIP_LOG2).reshape(b, h // GROUP_HEADS, GROUP_HEADS, nq, kn.shape[-1]).all(axis=2)
    lead = jnp.sum(jnp.cumprod(dead.astype(jnp.int32), axis=-1), axis=-1)
    return jnp.minimum(lead, jnp.arange(nq, dtype=jnp.int32) * r).astype(jnp.int32)


def _tile(n, pref):
    t = min(n, pref)
    assert n % t == 0, (n, t)
    return t


def _trunk(x, mods, params, ffn_w, g_norm, lam_params, mixer, tm, tiles_per_mod, with_qa):
    b, t, _ = x.shape
    n = b * t
    depth = len(params)
    xf = x.reshape(n, D_MODEL)
    stacked = None
    gla_states = []
    for l in range(depth):
        p = params[l]
        gn = lambda k: g_norm[l, k][None, :]
        xf = _ffn_call(xf, mods[l][0], gn(0), gn(1), *ffn_w, l, 0, 0.5, tm, tiles_per_mod)
        pr = _proj_call(xf, mods[l][1], gn(2), p["w_ext"], p["w_kvt"], p["bf"], p["wup"], p["bup"], p["vone"], tm,
                        tiles_per_mod, l, depth, stacked, with_qa, t)
        stacked = {nm: pr[nm] for nm, _, _ in _STACKED_OUTS}
        lam_init = 0.8 - 0.6 * math.exp(-0.3 * l)
        oa, gla, dif, s_fin, w_oa = mixer(l, pr, p, lam_params[l], lam_init, b, t)
        gla_states.append(s_fin)
        xf = _out_call(xf, oa, gla, dif, mods[l][1], gn(3), w_oa, p["w_og"], p["w_od"], tm, tiles_per_mod)
        xf = _ffn_call(xf, mods[l][2], gn(4), gn(5), *ffn_w, l, 1, 0.5, tm, tiles_per_mod)
    shp = lambda nm, *tail: stacked[nm].reshape((depth, b, t) + tail)
    if stacked["ka"].ndim == 4:
        shp_a = lambda nm: jnp.transpose(stacked[nm].reshape(depth, b, N_HEADS, HD_A, t), (0, 1, 4, 2, 3))
    else:
        shp_a = lambda nm: shp(nm, N_HEADS, HD_A)
    states = [shp_a("ka"), shp_a("va"), shp("logf", N_HEADS), jnp.stack(gla_states),
              shp("kc", N_HEADS, 2 * HD_C), shp("vc", N_HEADS, 2 * HD_C)]
    return xf.reshape(b, t, D_MODEL), states


def kernel(x_prompt, x_sample, c_prompt, c_sample, cache_fox_k, cache_fox_v, cache_fox_logf, state_gla,
           cache_diff_k, cache_diff_v, w_ada, b_ada, g_norm, w_ffn_in, w_ffn_out, w_in, b_f, w_gla_up,
           b_gla_up, g_gla, g_diff, lam_params, w_out):
    depth = w_in.shape[0]
    bp, tp, _ = x_prompt.shape
    bs, ts, _ = x_sample.shape
    past = cache_fox_k.shape[2]

    params = [_layer_params(l, w_in, b_f, w_gla_up, b_gla_up, g_gla, g_diff, w_out) for l in range(depth)]
    ffn_w = (w_ffn_in.astype(BF16), w_ffn_out.astype(BF16))

    slopes = 2.0 ** (-8.0 * jnp.arange(1, N_HEADS + 1, dtype=F32) / N_HEADS)
    slopes = jnp.broadcast_to(slopes[:, None, None], (N_HEADS, 1, LANES))

    rows = bp + bs
    rows_pad = -(-rows // 8) * 8
    c_all = jnp.pad(jnp.concatenate([c_prompt, c_sample], axis=0), ((0, rows_pad - rows), (0, 0)))
    mods_p, mods_s = [], []
    for l in range(depth):
        mod = _ada_call(c_all, w_ada[l].astype(BF16), b_ada[l][None, :]).reshape(rows_pad, N_SUB, 3, D_MODEL)
        mods_p.append([[mod[:bp, s, k][:, None, :] for k in range(3)] for s in range(N_SUB)])
        mods_s.append([[jnp.repeat(mod[bp:rows, s, k], ts, axis=0)[None] for k in range(3)] for s in range(N_SUB)])

    tm_p = _tile(tp, 512)
    tq = _tile(tp, 1024)
    tk = _tile(tp, 512)
    tg = _tile(tp, 512)

    def prompt_mixer(l, pr, p, lam_p, lam_init, b, t):
        r3 = lambda a: a.reshape(b, t, a.shape[-1])
        logf_t = jnp.swapaxes(r3(pr["logf"][l]), 1, 2).reshape(b * N_HEADS, t // LANES, LANES)
        c = _cumsum_call(logf_t).reshape(b, N_HEADS, t)
        nrm = jnp.sqrt(pr["norms"][:, 0, :N_HEADS * len(_NORM_ROWS)])
        nrm = dict(zip(_NORM_ROWS, jnp.moveaxis(nrm.reshape(b, t // tm_p, len(_NORM_ROWS), N_HEADS), (2, 3), (0, 2))))
        q_tiles = lambda a: _tile_max(a, tq // tm_p)
        k_tiles = lambda a: _tile_max(a, tk // tm_p)
        gap_a = (c[:, :, ::tq][..., None] - c[:, :, tk - 1::tk][:, :, None, :]) * LOG2E
        first_a = _first_tiles(q_tiles(nrm["qa_p"]), k_tiles(nrm["ka_p"]), gap_a, tq, tk)
        oa = _fox_call(first_a, r3(pr["qa_p"]), r3(pr["ka_p"]), r3(pr["va_p"]), c, tq, tk)
        s0_t = jnp.zeros((b, W_B, N_HEADS * DK_B), F32)
        gla, s_t = _gla_call(r3(pr["qb"]), r3(pr["kb"]), r3(pr["vb"]), r3(pr["loga"]), r3(pr["rb"]), s0_t,
                             p["g_gla"], tg, min(CHUNK, t))
        last_key = jnp.arange(tk - 1, t, tk, dtype=F32)[None, :] - jnp.arange(0, t, tq, dtype=F32)[:, None]
        gap_c = slopes[:, 0, :1, None] * LOG2E * last_key[None]
        first_c = _first_tiles(q_tiles(jnp.maximum(nrm["qc1_p"], nrm["qc2_p"])), k_tiles(nrm["kc_b"]), gap_c, tq, tk)
        dif = _diff_call(first_c, r3(pr["qc1_p"]), r3(pr["qc2_p"]), r3(pr["kc_b"]), r3(pr["vc_b"]), slopes, lam_p,
                         p["g_diff"], lam_init, tq, tk)
        flat = lambda a: a.reshape(b * t, a.shape[-1])
        return flat(oa), flat(gla), flat(dif), _blockdiag_t_to_state(s_t), p["w_oa_p"]

    feature_time = lambda a: jnp.transpose(a, (0, 1, 3, 4, 2)).reshape(depth, bs, W_A, past)
    cache_kt, cache_vt = feature_time(cache_fox_k), feature_time(cache_fox_v)
    cache_dk = cache_diff_k.reshape(depth, bs, past * N_HEADS, HEAD_PAD)
    cache_dv = cache_diff_v.reshape(depth, bs, past * N_HEADS, HEAD_PAD)

    def sample_mixer(l, pr, p, lam_p, lam_init, b, t):
        r3 = lambda a: a.reshape(b, t, a.shape[-1])
        total = past + t
        nb = -(-total // (8 * LANES)) * 8
        lf = jnp.concatenate([cache_fox_logf[l].astype(F32), r3(pr["logf"][l])], axis=1)
        lf = jnp.pad(jnp.swapaxes(lf, 1, 2), ((0, 0), (0, 0), (0, nb * LANES - total)))
        c = _cumsum_call(lf.reshape(b * N_HEADS, nb, LANES)).reshape(b, N_HEADS, nb * LANES)
        oa = _fox_s_call(l, r3(pr["qa"]), pr["ka"], pr["va"], cache_kt, cache_vt, c)
        gla, s_t = _gla_call(r3(pr["qb"]), r3(pr["kb"]), r3(pr["vb"]), r3(pr["loga"]), r3(pr["rb"]),
                             _state_to_blockdiag_t(state_gla[l].astype(F32)), p["g_gla"], t, min(CHUNK, t))
        dif = _diff_s_call(l, r3(pr["qc1_p"]), r3(pr["qc2_p"]), pr["kc"], pr["vc"], cache_dk, cache_dv,
                           slopes, lam_p, p["g_diff"], lam_init)
        flat = lambda a: a.reshape(b * t, a.shape[-1])
        return flat(oa), flat(gla), flat(dif), _blockdiag_t_to_state(s_t), p["w_oa"]

    y_p, sp = _trunk(x_prompt, mods_p, params, ffn_w, g_norm, lam_params, prompt_mixer, tm_p, tp // tm_p, False)
    y_s, ss = _trunk(x_sample, mods_s, params, ffn_w, g_norm, lam_params, sample_mixer, bs * ts, 1, True)
    return (y_p, y_s, sp[0], sp[1], sp[2], sp[3], sp[4], sp[5], ss[0], ss[1], ss[2], ss[3], ss[4], ss[5])
```

```python
import functools
import math

import jax
import jax.numpy as jnp
import numpy as np
from jax import lax
from jax.experimental import pallas as pl
from jax.experimental.pallas import tpu as pltpu

F32 = jnp.float32
BF16 = jnp.bfloat16

D_MODEL = 1024
N_HEADS = 4
HD_A = 64
DK_B = 32
DV_B = 64
GLA_RANK = 16
GLA_TAU = 16.0
HD_C = 64
CHUNK = 64
W_A = N_HEADS * HD_A
W_B = N_HEADS * DV_B
W_C = N_HEADS * 2 * HD_C
D_FF = ((8 * D_MODEL // 3 + 255) // 256) * 256
N_SUB = 3
PROJ_SIZES = (W_A, W_A, W_A, N_HEADS, N_HEADS * DK_B, N_HEADS * DK_B, W_B, GLA_RANK, W_B, W_C, W_C, W_C)
NEG = -1e30
EPS = 1e-6
LOG2E = math.log2(math.e)

LANES = 128
HEAD_PAD = 128
W_PAD = N_HEADS * HEAD_PAD
VMEM_LIMIT = 56 * 1024 * 1024

_SEG_WIDTHS = (
    ("ka", W_A), ("va", W_A), ("kc", W_C), ("vc", W_C), ("qa", W_A),
    ("qa_p", W_PAD), ("ka_p", W_PAD), ("va_p", W_PAD), ("qc", W_C),
    ("fa", LANES), ("qb", LANES), ("kb", LANES), ("vb", W_B), ("rb", W_B), ("gb", LANES),
)
_SEG = {}
_off = 0
for _name, _w in _SEG_WIDTHS:
    _SEG[_name] = (_off, _off + _w)
    _off += _w
N_EXT = _off


def _nt_dot(a, b):
    return lax.dot_general(a, b, (((1,), (1,)), ((), ())), preferred_element_type=F32)


def _dot(a, b):
    return jnp.dot(a, b, preferred_element_type=F32)


def _split3(x):
    hi = x.astype(BF16)
    r1 = x - hi.astype(F32)
    mid = r1.astype(BF16)
    lo = (r1 - mid.astype(F32)).astype(BF16)
    return hi, mid, lo


def _dot_exact_rhs(x, m):
    mb = m.astype(BF16)
    hi, mid, lo = _split3(x)
    return _dot(hi, mb) + _dot(mid, mb) + _dot(lo, mb)


def _dot_exact_lhs(m, x):
    mb = m.astype(BF16)
    hi, mid, lo = _split3(x)
    return _dot(mb, hi) + _dot(mb, mid) + _dot(mb, lo)


def _rms(x, g):
    return x * lax.rsqrt(jnp.mean(x * x, axis=-1, keepdims=True) + EPS) * g


def _log_sigmoid(x):
    return jnp.minimum(x, 0.0) - jnp.log1p(jnp.exp(-jnp.abs(x)))


def _const_spec(shape):
    nd = len(shape)
    return pl.BlockSpec(shape, lambda *_: (0,) * nd, pipeline_mode=pl.Buffered(1))


def _params(sem):
    return pltpu.CompilerParams(dimension_semantics=sem, vmem_limit_bytes=VMEM_LIMIT)


def _ada_kernel(c_ref, w_ref, b_ref, o_ref):
    c = c_ref[...]
    a = (c * jax.nn.sigmoid(c)).astype(BF16)
    o_ref[...] = _dot(a, w_ref[...]) + b_ref[...]


def _ada_call(c, w, b):
    m, n = c.shape[0], w.shape[1]
    tn = n // 8
    return pl.pallas_call(
        _ada_kernel,
        grid=(n // tn,),
        in_specs=[pl.BlockSpec((m, D_MODEL), lambda i: (0, 0)),
                  pl.BlockSpec((D_MODEL, tn), lambda i: (0, i)),
                  pl.BlockSpec((1, tn), lambda i: (0, i))],
        out_specs=pl.BlockSpec((m, tn), lambda i: (0, i)),
        out_shape=jax.ShapeDtypeStruct((m, n), F32),
        compiler_params=_params(("parallel",)),
        name="ada",
    )(c, w, b)


FF_CHUNK = 256


def _ffn_kernel(x_ref, shift_ref, scale_ref, gate_ref, g1_ref, g2_ref, win_ref, wout_ref, o_ref, *, res_w):
    x = x_ref[...]
    h = _rms(x, g1_ref[...]) * (1.0 + scale_ref[0]) + shift_ref[0]
    hb = h.astype(BF16)
    y = jnp.zeros(x.shape, F32)
    for c in range(D_FF // FF_CHUNK):
        lo, hi = c * FF_CHUNK, (c + 1) * FF_CHUNK
        g = _dot(hb, win_ref[:, lo:hi])
        u = _dot(hb, win_ref[:, D_FF + lo:D_FF + hi])
        a = (g * jax.nn.sigmoid(g) * u).astype(BF16)
        y = y + _dot(a, wout_ref[lo:hi, :])
    o_ref[...] = x + res_w * gate_ref[0] * _rms(y, g2_ref[...])


def _mod_spec(r, tiles_per_mod, col):
    return pl.BlockSpec((1, r, D_MODEL), lambda i: (i // tiles_per_mod, 0, col))


def _ffn_call(x, mod, g1, g2, w_in, w_out, layer, which, res_w, tm, tiles_per_mod):
    n = x.shape[0]
    mod_all, sub = mod
    r = mod_all.shape[1]
    tok = pl.BlockSpec((tm, D_MODEL), lambda i: (i, 0))
    pick = lambda rows, cols: pl.BlockSpec((None, None, rows, cols), lambda i: (layer, which, 0, 0),
                                           pipeline_mode=pl.Buffered(1))
    return pl.pallas_call(
        functools.partial(_ffn_kernel, res_w=res_w),
        grid=(n // tm,),
        in_specs=[tok] + [_mod_spec(r, tiles_per_mod, 3 * sub + k) for k in range(3)]
        + [_const_spec((1, D_MODEL)), _const_spec((1, D_MODEL)), pick(D_MODEL, 2 * D_FF), pick(D_FF, D_MODEL)],
        out_specs=tok,
        out_shape=jax.ShapeDtypeStruct((n, D_MODEL), F32),
        compiler_params=_params(("parallel",)),
        name="ffn",
    )(x, mod_all, mod_all, mod_all, g1, g2, w_in, w_out)


_STACKED_OUTS = (
    ("ka", 1, W_A), ("va", 1, W_A), ("kc", N_HEADS, HEAD_PAD), ("vc", N_HEADS, HEAD_PAD), ("logf", 1, N_HEADS),
)
_LAYER_OUTS = (
    ("qa_p", W_PAD, BF16), ("ka_p", W_PAD, BF16), ("va_p", W_PAD, BF16),
    ("qc1_p", W_PAD, BF16), ("qc2_p", W_PAD, BF16), ("kc_b", W_C, BF16), ("vc_b", W_C, BF16),
    ("qb", LANES, F32), ("kb", LANES, F32), ("vb", W_B, F32), ("loga", LANES, F32), ("rb", W_B, F32),
)
_NORM_ROWS = ("qa_p", "ka_p", "qc1_p", "qc2_p", "kc_b")


def _proj_kernel(x_ref, shift_ref, scale_ref, g1_ref, w_ref, wkvt_ref, bf_ref, wup_ref, bup_ref, vone_ref, *refs,
                 n_alias, with_qa, transposed_a):
    refs = refs[n_alias:]
    names = [n for n, _, _ in _STACKED_OUTS] + [n for n, _, _ in _LAYER_OUTS] + (["qa"] if with_qa else [])
    o = dict(zip(names + ["norms"], refs))
    tm = x_ref.shape[0]
    x = x_ref[...]
    hb = (_rms(x, g1_ref[...]) * (1.0 + scale_ref[0]) + shift_ref[0]).astype(BF16)

    def seg(name):
        lo, hi = _SEG[name]
        return _dot(hb, w_ref[:, lo:hi])

    def store_heads(ref, val):
        for h in range(N_HEADS):
            ref[0, pl.ds(h, tm, stride=N_HEADS), :] = val[:, h * HEAD_PAD:(h + 1) * HEAD_PAD]

    if transposed_a:
        kvt = _nt_dot(wkvt_ref[...], hb)
        o["ka"][0, 0] = kvt[:W_A]
        o["va"][0, 0] = kvt[W_A:]
    else:
        o["ka"][0] = seg("ka")
        o["va"][0] = seg("va")
    kc = seg("kc")
    store_heads(o["kc"], kc)
    rounded = {"kc_b": kc.astype(BF16)}
    o["kc_b"][...] = rounded["kc_b"]
    vc = seg("vc")
    store_heads(o["vc"], vc)
    o["vc_b"][...] = vc.astype(BF16)
    if with_qa:
        o["qa"][...] = seg("qa") * (HD_A ** -0.5)
    rounded["qa_p"] = (seg("qa_p") * (HD_A ** -0.5 * LOG2E)).astype(BF16)
    rounded["ka_p"] = seg("ka_p").astype(BF16)
    o["qa_p"][...] = rounded["qa_p"]
    o["ka_p"][...] = rounded["ka_p"]
    o["va_p"][...] = (seg("va_p") + vone_ref[...]).astype(BF16)
    qc = seg("qc") * (HD_C ** -0.5 * LOG2E)
    first_map = lax.broadcasted_iota(jnp.int32, qc.shape, 1) % HEAD_PAD < HD_C
    rounded["qc1_p"] = jnp.where(first_map, qc, 0.0).astype(BF16)
    rounded["qc2_p"] = jnp.where(first_map, 0.0, qc).astype(BF16)
    o["qc1_p"][...] = rounded["qc1_p"]
    o["qc2_p"][...] = rounded["qc2_p"]
    fa = seg("fa") + bf_ref[...]
    o["logf"][0] = _log_sigmoid(fa)[:, :N_HEADS]
    o["qb"][...] = seg("qb")
    o["kb"][...] = seg("kb")
    o["vb"][...] = seg("vb")
    o["rb"][...] = seg("rb")
    gb = seg("gb").astype(BF16)
    o["loga"][...] = _log_sigmoid(_dot(gb, wup_ref[...]) + bup_ref[...]) * (1.0 / GLA_TAU)
    lane = lax.broadcasted_iota(jnp.int32, (8, LANES), 1)
    tile_norms = jnp.zeros((8, LANES), F32)
    for a, name in enumerate(_NORM_ROWS):
        sq = rounded[name].astype(F32)
        sq = sq * sq
        for h in range(N_HEADS):
            ss = jnp.sum(sq[:, h * HEAD_PAD:(h + 1) * HEAD_PAD], axis=1, keepdims=True)
            tile_norms = jnp.where(lane == N_HEADS * a + h, jnp.max(ss, axis=0, keepdims=True), tile_norms)
    o["norms"][0] = tile_norms


def _proj_call(x, mod, g1, w_ext, w_kvt, bf, wup, bup, vone, tm, tiles_per_mod, layer, depth, stacked, with_qa,
               seq_len):
    n = x.shape[0]
    mod_all, sub = mod
    r = mod_all.shape[1]
    tok = lambda w: pl.BlockSpec((tm, w), lambda i: (i, 0))
    layer_outs = _LAYER_OUTS + ((("qa", W_A, F32),) if with_qa else ())
    alias_in = [] if stacked is None else [stacked[nm] for nm, _, _ in _STACKED_OUTS]
    transposed_a = seq_len % tm == 0
    tps = max(seq_len // tm, 1)
    stacked_specs, stacked_shapes = [], []
    for nm, rows, w in _STACKED_OUTS:
        if transposed_a and nm in ("ka", "va"):
            stacked_specs.append(pl.BlockSpec((1, 1, w, tm), lambda i: (layer, i // tps, 0, i % tps)))
            stacked_shapes.append(jax.ShapeDtypeStruct((depth, n // seq_len, w, seq_len), F32))
        else:
            stacked_specs.append(pl.BlockSpec((1, tm * rows, w), lambda i: (layer, i, 0)))
            stacked_shapes.append(jax.ShapeDtypeStruct((depth, n * rows, w), F32))
    n_fixed = 10
    outs = pl.pallas_call(
        functools.partial(_proj_kernel, n_alias=len(alias_in), with_qa=with_qa, transposed_a=transposed_a),
        grid=(n // tm,),
        in_specs=[tok(D_MODEL), _mod_spec(r, tiles_per_mod, 3 * sub), _mod_spec(r, tiles_per_mod, 3 * sub + 1),
                  _const_spec((1, D_MODEL)), _const_spec((D_MODEL, N_EXT)), _const_spec((2 * W_A, D_MODEL)),
                  _const_spec((1, LANES)), _const_spec((LANES, LANES)), _const_spec((1, LANES)),
                  _const_spec((1, W_PAD))]
        + [pl.BlockSpec(memory_space=pl.ANY)] * len(alias_in),
        out_specs=stacked_specs + [tok(w) for _, w, _ in layer_outs]
        + [pl.BlockSpec((1, 8, LANES), lambda i: (i, 0, 0))],
        out_shape=stacked_shapes + [jax.ShapeDtypeStruct((n, w), dt) for _, w, dt in layer_outs]
        + [jax.ShapeDtypeStruct((n // tm, 8, LANES), F32)],
        input_output_aliases={n_fixed + k: k for k in range(len(alias_in))},
        compiler_params=_params(("parallel",)),
        name="proj",
    )(x, mod_all, mod_all, g1, w_ext, w_kvt, bf, wup, bup, vone, *alias_in)
    names = [nm for nm, _, _ in _STACKED_OUTS] + [nm for nm, _, _ in layer_outs] + ["norms"]
    return dict(zip(names, outs))


def _cumsum_kernel(x_ref, o_ref):
    x = x_ref[0]
    nb = x.shape[0]
    r = lax.broadcasted_iota(jnp.int32, (LANES, LANES), 0)
    c = lax.broadcasted_iota(jnp.int32, (LANES, LANES), 1)
    local = _dot_exact_rhs(x, (r <= c).astype(F32))
    tot = jnp.broadcast_to(local[:, LANES - 1:LANES], (nb, LANES))
    rr = lax.broadcasted_iota(jnp.int32, (nb, nb), 0)
    cc = lax.broadcasted_iota(jnp.int32, (nb, nb), 1)
    o_ref[0] = local + _dot_exact_lhs((cc < rr).astype(F32), tot)


def _cumsum_call(x):
    rows, nb, _ = x.shape
    spec = pl.BlockSpec((1, nb, LANES), lambda i: (i, 0, 0))
    return pl.pallas_call(
        _cumsum_kernel, grid=(rows,), in_specs=[spec], out_specs=spec,
        out_shape=jax.ShapeDtypeStruct(x.shape, F32),
        compiler_params=_params(("parallel",)), name="cumsum",
    )(x)


def _lane_tile(x, width):
    return jnp.concatenate([x] * (width // LANES), axis=1)


GROUP_HEADS = 2
W_GROUP = GROUP_HEADS * HEAD_PAD


def _causal_sweep(i, first, tq, tk, tiles_per_iter, tile):
    r = tq // tk

    def body(jj, carry):
        for u in range(tiles_per_iter):
            tile(jj * tiles_per_iter + u, slice(0, tq), None)
        return carry

    assert r % tiles_per_iter == 0
    lax.fori_loop(lax.div(first, tiles_per_iter), lax.div(i * r, tiles_per_iter), body, 0)
    for d in range(r):
        tile(i * r + d, slice(d * tk, (d + 1) * tk), d * tk)
        if d + 1 < r:
            tile(i * r + d, slice((d + 1) * tk, tq), None)


def _fox_kernel(first_ref, q_ref, k_ref, v_ref, c_ref, o_ref, m_sc, acc_sc, *, tk, tiles_per_iter):
    i = pl.program_id(2)
    first = first_ref[pl.program_id(0), pl.program_id(1), i]
    tq = q_ref.shape[1]
    m_sc[...] = jnp.full(m_sc.shape, NEG, F32)
    acc_sc[...] = jnp.zeros(acc_sc.shape, F32)
    row = lax.broadcasted_iota(jnp.int32, (tk, tk), 0)
    col = lax.broadcasted_iota(jnp.int32, (tk, tk), 1)
    causal = col <= row

    def tile(j, rows, local):
        ks = pl.ds(pl.multiple_of(j * tk, tk), tk)
        for h in range(GROUP_HEADS):
            hs = slice(h * HEAD_PAD, (h + 1) * HEAD_PAD)
            c_first = c_ref[0, h, pl.ds(i * (tq // tk), 1), :][:, 0:1]
            t = _nt_dot(q_ref[0, rows, hs], k_ref[0, ks, hs]) + (c_first - c_ref[0, h, pl.ds(j, 1), :]) * LOG2E
            if local is not None:
                t = jnp.where(causal, t, NEG)
            m_old = m_sc[h, rows, :]
            m_new = jnp.maximum(m_old, jnp.max(t, axis=1, keepdims=True))
            p = jnp.exp2(t - _lane_tile(m_new, tk)).astype(BF16)
            acc_sc[h, rows, :] = jnp.exp2(m_old - m_new) * acc_sc[h, rows, :] + _dot(p, v_ref[0, ks, hs])
            m_sc[h, rows, :] = m_new

    _causal_sweep(i, first, tq, tk, tiles_per_iter, tile)
    for h in range(GROUP_HEADS):
        acc = acc_sc[h]
        o_ref[0, :, h * HEAD_PAD:(h + 1) * HEAD_PAD] = (acc / acc[:, HD_A:HD_A + 1]).astype(BF16)


def _resident_spec(t):
    return pl.BlockSpec((1, t, W_GROUP), lambda b_, g, i, first: (b_, 0, g), pipeline_mode=pl.Buffered(1))


def _fox_call(first, qa_p, ka_p, va_p, c, tq, tk):
    b, t, _ = qa_p.shape
    qspec = pl.BlockSpec((1, tq, W_GROUP), lambda b_, g, i, first: (b_, i, g))
    grid_spec = pltpu.PrefetchScalarGridSpec(
        num_scalar_prefetch=1,
        grid=(b, N_HEADS // GROUP_HEADS, t // tq),
        in_specs=[qspec, _resident_spec(t), _resident_spec(t),
                  pl.BlockSpec((1, GROUP_HEADS, t // tk, tk), lambda b_, g, i, first: (b_, g, 0, 0))],
        out_specs=qspec,
        scratch_shapes=[pltpu.VMEM((GROUP_HEADS, tq, LANES), F32), pltpu.VMEM((GROUP_HEADS, tq, HEAD_PAD), F32)])
    return pl.pallas_call(
        functools.partial(_fox_kernel, tk=tk, tiles_per_iter=1),
        grid_spec=grid_spec,
        out_shape=jax.ShapeDtypeStruct((b, t, W_PAD), BF16),
        compiler_params=_params(("parallel", "parallel", "arbitrary")),
        name="fox",
    )(first, qa_p, ka_p, va_p, c.reshape(b, N_HEADS, t // tk, tk))


def _diff_lambda(lp, lam_init):
    a = jnp.sum(lp[0:1] * lp[1:2], axis=1, keepdims=True)
    b = jnp.sum(lp[2:3] * lp[3:4], axis=1, keepdims=True)
    return jnp.exp(a) - jnp.exp(b) + lam_init


def _diff_kernel(first_ref, q1_ref, q2_ref, k_ref, v_ref, slope_ref, lam_ref, g_ref, o_ref, m_sc, l_sc, a_sc, *,
                 tk, tiles_per_iter, lam_init):
    g = pl.program_id(1)
    i = pl.program_id(2)
    first = first_ref[pl.program_id(0), g, i]
    tq = q1_ref.shape[1]
    m_sc[...] = jnp.full(m_sc.shape, NEG, F32)
    l_sc[...] = jnp.zeros(l_sc.shape, F32)
    a_sc[...] = jnp.zeros(a_sc.shape, F32)
    row = lax.broadcasted_iota(jnp.int32, (tk, tk), 0)
    col = lax.broadcasted_iota(jnp.int32, (tk, tk), 1)
    rel_diag = (2 * jnp.minimum(row, col) - col).astype(F32)
    visible = lax.shift_right_logical(col, 6) <= lax.shift_right_logical(row, 6)
    lane = lax.broadcasted_iota(jnp.int32, (1, tk), 1)

    def tile(j, rows, local):
        ks = pl.ds(pl.multiple_of(j * tk, tk), tk)
        if local is None:
            rel = (lane + (j * tk - i * tq)).astype(F32)
        else:
            rel = rel_diag + float(local)
        for h in range(GROUP_HEADS):
            hs = slice(h * HEAD_PAD, (h + 1) * HEAD_PAD)
            bias = (slope_ref[g * GROUP_HEADS + h][:, 0:1] * LOG2E) * rel
            if local is not None:
                bias = jnp.where(visible, bias, NEG)
            k = k_ref[0, ks, hs]
            v = v_ref[0, ks, hs]
            nrows = rows.stop - rows.start
            scores = _nt_dot(jnp.concatenate([q1_ref[0, rows, hs], q2_ref[0, rows, hs]], axis=0), k)
            probs, alphas = [], []
            for m in range(2):
                n = 2 * h + m
                t = scores[m * nrows:(m + 1) * nrows] + bias
                m_old = m_sc[n, rows, :]
                m_new = jnp.maximum(m_old, jnp.max(t, axis=1, keepdims=True))
                alpha = jnp.exp2(m_old - m_new)
                p = jnp.exp2(t - _lane_tile(m_new, tk))
                l_sc[n, rows, :] = alpha * l_sc[n, rows, :] + jnp.sum(p, axis=1, keepdims=True)
                m_sc[n, rows, :] = m_new
                probs.append(p.astype(BF16))
                alphas.append(alpha)
            pv = _dot(jnp.concatenate(probs, axis=0), v)
            for m in range(2):
                n = 2 * h + m
                a_sc[n, rows, :] = alphas[m] * a_sc[n, rows, :] + pv[m * nrows:(m + 1) * nrows]

    _causal_sweep(i, first, tq, tk, tiles_per_iter, tile)
    lam = _diff_lambda(lam_ref[...], lam_init)
    for h in range(GROUP_HEADS):
        out = a_sc[2 * h] / l_sc[2 * h] - lam * (a_sc[2 * h + 1] / l_sc[2 * h + 1])
        o_ref[0, :, h * HEAD_PAD:(h + 1) * HEAD_PAD] = (_rms(out, g_ref[...]) * (1.0 - lam_init)).astype(BF16)


def _diff_call(first, qc1_p, qc2_p, kc_b, vc_b, slopes, lam_p, g_diff, lam_init, tq, tk):
    assert CHUNK == 64 and tk % CHUNK == 0
    b, t, _ = qc1_p.shape
    qspec = pl.BlockSpec((1, tq, W_GROUP), lambda b_, g, i, first: (b_, i, g))
    stat = pltpu.VMEM((2 * GROUP_HEADS, tq, LANES), F32)
    grid_spec = pltpu.PrefetchScalarGridSpec(
        num_scalar_prefetch=1,
        grid=(b, N_HEADS // GROUP_HEADS, t // tq),
        in_specs=[qspec, qspec, _resident_spec(t), _resident_spec(t),
                  pl.BlockSpec((N_HEADS, 1, LANES), lambda b_, g, i, first: (0, 0, 0)),
                  pl.BlockSpec((4, HD_C), lambda b_, g, i, first: (0, 0)),
                  pl.BlockSpec((1, HEAD_PAD), lambda b_, g, i, first: (0, 0))],
        out_specs=qspec,
        scratch_shapes=[stat, stat, pltpu.VMEM((2 * GROUP_HEADS, tq, HEAD_PAD), F32)])
    return pl.pallas_call(
        functools.partial(_diff_kernel, tk=tk, tiles_per_iter=1, lam_init=lam_init),
        grid_spec=grid_spec,
        out_shape=jax.ShapeDtypeStruct((b, t, W_C), BF16),
        compiler_params=_params(("parallel", "parallel", "arbitrary")),
        name="diff",
    )(first, qc1_p, qc2_p, kc_b, vc_b, slopes, lam_p, g_diff)


def _gla_kernel(q_ref, k_ref, v_ref, la_ref, r_ref, s0_ref, g_ref, o_ref, s_ref, *, chunk):
    ti = pl.program_id(1)
    n_chunks = q_ref.shape[1] // chunk
    n_seq = q_ref.shape[0]

    @pl.when(ti == 0)
    def _():
        s_ref[...] = s0_ref[...]

    tri_r = lax.broadcasted_iota(jnp.int32, (chunk, chunk), 0)
    tri_c = lax.broadcasted_iota(jnp.int32, (chunk, chunk), 1)
    incl = (tri_c <= tri_r).astype(F32)
    srow = lax.broadcasted_iota(jnp.int32, (N_HEADS * chunk, LANES), 0) // chunk
    slane = lax.broadcasted_iota(jnp.int32, (N_HEADS * chunk, LANES), 1) // DK_B
    qmask = srow == slane
    arow = lax.broadcasted_iota(jnp.int32, (N_HEADS * chunk, chunk), 0) % chunk
    acol = lax.broadcasted_iota(jnp.int32, (N_HEADS * chunk, chunk), 1)
    causal = acol <= arow
    vlane = lax.broadcasted_iota(jnp.int32, (chunk, W_B), 1) // DV_B
    st_row = lax.broadcasted_iota(jnp.int32, (W_B, LANES), 0) // DV_B
    st_lane = lax.broadcasted_iota(jnp.int32, (W_B, LANES), 1) // DK_B
    diag_blocks = st_row == st_lane
    gr = lax.broadcasted_iota(jnp.int32, (W_B, W_B), 0) // DV_B
    gc = lax.broadcasted_iota(jnp.int32, (W_B, W_B), 1) // DV_B
    group_mean = jnp.where(gr == gc, 1.0 / DV_B, 0.0)

    for c, sq in [(c, sq) for c in range(n_chunks) for sq in range(n_seq)]:
        sl = slice(c * chunk, (c + 1) * chunk)
        cb = _dot_exact_lhs(incl, la_ref[sq, sl, :])
        cb_last = cb[chunk - 1:chunk, :]
        q = q_ref[sq, sl, :] * (DK_B ** -0.5)
        k = k_ref[sq, sl, :]
        v = v_ref[sq, sl, :]
        vb = v.astype(BF16)
        qe = q * jnp.exp(cb)
        ke = (k * jnp.exp(-cb)).astype(BF16)
        kl = (k * jnp.exp(cb_last - cb)).astype(BF16)
        decay = jnp.exp(cb_last)
        qstack = jnp.where(qmask, jnp.concatenate([qe] * N_HEADS, axis=0), 0.0).astype(BF16)
        a = jnp.where(causal, _nt_dot(qstack, ke), 0.0).astype(BF16)
        oi = _dot(a, vb)
        o = jnp.zeros((chunk, W_B), F32)
        for h in range(N_HEADS):
            o = o + jnp.where(vlane == h, oi[h * chunk:(h + 1) * chunk], 0.0)
        st = s_ref[sq]
        o = o + _nt_dot(qe.astype(BF16), st.astype(BF16))
        upd = _dot(v.T.astype(BF16), kl)
        s_ref[sq] = decay * st + jnp.where(diag_blocks, upd, 0.0)
        ms = _dot_exact_rhs(o * o, group_mean)
        r = r_ref[sq, sl, :]
        o_ref[sq, sl, :] = (o * lax.rsqrt(ms + EPS) * g_ref[...] * (r * jax.nn.sigmoid(r))).astype(BF16)


def _gla_call(qb, kb, vb, loga, rb, s0_t, g_gla, tg, chunk, group):
    b, t, _ = qb.shape
    tok = lambda w: pl.BlockSpec((group, tg, w), lambda b_, i: (b_, i, 0))
    sspec = pl.BlockSpec((group, W_B, LANES), lambda b_, i: (b_, 0, 0))
    return pl.pallas_call(
        functools.partial(_gla_kernel, chunk=chunk),
        grid=(b // group, t // tg),
        in_specs=[tok(LANES), tok(LANES), tok(W_B), tok(LANES), tok(W_B), sspec,
                  pl.BlockSpec((1, W_B), lambda b_, i: (0, 0))],
        out_specs=[tok(W_B), sspec],
        out_shape=[jax.ShapeDtypeStruct((b, t, W_B), BF16), jax.ShapeDtypeStruct((b, W_B, LANES), F32)],
        compiler_params=_params(("parallel", "arbitrary")),
        name="gla",
    )(qb, kb, vb, loga, rb, s0_t, g_gla)


def _softmax_two(t_past, t_new, exp_fn):
    m = jnp.maximum(jnp.max(t_past, axis=1, keepdims=True), jnp.max(t_new, axis=1, keepdims=True))
    p_past = exp_fn(t_past - m)
    p_new = exp_fn(t_new - m)
    l = jnp.sum(p_past, axis=1, keepdims=True) + jnp.sum(p_new, axis=1, keepdims=True)
    return p_past, p_new, l


def _fox_s_kernel(q_ref, kn_ref, vn_ref, kpt_ref, vpt_ref, c_ref, o_ref):
    t, past = q_ref.shape[1], kpt_ref.shape[1]
    q = q_ref[0]
    kn = kn_ref[...].astype(BF16)
    vn = vn_ref[...].astype(BF16)
    kpt = kpt_ref[...].astype(BF16)
    vpt = vpt_ref[...].astype(BF16)
    lane_head = lax.broadcasted_iota(jnp.int32, (t, W_A), 1) // HD_A
    row = lax.broadcasted_iota(jnp.int32, (t, t), 0)
    col = lax.broadcasted_iota(jnp.int32, (t, t), 1)
    o = jnp.zeros((t, W_A), F32)
    for h in range(N_HEADS):
        qh = jnp.where(lane_head == h, q, 0.0).astype(BF16)
        ch = c_ref[0, h:h + 1, :]
        cref = ch[:, past:past + 1]
        t_past = _dot(qh, kpt) + (cref - ch[:, :past])
        t_new = jnp.where(col <= row, _nt_dot(qh, kn) + (cref - ch[:, past:past + t]), NEG)
        p_past, p_new, l = _softmax_two(t_past, t_new, jnp.exp)
        oh = (_nt_dot(p_past.astype(BF16), vpt) + _dot(p_new.astype(BF16), vn)) / l
        o = o + jnp.where(lane_head == h, oh, 0.0)
    o_ref[0] = o.astype(BF16)


def _fox_s_call(layer, qa, ka, va, cache_kt, cache_vt, c):
    b, t, _ = qa.shape
    past = cache_kt.shape[3]
    new = pl.BlockSpec((1, t, W_A), lambda i: (i, 0, 0))
    new_l = lambda: pl.BlockSpec((None, t, W_A), lambda i: (layer, i, 0))
    old_l = lambda: pl.BlockSpec((None, None, W_A, past), lambda i: (layer, i, 0, 0))
    return pl.pallas_call(
        _fox_s_kernel,
        grid=(b,),
        in_specs=[new, new_l(), new_l(), old_l(), old_l(),
                  pl.BlockSpec((1, N_HEADS, c.shape[2]), lambda i: (i, 0, 0))],
        out_specs=new,
        out_shape=jax.ShapeDtypeStruct((b, t, W_A), BF16),
        compiler_params=_params(("parallel",)),
        name="fox_s",
    )(qa, ka, va, cache_kt, cache_vt, c)


def _diff_s_kernel(q1_ref, q2_ref, kn_ref, vn_ref, kp_ref, vp_ref, slope_ref, lam_ref, g_ref, o_ref, *, lam_init):
    t, past = q1_ref.shape[1], kp_ref.shape[0] // N_HEADS
    lam = _diff_lambda(lam_ref[...], lam_init)
    row = lax.broadcasted_iota(jnp.int32, (t, t), 0) + past
    col = lax.broadcasted_iota(jnp.int32, (t, t), 1) + past
    visible_new = lax.shift_right_logical(col, 6) <= lax.shift_right_logical(row, 6)
    dist_new = jnp.abs(row - col).astype(F32)
    prow = lax.broadcasted_iota(jnp.int32, (t, past), 0) + past
    pcol = lax.broadcasted_iota(jnp.int32, (t, past), 1)
    visible_past = lax.shift_right_logical(pcol, 6) <= lax.shift_right_logical(prow, 6)
    dist_past = (prow - pcol).astype(F32)
    for h in range(N_HEADS):
        hs = slice(h * HEAD_PAD, (h + 1) * HEAD_PAD)
        slope = slope_ref[h][:, 0:1] * LOG2E
        kn = kn_ref[pl.ds(h, t, stride=N_HEADS), :].astype(BF16)
        vn = vn_ref[pl.ds(h, t, stride=N_HEADS), :].astype(BF16)
        kp = kp_ref[pl.ds(h, past, stride=N_HEADS), :].astype(BF16)
        vp = vp_ref[pl.ds(h, past, stride=N_HEADS), :].astype(BF16)
        outs = []
        for q_ref in (q1_ref, q2_ref):
            q = q_ref[0, :, hs]
            t_past = jnp.where(visible_past, _nt_dot(q, kp) - slope * dist_past, NEG)
            t_new = jnp.where(visible_new, _nt_dot(q, kn) - slope * dist_new, NEG)
            p_past, p_new, l = _softmax_two(t_past, t_new, jnp.exp2)
            outs.append((_dot(p_past.astype(BF16), vp) + _dot(p_new.astype(BF16), vn)) / l)
        out = outs[0] - lam * outs[1]
        o_ref[0, :, hs] = (_rms(out, g_ref[...]) * (1.0 - lam_init)).astype(BF16)


def _diff_s_call(layer, qc1_p, qc2_p, kc, vc, cache_k, cache_v, slopes, lam_p, g_diff, lam_init):
    assert CHUNK == 64
    b, t, _ = qc1_p.shape
    rows_past = cache_k.shape[2]
    new = lambda: pl.BlockSpec((1, t, W_C), lambda i: (i, 0, 0))
    new_l = lambda: pl.BlockSpec((None, t * N_HEADS, HEAD_PAD), lambda i: (layer, i, 0))
    old_l = lambda: pl.BlockSpec((None, None, rows_past, HEAD_PAD), lambda i: (layer, i, 0, 0))
    return pl.pallas_call(
        functools.partial(_diff_s_kernel, lam_init=lam_init),
        grid=(b,),
        in_specs=[new(), new(), new_l(), new_l(), old_l(), old_l(),
                  pl.BlockSpec((N_HEADS, 1, LANES), lambda i: (0, 0, 0)),
                  pl.BlockSpec((4, HD_C), lambda i: (0, 0)),
                  pl.BlockSpec((1, HEAD_PAD), lambda i: (0, 0))],
        out_specs=new(),
        out_shape=jax.ShapeDtypeStruct((b, t, W_C), BF16),
        compiler_params=_params(("parallel",)),
        name="diff_s",
    )(qc1_p, qc2_p, kc, vc, cache_k, cache_v, slopes, lam_p, g_diff)


def _out_kernel(x_ref, oa_ref, gla_ref, dif_ref, gate_ref, g2_ref, woa_ref, wog_ref, wod_ref, o_ref):
    y = _dot(oa_ref[...], woa_ref[...]) + _dot(gla_ref[...], wog_ref[...]) + _dot(dif_ref[...], wod_ref[...])
    o_ref[...] = x_ref[...] + gate_ref[0] * _rms(y, g2_ref[...])


def _out_call(x, oa, gla, dif, mod, g2, w_oa, w_og, w_od, tm, tiles_per_mod):
    n = x.shape[0]
    mod_all, sub = mod
    r = mod_all.shape[1]
    tok = lambda w: pl.BlockSpec((tm, w), lambda i: (i, 0))
    return pl.pallas_call(
        _out_kernel,
        grid=(n // tm,),
        in_specs=[tok(D_MODEL), tok(oa.shape[1]), tok(W_B), tok(W_C), _mod_spec(r, tiles_per_mod, 3 * sub + 2),
                  _const_spec((1, D_MODEL)), _const_spec(w_oa.shape), _const_spec(w_og.shape),
                  _const_spec(w_od.shape)],
        out_specs=tok(D_MODEL),
        out_shape=jax.ShapeDtypeStruct((n, D_MODEL), F32),
        compiler_params=_params(("parallel",)),
        name="out_proj",
    )(x, oa, gla, dif, mod_all, g2, w_oa, w_og, w_od)


def _pad_heads(w, hd, off=0):
    rows = w.shape[0]
    zeros = lambda width: [jnp.zeros((rows, width), w.dtype)] if width else []
    pieces = []
    for h in range(N_HEADS):
        pieces += zeros(off) + [w[:, h * hd:(h + 1) * hd]] + zeros(HEAD_PAD - off - hd)
    return jnp.concatenate(pieces, axis=1)


def _pad_lanes(w, width=LANES):
    return jnp.pad(w, ((0, 0), (0, width - w.shape[1])))


def _layer_params(l, w_in, b_f, w_gla_up, b_gla_up, g_gla, g_diff, w_out):
    offs = [int(o) for o in np.cumsum(PROJ_SIZES)[:-1]]
    qa, ka, va, fa, qb, kb, vb, gb, rb, qc, kc, vc = jnp.split(w_in[l], offs, axis=1)
    segs = {
        "ka": ka, "va": va, "kc": kc, "vc": vc, "qa": qa, "qc": qc,
        "qa_p": _pad_heads(qa, HD_A), "ka_p": _pad_heads(ka, HD_A), "va_p": _pad_heads(va, HD_A),
        "fa": _pad_lanes(fa), "qb": qb, "kb": kb, "vb": vb, "rb": rb, "gb": _pad_lanes(gb),
    }
    w_ext = jnp.concatenate([segs[n] for n, _ in _SEG_WIDTHS], axis=1).astype(BF16)
    wo = w_out[l]
    vone = np.zeros((1, W_PAD), np.float32)
    vone[0, HD_A::HEAD_PAD] = 1.0
    return dict(
        w_ext=w_ext, w_kvt=jnp.concatenate([ka, va], axis=1).T.astype(BF16),
        bf=_pad_lanes(b_f[l][None, :]),
        wup=jnp.pad(w_gla_up[l], ((0, LANES - GLA_RANK), (0, 0))).astype(BF16),
        bup=b_gla_up[l][None, :],
        vone=jnp.asarray(vone),
        g_gla=jnp.tile(g_gla[l], N_HEADS)[None, :], g_diff=g_diff[l][None, :],
        w_oa=wo[:W_A].astype(BF16),
        w_oa_p=_pad_heads(wo[:W_A].T, HD_A).T.astype(BF16),
        w_og=wo[W_A:W_A + W_B].astype(BF16), w_od=wo[W_A + W_B:].astype(BF16),
    )


def _state_to_blockdiag_t(s):
    eye = jnp.eye(N_HEADS, dtype=s.dtype)
    return jnp.einsum("bhkv,hg->bhvgk", s, eye).reshape(s.shape[0], W_B, N_HEADS * DK_B)


def _blockdiag_t_to_state(st):
    b = st.shape[0]
    s5 = st.reshape(b, N_HEADS, DV_B, N_HEADS, DK_B)
    diag = jnp.stack([s5[:, h, :, h, :] for h in range(N_HEADS)], axis=1)
    return jnp.swapaxes(diag, 2, 3)


SKIP_LOG2 = 160.0
NORM_SLACK = 1.01


def _tile_max(x, factor):
    return jnp.max(x.reshape(x.shape[:-1] + (x.shape[-1] // factor, factor)), axis=-1)


def _first_tiles(qn, kn, gap, tq, tk):
    b, h, nq = qn.shape
    r = tq // tk
    kn_diag = _tile_max(kn, r)
    bound = NORM_SLACK * (qn[..., None] * kn[:, :, None, :] + (qn * kn_diag)[..., None]) + gap
    dead = (bound < -SKIP_LOG2).reshape(b, h // GROUP_HEADS, GROUP_HEADS, nq, kn.shape[-1]).all(axis=2)
    lead = jnp.sum(jnp.cumprod(dead.astype(jnp.int32), axis=-1), axis=-1)
    return jnp.minimum(lead, jnp.arange(nq, dtype=jnp.int32) * r).astype(jnp.int32)


def _tile(n, pref):
    t = min(n, pref)
    assert n % t == 0, (n, t)
    return t


def _trunk(x, mods, params, ffn_w, g_norm, lam_params, mixer, tm, tiles_per_mod, with_qa):
    b, t, _ = x.shape
    n = b * t
    depth = len(params)
    xf = x.reshape(n, D_MODEL)
    stacked = None
    gla_states = []
    for l in range(depth):
        p = params[l]
        gn = lambda k: g_norm[l, k][None, :]
        xf = _ffn_call(xf, (mods[l], 0), gn(0), gn(1), *ffn_w, l, 0, 0.5, tm, tiles_per_mod)
        pr = _proj_call(xf, (mods[l], 1), gn(2), p["w_ext"], p["w_kvt"], p["bf"], p["wup"], p["bup"], p["vone"], tm,
                        tiles_per_mod, l, depth, stacked, with_qa, t)
        stacked = {nm: pr[nm] for nm, _, _ in _STACKED_OUTS}
        lam_init = 0.8 - 0.6 * math.exp(-0.3 * l)
        oa, gla, dif, s_fin, w_oa = mixer(l, pr, p, lam_params[l], lam_init, b, t)
        gla_states.append(s_fin)
        xf = _out_call(xf, oa, gla, dif, (mods[l], 1), gn(3), w_oa, p["w_og"], p["w_od"], tm, tiles_per_mod)
        xf = _ffn_call(xf, (mods[l], 2), gn(4), gn(5), *ffn_w, l, 1, 0.5, tm, tiles_per_mod)
    shp = lambda nm, *tail: stacked[nm].reshape((depth, b, t) + tail)
    if stacked["ka"].ndim == 4:
        shp_a = lambda nm: jnp.transpose(stacked[nm].reshape(depth, b, N_HEADS, HD_A, t), (0, 1, 4, 2, 3))
    else:
        shp_a = lambda nm: shp(nm, N_HEADS, HD_A)
    states = [shp_a("ka"), shp_a("va"), shp("logf", N_HEADS), jnp.stack(gla_states),
              shp("kc", N_HEADS, 2 * HD_C), shp("vc", N_HEADS, 2 * HD_C)]
    return xf.reshape(b, t, D_MODEL), states


def kernel(x_prompt, x_sample, c_prompt, c_sample, cache_fox_k, cache_fox_v, cache_fox_logf, state_gla,
           cache_diff_k, cache_diff_v, w_ada, b_ada, g_norm, w_ffn_in, w_ffn_out, w_in, b_f, w_gla_up,
           b_gla_up, g_gla, g_diff, lam_params, w_out):
    depth = w_in.shape[0]
    bp, tp, _ = x_prompt.shape
    bs, ts, _ = x_sample.shape
    past = cache_fox_k.shape[2]

    params = [_layer_params(l, w_in, b_f, w_gla_up, b_gla_up, g_gla, g_diff, w_out) for l in range(depth)]
    ffn_w = (w_ffn_in.astype(BF16), w_ffn_out.astype(BF16))

    slopes = 2.0 ** (-8.0 * jnp.arange(1, N_HEADS + 1, dtype=F32) / N_HEADS)
    slopes = jnp.broadcast_to(slopes[:, None, None], (N_HEADS, 1, LANES))

    rows = bp + bs
    rows_pad = -(-rows // 8) * 8
    c_all = jnp.pad(jnp.concatenate([c_prompt, c_sample], axis=0), ((0, rows_pad - rows), (0, 0)))
    mods_p, mods_s = [], []
    for l in range(depth):
        mod = _ada_call(c_all, w_ada[l].astype(BF16), b_ada[l][None, :])
        mods_p.append(mod[:bp, None, :])
        mods_s.append(jnp.repeat(mod[bp:rows], ts, axis=0)[None])

    tm_p = _tile(tp, 512)
    tq = _tile(tp, 1024)
    tk = _tile(tp, 512)
    tg = _tile(tp, 512)

    def prompt_mixer(l, pr, p, lam_p, lam_init, b, t):
        r3 = lambda a: a.reshape(b, t, a.shape[-1])
        logf_t = jnp.swapaxes(r3(pr["logf"][l]), 1, 2).reshape(b * N_HEADS, t // LANES, LANES)
        c = _cumsum_call(logf_t).reshape(b, N_HEADS, t)
        nrm = jnp.sqrt(pr["norms"][:, 0, :N_HEADS * len(_NORM_ROWS)])
        nrm = dict(zip(_NORM_ROWS, jnp.moveaxis(nrm.reshape(b, t // tm_p, len(_NORM_ROWS), N_HEADS), (2, 3), (0, 2))))
        q_tiles = lambda a: _tile_max(a, tq // tm_p)
        k_tiles = lambda a: _tile_max(a, tk // tm_p)
        gap_a = (c[:, :, ::tq][..., None] - c[:, :, tk - 1::tk][:, :, None, :]) * LOG2E
        first_a = _first_tiles(q_tiles(nrm["qa_p"]), k_tiles(nrm["ka_p"]), gap_a, tq, tk)
        oa = _fox_call(first_a, r3(pr["qa_p"]), r3(pr["ka_p"]), r3(pr["va_p"]), c, tq, tk)
        s0_t = jnp.zeros((b, W_B, N_HEADS * DK_B), F32)
        gla, s_t = _gla_call(r3(pr["qb"]), r3(pr["kb"]), r3(pr["vb"]), r3(pr["loga"]), r3(pr["rb"]), s0_t,
                             p["g_gla"], tg, min(CHUNK, t), math.gcd(b, 2))
        last_key = jnp.arange(tk - 1, t, tk, dtype=F32)[None, :] - jnp.arange(0, t, tq, dtype=F32)[:, None]
        gap_c = slopes[:, 0, :1, None] * LOG2E * last_key[None]
        first_c = _first_tiles(q_tiles(jnp.maximum(nrm["qc1_p"], nrm["qc2_p"])), k_tiles(nrm["kc_b"]), gap_c, tq, tk)
        dif = _diff_call(first_c, r3(pr["qc1_p"]), r3(pr["qc2_p"]), r3(pr["kc_b"]), r3(pr["vc_b"]), slopes, lam_p,
                         p["g_diff"], lam_init, tq, tk)
        flat = lambda a: a.reshape(b * t, a.shape[-1])
        return flat(oa), flat(gla), flat(dif), _blockdiag_t_to_state(s_t), p["w_oa_p"]

    feature_time = lambda a: jnp.transpose(a, (0, 1, 3, 4, 2)).reshape(depth, bs, W_A, past)
    cache_kt, cache_vt = feature_time(cache_fox_k), feature_time(cache_fox_v)
    cache_dk = cache_diff_k.reshape(depth, bs, past * N_HEADS, HEAD_PAD)
    cache_dv = cache_diff_v.reshape(depth, bs, past * N_HEADS, HEAD_PAD)

    def sample_mixer(l, pr, p, lam_p, lam_init, b, t):
        r3 = lambda a: a.reshape(b, t, a.shape[-1])
        total = past + t
        nb = -(-total // (8 * LANES)) * 8
        lf = jnp.concatenate([cache_fox_logf[l].astype(F32), r3(pr["logf"][l])], axis=1)
        lf = jnp.pad(jnp.swapaxes(lf, 1, 2), ((0, 0), (0, 0), (0, nb * LANES - total)))
        c = _cumsum_call(lf.reshape(b * N_HEADS, nb, LANES)).reshape(b, N_HEADS, nb * LANES)
        oa = _fox_s_call(l, r3(pr["qa"]), pr["ka"], pr["va"], cache_kt, cache_vt, c)
        gla, s_t = _gla_call(r3(pr["qb"]), r3(pr["kb"]), r3(pr["vb"]), r3(pr["loga"]), r3(pr["rb"]),
                             _state_to_blockdiag_t(state_gla[l].astype(F32)), p["g_gla"], t, min(CHUNK, t),
                             math.gcd(b, 8))
        dif = _diff_s_call(l, r3(pr["qc1_p"]), r3(pr["qc2_p"]), pr["kc"], pr["vc"], cache_dk, cache_dv,
                           slopes, lam_p, p["g_diff"], lam_init)
        flat = lambda a: a.reshape(b * t, a.shape[-1])
        return flat(oa), flat(gla), flat(dif), _blockdiag_t_to_state(s_t), p["w_oa"]

    y_p, sp = _trunk(x_prompt, mods_p, params, ffn_w, g_norm, lam_params, prompt_mixer, tm_p, tp // tm_p, False)
    y_s, ss = _trunk(x_sample, mods_s, params, ffn_w, g_norm, lam_params, sample_mixer, bs * ts, 1, True)
    return (y_p, y_s, sp[0], sp[1], sp[2], sp[3], sp[4], sp[5], ss[0], ss[1], ss[2], ss[3], ss[4], ss[5])
```

```python
import functools
import math

import jax
import jax.numpy as jnp
import numpy as np
from jax import lax
from jax.experimental import pallas as pl
from jax.experimental.pallas import tpu as pltpu

F32 = jnp.float32
BF16 = jnp.bfloat16

D_MODEL = 1024
N_HEADS = 4
HD_A = 64
DK_B = 32
DV_B = 64
GLA_RANK = 16
GLA_TAU = 16.0
HD_C = 64
CHUNK = 64
W_A = N_HEADS * HD_A
W_B = N_HEADS * DV_B
W_C = N_HEADS * 2 * HD_C
D_FF = ((8 * D_MODEL // 3 + 255) // 256) * 256
N_SUB = 3
PROJ_SIZES = (W_A, W_A, W_A, N_HEADS, N_HEADS * DK_B, N_HEADS * DK_B, W_B, GLA_RANK, W_B, W_C, W_C, W_C)
NEG = -1e30
EPS = 1e-6
LOG2E = math.log2(math.e)

LANES = 128
HEAD_PAD = 128
W_PAD = N_HEADS * HEAD_PAD
VMEM_LIMIT = 56 * 1024 * 1024

_SEG_WIDTHS = (
    ("ka", W_A), ("va", W_A), ("kc", W_C), ("vc", W_C), ("qa", W_A),
    ("qa_p", W_PAD), ("ka_p", W_PAD), ("va_p", W_PAD), ("qc", W_C),
    ("fa", LANES), ("qb", LANES), ("kb", LANES), ("vb", W_B), ("rb", W_B), ("gb", LANES),
)
_SEG = {}
_off = 0
for _name, _w in _SEG_WIDTHS:
    _SEG[_name] = (_off, _off + _w)
    _off += _w
N_EXT = _off


def _nt_dot(a, b):
    return lax.dot_general(a, b, (((1,), (1,)), ((), ())), preferred_element_type=F32)


def _dot(a, b):
    return jnp.dot(a, b, preferred_element_type=F32)


def _split3(x):
    hi = x.astype(BF16)
    r1 = x - hi.astype(F32)
    mid = r1.astype(BF16)
    lo = (r1 - mid.astype(F32)).astype(BF16)
    return hi, mid, lo


def _dot_exact_rhs(x, m):
    mb = m.astype(BF16)
    hi, mid, lo = _split3(x)
    return _dot(hi, mb) + _dot(mid, mb) + _dot(lo, mb)


def _dot_exact_lhs(m, x):
    mb = m.astype(BF16)
    hi, mid, lo = _split3(x)
    return _dot(mb, hi) + _dot(mb, mid) + _dot(mb, lo)


def _rms(x, g):
    return x * lax.rsqrt(jnp.mean(x * x, axis=-1, keepdims=True) + EPS) * g


def _log_sigmoid(x):
    return jnp.minimum(x, 0.0) - jnp.log1p(jnp.exp(-jnp.abs(x)))


def _const_spec(shape):
    nd = len(shape)
    return pl.BlockSpec(shape, lambda *_: (0,) * nd, pipeline_mode=pl.Buffered(1))


def _params(sem):
    return pltpu.CompilerParams(dimension_semantics=sem, vmem_limit_bytes=VMEM_LIMIT)


def _ada_kernel(c_ref, w_ref, b_ref, o_ref):
    c = c_ref[...]
    a = (c * jax.nn.sigmoid(c)).astype(BF16)
    o_ref[...] = _dot(a, w_ref[...]) + b_ref[...]


def _ada_call(c, w, b):
    m, n = c.shape[0], w.shape[1]
    tn = n // 8
    return pl.pallas_call(
        _ada_kernel,
        grid=(n // tn,),
        in_specs=[pl.BlockSpec((m, D_MODEL), lambda i: (0, 0)),
                  pl.BlockSpec((D_MODEL, tn), lambda i: (0, i)),
                  pl.BlockSpec((1, tn), lambda i: (0, i))],
        out_specs=pl.BlockSpec((m, tn), lambda i: (0, i)),
        out_shape=jax.ShapeDtypeStruct((m, n), F32),
        compiler_params=_params(("parallel",)),
        name="ada",
    )(c, w, b)


FF_CHUNK = 256


def _ffn_kernel(x_ref, shift_ref, scale_ref, gate_ref, g1_ref, g2_ref, win_ref, wout_ref, o_ref, *, res_w):
    x = x_ref[...]
    h = _rms(x, g1_ref[...]) * (1.0 + scale_ref[0]) + shift_ref[0]
    hb = h.astype(BF16)
    y = jnp.zeros(x.shape, F32)
    for c in range(D_FF // FF_CHUNK):
        lo, hi = c * FF_CHUNK, (c + 1) * FF_CHUNK
        g = _dot(hb, win_ref[:, lo:hi])
        u = _dot(hb, win_ref[:, D_FF + lo:D_FF + hi])
        a = (g * jax.nn.sigmoid(g) * u).astype(BF16)
        y = y + _dot(a, wout_ref[lo:hi, :])
    o_ref[...] = x + res_w * gate_ref[0] * _rms(y, g2_ref[...])


def _mod_spec(r, tiles_per_mod, col):
    return pl.BlockSpec((1, r, D_MODEL), lambda i: (i // tiles_per_mod, 0, col))


def _ffn_call(x, mod, g1, g2, w_in, w_out, layer, which, res_w, tm, tiles_per_mod):
    n = x.shape[0]
    mod_all, sub = mod
    r = mod_all.shape[1]
    tok = pl.BlockSpec((tm, D_MODEL), lambda i: (i, 0))
    pick = lambda rows, cols: pl.BlockSpec((None, None, rows, cols), lambda i: (layer, which, 0, 0),
                                           pipeline_mode=pl.Buffered(1))
    return pl.pallas_call(
        functools.partial(_ffn_kernel, res_w=res_w),
        grid=(n // tm,),
        in_specs=[tok] + [_mod_spec(r, tiles_per_mod, 3 * sub + k) for k in range(3)]
        + [_const_spec((1, D_MODEL)), _const_spec((1, D_MODEL)), pick(D_MODEL, 2 * D_FF), pick(D_FF, D_MODEL)],
        out_specs=tok,
        out_shape=jax.ShapeDtypeStruct((n, D_MODEL), F32),
        compiler_params=_params(("parallel",)),
        name="ffn",
    )(x, mod_all, mod_all, mod_all, g1, g2, w_in, w_out)


_STACKED_OUTS = (
    ("ka", 1, W_A), ("va", 1, W_A), ("kc", N_HEADS, HEAD_PAD), ("vc", N_HEADS, HEAD_PAD), ("logf", 1, N_HEADS),
)
_LAYER_OUTS = (
    ("qa_p", W_PAD, BF16), ("ka_p", W_PAD, BF16), ("va_p", W_PAD, BF16),
    ("qc1_p", W_PAD, BF16), ("qc2_p", W_PAD, BF16), ("kc_b", W_C, BF16), ("vc_b", W_C, BF16),
    ("qb", LANES, F32), ("kb", LANES, F32), ("vb", W_B, F32), ("loga", LANES, F32), ("rb", W_B, F32),
)
_NORM_ROWS = ("qa_p", "ka_p", "qc1_p", "qc2_p", "kc_b")


def _proj_kernel(x_ref, shift_ref, scale_ref, g1_ref, w_ref, wkvt_ref, bf_ref, wup_ref, bup_ref, vone_ref, *refs,
                 n_alias, with_qa, transposed_a):
    refs = refs[n_alias:]
    names = [n for n, _, _ in _STACKED_OUTS] + [n for n, _, _ in _LAYER_OUTS] + (["qa"] if with_qa else [])
    o = dict(zip(names + ["norms"], refs))
    tm = x_ref.shape[0]
    x = x_ref[...]
    hb = (_rms(x, g1_ref[...]) * (1.0 + scale_ref[0]) + shift_ref[0]).astype(BF16)

    def seg(name):
        lo, hi = _SEG[name]
        return _dot(hb, w_ref[:, lo:hi])

    def store_heads(ref, val):
        for h in range(N_HEADS):
            ref[0, pl.ds(h, tm, stride=N_HEADS), :] = val[:, h * HEAD_PAD:(h + 1) * HEAD_PAD]

    if transposed_a:
        kvt = _nt_dot(wkvt_ref[...], hb)
        o["ka"][0, 0] = kvt[:W_A]
        o["va"][0, 0] = kvt[W_A:]
    else:
        o["ka"][0] = seg("ka")
        o["va"][0] = seg("va")
    kc = seg("kc")
    store_heads(o["kc"], kc)
    rounded = {"kc_b": kc.astype(BF16)}
    o["kc_b"][...] = rounded["kc_b"]
    vc = seg("vc")
    store_heads(o["vc"], vc)
    o["vc_b"][...] = vc.astype(BF16)
    if with_qa:
        o["qa"][...] = seg("qa") * (HD_A ** -0.5)
    rounded["qa_p"] = (seg("qa_p") * (HD_A ** -0.5 * LOG2E)).astype(BF16)
    rounded["ka_p"] = seg("ka_p").astype(BF16)
    o["qa_p"][...] = rounded["qa_p"]
    o["ka_p"][...] = rounded["ka_p"]
    o["va_p"][...] = (seg("va_p") + vone_ref[...]).astype(BF16)
    qc = seg("qc") * (HD_C ** -0.5 * LOG2E)
    first_map = lax.broadcasted_iota(jnp.int32, qc.shape, 1) % HEAD_PAD < HD_C
    rounded["qc1_p"] = jnp.where(first_map, qc, 0.0).astype(BF16)
    rounded["qc2_p"] = jnp.where(first_map, 0.0, qc).astype(BF16)
    o["qc1_p"][...] = rounded["qc1_p"]
    o["qc2_p"][...] = rounded["qc2_p"]
    fa = seg("fa") + bf_ref[...]
    o["logf"][0] = _log_sigmoid(fa)[:, :N_HEADS]
    o["qb"][...] = seg("qb")
    o["kb"][...] = seg("kb")
    o["vb"][...] = seg("vb")
    o["rb"][...] = seg("rb")
    gb = seg("gb").astype(BF16)
    o["loga"][...] = _log_sigmoid(_dot(gb, wup_ref[...]) + bup_ref[...]) * (1.0 / GLA_TAU)
    lane = lax.broadcasted_iota(jnp.int32, (8, LANES), 1)
    tile_norms = jnp.zeros((8, LANES), F32)
    for a, name in enumerate(_NORM_ROWS):
        sq = rounded[name].astype(F32)
        sq = sq * sq
        for h in range(N_HEADS):
            ss = jnp.sum(sq[:, h * HEAD_PAD:(h + 1) * HEAD_PAD], axis=1, keepdims=True)
            tile_norms = jnp.where(lane == N_HEADS * a + h, jnp.max(ss, axis=0, keepdims=True), tile_norms)
    o["norms"][0] = tile_norms


def _proj_call(x, mod, g1, w_ext, w_kvt, bf, wup, bup, vone, tm, tiles_per_mod, layer, depth, stacked, with_qa,
               seq_len):
    n = x.shape[0]
    mod_all, sub = mod
    r = mod_all.shape[1]
    tok = lambda w: pl.BlockSpec((tm, w), lambda i: (i, 0))
    layer_outs = _LAYER_OUTS + ((("qa", W_A, F32),) if with_qa else ())
    alias_in = [] if stacked is None else [stacked[nm] for nm, _, _ in _STACKED_OUTS]
    transposed_a = seq_len % tm == 0
    tps = max(seq_len // tm, 1)
    stacked_specs, stacked_shapes = [], []
    for nm, rows, w in _STACKED_OUTS:
        if transposed_a and nm in ("ka", "va"):
            stacked_specs.append(pl.BlockSpec((1, 1, w, tm), lambda i: (layer, i // tps, 0, i % tps)))
            stacked_shapes.append(jax.ShapeDtypeStruct((depth, n // seq_len, w, seq_len), F32))
        else:
            stacked_specs.append(pl.BlockSpec((1, tm * rows, w), lambda i: (layer, i, 0)))
            stacked_shapes.append(jax.ShapeDtypeStruct((depth, n * rows, w), F32))
    n_fixed = 10
    outs = pl.pallas_call(
        functools.partial(_proj_kernel, n_alias=len(alias_in), with_qa=with_qa, transposed_a=transposed_a),
        grid=(n // tm,),
        in_specs=[tok(D_MODEL), _mod_spec(r, tiles_per_mod, 3 * sub), _mod_spec(r, tiles_per_mod, 3 * sub + 1),
                  _const_spec((1, D_MODEL)), _const_spec((D_MODEL, N_EXT)), _const_spec((2 * W_A, D_MODEL)),
                  _const_spec((1, LANES)), _const_spec((LANES, LANES)), _const_spec((1, LANES)),
                  _const_spec((1, W_PAD))]
        + [pl.BlockSpec(memory_space=pl.ANY)] * len(alias_in),
        out_specs=stacked_specs + [tok(w) for _, w, _ in layer_outs]
        + [pl.BlockSpec((1, 8, LANES), lambda i: (i, 0, 0))],
        out_shape=stacked_shapes + [jax.ShapeDtypeStruct((n, w), dt) for _, w, dt in layer_outs]
        + [jax.ShapeDtypeStruct((n // tm, 8, LANES), F32)],
        input_output_aliases={n_fixed + k: k for k in range(len(alias_in))},
        compiler_params=_params(("parallel",)),
        name="proj",
    )(x, mod_all, mod_all, g1, w_ext, w_kvt, bf, wup, bup, vone, *alias_in)
    names = [nm for nm, _, _ in _STACKED_OUTS] + [nm for nm, _, _ in layer_outs] + ["norms"]
    return dict(zip(names, outs))


def _cumsum_kernel(x_ref, o_ref):
    x = x_ref[0]
    nb = x.shape[0]
    r = lax.broadcasted_iota(jnp.int32, (LANES, LANES), 0)
    c = lax.broadcasted_iota(jnp.int32, (LANES, LANES), 1)
    local = _dot_exact_rhs(x, (r <= c).astype(F32))
    tot = jnp.broadcast_to(local[:, LANES - 1:LANES], (nb, LANES))
    rr = lax.broadcasted_iota(jnp.int32, (nb, nb), 0)
    cc = lax.broadcasted_iota(jnp.int32, (nb, nb), 1)
    o_ref[0] = local + _dot_exact_lhs((cc < rr).astype(F32), tot)


def _cumsum_call(x):
    rows, nb, _ = x.shape
    spec = pl.BlockSpec((1, nb, LANES), lambda i: (i, 0, 0))
    return pl.pallas_call(
        _cumsum_kernel, grid=(rows,), in_specs=[spec], out_specs=spec,
        out_shape=jax.ShapeDtypeStruct(x.shape, F32),
        compiler_params=_params(("parallel",)), name="cumsum",
    )(x)


def _lane_tile(x, width):
    return jnp.concatenate([x] * (width // LANES), axis=1)


GROUP_HEADS = 2
W_GROUP = GROUP_HEADS * HEAD_PAD


def _causal_sweep(i, first, tq, tk, tiles_per_iter, tile):
    r = tq // tk

    def body(jj, carry):
        for u in range(tiles_per_iter):
            tile(jj * tiles_per_iter + u, slice(0, tq), None)
        return carry

    assert r % tiles_per_iter == 0
    lax.fori_loop(lax.div(first, tiles_per_iter), lax.div(i * r, tiles_per_iter), body, 0)
    for d in range(r):
        tile(i * r + d, slice(d * tk, (d + 1) * tk), d * tk)
        if d + 1 < r:
            tile(i * r + d, slice((d + 1) * tk, tq), None)


def _fox_kernel(first_ref, q_ref, k_ref, v_ref, c_ref, o_ref, m_sc, acc_sc, *, tk, tiles_per_iter):
    i = pl.program_id(2)
    first = first_ref[pl.program_id(0), pl.program_id(1), i]
    tq = q_ref.shape[1]
    m_sc[...] = jnp.full(m_sc.shape, NEG, F32)
    acc_sc[...] = jnp.zeros(acc_sc.shape, F32)
    row = lax.broadcasted_iota(jnp.int32, (tk, tk), 0)
    col = lax.broadcasted_iota(jnp.int32, (tk, tk), 1)
    causal = col <= row

    def tile(j, rows, local):
        ks = pl.ds(pl.multiple_of(j * tk, tk), tk)
        for h in range(GROUP_HEADS):
            hs = slice(h * HEAD_PAD, (h + 1) * HEAD_PAD)
            c_first = c_ref[0, h, pl.ds(i * (tq // tk), 1), :][:, 0:1]
            t = _nt_dot(q_ref[0, rows, hs], k_ref[0, ks, hs]) + (c_first - c_ref[0, h, pl.ds(j, 1), :]) * LOG2E
            if local is not None:
                t = jnp.where(causal, t, NEG)
            m_old = m_sc[h, rows, :]
            m_new = jnp.maximum(m_old, jnp.max(t, axis=1, keepdims=True))
            p = jnp.exp2(t - _lane_tile(m_new, tk)).astype(BF16)
            acc_sc[h, rows, :] = jnp.exp2(m_old - m_new) * acc_sc[h, rows, :] + _dot(p, v_ref[0, ks, hs])
            m_sc[h, rows, :] = m_new

    _causal_sweep(i, first, tq, tk, tiles_per_iter, tile)
    for h in range(GROUP_HEADS):
        acc = acc_sc[h]
        o_ref[0, :, h * HEAD_PAD:(h + 1) * HEAD_PAD] = (acc / acc[:, HD_A:HD_A + 1]).astype(BF16)


def _resident_spec(t, width):
    return pl.BlockSpec((1, t, width), lambda b_, g, i, first: (b_, 0, g), pipeline_mode=pl.Buffered(1))


def _fox_call(first, qa_p, ka_p, va_p, c, tq, tk):
    b, t, _ = qa_p.shape
    qspec = pl.BlockSpec((1, tq, W_GROUP), lambda b_, g, i, first: (b_, i, g))
    grid_spec = pltpu.PrefetchScalarGridSpec(
        num_scalar_prefetch=1,
        grid=(b, N_HEADS // GROUP_HEADS, t // tq),
        in_specs=[qspec, _resident_spec(t, W_GROUP), _resident_spec(t, W_GROUP),
                  pl.BlockSpec((1, GROUP_HEADS, t // tk, tk), lambda b_, g, i, first: (b_, g, 0, 0))],
        out_specs=qspec,
        scratch_shapes=[pltpu.VMEM((GROUP_HEADS, tq, LANES), F32), pltpu.VMEM((GROUP_HEADS, tq, HEAD_PAD), F32)])
    return pl.pallas_call(
        functools.partial(_fox_kernel, tk=tk, tiles_per_iter=1),
        grid_spec=grid_spec,
        out_shape=jax.ShapeDtypeStruct((b, t, W_PAD), BF16),
        compiler_params=_params(("parallel", "parallel", "arbitrary")),
        name="fox",
    )(first, qa_p, ka_p, va_p, c.reshape(b, N_HEADS, t // tk, tk))


def _diff_lambda(lp, lam_init):
    a = jnp.sum(lp[0:1] * lp[1:2], axis=1, keepdims=True)
    b = jnp.sum(lp[2:3] * lp[3:4], axis=1, keepdims=True)
    return jnp.exp(a) - jnp.exp(b) + lam_init


DIFF_GROUP = 1
W_DIFF_GROUP = DIFF_GROUP * HEAD_PAD


def _diff_kernel(first_ref, q1_ref, q2_ref, k_ref, v_ref, slope_ref, lam_ref, g_ref, o_ref, m_sc, l_sc, a_sc, *,
                 tk, tiles_per_iter, lam_init):
    g = pl.program_id(1)
    i = pl.program_id(2)
    first = first_ref[pl.program_id(0), g, i]
    tq = q1_ref.shape[1]
    m_sc[...] = jnp.full(m_sc.shape, NEG, F32)
    l_sc[...] = jnp.zeros(l_sc.shape, F32)
    a_sc[...] = jnp.zeros(a_sc.shape, F32)
    row = lax.broadcasted_iota(jnp.int32, (tk, tk), 0)
    col = lax.broadcasted_iota(jnp.int32, (tk, tk), 1)
    rel_diag = (2 * jnp.minimum(row, col) - col).astype(F32)
    visible = lax.shift_right_logical(col, 6) <= lax.shift_right_logical(row, 6)
    lane = lax.broadcasted_iota(jnp.int32, (1, tk), 1)

    def tile(j, rows, local):
        ks = pl.ds(pl.multiple_of(j * tk, tk), tk)
        if local is None:
            rel = (lane + (j * tk - i * tq)).astype(F32)
        else:
            rel = rel_diag + float(local)
        for h in range(DIFF_GROUP):
            hs = slice(h * HEAD_PAD, (h + 1) * HEAD_PAD)
            bias = (slope_ref[g * DIFF_GROUP + h][:, 0:1] * LOG2E) * rel
            if local is not None:
                bias = jnp.where(visible, bias, NEG)
            k = k_ref[0, ks, hs]
            v = v_ref[0, ks, hs]
            nrows = rows.stop - rows.start
            scores = _nt_dot(jnp.concatenate([q1_ref[0, rows, hs], q2_ref[0, rows, hs]], axis=0), k)
            probs, alphas = [], []
            for m in range(2):
                n = 2 * h + m
                t = scores[m * nrows:(m + 1) * nrows] + bias
                m_old = m_sc[n, rows, :]
                m_new = jnp.maximum(m_old, jnp.max(t, axis=1, keepdims=True))
                alpha = jnp.exp2(m_old - m_new)
                p = jnp.exp2(t - _lane_tile(m_new, tk))
                l_sc[n, rows, :] = alpha * l_sc[n, rows, :] + jnp.sum(p, axis=1, keepdims=True)
                m_sc[n, rows, :] = m_new
                probs.append(p.astype(BF16))
                alphas.append(alpha)
            pv = _dot(jnp.concatenate(probs, axis=0), v)
            for m in range(2):
                n = 2 * h + m
                a_sc[n, rows, :] = alphas[m] * a_sc[n, rows, :] + pv[m * nrows:(m + 1) * nrows]

    _causal_sweep(i, first, tq, tk, tiles_per_iter, tile)
    lam = _diff_lambda(lam_ref[...], lam_init)
    for h in range(DIFF_GROUP):
        out = a_sc[2 * h] / l_sc[2 * h] - lam * (a_sc[2 * h + 1] / l_sc[2 * h + 1])
        o_ref[0, :, h * HEAD_PAD:(h + 1) * HEAD_PAD] = (_rms(out, g_ref[...]) * (1.0 - lam_init)).astype(BF16)


def _diff_call(first, qc1_p, qc2_p, kc_b, vc_b, slopes, lam_p, g_diff, lam_init, tq, tk):
    assert CHUNK == 64 and tk % CHUNK == 0
    b, t, _ = qc1_p.shape
    qspec = pl.BlockSpec((1, tq, W_DIFF_GROUP), lambda b_, g, i, first: (b_, i, g))
    stat = pltpu.VMEM((2 * DIFF_GROUP, tq, LANES), F32)
    grid_spec = pltpu.PrefetchScalarGridSpec(
        num_scalar_prefetch=1,
        grid=(b, N_HEADS // DIFF_GROUP, t // tq),
        in_specs=[qspec, qspec, _resident_spec(t, W_DIFF_GROUP), _resident_spec(t, W_DIFF_GROUP),
                  pl.BlockSpec((N_HEADS, 1, LANES), lambda b_, g, i, first: (0, 0, 0)),
                  pl.BlockSpec((4, HD_C), lambda b_, g, i, first: (0, 0)),
                  pl.BlockSpec((1, HEAD_PAD), lambda b_, g, i, first: (0, 0))],
        out_specs=qspec,
        scratch_shapes=[stat, stat, pltpu.VMEM((2 * DIFF_GROUP, tq, HEAD_PAD), F32)])
    return pl.pallas_call(
        functools.partial(_diff_kernel, tk=tk, tiles_per_iter=2, lam_init=lam_init),
        grid_spec=grid_spec,
        out_shape=jax.ShapeDtypeStruct((b, t, W_C), BF16),
        compiler_params=_params(("parallel", "parallel", "arbitrary")),
        name="diff",
    )(first, qc1_p, qc2_p, kc_b, vc_b, slopes, lam_p, g_diff)


def _gla_kernel(q_ref, k_ref, v_ref, la_ref, r_ref, s0_ref, g_ref, o_ref, s_ref, *, chunk):
    ti = pl.program_id(1)
    n_chunks = q_ref.shape[1] // chunk
    n_seq = q_ref.shape[0]

    @pl.when(ti == 0)
    def _():
        s_ref[...] = s0_ref[...]

    tri_r = lax.broadcasted_iota(jnp.int32, (chunk, chunk), 0)
    tri_c = lax.broadcasted_iota(jnp.int32, (chunk, chunk), 1)
    incl = (tri_c <= tri_r).astype(F32)
    srow = lax.broadcasted_iota(jnp.int32, (N_HEADS * chunk, LANES), 0) // chunk
    slane = lax.broadcasted_iota(jnp.int32, (N_HEADS * chunk, LANES), 1) // DK_B
    qmask = srow == slane
    arow = lax.broadcasted_iota(jnp.int32, (N_HEADS * chunk, chunk), 0) % chunk
    acol = lax.broadcasted_iota(jnp.int32, (N_HEADS * chunk, chunk), 1)
    causal = acol <= arow
    vlane = lax.broadcasted_iota(jnp.int32, (chunk, W_B), 1) // DV_B
    st_row = lax.broadcasted_iota(jnp.int32, (W_B, LANES), 0) // DV_B
    st_lane = lax.broadcasted_iota(jnp.int32, (W_B, LANES), 1) // DK_B
    diag_blocks = st_row == st_lane
    gr = lax.broadcasted_iota(jnp.int32, (W_B, W_B), 0) // DV_B
    gc = lax.broadcasted_iota(jnp.int32, (W_B, W_B), 1) // DV_B
    group_mean = jnp.where(gr == gc, 1.0 / DV_B, 0.0)

    for c, sq in [(c, sq) for c in range(n_chunks) for sq in range(n_seq)]:
        sl = slice(c * chunk, (c + 1) * chunk)
        cb = _dot_exact_lhs(incl, la_ref[sq, sl, :])
        cb_last = cb[chunk - 1:chunk, :]
        q = q_ref[sq, sl, :] * (DK_B ** -0.5)
        k = k_ref[sq, sl, :]
        v = v_ref[sq, sl, :]
        vb = v.astype(BF16)
        qe = q * jnp.exp(cb)
        ke = (k * jnp.exp(-cb)).astype(BF16)
        kl = (k * jnp.exp(cb_last - cb)).astype(BF16)
        decay = jnp.exp(cb_last)
        qstack = jnp.where(qmask, jnp.concatenate([qe] * N_HEADS, axis=0), 0.0).astype(BF16)
        a = jnp.where(causal, _nt_dot(qstack, ke), 0.0).astype(BF16)
        oi = _dot(a, vb)
        o = jnp.zeros((chunk, W_B), F32)
        for h in range(N_HEADS):
            o = o + jnp.where(vlane == h, oi[h * chunk:(h + 1) * chunk], 0.0)
        st = s_ref[sq]
        o = o + _nt_dot(qe.astype(BF16), st.astype(BF16))
        upd = _dot(v.T.astype(BF16), kl)
        s_ref[sq] = decay * st + jnp.where(diag_blocks, upd, 0.0)
        ms = _dot_exact_rhs(o * o, group_mean)
        r = r_ref[sq, sl, :]
        o_ref[sq, sl, :] = (o * lax.rsqrt(ms + EPS) * g_ref[...] * (r * jax.nn.sigmoid(r))).astype(BF16)


def _gla_call(qb, kb, vb, loga, rb, s0_t, g_gla, tg, chunk, group):
    b, t, _ = qb.shape
    tok = lambda w: pl.BlockSpec((group, tg, w), lambda b_, i: (b_, i, 0))
    sspec = pl.BlockSpec((group, W_B, LANES), lambda b_, i: (b_, 0, 0))
    return pl.pallas_call(
        functools.partial(_gla_kernel, chunk=chunk),
        grid=(b // group, t // tg),
        in_specs=[tok(LANES), tok(LANES), tok(W_B), tok(LANES), tok(W_B), sspec,
                  pl.BlockSpec((1, W_B), lambda b_, i: (0, 0))],
        out_specs=[tok(W_B), sspec],
        out_shape=[jax.ShapeDtypeStruct((b, t, W_B), BF16), jax.ShapeDtypeStruct((b, W_B, LANES), F32)],
        compiler_params=_params(("parallel", "arbitrary")),
        name="gla",
    )(qb, kb, vb, loga, rb, s0_t, g_gla)


def _softmax_two(t_past, t_new, exp_fn):
    m = jnp.maximum(jnp.max(t_past, axis=1, keepdims=True), jnp.max(t_new, axis=1, keepdims=True))
    p_past = exp_fn(t_past - m)
    p_new = exp_fn(t_new - m)
    l = jnp.sum(p_past, axis=1, keepdims=True) + jnp.sum(p_new, axis=1, keepdims=True)
    return p_past, p_new, l


def _fox_s_kernel(q_ref, kn_ref, vn_ref, kpt_ref, vpt_ref, c_ref, o_ref):
    t, past = q_ref.shape[1], kpt_ref.shape[1]
    q = q_ref[0]
    kn = kn_ref[...].astype(BF16)
    vn = vn_ref[...].astype(BF16)
    kpt = kpt_ref[...].astype(BF16)
    vpt = vpt_ref[...].astype(BF16)
    lane_head = lax.broadcasted_iota(jnp.int32, (t, W_A), 1) // HD_A
    row = lax.broadcasted_iota(jnp.int32, (t, t), 0)
    col = lax.broadcasted_iota(jnp.int32, (t, t), 1)
    o = jnp.zeros((t, W_A), F32)
    for h in range(N_HEADS):
        qh = jnp.where(lane_head == h, q, 0.0).astype(BF16)
        ch = c_ref[0, h:h + 1, :]
        cref = ch[:, past:past + 1]
        t_past = _dot(qh, kpt) + (cref - ch[:, :past])
        t_new = jnp.where(col <= row, _nt_dot(qh, kn) + (cref - ch[:, past:past + t]), NEG)
        p_past, p_new, l = _softmax_two(t_past, t_new, jnp.exp)
        oh = (_nt_dot(p_past.astype(BF16), vpt) + _dot(p_new.astype(BF16), vn)) / l
        o = o + jnp.where(lane_head == h, oh, 0.0)
    o_ref[0] = o.astype(BF16)


def _fox_s_call(layer, qa, ka, va, cache_kt, cache_vt, c):
    b, t, _ = qa.shape
    past = cache_kt.shape[3]
    new = pl.BlockSpec((1, t, W_A), lambda i: (i, 0, 0))
    new_l = lambda: pl.BlockSpec((None, t, W_A), lambda i: (layer, i, 0))
    old_l = lambda: pl.BlockSpec((None, None, W_A, past), lambda i: (layer, i, 0, 0))
    return pl.pallas_call(
        _fox_s_kernel,
        grid=(b,),
        in_specs=[new, new_l(), new_l(), old_l(), old_l(),
                  pl.BlockSpec((1, N_HEADS, c.shape[2]), lambda i: (i, 0, 0))],
        out_specs=new,
        out_shape=jax.ShapeDtypeStruct((b, t, W_A), BF16),
        compiler_params=_params(("parallel",)),
        name="fox_s",
    )(qa, ka, va, cache_kt, cache_vt, c)


def _diff_s_kernel(q1_ref, q2_ref, kn_ref, vn_ref, kp_ref, vp_ref, slope_ref, lam_ref, g_ref, o_ref, *, lam_init):
    t, past = q1_ref.shape[1], kp_ref.shape[0] // N_HEADS
    lam = _diff_lambda(lam_ref[...], lam_init)
    row = lax.broadcasted_iota(jnp.int32, (t, t), 0) + past
    col = lax.broadcasted_iota(jnp.int32, (t, t), 1) + past
    visible_new = lax.shift_right_logical(col, 6) <= lax.shift_right_logical(row, 6)
    dist_new = jnp.abs(row - col).astype(F32)
    prow = lax.broadcasted_iota(jnp.int32, (t, past), 0) + past
    pcol = lax.broadcasted_iota(jnp.int32, (t, past), 1)
    visible_past = lax.shift_right_logical(pcol, 6) <= lax.shift_right_logical(prow, 6)
    dist_past = (prow - pcol).astype(F32)
    for h in range(N_HEADS):
        hs = slice(h * HEAD_PAD, (h + 1) * HEAD_PAD)
        slope = slope_ref[h][:, 0:1] * LOG2E
        kn = kn_ref[pl.ds(h, t, stride=N_HEADS), :].astype(BF16)
        vn = vn_ref[pl.ds(h, t, stride=N_HEADS), :].astype(BF16)
        kp = kp_ref[pl.ds(h, past, stride=N_HEADS), :].astype(BF16)
        vp = vp_ref[pl.ds(h, past, stride=N_HEADS), :].astype(BF16)
        outs = []
        for q_ref in (q1_ref, q2_ref):
            q = q_ref[0, :, hs]
            t_past = jnp.where(visible_past, _nt_dot(q, kp) - slope * dist_past, NEG)
            t_new = jnp.where(visible_new, _nt_dot(q, kn) - slope * dist_new, NEG)
            p_past, p_new, l = _softmax_two(t_past, t_new, jnp.exp2)
            outs.append((_dot(p_past.astype(BF16), vp) + _dot(p_new.astype(BF16), vn)) / l)
        out = outs[0] - lam * outs[1]
        o_ref[0, :, hs] = (_rms(out, g_ref[...]) * (1.0 - lam_init)).astype(BF16)


def _diff_s_call(layer, qc1_p, qc2_p, kc, vc, cache_k, cache_v, slopes, lam_p, g_diff, lam_init):
    assert CHUNK == 64
    b, t, _ = qc1_p.shape
    rows_past = cache_k.shape[2]
    new = lambda: pl.BlockSpec((1, t, W_C), lambda i: (i, 0, 0))
    new_l = lambda: pl.BlockSpec((None, t * N_HEADS, HEAD_PAD), lambda i: (layer, i, 0))
    old_l = lambda: pl.BlockSpec((None, None, rows_past, HEAD_PAD), lambda i: (layer, i, 0, 0))
    return pl.pallas_call(
        functools.partial(_diff_s_kernel, lam_init=lam_init),
        grid=(b,),
        in_specs=[new(), new(), new_l(), new_l(), old_l(), old_l(),
                  pl.BlockSpec((N_HEADS, 1, LANES), lambda i: (0, 0, 0)),
                  pl.BlockSpec((4, HD_C), lambda i: (0, 0)),
                  pl.BlockSpec((1, HEAD_PAD), lambda i: (0, 0))],
        out_specs=new(),
        out_shape=jax.ShapeDtypeStruct((b, t, W_C), BF16),
        compiler_params=_params(("parallel",)),
        name="diff_s",
    )(qc1_p, qc2_p, kc, vc, cache_k, cache_v, slopes, lam_p, g_diff)


def _out_kernel(x_ref, oa_ref, gla_ref, dif_ref, gate_ref, g2_ref, woa_ref, wog_ref, wod_ref, o_ref):
    y = _dot(oa_ref[...], woa_ref[...]) + _dot(gla_ref[...], wog_ref[...]) + _dot(dif_ref[...], wod_ref[...])
    o_ref[...] = x_ref[...] + gate_ref[0] * _rms(y, g2_ref[...])


def _out_call(x, oa, gla, dif, mod, g2, w_oa, w_og, w_od, tm, tiles_per_mod):
    n = x.shape[0]
    mod_all, sub = mod
    r = mod_all.shape[1]
    tok = lambda w: pl.BlockSpec((tm, w), lambda i: (i, 0))
    return pl.pallas_call(
        _out_kernel,
        grid=(n // tm,),
        in_specs=[tok(D_MODEL), tok(oa.shape[1]), tok(W_B), tok(W_C), _mod_spec(r, tiles_per_mod, 3 * sub + 2),
                  _const_spec((1, D_MODEL)), _const_spec(w_oa.shape), _const_spec(w_og.shape),
                  _const_spec(w_od.shape)],
        out_specs=tok(D_MODEL),
        out_shape=jax.ShapeDtypeStruct((n, D_MODEL), F32),
        compiler_params=_params(("parallel",)),
        name="out_proj",
    )(x, oa, gla, dif, mod_all, g2, w_oa, w_og, w_od)


def _pad_heads(w, hd, off=0):
    rows = w.shape[0]
    zeros = lambda width: [jnp.zeros((rows, width), w.dtype)] if width else []
    pieces = []
    for h in range(N_HEADS):
        pieces += zeros(off) + [w[:, h * hd:(h + 1) * hd]] + zeros(HEAD_PAD - off - hd)
    return jnp.concatenate(pieces, axis=1)


def _pad_lanes(w, width=LANES):
    return jnp.pad(w, ((0, 0), (0, width - w.shape[1])))


def _layer_params(l, w_in, b_f, w_gla_up, b_gla_up, g_gla, g_diff, w_out):
    offs = [int(o) for o in np.cumsum(PROJ_SIZES)[:-1]]
    qa, ka, va, fa, qb, kb, vb, gb, rb, qc, kc, vc = jnp.split(w_in[l], offs, axis=1)
    segs = {
        "ka": ka, "va": va, "kc": kc, "vc": vc, "qa": qa, "qc": qc,
        "qa_p": _pad_heads(qa, HD_A), "ka_p": _pad_heads(ka, HD_A), "va_p": _pad_heads(va, HD_A),
        "fa": _pad_lanes(fa), "qb": qb, "kb": kb, "vb": vb, "rb": rb, "gb": _pad_lanes(gb),
    }
    w_ext = jnp.concatenate([segs[n] for n, _ in _SEG_WIDTHS], axis=1).astype(BF16)
    wo = w_out[l]
    vone = np.zeros((1, W_PAD), np.float32)
    vone[0, HD_A::HEAD_PAD] = 1.0
    return dict(
        w_ext=w_ext, w_kvt=jnp.concatenate([ka, va], axis=1).T.astype(BF16),
        bf=_pad_lanes(b_f[l][None, :]),
        wup=jnp.pad(w_gla_up[l], ((0, LANES - GLA_RANK), (0, 0))).astype(BF16),
        bup=b_gla_up[l][None, :],
        vone=jnp.asarray(vone),
        g_gla=jnp.tile(g_gla[l], N_HEADS)[None, :], g_diff=g_diff[l][None, :],
        w_oa=wo[:W_A].astype(BF16),
        w_oa_p=_pad_heads(wo[:W_A].T, HD_A).T.astype(BF16),
        w_og=wo[W_A:W_A + W_B].astype(BF16), w_od=wo[W_A + W_B:].astype(BF16),
    )


def _state_to_blockdiag_t(s):
    eye = jnp.eye(N_HEADS, dtype=s.dtype)
    return jnp.einsum("bhkv,hg->bhvgk", s, eye).reshape(s.shape[0], W_B, N_HEADS * DK_B)


def _blockdiag_t_to_state(st):
    b = st.shape[0]
    s5 = st.reshape(b, N_HEADS, DV_B, N_HEADS, DK_B)
    diag = jnp.stack([s5[:, h, :, h, :] for h in range(N_HEADS)], axis=1)
    return jnp.swapaxes(diag, 2, 3)


SKIP_LOG2 = 160.0
NORM_SLACK = 1.01


def _tile_max(x, factor):
    return jnp.max(x.reshape(x.shape[:-1] + (x.shape[-1] // factor, factor)), axis=-1)


def _first_tiles(qn, kn, gap, tq, tk, group):
    b, h, nq = qn.shape
    r = tq // tk
    kn_diag = _tile_max(kn, r)
    bound = NORM_SLACK * (qn[..., None] * kn[:, :, None, :] + (qn * kn_diag)[..., None]) + gap
    dead = (bound < -SKIP_LOG2).reshape(b, h // group, group, nq, kn.shape[-1]).all(axis=2)
    lead = jnp.sum(jnp.cumprod(dead.astype(jnp.int32), axis=-1), axis=-1)
    return jnp.minimum(lead, jnp.arange(nq, dtype=jnp.int32) * r).astype(jnp.int32)


def _tile(n, pref):
    t = min(n, pref)
    assert n % t == 0, (n, t)
    return t


def _trunk(x, mods, params, ffn_w, g_norm, lam_params, mixer, tm, tiles_per_mod, with_qa):
    b, t, _ = x.shape
    n = b * t
    depth = len(params)
    xf = x.reshape(n, D_MODEL)
    stacked = None
    gla_states = []
    for l in range(depth):
        p = params[l]
        gn = lambda k: g_norm[l, k][None, :]
        xf = _ffn_call(xf, (mods[l], 0), gn(0), gn(1), *ffn_w, l, 0, 0.5, tm, tiles_per_mod)
        pr = _proj_call(xf, (mods[l], 1), gn(2), p["w_ext"], p["w_kvt"], p["bf"], p["wup"], p["bup"], p["vone"], tm,
                        tiles_per_mod, l, depth, stacked, with_qa, t)
        stacked = {nm: pr[nm] for nm, _, _ in _STACKED_OUTS}
        lam_init = 0.8 - 0.6 * math.exp(-0.3 * l)
        oa, gla, dif, s_fin, w_oa = mixer(l, pr, p, lam_params[l], lam_init, b, t)
        gla_states.append(s_fin)
        xf = _out_call(xf, oa, gla, dif, (mods[l], 1), gn(3), w_oa, p["w_og"], p["w_od"], tm, tiles_per_mod)
        xf = _ffn_call(xf, (mods[l], 2), gn(4), gn(5), *ffn_w, l, 1, 0.5, tm, tiles_per_mod)
    shp = lambda nm, *tail: stacked[nm].reshape((depth, b, t) + tail)
    if stacked["ka"].ndim == 4:
        shp_a = lambda nm: jnp.transpose(stacked[nm].reshape(depth, b, N_HEADS, HD_A, t), (0, 1, 4, 2, 3))
    else:
        shp_a = lambda nm: shp(nm, N_HEADS, HD_A)
    states = [shp_a("ka"), shp_a("va"), shp("logf", N_HEADS), jnp.stack(gla_states),
              shp("kc", N_HEADS, 2 * HD_C), shp("vc", N_HEADS, 2 * HD_C)]
    return xf.reshape(b, t, D_MODEL), states


def kernel(x_prompt, x_sample, c_prompt, c_sample, cache_fox_k, cache_fox_v, cache_fox_logf, state_gla,
           cache_diff_k, cache_diff_v, w_ada, b_ada, g_norm, w_ffn_in, w_ffn_out, w_in, b_f, w_gla_up,
           b_gla_up, g_gla, g_diff, lam_params, w_out):
    depth = w_in.shape[0]
    bp, tp, _ = x_prompt.shape
    bs, ts, _ = x_sample.shape
    past = cache_fox_k.shape[2]

    params = [_layer_params(l, w_in, b_f, w_gla_up, b_gla_up, g_gla, g_diff, w_out) for l in range(depth)]
    ffn_w = (w_ffn_in.astype(BF16), w_ffn_out.astype(BF16))

    slopes = 2.0 ** (-8.0 * jnp.arange(1, N_HEADS + 1, dtype=F32) / N_HEADS)
    slopes = jnp.broadcast_to(slopes[:, None, None], (N_HEADS, 1, LANES))

    rows = bp + bs
    rows_pad = -(-rows // 8) * 8
    c_all = jnp.pad(jnp.concatenate([c_prompt, c_sample], axis=0), ((0, rows_pad - rows), (0, 0)))
    mods_p, mods_s = [], []
    for l in range(depth):
        mod = _ada_call(c_all, w_ada[l].astype(BF16), b_ada[l][None, :])
        mods_p.append(mod[:bp, None, :])
        mods_s.append(jnp.repeat(mod[bp:rows], ts, axis=0)[None])

    tm_p = _tile(tp, 512)
    tq = _tile(tp, 1024)
    tk = _tile(tp, 512)
    tg = _tile(tp, 512)

    def prompt_mixer(l, pr, p, lam_p, lam_init, b, t):
        r3 = lambda a: a.reshape(b, t, a.shape[-1])
        logf_t = jnp.swapaxes(r3(pr["logf"][l]), 1, 2).reshape(b * N_HEADS, t // LANES, LANES)
        c = _cumsum_call(logf_t).reshape(b, N_HEADS, t)
        nrm = jnp.sqrt(pr["norms"][:, 0, :N_HEADS * len(_NORM_ROWS)])
        nrm = dict(zip(_NORM_ROWS, jnp.moveaxis(nrm.reshape(b, t // tm_p, len(_NORM_ROWS), N_HEADS), (2, 3), (0, 2))))
        q_tiles = lambda a: _tile_max(a, tq // tm_p)
        k_tiles = lambda a: _tile_max(a, tk // tm_p)
        gap_a = (c[:, :, ::tq][..., None] - c[:, :, tk - 1::tk][:, :, None, :]) * LOG2E
        first_a = _first_tiles(q_tiles(nrm["qa_p"]), k_tiles(nrm["ka_p"]), gap_a, tq, tk, GROUP_HEADS)
        oa = _fox_call(first_a, r3(pr["qa_p"]), r3(pr["ka_p"]), r3(pr["va_p"]), c, tq, tk)
        s0_t = jnp.zeros((b, W_B, N_HEADS * DK_B), F32)
        gla, s_t = _gla_call(r3(pr["qb"]), r3(pr["kb"]), r3(pr["vb"]), r3(pr["loga"]), r3(pr["rb"]), s0_t,
                             p["g_gla"], tg, min(CHUNK, t), math.gcd(b, 2))
        last_key = jnp.arange(tk - 1, t, tk, dtype=F32)[None, :] - jnp.arange(0, t, tq, dtype=F32)[:, None]
        gap_c = slopes[:, 0, :1, None] * LOG2E * last_key[None]
        first_c = _first_tiles(q_tiles(jnp.maximum(nrm["qc1_p"], nrm["qc2_p"])), k_tiles(nrm["kc_b"]), gap_c, tq, tk,
                                 DIFF_GROUP)
        dif = _diff_call(first_c, r3(pr["qc1_p"]), r3(pr["qc2_p"]), r3(pr["kc_b"]), r3(pr["vc_b"]), slopes, lam_p,
                         p["g_diff"], lam_init, tq, tk)
        flat = lambda a: a.reshape(b * t, a.shape[-1])
        return flat(oa), flat(gla), flat(dif), _blockdiag_t_to_state(s_t), p["w_oa_p"]

    feature_time = lambda a: jnp.transpose(a, (0, 1, 3, 4, 2)).reshape(depth, bs, W_A, past)
    cache_kt, cache_vt = feature_time(cache_fox_k), feature_time(cache_fox_v)
    cache_dk = cache_diff_k.reshape(depth, bs, past * N_HEADS, HEAD_PAD)
    cache_dv = cache_diff_v.reshape(depth, bs, past * N_HEADS, HEAD_PAD)

    def sample_mixer(l, pr, p, lam_p, lam_init, b, t):
        r3 = lambda a: a.reshape(b, t, a.shape[-1])
        total = past + t
        nb = -(-total // (8 * LANES)) * 8
        lf = jnp.concatenate([cache_fox_logf[l].astype(F32), r3(pr["logf"][l])], axis=1)
        lf = jnp.pad(jnp.swapaxes(lf, 1, 2), ((0, 0), (0, 0), (0, nb * LANES - total)))
        c = _cumsum_call(lf.reshape(b * N_HEADS, nb, LANES)).reshape(b, N_HEADS, nb * LANES)
        oa = _fox_s_call(l, r3(pr["qa"]), pr["ka"], pr["va"], cache_kt, cache_vt, c)
        gla, s_t = _gla_call(r3(pr["qb"]), r3(pr["kb"]), r3(pr["vb"]), r3(pr["loga"]), r3(pr["rb"]),
                             _state_to_blockdiag_t(state_gla[l].astype(F32)), p["g_gla"], t, min(CHUNK, t),
                             math.gcd(b, 8))
        dif = _diff_s_call(l, r3(pr["qc1_p"]), r3(pr["qc2_p"]), pr["kc"], pr["vc"], cache_dk, cache_dv,
                           slopes, lam_p, p["g_diff"], lam_init)
        flat = lambda a: a.reshape(b * t, a.shape[-1])
        return flat(oa), flat(gla), flat(dif), _blockdiag_t_to_state(s_t), p["w_oa"]

    y_p, sp = _trunk(x_prompt, mods_p, params, ffn_w, g_norm, lam_params, prompt_mixer, tm_p, tp // tm_p, False)
    y_s, ss = _trunk(x_sample, mods_s, params, ffn_w, g_norm, lam_params, sample_mixer, bs * ts, 1, True)
    return (y_p, y_s, sp[0], sp[1], sp[2], sp[3], sp[4], sp[5], ss[0], ss[1], ss[2], ss[3], ss[4], ss[5])
```

```python
import functools
import math

import jax
import jax.numpy as jnp
import numpy as np
from jax import lax
from jax.experimental import pallas as pl
from jax.experimental.pallas import tpu as pltpu

F32 = jnp.float32
BF16 = jnp.bfloat16

D_MODEL = 1024
N_HEADS = 4
HD_A = 64
DK_B = 32
DV_B = 64
GLA_RANK = 16
GLA_TAU = 16.0
HD_C = 64
CHUNK = 64
W_A = N_HEADS * HD_A
W_B = N_HEADS * DV_B
W_C = N_HEADS * 2 * HD_C
D_FF = ((8 * D_MODEL // 3 + 255) // 256) * 256
N_SUB = 3
PROJ_SIZES = (W_A, W_A, W_A, N_HEADS, N_HEADS * DK_B, N_HEADS * DK_B, W_B, GLA_RANK, W_B, W_C, W_C, W_C)
NEG = -1e30
EPS = 1e-6
LOG2E = math.log2(math.e)

LANES = 128
HEAD_PAD = 128
W_PAD = N_HEADS * HEAD_PAD
VMEM_LIMIT = 56 * 1024 * 1024

_SEG_WIDTHS = (
    ("ka", W_A), ("va", W_A), ("kc", W_C), ("vc", W_C), ("qa", W_A),
    ("qa_p", W_PAD), ("ka_p", W_PAD), ("va_p", W_PAD), ("qc", W_C),
    ("fa", LANES), ("qb", LANES), ("kb", LANES), ("vb", W_B), ("rb", W_B), ("gb", LANES),
)
_SEG = {}
_off = 0
for _name, _w in _SEG_WIDTHS:
    _SEG[_name] = (_off, _off + _w)
    _off += _w
N_EXT = _off


def _nt_dot(a, b):
    return lax.dot_general(a, b, (((1,), (1,)), ((), ())), preferred_element_type=F32)


def _dot(a, b):
    return jnp.dot(a, b, preferred_element_type=F32)


def _split3(x):
    hi = x.astype(BF16)
    r1 = x - hi.astype(F32)
    mid = r1.astype(BF16)
    lo = (r1 - mid.astype(F32)).astype(BF16)
    return hi, mid, lo


def _dot_exact_rhs(x, m):
    mb = m.astype(BF16)
    hi, mid, lo = _split3(x)
    return _dot(hi, mb) + _dot(mid, mb) + _dot(lo, mb)


def _dot_exact_lhs(m, x):
    mb = m.astype(BF16)
    hi, mid, lo = _split3(x)
    return _dot(mb, hi) + _dot(mb, mid) + _dot(mb, lo)


def _rms(x, g):
    return x * lax.rsqrt(jnp.mean(x * x, axis=-1, keepdims=True) + EPS) * g


def _log_sigmoid(x):
    return jnp.minimum(x, 0.0) - jnp.log1p(jnp.exp(-jnp.abs(x)))


def _const_spec(shape):
    nd = len(shape)
    return pl.BlockSpec(shape, lambda *_: (0,) * nd, pipeline_mode=pl.Buffered(1))


def _params(sem):
    return pltpu.CompilerParams(dimension_semantics=sem, vmem_limit_bytes=VMEM_LIMIT)


def _ada_kernel(c_ref, w_ref, b_ref, o_ref):
    c = c_ref[...]
    a = (c * jax.nn.sigmoid(c)).astype(BF16)
    o_ref[...] = _dot(a, w_ref[...]) + b_ref[...]


def _ada_call(c, w, b):
    m, n = c.shape[0], w.shape[1]
    tn = n // 8
    return pl.pallas_call(
        _ada_kernel,
        grid=(n // tn,),
        in_specs=[pl.BlockSpec((m, D_MODEL), lambda i: (0, 0)),
                  pl.BlockSpec((D_MODEL, tn), lambda i: (0, i)),
                  pl.BlockSpec((1, tn), lambda i: (0, i))],
        out_specs=pl.BlockSpec((m, tn), lambda i: (0, i)),
        out_shape=jax.ShapeDtypeStruct((m, n), F32),
        compiler_params=_params(("parallel",)),
        name="ada",
    )(c, w, b)


FF_CHUNK = 256


def _ffn_kernel(x_ref, shift_ref, scale_ref, gate_ref, g1_ref, g2_ref, win_ref, wout_ref, o_ref, *, res_w):
    x = x_ref[...]
    h = _rms(x, g1_ref[...]) * (1.0 + scale_ref[0]) + shift_ref[0]
    hb = h.astype(BF16)
    y = jnp.zeros(x.shape, F32)
    for c in range(D_FF // FF_CHUNK):
        lo, hi = c * FF_CHUNK, (c + 1) * FF_CHUNK
        g = _dot(hb, win_ref[:, lo:hi])
        u = _dot(hb, win_ref[:, D_FF + lo:D_FF + hi])
        a = (g * jax.nn.sigmoid(g) * u).astype(BF16)
        y = y + _dot(a, wout_ref[lo:hi, :])
    o_ref[...] = x + res_w * gate_ref[0] * _rms(y, g2_ref[...])


def _mod_spec(r, tiles_per_mod, col):
    return pl.BlockSpec((1, r, D_MODEL), lambda i: (i // tiles_per_mod, 0, col))


def _ffn_call(x, mod, g1, g2, w_in, w_out, layer, which, res_w, tm, tiles_per_mod):
    n = x.shape[0]
    mod_all, sub = mod
    r = mod_all.shape[1]
    tok = pl.BlockSpec((tm, D_MODEL), lambda i: (i, 0))
    pick = lambda rows, cols: pl.BlockSpec((None, None, rows, cols), lambda i: (layer, which, 0, 0),
                                           pipeline_mode=pl.Buffered(1))
    return pl.pallas_call(
        functools.partial(_ffn_kernel, res_w=res_w),
        grid=(n // tm,),
        in_specs=[tok] + [_mod_spec(r, tiles_per_mod, 3 * sub + k) for k in range(3)]
        + [_const_spec((1, D_MODEL)), _const_spec((1, D_MODEL)), pick(D_MODEL, 2 * D_FF), pick(D_FF, D_MODEL)],
        out_specs=tok,
        out_shape=jax.ShapeDtypeStruct((n, D_MODEL), F32),
        compiler_params=_params(("parallel",)),
        name="ffn",
    )(x, mod_all, mod_all, mod_all, g1, g2, w_in, w_out)


_STACKED_OUTS = (
    ("ka", 1, W_A), ("va", 1, W_A), ("kc", N_HEADS, HEAD_PAD), ("vc", N_HEADS, HEAD_PAD), ("logf", 1, N_HEADS),
)
_LAYER_OUTS = (
    ("qa_p", W_PAD, BF16), ("ka_p", W_PAD, BF16), ("va_p", W_PAD, BF16),
    ("qc1_p", W_PAD, BF16), ("qc2_p", W_PAD, BF16), ("kc_b", W_C, BF16), ("vc_b", W_C, BF16),
    ("qb", LANES, F32), ("kb", LANES, F32), ("vb", W_B, F32), ("loga", LANES, F32), ("rb", W_B, F32),
)
_NORM_ROWS = ("qa_p", "ka_p", "qc1_p", "qc2_p", "kc_b")


def _proj_kernel(x_ref, shift_ref, scale_ref, g1_ref, w_ref, wkvt_ref, bf_ref, wup_ref, bup_ref, vone_ref, *refs,
                 n_alias, with_qa, transposed_a):
    refs = refs[n_alias:]
    names = [n for n, _, _ in _STACKED_OUTS] + [n for n, _, _ in _LAYER_OUTS] + (["qa"] if with_qa else [])
    o = dict(zip(names + ["norms"], refs))
    tm = x_ref.shape[0]
    x = x_ref[...]
    hb = (_rms(x, g1_ref[...]) * (1.0 + scale_ref[0]) + shift_ref[0]).astype(BF16)

    def seg(name):
        lo, hi = _SEG[name]
        return _dot(hb, w_ref[:, lo:hi])

    def store_heads(ref, val):
        for d in range(ref.shape[0]):
            for h in range(N_HEADS):
                ref[d, pl.ds(h, tm, stride=N_HEADS), :] = val[:, h * HEAD_PAD:(h + 1) * HEAD_PAD]

    def store_slots(ref, val):
        for d in range(ref.shape[0]):
            ref[d] = val

    if transposed_a:
        kvt = _nt_dot(wkvt_ref[...], hb)
        store_slots(o["ka"], kvt[None, :W_A])
        store_slots(o["va"], kvt[None, W_A:])
    else:
        store_slots(o["ka"], seg("ka"))
        store_slots(o["va"], seg("va"))
    kc = seg("kc")
    store_heads(o["kc"], kc)
    rounded = {"kc_b": kc.astype(BF16)}
    o["kc_b"][...] = rounded["kc_b"]
    vc = seg("vc")
    store_heads(o["vc"], vc)
    o["vc_b"][...] = vc.astype(BF16)
    if with_qa:
        o["qa"][...] = seg("qa") * (HD_A ** -0.5)
    rounded["qa_p"] = (seg("qa_p") * (HD_A ** -0.5 * LOG2E)).astype(BF16)
    rounded["ka_p"] = seg("ka_p").astype(BF16)
    o["qa_p"][...] = rounded["qa_p"]
    o["ka_p"][...] = rounded["ka_p"]
    o["va_p"][...] = (seg("va_p") + vone_ref[...]).astype(BF16)
    qc = seg("qc") * (HD_C ** -0.5 * LOG2E)
    first_map = lax.broadcasted_iota(jnp.int32, qc.shape, 1) % HEAD_PAD < HD_C
    rounded["qc1_p"] = jnp.where(first_map, qc, 0.0).astype(BF16)
    rounded["qc2_p"] = jnp.where(first_map, 0.0, qc).astype(BF16)
    o["qc1_p"][...] = rounded["qc1_p"]
    o["qc2_p"][...] = rounded["qc2_p"]
    fa = seg("fa") + bf_ref[...]
    store_slots(o["logf"], _log_sigmoid(fa)[:, :N_HEADS])
    o["qb"][...] = seg("qb")
    o["kb"][...] = seg("kb")
    o["vb"][...] = seg("vb")
    o["rb"][...] = seg("rb")
    gb = seg("gb").astype(BF16)
    o["loga"][...] = _log_sigmoid(_dot(gb, wup_ref[...]) + bup_ref[...]) * (1.0 / GLA_TAU)
    lane = lax.broadcasted_iota(jnp.int32, (8, LANES), 1)
    tile_norms = jnp.zeros((8, LANES), F32)
    for a, name in enumerate(_NORM_ROWS):
        sq = rounded[name].astype(F32)
        sq = sq * sq
        for h in range(N_HEADS):
            ss = jnp.sum(sq[:, h * HEAD_PAD:(h + 1) * HEAD_PAD], axis=1, keepdims=True)
            tile_norms = jnp.where(lane == N_HEADS * a + h, jnp.max(ss, axis=0, keepdims=True), tile_norms)
    o["norms"][0] = tile_norms


def _proj_call(x, mod, g1, w_ext, w_kvt, bf, wup, bup, vone, tm, tiles_per_mod, layer, depth, stacked, with_qa,
               seq_len):
    n = x.shape[0]
    mod_all, sub = mod
    r = mod_all.shape[1]
    tok = lambda w: pl.BlockSpec((tm, w), lambda i: (i, 0))
    layer_outs = _LAYER_OUTS + ((("qa", W_A, F32),) if with_qa else ())
    alias_in = [] if stacked is None else [stacked[nm] for nm, _, _ in _STACKED_OUTS]
    transposed_a = seq_len % tm == 0
    tps = max(seq_len // tm, 1)
    stacked_specs, stacked_shapes = [], []
    slots = depth if stacked is None else 1
    for nm, rows, w in _STACKED_OUTS:
        if transposed_a and nm in ("ka", "va"):
            stacked_specs.append(pl.BlockSpec((slots, 1, w, tm), lambda i: (layer, i // tps, 0, i % tps)))
            stacked_shapes.append(jax.ShapeDtypeStruct((depth, n // seq_len, w, seq_len), F32))
        else:
            stacked_specs.append(pl.BlockSpec((slots, tm * rows, w), lambda i: (layer, i, 0)))
            stacked_shapes.append(jax.ShapeDtypeStruct((depth, n * rows, w), F32))
    n_fixed = 10
    outs = pl.pallas_call(
        functools.partial(_proj_kernel, n_alias=len(alias_in), with_qa=with_qa, transposed_a=transposed_a),
        grid=(n // tm,),
        in_specs=[tok(D_MODEL), _mod_spec(r, tiles_per_mod, 3 * sub), _mod_spec(r, tiles_per_mod, 3 * sub + 1),
                  _const_spec((1, D_MODEL)), _const_spec((D_MODEL, N_EXT)), _const_spec((2 * W_A, D_MODEL)),
                  _const_spec((1, LANES)), _const_spec((LANES, LANES)), _const_spec((1, LANES)),
                  _const_spec((1, W_PAD))]
        + [pl.BlockSpec(memory_space=pl.ANY)] * len(alias_in),
        out_specs=stacked_specs + [tok(w) for _, w, _ in layer_outs]
        + [pl.BlockSpec((1, 8, LANES), lambda i: (i, 0, 0))],
        out_shape=stacked_shapes + [jax.ShapeDtypeStruct((n, w), dt) for _, w, dt in layer_outs]
        + [jax.ShapeDtypeStruct((n // tm, 8, LANES), F32)],
        input_output_aliases={n_fixed + k: k for k in range(len(alias_in))},
        compiler_params=_params(("parallel",)),
        name="proj",
    )(x, mod_all, mod_all, g1, w_ext, w_kvt, bf, wup, bup, vone, *alias_in)
    names = [nm for nm, _, _ in _STACKED_OUTS] + [nm for nm, _, _ in layer_outs] + ["norms"]
    return dict(zip(names, outs))


def _cumsum_kernel(x_ref, o_ref):
    x = x_ref[0]
    nb = x.shape[0]
    r = lax.broadcasted_iota(jnp.int32, (LANES, LANES), 0)
    c = lax.broadcasted_iota(jnp.int32, (LANES, LANES), 1)
    local = _dot_exact_rhs(x, (r <= c).astype(F32))
    tot = jnp.broadcast_to(local[:, LANES - 1:LANES], (nb, LANES))
    rr = lax.broadcasted_iota(jnp.int32, (nb, nb), 0)
    cc = lax.broadcasted_iota(jnp.int32, (nb, nb), 1)
    o_ref[0] = local + _dot_exact_lhs((cc < rr).astype(F32), tot)


def _cumsum_call(x):
    rows, nb, _ = x.shape
    spec = pl.BlockSpec((1, nb, LANES), lambda i: (i, 0, 0))
    return pl.pallas_call(
        _cumsum_kernel, grid=(rows,), in_specs=[spec], out_specs=spec,
        out_shape=jax.ShapeDtypeStruct(x.shape, F32),
        compiler_params=_params(("parallel",)), name="cumsum",
    )(x)


def _lane_tile(x, width):
    return jnp.concatenate([x] * (width // LANES), axis=1)


GROUP_HEADS = 2
W_GROUP = GROUP_HEADS * HEAD_PAD


def _causal_sweep(i, first, tq, tk, tiles_per_iter, tile):
    r = tq // tk

    def body(jj, carry):
        for u in range(tiles_per_iter):
            tile(jj * tiles_per_iter + u, slice(0, tq), None)
        return carry

    assert r % tiles_per_iter == 0
    lax.fori_loop(lax.div(first, tiles_per_iter), lax.div(i * r, tiles_per_iter), body, 0)
    for d in range(r):
        tile(i * r + d, slice(d * tk, (d + 1) * tk), d * tk)
        if d + 1 < r:
            tile(i * r + d, slice((d + 1) * tk, tq), None)


def _fox_kernel(first_ref, q_ref, k_ref, v_ref, c_ref, o_ref, m_sc, acc_sc, *, tk, tiles_per_iter):
    i = pl.program_id(2)
    first = first_ref[pl.program_id(0), pl.program_id(1), i]
    tq = q_ref.shape[1]
    m_sc[...] = jnp.full(m_sc.shape, NEG, F32)
    acc_sc[...] = jnp.zeros(acc_sc.shape, F32)
    row = lax.broadcasted_iota(jnp.int32, (tk, tk), 0)
    col = lax.broadcasted_iota(jnp.int32, (tk, tk), 1)
    causal = col <= row

    def tile(j, rows, local):
        ks = pl.ds(pl.multiple_of(j * tk, tk), tk)
        for h in range(GROUP_HEADS):
            hs = slice(h * HEAD_PAD, (h + 1) * HEAD_PAD)
            c_first = c_ref[0, h, pl.ds(i * (tq // tk), 1), :][:, 0:1]
            t = _nt_dot(q_ref[0, rows, hs], k_ref[0, ks, hs]) + (c_first - c_ref[0, h, pl.ds(j, 1), :]) * LOG2E
            if local is not None:
                t = jnp.where(causal, t, NEG)
            m_old = m_sc[h, rows, :]
            m_new = jnp.maximum(m_old, jnp.max(t, axis=1, keepdims=True))
            p = jnp.exp2(t - _lane_tile(m_new, tk)).astype(BF16)
            acc_sc[h, rows, :] = jnp.exp2(m_old - m_new) * acc_sc[h, rows, :] + _dot(p, v_ref[0, ks, hs])
            m_sc[h, rows, :] = m_new

    _causal_sweep(i, first, tq, tk, tiles_per_iter, tile)
    for h in range(GROUP_HEADS):
        acc = acc_sc[h]
        o_ref[0, :, h * HEAD_PAD:(h + 1) * HEAD_PAD] = (acc / acc[:, HD_A:HD_A + 1]).astype(BF16)


def _resident_spec(t, width):
    return pl.BlockSpec((1, t, width), lambda b_, g, i, first: (b_, 0, g), pipeline_mode=pl.Buffered(1))


def _fox_call(first, qa_p, ka_p, va_p, c, tq, tk):
    b, t, _ = qa_p.shape
    qspec = pl.BlockSpec((1, tq, W_GROUP), lambda b_, g, i, first: (b_, i, g))
    grid_spec = pltpu.PrefetchScalarGridSpec(
        num_scalar_prefetch=1,
        grid=(b, N_HEADS // GROUP_HEADS, t // tq),
        in_specs=[qspec, _resident_spec(t, W_GROUP), _resident_spec(t, W_GROUP),
                  pl.BlockSpec((1, GROUP_HEADS, t // tk, tk), lambda b_, g, i, first: (b_, g, 0, 0))],
        out_specs=qspec,
        scratch_shapes=[pltpu.VMEM((GROUP_HEADS, tq, LANES), F32), pltpu.VMEM((GROUP_HEADS, tq, HEAD_PAD), F32)])
    return pl.pallas_call(
        functools.partial(_fox_kernel, tk=tk, tiles_per_iter=1),
        grid_spec=grid_spec,
        out_shape=jax.ShapeDtypeStruct((b, t, W_PAD), BF16),
        compiler_params=_params(("parallel", "parallel", "arbitrary")),
        name="fox",
    )(first, qa_p, ka_p, va_p, c.reshape(b, N_HEADS, t // tk, tk))


def _diff_lambda(lp, lam_init):
    a = jnp.sum(lp[0:1] * lp[1:2], axis=1, keepdims=True)
    b = jnp.sum(lp[2:3] * lp[3:4], axis=1, keepdims=True)
    return jnp.exp(a) - jnp.exp(b) + lam_init


DIFF_GROUP = 1
W_DIFF_GROUP = DIFF_GROUP * HEAD_PAD


def _diff_kernel(first_ref, q1_ref, q2_ref, k_ref, v_ref, slope_ref, lam_ref, g_ref, o_ref, m_sc, l_sc, a_sc, *,
                 tk, tiles_per_iter, lam_init):
    g = pl.program_id(1)
    i = pl.program_id(2)
    first = first_ref[pl.program_id(0), g, i]
    tq = q1_ref.shape[1]
    m_sc[...] = jnp.full(m_sc.shape, NEG, F32)
    l_sc[...] = jnp.zeros(l_sc.shape, F32)
    a_sc[...] = jnp.zeros(a_sc.shape, F32)
    row = lax.broadcasted_iota(jnp.int32, (tk, tk), 0)
    col = lax.broadcasted_iota(jnp.int32, (tk, tk), 1)
    rel_diag = (2 * jnp.minimum(row, col) - col).astype(F32)
    visible = lax.shift_right_logical(col, 6) <= lax.shift_right_logical(row, 6)
    lane = lax.broadcasted_iota(jnp.int32, (1, tk), 1)

    def tile(j, rows, local):
        ks = pl.ds(pl.multiple_of(j * tk, tk), tk)
        if local is None:
            rel = (lane + (j * tk - i * tq)).astype(F32)
        else:
            rel = rel_diag + float(local)
        for h in range(DIFF_GROUP):
            hs = slice(h * HEAD_PAD, (h + 1) * HEAD_PAD)
            bias = (slope_ref[g * DIFF_GROUP + h][:, 0:1] * LOG2E) * rel
            if local is not None:
                bias = jnp.where(visible, bias, NEG)
            k = k_ref[0, ks, hs]
            v = v_ref[0, ks, hs]
            nrows = rows.stop - rows.start
            scores = _nt_dot(jnp.concatenate([q1_ref[0, rows, hs], q2_ref[0, rows, hs]], axis=0), k)
            probs, alphas = [], []
            for m in range(2):
                n = 2 * h + m
                t = scores[m * nrows:(m + 1) * nrows] + bias
                m_old = m_sc[n, rows, :]
                m_new = jnp.maximum(m_old, jnp.max(t, axis=1, keepdims=True))
                alpha = jnp.exp2(m_old - m_new)
                p = jnp.exp2(t - _lane_tile(m_new, tk))
                l_sc[n, rows, :] = alpha * l_sc[n, rows, :] + jnp.sum(p, axis=1, keepdims=True)
                m_sc[n, rows, :] = m_new
                probs.append(p.astype(BF16))
                alphas.append(alpha)
            pv = _dot(jnp.concatenate(probs, axis=0), v)
            for m in range(2):
                n = 2 * h + m
                a_sc[n, rows, :] = alphas[m] * a_sc[n, rows, :] + pv[m * nrows:(m + 1) * nrows]

    _causal_sweep(i, first, tq, tk, tiles_per_iter, tile)
    lam = _diff_lambda(lam_ref[...], lam_init)
    for h in range(DIFF_GROUP):
        out = a_sc[2 * h] / l_sc[2 * h] - lam * (a_sc[2 * h + 1] / l_sc[2 * h + 1])
        o_ref[0, :, h * HEAD_PAD:(h + 1) * HEAD_PAD] = (_rms(out, g_ref[...]) * (1.0 - lam_init)).astype(BF16)


def _diff_call(first, qc1_p, qc2_p, kc_b, vc_b, slopes, lam_p, g_diff, lam_init, tq, tk):
    assert CHUNK == 64 and tk % CHUNK == 0
    b, t, _ = qc1_p.shape
    qspec = pl.BlockSpec((1, tq, W_DIFF_GROUP), lambda b_, g, i, first: (b_, i, g))
    stat = pltpu.VMEM((2 * DIFF_GROUP, tq, LANES), F32)
    grid_spec = pltpu.PrefetchScalarGridSpec(
        num_scalar_prefetch=1,
        grid=(b, N_HEADS // DIFF_GROUP, t // tq),
        in_specs=[qspec, qspec, _resident_spec(t, W_DIFF_GROUP), _resident_spec(t, W_DIFF_GROUP),
                  pl.BlockSpec((N_HEADS, 1, LANES), lambda b_, g, i, first: (0, 0, 0)),
                  pl.BlockSpec((4, HD_C), lambda b_, g, i, first: (0, 0)),
                  pl.BlockSpec((1, HEAD_PAD), lambda b_, g, i, first: (0, 0))],
        out_specs=qspec,
        scratch_shapes=[stat, stat, pltpu.VMEM((2 * DIFF_GROUP, tq, HEAD_PAD), F32)])
    return pl.pallas_call(
        functools.partial(_diff_kernel, tk=tk, tiles_per_iter=2, lam_init=lam_init),
        grid_spec=grid_spec,
        out_shape=jax.ShapeDtypeStruct((b, t, W_C), BF16),
        compiler_params=_params(("parallel", "parallel", "arbitrary")),
        name="diff",
    )(first, qc1_p, qc2_p, kc_b, vc_b, slopes, lam_p, g_diff)


def _gla_kernel(q_ref, k_ref, v_ref, la_ref, r_ref, s0_ref, g_ref, o_ref, s_ref, *, chunk):
    ti = pl.program_id(1)
    n_chunks = q_ref.shape[1] // chunk
    n_seq = q_ref.shape[0]

    @pl.when(ti == 0)
    def _():
        s_ref[...] = s0_ref[...]

    tri_r = lax.broadcasted_iota(jnp.int32, (chunk, chunk), 0)
    tri_c = lax.broadcasted_iota(jnp.int32, (chunk, chunk), 1)
    incl = (tri_c <= tri_r).astype(F32)
    srow = lax.broadcasted_iota(jnp.int32, (N_HEADS * chunk, LANES), 0) // chunk
    slane = lax.broadcasted_iota(jnp.int32, (N_HEADS * chunk, LANES), 1) // DK_B
    qmask = srow == slane
    arow = lax.broadcasted_iota(jnp.int32, (N_HEADS * chunk, chunk), 0) % chunk
    acol = lax.broadcasted_iota(jnp.int32, (N_HEADS * chunk, chunk), 1)
    causal = acol <= arow
    vlane = lax.broadcasted_iota(jnp.int32, (chunk, W_B), 1) // DV_B
    st_row = lax.broadcasted_iota(jnp.int32, (W_B, LANES), 0) // DV_B
    st_lane = lax.broadcasted_iota(jnp.int32, (W_B, LANES), 1) // DK_B
    diag_blocks = st_row == st_lane
    gr = lax.broadcasted_iota(jnp.int32, (W_B, W_B), 0) // DV_B
    gc = lax.broadcasted_iota(jnp.int32, (W_B, W_B), 1) // DV_B
    group_mean = jnp.where(gr == gc, 1.0 / DV_B, 0.0)

    for c, sq in [(c, sq) for c in range(n_chunks) for sq in range(n_seq)]:
        sl = slice(c * chunk, (c + 1) * chunk)
        cb = _dot_exact_lhs(incl, la_ref[sq, sl, :])
        cb_last = cb[chunk - 1:chunk, :]
        q = q_ref[sq, sl, :] * (DK_B ** -0.5)
        k = k_ref[sq, sl, :]
        v = v_ref[sq, sl, :]
        vb = v.astype(BF16)
        qe = q * jnp.exp(cb)
        ke = (k * jnp.exp(-cb)).astype(BF16)
        kl = (k * jnp.exp(cb_last - cb)).astype(BF16)
        decay = jnp.exp(cb_last)
        qstack = jnp.where(qmask, jnp.concatenate([qe] * N_HEADS, axis=0), 0.0).astype(BF16)
        a = jnp.where(causal, _nt_dot(qstack, ke), 0.0).astype(BF16)
        oi = _dot(a, vb)
        o = jnp.zeros((chunk, W_B), F32)
        for h in range(N_HEADS):
            o = o + jnp.where(vlane == h, oi[h * chunk:(h + 1) * chunk], 0.0)
        st = s_ref[sq]
        o = o + _nt_dot(qe.astype(BF16), st.astype(BF16))
        upd = _dot(v.T.astype(BF16), kl)
        s_ref[sq] = decay * st + jnp.where(diag_blocks, upd, 0.0)
        ms = _dot_exact_rhs(o * o, group_mean)
        r = r_ref[sq, sl, :]
        o_ref[sq, sl, :] = (o * lax.rsqrt(ms + EPS) * g_ref[...] * (r * jax.nn.sigmoid(r))).astype(BF16)


def _gla_call(qb, kb, vb, loga, rb, s0_t, g_gla, tg, chunk, group):
    b, t, _ = qb.shape
    tok = lambda w: pl.BlockSpec((group, tg, w), lambda b_, i: (b_, i, 0))
    sspec = pl.BlockSpec((group, W_B, LANES), lambda b_, i: (b_, 0, 0))
    return pl.pallas_call(
        functools.partial(_gla_kernel, chunk=chunk),
        grid=(b // group, t // tg),
        in_specs=[tok(LANES), tok(LANES), tok(W_B), tok(LANES), tok(W_B), sspec,
                  pl.BlockSpec((1, W_B), lambda b_, i: (0, 0))],
        out_specs=[tok(W_B), sspec],
        out_shape=[jax.ShapeDtypeStruct((b, t, W_B), BF16), jax.ShapeDtypeStruct((b, W_B, LANES), F32)],
        compiler_params=_params(("parallel", "arbitrary")),
        name="gla",
    )(qb, kb, vb, loga, rb, s0_t, g_gla)


def _softmax_two(t_past, t_new, exp_fn):
    m = jnp.maximum(jnp.max(t_past, axis=1, keepdims=True), jnp.max(t_new, axis=1, keepdims=True))
    p_past = exp_fn(t_past - m)
    p_new = exp_fn(t_new - m)
    l = jnp.sum(p_past, axis=1, keepdims=True) + jnp.sum(p_new, axis=1, keepdims=True)
    return p_past, p_new, l


def _fox_s_kernel(q_ref, kn_ref, vn_ref, kpt_ref, vpt_ref, c_ref, o_ref):
    t, past = q_ref.shape[1], kpt_ref.shape[1]
    q = q_ref[0]
    kn = kn_ref[...].astype(BF16)
    vn = vn_ref[...].astype(BF16)
    kpt = kpt_ref[...].astype(BF16)
    vpt = vpt_ref[...].astype(BF16)
    lane_head = lax.broadcasted_iota(jnp.int32, (t, W_A), 1) // HD_A
    row = lax.broadcasted_iota(jnp.int32, (t, t), 0)
    col = lax.broadcasted_iota(jnp.int32, (t, t), 1)
    o = jnp.zeros((t, W_A), F32)
    for h in range(N_HEADS):
        qh = jnp.where(lane_head == h, q, 0.0).astype(BF16)
        ch = c_ref[0, h:h + 1, :]
        cref = ch[:, past:past + 1]
        t_past = _dot(qh, kpt) + (cref - ch[:, :past])
        t_new = jnp.where(col <= row, _nt_dot(qh, kn) + (cref - ch[:, past:past + t]), NEG)
        p_past, p_new, l = _softmax_two(t_past, t_new, jnp.exp)
        oh = (_nt_dot(p_past.astype(BF16), vpt) + _dot(p_new.astype(BF16), vn)) / l
        o = o + jnp.where(lane_head == h, oh, 0.0)
    o_ref[0] = o.astype(BF16)


def _fox_s_call(layer, qa, ka, va, cache_kt, cache_vt, c):
    b, t, _ = qa.shape
    past = cache_kt.shape[3]
    new = pl.BlockSpec((1, t, W_A), lambda i: (i, 0, 0))
    new_l = lambda: pl.BlockSpec((None, t, W_A), lambda i: (layer, i, 0))
    old_l = lambda: pl.BlockSpec((None, None, W_A, past), lambda i: (layer, i, 0, 0))
    return pl.pallas_call(
        _fox_s_kernel,
        grid=(b,),
        in_specs=[new, new_l(), new_l(), old_l(), old_l(),
                  pl.BlockSpec((1, N_HEADS, c.shape[2]), lambda i: (i, 0, 0))],
        out_specs=new,
        out_shape=jax.ShapeDtypeStruct((b, t, W_A), BF16),
        compiler_params=_params(("parallel",)),
        name="fox_s",
    )(qa, ka, va, cache_kt, cache_vt, c)


def _diff_s_kernel(q1_ref, q2_ref, kn_ref, vn_ref, kp_ref, vp_ref, slope_ref, lam_ref, g_ref, o_ref, *, lam_init):
    t, past = q1_ref.shape[1], kp_ref.shape[0] // N_HEADS
    lam = _diff_lambda(lam_ref[...], lam_init)
    row = lax.broadcasted_iota(jnp.int32, (t, t), 0) + past
    col = lax.broadcasted_iota(jnp.int32, (t, t), 1) + past
    visible_new = lax.shift_right_logical(col, 6) <= lax.shift_right_logical(row, 6)
    dist_new = jnp.abs(row - col).astype(F32)
    prow = lax.broadcasted_iota(jnp.int32, (t, past), 0) + past
    pcol = lax.broadcasted_iota(jnp.int32, (t, past), 1)
    visible_past = lax.shift_right_logical(pcol, 6) <= lax.shift_right_logical(prow, 6)
    dist_past = (prow - pcol).astype(F32)
    for h in range(N_HEADS):
        hs = slice(h * HEAD_PAD, (h + 1) * HEAD_PAD)
        slope = slope_ref[h][:, 0:1] * LOG2E
        kn = kn_ref[pl.ds(h, t, stride=N_HEADS), :].astype(BF16)
        vn = vn_ref[pl.ds(h, t, stride=N_HEADS), :].astype(BF16)
        kp = kp_ref[pl.ds(h, past, stride=N_HEADS), :].astype(BF16)
        vp = vp_ref[pl.ds(h, past, stride=N_HEADS), :].astype(BF16)
        outs = []
        for q_ref in (q1_ref, q2_ref):
            q = q_ref[0, :, hs]
            t_past = jnp.where(visible_past, _nt_dot(q, kp) - slope * dist_past, NEG)
            t_new = jnp.where(visible_new, _nt_dot(q, kn) - slope * dist_new, NEG)
            p_past, p_new, l = _softmax_two(t_past, t_new, jnp.exp2)
            outs.append((_dot(p_past.astype(BF16), vp) + _dot(p_new.astype(BF16), vn)) / l)
        out = outs[0] - lam * outs[1]
        o_ref[0, :, hs] = (_rms(out, g_ref[...]) * (1.0 - lam_init)).astype(BF16)


def _diff_s_call(layer, qc1_p, qc2_p, kc, vc, cache_k, cache_v, slopes, lam_p, g_diff, lam_init):
    assert CHUNK == 64
    b, t, _ = qc1_p.shape
    rows_past = cache_k.shape[2]
    new = lambda: pl.BlockSpec((1, t, W_C), lambda i: (i, 0, 0))
    new_l = lambda: pl.BlockSpec((None, t * N_HEADS, HEAD_PAD), lambda i: (layer, i, 0))
    old_l = lambda: pl.BlockSpec((None, None, rows_past, HEAD_PAD), lambda i: (layer, i, 0, 0))
    return pl.pallas_call(
        functools.partial(_diff_s_kernel, lam_init=lam_init),
        grid=(b,),
        in_specs=[new(), new(), new_l(), new_l(), old_l(), old_l(),
                  pl.BlockSpec((N_HEADS, 1, LANES), lambda i: (0, 0, 0)),
                  pl.BlockSpec((4, HD_C), lambda i: (0, 0)),
                  pl.BlockSpec((1, HEAD_PAD), lambda i: (0, 0))],
        out_specs=new(),
        out_shape=jax.ShapeDtypeStruct((b, t, W_C), BF16),
        compiler_params=_params(("parallel",)),
        name="diff_s",
    )(qc1_p, qc2_p, kc, vc, cache_k, cache_v, slopes, lam_p, g_diff)


def _out_kernel(x_ref, oa_ref, gla_ref, dif_ref, gate_ref, g2_ref, woa_ref, wog_ref, wod_ref, o_ref):
    y = _dot(oa_ref[...], woa_ref[...]) + _dot(gla_ref[...], wog_ref[...]) + _dot(dif_ref[...], wod_ref[...])
    o_ref[...] = x_ref[...] + gate_ref[0] * _rms(y, g2_ref[...])


def _out_call(x, oa, gla, dif, mod, g2, w_oa, w_og, w_od, tm, tiles_per_mod):
    n = x.shape[0]
    mod_all, sub = mod
    r = mod_all.shape[1]
    tok = lambda w: pl.BlockSpec((tm, w), lambda i: (i, 0))
    return pl.pallas_call(
        _out_kernel,
        grid=(n // tm,),
        in_specs=[tok(D_MODEL), tok(oa.shape[1]), tok(W_B), tok(W_C), _mod_spec(r, tiles_per_mod, 3 * sub + 2),
                  _const_spec((1, D_MODEL)), _const_spec(w_oa.shape), _const_spec(w_og.shape),
                  _const_spec(w_od.shape)],
        out_specs=tok(D_MODEL),
        out_shape=jax.ShapeDtypeStruct((n, D_MODEL), F32),
        compiler_params=_params(("parallel",)),
        name="out_proj",
    )(x, oa, gla, dif, mod_all, g2, w_oa, w_og, w_od)


def _pad_heads(w, hd, off=0):
    rows = w.shape[0]
    zeros = lambda width: [jnp.zeros((rows, width), w.dtype)] if width else []
    pieces = []
    for h in range(N_HEADS):
        pieces += zeros(off) + [w[:, h * hd:(h + 1) * hd]] + zeros(HEAD_PAD - off - hd)
    return jnp.concatenate(pieces, axis=1)


def _pad_lanes(w, width=LANES):
    return jnp.pad(w, ((0, 0), (0, width - w.shape[1])))


def _layer_params(l, w_in, b_f, w_gla_up, b_gla_up, g_gla, g_diff, w_out):
    offs = [int(o) for o in np.cumsum(PROJ_SIZES)[:-1]]
    qa, ka, va, fa, qb, kb, vb, gb, rb, qc, kc, vc = jnp.split(w_in[l], offs, axis=1)
    segs = {
        "ka": ka, "va": va, "kc": kc, "vc": vc, "qa": qa, "qc": qc,
        "qa_p": _pad_heads(qa, HD_A), "ka_p": _pad_heads(ka, HD_A), "va_p": _pad_heads(va, HD_A),
        "fa": _pad_lanes(fa), "qb": qb, "kb": kb, "vb": vb, "rb": rb, "gb": _pad_lanes(gb),
    }
    w_ext = jnp.concatenate([segs[n] for n, _ in _SEG_WIDTHS], axis=1).astype(BF16)
    wo = w_out[l]
    vone = np.zeros((1, W_PAD), np.float32)
    vone[0, HD_A::HEAD_PAD] = 1.0
    return dict(
        w_ext=w_ext, w_kvt=jnp.concatenate([ka, va], axis=1).T.astype(BF16),
        bf=_pad_lanes(b_f[l][None, :]),
        wup=jnp.pad(w_gla_up[l], ((0, LANES - GLA_RANK), (0, 0))).astype(BF16),
        bup=b_gla_up[l][None, :],
        vone=jnp.asarray(vone),
        g_gla=jnp.tile(g_gla[l], N_HEADS)[None, :], g_diff=g_diff[l][None, :],
        w_oa=wo[:W_A].astype(BF16),
        w_oa_p=_pad_heads(wo[:W_A].T, HD_A).T.astype(BF16),
        w_og=wo[W_A:W_A + W_B].astype(BF16), w_od=wo[W_A + W_B:].astype(BF16),
    )


def _state_to_blockdiag_t(s):
    eye = jnp.eye(N_HEADS, dtype=s.dtype)
    return jnp.einsum("bhkv,hg->bhvgk", s, eye).reshape(s.shape[0], W_B, N_HEADS * DK_B)


def _blockdiag_t_to_state(st):
    b = st.shape[0]
    s5 = st.reshape(b, N_HEADS, DV_B, N_HEADS, DK_B)
    diag = jnp.stack([s5[:, h, :, h, :] for h in range(N_HEADS)], axis=1)
    return jnp.swapaxes(diag, 2, 3)


SKIP_LOG2 = 160.0
NORM_SLACK = 1.01


def _tile_max(x, factor):
    return jnp.max(x.reshape(x.shape[:-1] + (x.shape[-1] // factor, factor)), axis=-1)


def _first_tiles(qn, kn, gap, tq, tk, group):
    b, h, nq = qn.shape
    r = tq // tk
    kn_diag = _tile_max(kn, r)
    bound = NORM_SLACK * (qn[..., None] * kn[:, :, None, :] + (qn * kn_diag)[..., None]) + gap
    dead = (bound < -SKIP_LOG2).reshape(b, h // group, group, nq, kn.shape[-1]).all(axis=2)
    lead = jnp.sum(jnp.cumprod(dead.astype(jnp.int32), axis=-1), axis=-1)
    return jnp.minimum(lead, jnp.arange(nq, dtype=jnp.int32) * r).astype(jnp.int32)


def _tile(n, pref):
    t = min(n, pref)
    assert n % t == 0, (n, t)
    return t


def _trunk(x, mods, params, ffn_w, g_norm, lam_params, mixer, tm, tiles_per_mod, with_qa):
    b, t, _ = x.shape
    n = b * t
    depth = len(params)
    xf = x.reshape(n, D_MODEL)
    stacked = None
    gla_states = []
    for l in range(depth):
        p = params[l]
        gn = lambda k: g_norm[l, k][None, :]
        xf = _ffn_call(xf, (mods[l], 0), gn(0), gn(1), *ffn_w, l, 0, 0.5, tm, tiles_per_mod)
        pr = _proj_call(xf, (mods[l], 1), gn(2), p["w_ext"], p["w_kvt"], p["bf"], p["wup"], p["bup"], p["vone"], tm,
                        tiles_per_mod, l, depth, stacked, with_qa, t)
        stacked = {nm: pr[nm] for nm, _, _ in _STACKED_OUTS}
        lam_init = 0.8 - 0.6 * math.exp(-0.3 * l)
        oa, gla, dif, s_fin, w_oa = mixer(l, pr, p, lam_params[l], lam_init, b, t)
        gla_states.append(s_fin)
        xf = _out_call(xf, oa, gla, dif, (mods[l], 1), gn(3), w_oa, p["w_og"], p["w_od"], tm, tiles_per_mod)
        xf = _ffn_call(xf, (mods[l], 2), gn(4), gn(5), *ffn_w, l, 1, 0.5, tm, tiles_per_mod)
    shp = lambda nm, *tail: stacked[nm].reshape((depth, b, t) + tail)
    if stacked["ka"].ndim == 4:
        shp_a = lambda nm: jnp.transpose(stacked[nm].reshape(depth, b, N_HEADS, HD_A, t), (0, 1, 4, 2, 3))
    else:
        shp_a = lambda nm: shp(nm, N_HEADS, HD_A)
    states = [shp_a("ka"), shp_a("va"), shp("logf", N_HEADS), jnp.stack(gla_states),
              shp("kc", N_HEADS, 2 * HD_C), shp("vc", N_HEADS, 2 * HD_C)]
    return xf.reshape(b, t, D_MODEL), states


def kernel(x_prompt, x_sample, c_prompt, c_sample, cache_fox_k, cache_fox_v, cache_fox_logf, state_gla,
           cache_diff_k, cache_diff_v, w_ada, b_ada, g_norm, w_ffn_in, w_ffn_out, w_in, b_f, w_gla_up,
           b_gla_up, g_gla, g_diff, lam_params, w_out):
    depth = w_in.shape[0]
    bp, tp, _ = x_prompt.shape
    bs, ts, _ = x_sample.shape
    past = cache_fox_k.shape[2]

    params = [_layer_params(l, w_in, b_f, w_gla_up, b_gla_up, g_gla, g_diff, w_out) for l in range(depth)]
    ffn_w = (w_ffn_in.astype(BF16), w_ffn_out.astype(BF16))

    slopes = 2.0 ** (-8.0 * jnp.arange(1, N_HEADS + 1, dtype=F32) / N_HEADS)
    slopes = jnp.broadcast_to(slopes[:, None, None], (N_HEADS, 1, LANES))

    rows = bp + bs
    rows_pad = -(-rows // 8) * 8
    c_all = jnp.pad(jnp.concatenate([c_prompt, c_sample], axis=0), ((0, rows_pad - rows), (0, 0)))
    mods_p, mods_s = [], []
    for l in range(depth):
        mod = _ada_call(c_all, w_ada[l].astype(BF16), b_ada[l][None, :])
        mods_p.append(mod[:bp, None, :])
        mods_s.append(jnp.repeat(mod[bp:rows], ts, axis=0)[None])

    tm_p = _tile(tp, 512)
    tq = _tile(tp, 1024)
    tk = _tile(tp, 512)
    tg = _tile(tp, 512)

    def prompt_mixer(l, pr, p, lam_p, lam_init, b, t):
        r3 = lambda a: a.reshape(b, t, a.shape[-1])
        logf_t = jnp.swapaxes(r3(pr["logf"][l]), 1, 2).reshape(b * N_HEADS, t // LANES, LANES)
        c = _cumsum_call(logf_t).reshape(b, N_HEADS, t)
        nrm = jnp.sqrt(pr["norms"][:, 0, :N_HEADS * len(_NORM_ROWS)])
        nrm = dict(zip(_NORM_ROWS, jnp.moveaxis(nrm.reshape(b, t // tm_p, len(_NORM_ROWS), N_HEADS), (2, 3), (0, 2))))
        q_tiles = lambda a: _tile_max(a, tq // tm_p)
        k_tiles = lambda a: _tile_max(a, tk // tm_p)
        gap_a = (c[:, :, ::tq][..., None] - c[:, :, tk - 1::tk][:, :, None, :]) * LOG2E
        first_a = _first_tiles(q_tiles(nrm["qa_p"]), k_tiles(nrm["ka_p"]), gap_a, tq, tk, GROUP_HEADS)
        oa = _fox_call(first_a, r3(pr["qa_p"]), r3(pr["ka_p"]), r3(pr["va_p"]), c, tq, tk)
        s0_t = jnp.zeros((b, W_B, N_HEADS * DK_B), F32)
        gla, s_t = _gla_call(r3(pr["qb"]), r3(pr["kb"]), r3(pr["vb"]), r3(pr["loga"]), r3(pr["rb"]), s0_t,
                             p["g_gla"], tg, min(CHUNK, t), math.gcd(b, 2))
        last_key = jnp.arange(tk - 1, t, tk, dtype=F32)[None, :] - jnp.arange(0, t, tq, dtype=F32)[:, None]
        gap_c = slopes[:, 0, :1, None] * LOG2E * last_key[None]
        first_c = _first_tiles(q_tiles(jnp.maximum(nrm["qc1_p"], nrm["qc2_p"])), k_tiles(nrm["kc_b"]), gap_c, tq, tk,
                                 DIFF_GROUP)
        dif = _diff_call(first_c, r3(pr["qc1_p"]), r3(pr["qc2_p"]), r3(pr["kc_b"]), r3(pr["vc_b"]), slopes, lam_p,
                         p["g_diff"], lam_init, tq, tk)
        flat = lambda a: a.reshape(b * t, a.shape[-1])
        return flat(oa), flat(gla), flat(dif), _blockdiag_t_to_state(s_t), p["w_oa_p"]

    feature_time = lambda a: jnp.transpose(a, (0, 1, 3, 4, 2)).reshape(depth, bs, W_A, past)
    cache_kt, cache_vt = feature_time(cache_fox_k), feature_time(cache_fox_v)
    cache_dk = cache_diff_k.reshape(depth, bs, past * N_HEADS, HEAD_PAD)
    cache_dv = cache_diff_v.reshape(depth, bs, past * N_HEADS, HEAD_PAD)

    def sample_mixer(l, pr, p, lam_p, lam_init, b, t):
        r3 = lambda a: a.reshape(b, t, a.shape[-1])
        total = past + t
        nb = -(-total // (8 * LANES)) * 8
        lf = jnp.concatenate([cache_fox_logf[l].astype(F32), r3(pr["logf"][l])], axis=1)
        lf = jnp.pad(jnp.swapaxes(lf, 1, 2), ((0, 0), (0, 0), (0, nb * LANES - total)))
        c = _cumsum_call(lf.reshape(b * N_HEADS, nb, LANES)).reshape(b, N_HEADS, nb * LANES)
        oa = _fox_s_call(l, r3(pr["qa"]), pr["ka"], pr["va"], cache_kt, cache_vt, c)
        gla, s_t = _gla_call(r3(pr["qb"]), r3(pr["kb"]), r3(pr["vb"]), r3(pr["loga"]), r3(pr["rb"]),
                             _state_to_blockdiag_t(state_gla[l].astype(F32)), p["g_gla"], t, min(CHUNK, t),
                             math.gcd(b, 8))
        dif = _diff_s_call(l, r3(pr["qc1_p"]), r3(pr["qc2_p"]), pr["kc"], pr["vc"], cache_dk, cache_dv,
                           slopes, lam_p, p["g_diff"], lam_init)
        flat = lambda a: a.reshape(b * t, a.shape[-1])
        return flat(oa), flat(gla), flat(dif), _blockdiag_t_to_state(s_t), p["w_oa"]

    y_p, sp = _trunk(x_prompt, mods_p, params, ffn_w, g_norm, lam_params, prompt_mixer, tm_p, tp // tm_p, False)
    y_s, ss = _trunk(x_sample, mods_s, params, ffn_w, g_norm, lam_params, sample_mixer, bs * ts, 1, True)
    return (y_p, y_s, sp[0], sp[1], sp[2], sp[3], sp[4], sp[5], ss[0], ss[1], ss[2], ss[3], ss[4], ss[5])
```

```python
import functools
import math

import jax
import jax.numpy as jnp
import numpy as np
from jax import lax
from jax.experimental import pallas as pl
from jax.experimental.pallas import tpu as pltpu

F32 = jnp.float32
BF16 = jnp.bfloat16

D_MODEL = 1024
N_HEADS = 4
HD_A = 64
DK_B = 32
DV_B = 64
GLA_RANK = 16
GLA_TAU = 16.0
HD_C = 64
CHUNK = 64
W_A = N_HEADS * HD_A
W_B = N_HEADS * DV_B
W_C = N_HEADS * 2 * HD_C
D_FF = ((8 * D_MODEL // 3 + 255) // 256) * 256
N_SUB = 3
PROJ_SIZES = (W_A, W_A, W_A, N_HEADS, N_HEADS * DK_B, N_HEADS * DK_B, W_B, GLA_RANK, W_B, W_C, W_C, W_C)
NEG = -1e30
HIDDEN = -1e34
EPS = 1e-6
LOG2E = math.log2(math.e)

LANES = 128
HEAD_PAD = 128
W_PAD = N_HEADS * HEAD_PAD
VMEM_LIMIT = 56 * 1024 * 1024

_SEG_WIDTHS = (
    ("ka", W_A), ("va", W_A), ("kc", W_C), ("vc", W_C), ("qa", W_A),
    ("qa_p", W_PAD), ("ka_p", W_PAD), ("va_p", W_PAD), ("qc", W_C),
    ("fa", LANES), ("qb", LANES), ("kb", LANES), ("vb", W_B), ("rb", W_B), ("gb", LANES),
)
_SEG = {}
_off = 0
for _name, _w in _SEG_WIDTHS:
    _SEG[_name] = (_off, _off + _w)
    _off += _w
N_EXT = _off


def _nt_dot(a, b):
    return lax.dot_general(a, b, (((1,), (1,)), ((), ())), preferred_element_type=F32)


def _dot(a, b):
    return jnp.dot(a, b, preferred_element_type=F32)


def _split3(x):
    hi = x.astype(BF16)
    r1 = x - hi.astype(F32)
    mid = r1.astype(BF16)
    lo = (r1 - mid.astype(F32)).astype(BF16)
    return hi, mid, lo


def _dot_exact_rhs(x, m):
    mb = m.astype(BF16)
    hi, mid, lo = _split3(x)
    return _dot(hi, mb) + _dot(mid, mb) + _dot(lo, mb)


def _dot_exact_lhs(m, x):
    mb = m.astype(BF16)
    hi, mid, lo = _split3(x)
    return _dot(mb, hi) + _dot(mb, mid) + _dot(mb, lo)


def _rms(x, g):
    return x * lax.rsqrt(jnp.mean(x * x, axis=-1, keepdims=True) + EPS) * g


def _log_sigmoid(x):
    return jnp.minimum(x, 0.0) - jnp.log1p(jnp.exp(-jnp.abs(x)))


def _const_spec(shape):
    nd = len(shape)
    return pl.BlockSpec(shape, lambda *_: (0,) * nd, pipeline_mode=pl.Buffered(1))


def _params(sem):
    return pltpu.CompilerParams(dimension_semantics=sem, vmem_limit_bytes=VMEM_LIMIT)


def _ada_kernel(c_ref, w_ref, b_ref, o_ref):
    c = c_ref[...]
    a = (c * jax.nn.sigmoid(c)).astype(BF16)
    o_ref[...] = _dot(a, w_ref[...]) + b_ref[...]


def _ada_call(c, w, b):
    m, n = c.shape[0], w.shape[1]
    tn = n // 8
    return pl.pallas_call(
        _ada_kernel,
        grid=(n // tn,),
        in_specs=[pl.BlockSpec((m, D_MODEL), lambda i: (0, 0)),
                  pl.BlockSpec((D_MODEL, tn), lambda i: (0, i)),
                  pl.BlockSpec((1, tn), lambda i: (0, i))],
        out_specs=pl.BlockSpec((m, tn), lambda i: (0, i)),
        out_shape=jax.ShapeDtypeStruct((m, n), F32),
        compiler_params=_params(("parallel",)),
        name="ada",
    )(c, w, b)


FF_CHUNK = 256


def _ffn_kernel(x_ref, shift_ref, scale_ref, gate_ref, g1_ref, g2_ref, win_ref, wout_ref, o_ref, *, res_w):
    x = x_ref[...]
    h = _rms(x, g1_ref[...]) * (1.0 + scale_ref[0]) + shift_ref[0]
    hb = h.astype(BF16)
    y = jnp.zeros(x.shape, F32)
    for c in range(D_FF // FF_CHUNK):
        lo, hi = c * FF_CHUNK, (c + 1) * FF_CHUNK
        g = _dot(hb, win_ref[:, lo:hi])
        u = _dot(hb, win_ref[:, D_FF + lo:D_FF + hi])
        a = (g * jax.nn.sigmoid(g) * u).astype(BF16)
        y = y + _dot(a, wout_ref[lo:hi, :])
    o_ref[...] = x + res_w * gate_ref[0] * _rms(y, g2_ref[...])


def _mod_spec(r, tiles_per_mod, col):
    return pl.BlockSpec((1, r, D_MODEL), lambda i: (i // tiles_per_mod, 0, col))


def _ffn_call(x, mod, g1, g2, w_in, w_out, layer, which, res_w, tm, tiles_per_mod):
    n = x.shape[0]
    mod_all, sub = mod
    r = mod_all.shape[1]
    tok = pl.BlockSpec((tm, D_MODEL), lambda i: (i, 0))
    pick = lambda rows, cols: pl.BlockSpec((None, None, rows, cols), lambda i: (layer, which, 0, 0),
                                           pipeline_mode=pl.Buffered(1))
    return pl.pallas_call(
        functools.partial(_ffn_kernel, res_w=res_w),
        grid=(n // tm,),
        in_specs=[tok] + [_mod_spec(r, tiles_per_mod, 3 * sub + k) for k in range(3)]
        + [_const_spec((1, D_MODEL)), _const_spec((1, D_MODEL)), pick(D_MODEL, 2 * D_FF), pick(D_FF, D_MODEL)],
        out_specs=tok,
        out_shape=jax.ShapeDtypeStruct((n, D_MODEL), F32),
        compiler_params=_params(("parallel",)),
        name="ffn",
    )(x, mod_all, mod_all, mod_all, g1, g2, w_in, w_out)


_STACKED_OUTS = (
    ("ka", 1, W_A), ("va", 1, W_A), ("kc", N_HEADS, HEAD_PAD), ("vc", N_HEADS, HEAD_PAD), ("logf", 1, N_HEADS),
)
_LAYER_OUTS = (
    ("qa_p", W_PAD, BF16), ("ka_p", W_PAD, BF16), ("va_p", W_PAD, BF16),
    ("qc1_p", W_PAD, BF16), ("qc2_p", W_PAD, BF16), ("kc_b", W_C, BF16), ("vc_b", W_C, BF16),
    ("qb", LANES, F32), ("kb", LANES, F32), ("vb", W_B, F32), ("loga", LANES, F32), ("rb", W_B, F32),
)
_NORM_ROWS = ("qa_p", "ka_p", "qc1_p", "qc2_p", "kc_b")


def _proj_kernel(x_ref, shift_ref, scale_ref, g1_ref, w_ref, wkvt_ref, bf_ref, wup_ref, bup_ref, vone_ref, *refs,
                 n_alias, with_qa, transposed_a):
    refs = refs[n_alias:]
    names = [n for n, _, _ in _STACKED_OUTS] + [n for n, _, _ in _LAYER_OUTS] + (["qa"] if with_qa else [])
    o = dict(zip(names + ["norms"], refs))
    tm = x_ref.shape[0]
    x = x_ref[...]
    hb = (_rms(x, g1_ref[...]) * (1.0 + scale_ref[0]) + shift_ref[0]).astype(BF16)

    def seg(name):
        lo, hi = _SEG[name]
        return _dot(hb, w_ref[:, lo:hi])

    def store_heads(ref, val):
        for d in range(ref.shape[0]):
            for h in range(N_HEADS):
                ref[d, pl.ds(h, tm, stride=N_HEADS), :] = val[:, h * HEAD_PAD:(h + 1) * HEAD_PAD]

    def store_slots(ref, val):
        for d in range(ref.shape[0]):
            ref[d] = val

    if transposed_a:
        kvt = _nt_dot(wkvt_ref[...], hb)
        store_slots(o["ka"], kvt[None, :W_A])
        store_slots(o["va"], kvt[None, W_A:])
    else:
        store_slots(o["ka"], seg("ka"))
        store_slots(o["va"], seg("va"))
    kc = seg("kc")
    store_heads(o["kc"], kc)
    rounded = {"kc_b": kc.astype(BF16)}
    o["kc_b"][...] = rounded["kc_b"]
    vc = seg("vc")
    store_heads(o["vc"], vc)
    o["vc_b"][...] = vc.astype(BF16)
    if with_qa:
        o["qa"][...] = seg("qa") * (HD_A ** -0.5)
    rounded["qa_p"] = (seg("qa_p") * (HD_A ** -0.5 * LOG2E)).astype(BF16)
    rounded["ka_p"] = seg("ka_p").astype(BF16)
    o["qa_p"][...] = rounded["qa_p"]
    o["ka_p"][...] = rounded["ka_p"]
    o["va_p"][...] = (seg("va_p") + vone_ref[...]).astype(BF16)
    qc = seg("qc") * (HD_C ** -0.5 * LOG2E)
    first_map = lax.broadcasted_iota(jnp.int32, qc.shape, 1) % HEAD_PAD < HD_C
    rounded["qc1_p"] = jnp.where(first_map, qc, 0.0).astype(BF16)
    rounded["qc2_p"] = jnp.where(first_map, 0.0, qc).astype(BF16)
    o["qc1_p"][...] = rounded["qc1_p"]
    o["qc2_p"][...] = rounded["qc2_p"]
    fa = seg("fa") + bf_ref[...]
    store_slots(o["logf"], _log_sigmoid(fa)[:, :N_HEADS])
    o["qb"][...] = seg("qb")
    o["kb"][...] = seg("kb")
    o["vb"][...] = seg("vb")
    o["rb"][...] = seg("rb")
    gb = seg("gb").astype(BF16)
    o["loga"][...] = _log_sigmoid(_dot(gb, wup_ref[...]) + bup_ref[...]) * (1.0 / GLA_TAU)
    lane = lax.broadcasted_iota(jnp.int32, (8, LANES), 1)
    tile_norms = jnp.zeros((8, LANES), F32)
    for a, name in enumerate(_NORM_ROWS):
        sq = rounded[name].astype(F32)
        sq = sq * sq
        for h in range(N_HEADS):
            ss = jnp.sum(sq[:, h * HEAD_PAD:(h + 1) * HEAD_PAD], axis=1, keepdims=True)
            tile_norms = jnp.where(lane == N_HEADS * a + h, jnp.max(ss, axis=0, keepdims=True), tile_norms)
    o["norms"][0] = tile_norms


def _proj_call(x, mod, g1, w_ext, w_kvt, bf, wup, bup, vone, tm, tiles_per_mod, layer, depth, stacked, with_qa,
               seq_len):
    n = x.shape[0]
    mod_all, sub = mod
    r = mod_all.shape[1]
    tok = lambda w: pl.BlockSpec((tm, w), lambda i: (i, 0))
    layer_outs = _LAYER_OUTS + ((("qa", W_A, F32),) if with_qa else ())
    alias_in = [] if stacked is None else [stacked[nm] for nm, _, _ in _STACKED_OUTS]
    transposed_a = seq_len % tm == 0
    tps = max(seq_len // tm, 1)
    stacked_specs, stacked_shapes = [], []
    slots = depth if stacked is None else 1
    for nm, rows, w in _STACKED_OUTS:
        if transposed_a and nm in ("ka", "va"):
            stacked_specs.append(pl.BlockSpec((slots, 1, w, tm), lambda i: (layer, i // tps, 0, i % tps)))
            stacked_shapes.append(jax.ShapeDtypeStruct((depth, n // seq_len, w, seq_len), F32))
        else:
            stacked_specs.append(pl.BlockSpec((slots, tm * rows, w), lambda i: (layer, i, 0)))
            stacked_shapes.append(jax.ShapeDtypeStruct((depth, n * rows, w), F32))
    n_fixed = 10
    outs = pl.pallas_call(
        functools.partial(_proj_kernel, n_alias=len(alias_in), with_qa=with_qa, transposed_a=transposed_a),
        grid=(n // tm,),
        in_specs=[tok(D_MODEL), _mod_spec(r, tiles_per_mod, 3 * sub), _mod_spec(r, tiles_per_mod, 3 * sub + 1),
                  _const_spec((1, D_MODEL)), _const_spec((D_MODEL, N_EXT)), _const_spec((2 * W_A, D_MODEL)),
                  _const_spec((1, LANES)), _const_spec((LANES, LANES)), _const_spec((1, LANES)),
                  _const_spec((1, W_PAD))]
        + [pl.BlockSpec(memory_space=pl.ANY)] * len(alias_in),
        out_specs=stacked_specs + [tok(w) for _, w, _ in layer_outs]
        + [pl.BlockSpec((1, 8, LANES), lambda i: (i, 0, 0))],
        out_shape=stacked_shapes + [jax.ShapeDtypeStruct((n, w), dt) for _, w, dt in layer_outs]
        + [jax.ShapeDtypeStruct((n // tm, 8, LANES), F32)],
        input_output_aliases={n_fixed + k: k for k in range(len(alias_in))},
        compiler_params=_params(("parallel",)),
        name="proj",
    )(x, mod_all, mod_all, g1, w_ext, w_kvt, bf, wup, bup, vone, *alias_in)
    names = [nm for nm, _, _ in _STACKED_OUTS] + [nm for nm, _, _ in layer_outs] + ["norms"]
    return dict(zip(names, outs))


def _cumsum_kernel(x_ref, o_ref):
    x = x_ref[0]
    nb = x.shape[0]
    r = lax.broadcasted_iota(jnp.int32, (LANES, LANES), 0)
    c = lax.broadcasted_iota(jnp.int32, (LANES, LANES), 1)
    local = _dot_exact_rhs(x, (r <= c).astype(F32))
    tot = jnp.broadcast_to(local[:, LANES - 1:LANES], (nb, LANES))
    rr = lax.broadcasted_iota(jnp.int32, (nb, nb), 0)
    cc = lax.broadcasted_iota(jnp.int32, (nb, nb), 1)
    o_ref[0] = local + _dot_exact_lhs((cc < rr).astype(F32), tot)


def _cumsum_call(x):
    rows, nb, _ = x.shape
    spec = pl.BlockSpec((1, nb, LANES), lambda i: (i, 0, 0))
    return pl.pallas_call(
        _cumsum_kernel, grid=(rows,), in_specs=[spec], out_specs=spec,
        out_shape=jax.ShapeDtypeStruct(x.shape, F32),
        compiler_params=_params(("parallel",)), name="cumsum",
    )(x)


def _lane_tile(x, width):
    return jnp.concatenate([x] * (width // LANES), axis=1)


GROUP_HEADS = 2
W_GROUP = GROUP_HEADS * HEAD_PAD


def _causal_sweep(i, first, tq, tk, tiles_per_iter, tile):
    r = tq // tk

    def body(jj, carry):
        for u in range(tiles_per_iter):
            tile(jj * tiles_per_iter + u, slice(0, tq), None)
        return carry

    assert r % tiles_per_iter == 0 and tiles_per_iter in (1, 2)
    lead = 0
    if tiles_per_iter == 2:
        lead = lax.rem(i * r - first, 2)

        @pl.when(lead == 1)
        def _():
            tile(first, slice(0, tq), None)

    lax.fori_loop(lax.div(first + lead, tiles_per_iter), lax.div(i * r, tiles_per_iter), body, 0)
    for d in range(r):
        tile(i * r + d, slice(d * tk, (d + 1) * tk), d * tk)
        if d + 1 < r:
            tile(i * r + d, slice((d + 1) * tk, tq), None)


def _fox_kernel(first_ref, q_ref, k_ref, v_ref, c_ref, o_ref, m_sc, acc_sc, *, tk, tiles_per_iter):
    i = pl.program_id(2)
    first = first_ref[pl.program_id(0), pl.program_id(1), i]
    tq = q_ref.shape[1]
    m_sc[...] = jnp.full(m_sc.shape, NEG, F32)
    acc_sc[...] = jnp.zeros(acc_sc.shape, F32)
    row = lax.broadcasted_iota(jnp.int32, (tk, tk), 0)
    col = lax.broadcasted_iota(jnp.int32, (tk, tk), 1)
    causal = col <= row

    def tile(j, rows, local):
        ks = pl.ds(pl.multiple_of(j * tk, tk), tk)
        for h in range(GROUP_HEADS):
            hs = slice(h * HEAD_PAD, (h + 1) * HEAD_PAD)
            c_first = c_ref[0, h, pl.ds(i * (tq // tk), 1), :][:, 0:1]
            t = _nt_dot(q_ref[0, rows, hs], k_ref[0, ks, hs]) + (c_first - c_ref[0, h, pl.ds(j, 1), :]) * LOG2E
            if local is not None:
                t = jnp.where(causal, t, NEG)
            m_old = m_sc[h, rows, :]
            m_new = jnp.maximum(m_old, jnp.max(t, axis=1, keepdims=True))
            p = jnp.exp2(t - _lane_tile(m_new, tk)).astype(BF16)
            acc_sc[h, rows, :] = jnp.exp2(m_old - m_new) * acc_sc[h, rows, :] + _dot(p, v_ref[0, ks, hs])
            m_sc[h, rows, :] = m_new

    _causal_sweep(i, first, tq, tk, tiles_per_iter, tile)
    for h in range(GROUP_HEADS):
        acc = acc_sc[h]
        o_ref[0, :, h * HEAD_PAD:(h + 1) * HEAD_PAD] = (acc / acc[:, HD_A:HD_A + 1]).astype(BF16)


def _resident_spec(t, width):
    return pl.BlockSpec((1, t, width), lambda b_, g, i, first: (b_, 0, g), pipeline_mode=pl.Buffered(1))


def _fox_call(first, qa_p, ka_p, va_p, c, tq, tk):
    b, t, _ = qa_p.shape
    qspec = pl.BlockSpec((1, tq, W_GROUP), lambda b_, g, i, first: (b_, i, g))
    grid_spec = pltpu.PrefetchScalarGridSpec(
        num_scalar_prefetch=1,
        grid=(b, N_HEADS // GROUP_HEADS, t // tq),
        in_specs=[qspec, _resident_spec(t, W_GROUP), _resident_spec(t, W_GROUP),
                  pl.BlockSpec((1, GROUP_HEADS, t // tk, tk), lambda b_, g, i, first: (b_, g, 0, 0))],
        out_specs=qspec,
        scratch_shapes=[pltpu.VMEM((GROUP_HEADS, tq, LANES), F32), pltpu.VMEM((GROUP_HEADS, tq, HEAD_PAD), F32)])
    return pl.pallas_call(
        functools.partial(_fox_kernel, tk=tk, tiles_per_iter=1),
        grid_spec=grid_spec,
        out_shape=jax.ShapeDtypeStruct((b, t, W_PAD), BF16),
        compiler_params=_params(("parallel", "parallel", "arbitrary")),
        name="fox",
    )(first, qa_p, ka_p, va_p, c.reshape(b, N_HEADS, t // tk, tk))


def _diff_lambda(lp, lam_init):
    a = jnp.sum(lp[0:1] * lp[1:2], axis=1, keepdims=True)
    b = jnp.sum(lp[2:3] * lp[3:4], axis=1, keepdims=True)
    return jnp.exp(a) - jnp.exp(b) + lam_init


def _diag_distance(tk):
    q = np.arange(tk)[:, None]
    k = np.arange(tk)[None, :]
    return np.where(k // CHUNK <= q // CHUNK, 2 * np.minimum(q, k) - k, HIDDEN).astype(np.float32)


DIFF_GROUP = 1
W_DIFF_GROUP = DIFF_GROUP * HEAD_PAD


def _diff_kernel(first_ref, q1_ref, q2_ref, k_ref, v_ref, slope_ref, lam_ref, g_ref, rel_ref, o_ref,
                 m_sc, l_sc, a_sc, *, tk, tiles_per_iter, lam_init):
    g = pl.program_id(1)
    i = pl.program_id(2)
    first = first_ref[pl.program_id(0), g, i]
    tq = q1_ref.shape[1]
    m_sc[...] = jnp.full(m_sc.shape, NEG, F32)
    l_sc[...] = jnp.zeros(l_sc.shape, F32)
    a_sc[...] = jnp.zeros(a_sc.shape, F32)
    lane = lax.broadcasted_iota(jnp.int32, (1, tk), 1)

    def tile(j, rows, local):
        ks = pl.ds(pl.multiple_of(j * tk, tk), tk)
        if local is None:
            rel = (lane + (j * tk - i * tq)).astype(F32)
        else:
            rel = rel_ref[...] + float(local)
        for h in range(DIFF_GROUP):
            hs = slice(h * HEAD_PAD, (h + 1) * HEAD_PAD)
            bias = (slope_ref[g * DIFF_GROUP + h][:, 0:1] * LOG2E) * rel
            k = k_ref[0, ks, hs]
            v = v_ref[0, ks, hs]
            nrows = rows.stop - rows.start
            scores = _nt_dot(jnp.concatenate([q1_ref[0, rows, hs], q2_ref[0, rows, hs]], axis=0), k)
            probs, alphas = [], []
            for m in range(2):
                n = 2 * h + m
                t = scores[m * nrows:(m + 1) * nrows] + bias
                m_old = m_sc[n, rows, :]
                m_new = jnp.maximum(m_old, jnp.max(t, axis=1, keepdims=True))
                alpha = jnp.exp2(m_old - m_new)
                p = jnp.exp2(t - _lane_tile(m_new, tk))
                l_sc[n, rows, :] = alpha * l_sc[n, rows, :] + jnp.sum(p, axis=1, keepdims=True)
                m_sc[n, rows, :] = m_new
                probs.append(p.astype(BF16))
                alphas.append(alpha)
            pv = _dot(jnp.concatenate(probs, axis=0), v)
            for m in range(2):
                n = 2 * h + m
                a_sc[n, rows, :] = alphas[m] * a_sc[n, rows, :] + pv[m * nrows:(m + 1) * nrows]

    _causal_sweep(i, first, tq, tk, tiles_per_iter, tile)
    lam = _diff_lambda(lam_ref[...], lam_init)
    for h in range(DIFF_GROUP):
        out = a_sc[2 * h] / l_sc[2 * h] - lam * (a_sc[2 * h + 1] / l_sc[2 * h + 1])
        o_ref[0, :, h * HEAD_PAD:(h + 1) * HEAD_PAD] = (_rms(out, g_ref[...]) * (1.0 - lam_init)).astype(BF16)


def _diff_call(first, qc1_p, qc2_p, kc_b, vc_b, slopes, lam_p, g_diff, lam_init, tq, tk):
    assert CHUNK == 64 and tk % CHUNK == 0
    b, t, _ = qc1_p.shape
    qspec = pl.BlockSpec((1, tq, W_DIFF_GROUP), lambda b_, g, i, first: (b_, i, g))
    stat = pltpu.VMEM((2 * DIFF_GROUP, tq, LANES), F32)
    grid_spec = pltpu.PrefetchScalarGridSpec(
        num_scalar_prefetch=1,
        grid=(b, N_HEADS // DIFF_GROUP, t // tq),
        in_specs=[qspec, qspec, _resident_spec(t, W_DIFF_GROUP), _resident_spec(t, W_DIFF_GROUP),
                  pl.BlockSpec((N_HEADS, 1, LANES), lambda b_, g, i, first: (0, 0, 0)),
                  pl.BlockSpec((4, HD_C), lambda b_, g, i, first: (0, 0)),
                  pl.BlockSpec((1, HEAD_PAD), lambda b_, g, i, first: (0, 0)),
                  pl.BlockSpec((tk, tk), lambda b_, g, i, first: (0, 0), pipeline_mode=pl.Buffered(1))],
        out_specs=qspec,
        scratch_shapes=[stat, stat, pltpu.VMEM((2 * DIFF_GROUP, tq, HEAD_PAD), F32)])
    return pl.pallas_call(
        functools.partial(_diff_kernel, tk=tk, tiles_per_iter=2, lam_init=lam_init),
        grid_spec=grid_spec,
        out_shape=jax.ShapeDtypeStruct((b, t, W_C), BF16),
        compiler_params=_params(("parallel", "parallel", "arbitrary")),
        name="diff",
    )(first, qc1_p, qc2_p, kc_b, vc_b, slopes, lam_p, g_diff, jnp.asarray(_diag_distance(tk)))


def _gla_kernel(q_ref, k_ref, v_ref, la_ref, r_ref, s0_ref, g_ref, o_ref, s_ref, *, chunk):
    ti = pl.program_id(1)
    n_chunks = q_ref.shape[1] // chunk
    n_seq = q_ref.shape[0]

    @pl.when(ti == 0)
    def _():
        s_ref[...] = s0_ref[...]

    tri_r = lax.broadcasted_iota(jnp.int32, (chunk, chunk), 0)
    tri_c = lax.broadcasted_iota(jnp.int32, (chunk, chunk), 1)
    incl = (tri_c <= tri_r).astype(F32)
    srow = lax.broadcasted_iota(jnp.int32, (N_HEADS * chunk, LANES), 0) // chunk
    slane = lax.broadcasted_iota(jnp.int32, (N_HEADS * chunk, LANES), 1) // DK_B
    qmask = srow == slane
    arow = lax.broadcasted_iota(jnp.int32, (N_HEADS * chunk, chunk), 0) % chunk
    acol = lax.broadcasted_iota(jnp.int32, (N_HEADS * chunk, chunk), 1)
    causal = acol <= arow
    vlane = lax.broadcasted_iota(jnp.int32, (chunk, W_B), 1) // DV_B
    st_row = lax.broadcasted_iota(jnp.int32, (W_B, LANES), 0) // DV_B
    st_lane = lax.broadcasted_iota(jnp.int32, (W_B, LANES), 1) // DK_B
    diag_blocks = st_row == st_lane
    gr = lax.broadcasted_iota(jnp.int32, (W_B, W_B), 0) // DV_B
    gc = lax.broadcasted_iota(jnp.int32, (W_B, W_B), 1) // DV_B
    group_mean = jnp.where(gr == gc, 1.0 / DV_B, 0.0)

    for c, sq in [(c, sq) for c in range(n_chunks) for sq in range(n_seq)]:
        sl = slice(c * chunk, (c + 1) * chunk)
        cb = _dot_exact_lhs(incl, la_ref[sq, sl, :])
        cb_last = cb[chunk - 1:chunk, :]
        q = q_ref[sq, sl, :] * (DK_B ** -0.5)
        k = k_ref[sq, sl, :]
        v = v_ref[sq, sl, :]
        vb = v.astype(BF16)
        qe = q * jnp.exp(cb)
        ke = (k * jnp.exp(-cb)).astype(BF16)
        kl = (k * jnp.exp(cb_last - cb)).astype(BF16)
        decay = jnp.exp(cb_last)
        qstack = jnp.where(qmask, jnp.concatenate([qe] * N_HEADS, axis=0), 0.0).astype(BF16)
        a = jnp.where(causal, _nt_dot(qstack, ke), 0.0).astype(BF16)
        oi = _dot(a, vb)
        o = jnp.zeros((chunk, W_B), F32)
        for h in range(N_HEADS):
            o = o + jnp.where(vlane == h, oi[h * chunk:(h + 1) * chunk], 0.0)
        st = s_ref[sq]
        o = o + _nt_dot(qe.astype(BF16), st.astype(BF16))
        upd = _dot(v.T.astype(BF16), kl)
        s_ref[sq] = decay * st + jnp.where(diag_blocks, upd, 0.0)
        ms = _dot_exact_rhs(o * o, group_mean)
        r = r_ref[sq, sl, :]
        o_ref[sq, sl, :] = (o * lax.rsqrt(ms + EPS) * g_ref[...] * (r * jax.nn.sigmoid(r))).astype(BF16)


def _gla_call(qb, kb, vb, loga, rb, s0_t, g_gla, tg, chunk, group):
    b, t, _ = qb.shape
    tok = lambda w: pl.BlockSpec((group, tg, w), lambda b_, i: (b_, i, 0))
    sspec = pl.BlockSpec((group, W_B, LANES), lambda b_, i: (b_, 0, 0))
    return pl.pallas_call(
        functools.partial(_gla_kernel, chunk=chunk),
        grid=(b // group, t // tg),
        in_specs=[tok(LANES), tok(LANES), tok(W_B), tok(LANES), tok(W_B), sspec,
                  pl.BlockSpec((1, W_B), lambda b_, i: (0, 0))],
        out_specs=[tok(W_B), sspec],
        out_shape=[jax.ShapeDtypeStruct((b, t, W_B), BF16), jax.ShapeDtypeStruct((b, W_B, LANES), F32)],
        compiler_params=_params(("parallel", "arbitrary")),
        name="gla",
    )(qb, kb, vb, loga, rb, s0_t, g_gla)


def _softmax_two(t_past, t_new, exp_fn):
    m = jnp.maximum(jnp.max(t_past, axis=1, keepdims=True), jnp.max(t_new, axis=1, keepdims=True))
    p_past = exp_fn(t_past - m)
    p_new = exp_fn(t_new - m)
    l = jnp.sum(p_past, axis=1, keepdims=True) + jnp.sum(p_new, axis=1, keepdims=True)
    return p_past, p_new, l


def _fox_s_kernel(q_ref, kn_ref, vn_ref, kpt_ref, vpt_ref, c_ref, o_ref):
    t, past = q_ref.shape[1], kpt_ref.shape[1]
    q = q_ref[0]
    kn = kn_ref[...].astype(BF16)
    vn = vn_ref[...].astype(BF16)
    kpt = kpt_ref[...].astype(BF16)
    vpt = vpt_ref[...].astype(BF16)
    lane_head = lax.broadcasted_iota(jnp.int32, (t, W_A), 1) // HD_A
    row = lax.broadcasted_iota(jnp.int32, (t, t), 0)
    col = lax.broadcasted_iota(jnp.int32, (t, t), 1)
    o = jnp.zeros((t, W_A), F32)
    for h in range(N_HEADS):
        qh = jnp.where(lane_head == h, q, 0.0).astype(BF16)
        ch = c_ref[0, h:h + 1, :]
        cref = ch[:, past:past + 1]
        t_past = _dot(qh, kpt) + (cref - ch[:, :past])
        t_new = jnp.where(col <= row, _nt_dot(qh, kn) + (cref - ch[:, past:past + t]), NEG)
        p_past, p_new, l = _softmax_two(t_past, t_new, jnp.exp)
        oh = (_nt_dot(p_past.astype(BF16), vpt) + _dot(p_new.astype(BF16), vn)) / l
        o = o + jnp.where(lane_head == h, oh, 0.0)
    o_ref[0] = o.astype(BF16)


def _fox_s_call(layer, qa, ka, va, cache_kt, cache_vt, c):
    b, t, _ = qa.shape
    past = cache_kt.shape[3]
    new = pl.BlockSpec((1, t, W_A), lambda i: (i, 0, 0))
    new_l = lambda: pl.BlockSpec((None, t, W_A), lambda i: (layer, i, 0))
    old_l = lambda: pl.BlockSpec((None, None, W_A, past), lambda i: (layer, i, 0, 0))
    return pl.pallas_call(
        _fox_s_kernel,
        grid=(b,),
        in_specs=[new, new_l(), new_l(), old_l(), old_l(),
                  pl.BlockSpec((1, N_HEADS, c.shape[2]), lambda i: (i, 0, 0))],
        out_specs=new,
        out_shape=jax.ShapeDtypeStruct((b, t, W_A), BF16),
        compiler_params=_params(("parallel",)),
        name="fox_s",
    )(qa, ka, va, cache_kt, cache_vt, c)


def _diff_s_kernel(q1_ref, q2_ref, kn_ref, vn_ref, kp_ref, vp_ref, slope_ref, lam_ref, g_ref, o_ref, *, lam_init):
    t, past = q1_ref.shape[1], kp_ref.shape[0] // N_HEADS
    lam = _diff_lambda(lam_ref[...], lam_init)
    row = lax.broadcasted_iota(jnp.int32, (t, t), 0) + past
    col = lax.broadcasted_iota(jnp.int32, (t, t), 1) + past
    visible_new = lax.shift_right_logical(col, 6) <= lax.shift_right_logical(row, 6)
    dist_new = jnp.abs(row - col).astype(F32)
    prow = lax.broadcasted_iota(jnp.int32, (t, past), 0) + past
    pcol = lax.broadcasted_iota(jnp.int32, (t, past), 1)
    visible_past = lax.shift_right_logical(pcol, 6) <= lax.shift_right_logical(prow, 6)
    dist_past = (prow - pcol).astype(F32)
    for h in range(N_HEADS):
        hs = slice(h * HEAD_PAD, (h + 1) * HEAD_PAD)
        slope = slope_ref[h][:, 0:1] * LOG2E
        kn = kn_ref[pl.ds(h, t, stride=N_HEADS), :].astype(BF16)
        vn = vn_ref[pl.ds(h, t, stride=N_HEADS), :].astype(BF16)
        kp = kp_ref[pl.ds(h, past, stride=N_HEADS), :].astype(BF16)
        vp = vp_ref[pl.ds(h, past, stride=N_HEADS), :].astype(BF16)
        outs = []
        for q_ref in (q1_ref, q2_ref):
            q = q_ref[0, :, hs]
            t_past = jnp.where(visible_past, _nt_dot(q, kp) - slope * dist_past, NEG)
            t_new = jnp.where(visible_new, _nt_dot(q, kn) - slope * dist_new, NEG)
            p_past, p_new, l = _softmax_two(t_past, t_new, jnp.exp2)
            outs.append((_dot(p_past.astype(BF16), vp) + _dot(p_new.astype(BF16), vn)) / l)
        out = outs[0] - lam * outs[1]
        o_ref[0, :, hs] = (_rms(out, g_ref[...]) * (1.0 - lam_init)).astype(BF16)


def _diff_s_call(layer, qc1_p, qc2_p, kc, vc, cache_k, cache_v, slopes, lam_p, g_diff, lam_init):
    assert CHUNK == 64
    b, t, _ = qc1_p.shape
    rows_past = cache_k.shape[2]
    new = lambda: pl.BlockSpec((1, t, W_C), lambda i: (i, 0, 0))
    new_l = lambda: pl.BlockSpec((None, t * N_HEADS, HEAD_PAD), lambda i: (layer, i, 0))
    old_l = lambda: pl.BlockSpec((None, None, rows_past, HEAD_PAD), lambda i: (layer, i, 0, 0))
    return pl.pallas_call(
        functools.partial(_diff_s_kernel, lam_init=lam_init),
        grid=(b,),
        in_specs=[new(), new(), new_l(), new_l(), old_l(), old_l(),
                  pl.BlockSpec((N_HEADS, 1, LANES), lambda i: (0, 0, 0)),
                  pl.BlockSpec((4, HD_C), lambda i: (0, 0)),
                  pl.BlockSpec((1, HEAD_PAD), lambda i: (0, 0))],
        out_specs=new(),
        out_shape=jax.ShapeDtypeStruct((b, t, W_C), BF16),
        compiler_params=_params(("parallel",)),
        name="diff_s",
    )(qc1_p, qc2_p, kc, vc, cache_k, cache_v, slopes, lam_p, g_diff)


def _out_kernel(x_ref, oa_ref, gla_ref, dif_ref, gate_ref, g2_ref, woa_ref, wog_ref, wod_ref, o_ref):
    y = _dot(oa_ref[...], woa_ref[...]) + _dot(gla_ref[...], wog_ref[...]) + _dot(dif_ref[...], wod_ref[...])
    o_ref[...] = x_ref[...] + gate_ref[0] * _rms(y, g2_ref[...])


def _out_call(x, oa, gla, dif, mod, g2, w_oa, w_og, w_od, tm, tiles_per_mod):
    n = x.shape[0]
    mod_all, sub = mod
    r = mod_all.shape[1]
    tok = lambda w: pl.BlockSpec((tm, w), lambda i: (i, 0))
    return pl.pallas_call(
        _out_kernel,
        grid=(n // tm,),
        in_specs=[tok(D_MODEL), tok(oa.shape[1]), tok(W_B), tok(W_C), _mod_spec(r, tiles_per_mod, 3 * sub + 2),
                  _const_spec((1, D_MODEL)), _const_spec(w_oa.shape), _const_spec(w_og.shape),
                  _const_spec(w_od.shape)],
        out_specs=tok(D_MODEL),
        out_shape=jax.ShapeDtypeStruct((n, D_MODEL), F32),
        compiler_params=_params(("parallel",)),
        name="out_proj",
    )(x, oa, gla, dif, mod_all, g2, w_oa, w_og, w_od)


def _pad_heads(w, hd, off=0):
    rows = w.shape[0]
    zeros = lambda width: [jnp.zeros((rows, width), w.dtype)] if width else []
    pieces = []
    for h in range(N_HEADS):
        pieces += zeros(off) + [w[:, h * hd:(h + 1) * hd]] + zeros(HEAD_PAD - off - hd)
    return jnp.concatenate(pieces, axis=1)


def _pad_lanes(w, width=LANES):
    return jnp.pad(w, ((0, 0), (0, width - w.shape[1])))


def _layer_params(l, w_in, b_f, w_gla_up, b_gla_up, g_gla, g_diff, w_out):
    offs = [int(o) for o in np.cumsum(PROJ_SIZES)[:-1]]
    qa, ka, va, fa, qb, kb, vb, gb, rb, qc, kc, vc = jnp.split(w_in[l], offs, axis=1)
    segs = {
        "ka": ka, "va": va, "kc": kc, "vc": vc, "qa": qa, "qc": qc,
        "qa_p": _pad_heads(qa, HD_A), "ka_p": _pad_heads(ka, HD_A), "va_p": _pad_heads(va, HD_A),
        "fa": _pad_lanes(fa), "qb": qb, "kb": kb, "vb": vb, "rb": rb, "gb": _pad_lanes(gb),
    }
    w_ext = jnp.concatenate([segs[n] for n, _ in _SEG_WIDTHS], axis=1).astype(BF16)
    wo = w_out[l]
    vone = np.zeros((1, W_PAD), np.float32)
    vone[0, HD_A::HEAD_PAD] = 1.0
    return dict(
        w_ext=w_ext, w_kvt=jnp.concatenate([ka, va], axis=1).T.astype(BF16),
        bf=_pad_lanes(b_f[l][None, :]),
        wup=jnp.pad(w_gla_up[l], ((0, LANES - GLA_RANK), (0, 0))).astype(BF16),
        bup=b_gla_up[l][None, :],
        vone=jnp.asarray(vone),
        g_gla=jnp.tile(g_gla[l], N_HEADS)[None, :], g_diff=g_diff[l][None, :],
        w_oa=wo[:W_A].astype(BF16),
        w_oa_p=_pad_heads(wo[:W_A].T, HD_A).T.astype(BF16),
        w_og=wo[W_A:W_A + W_B].astype(BF16), w_od=wo[W_A + W_B:].astype(BF16),
    )


def _state_to_blockdiag_t(s):
    eye = jnp.eye(N_HEADS, dtype=s.dtype)
    return jnp.einsum("bhkv,hg->bhvgk", s, eye).reshape(s.shape[0], W_B, N_HEADS * DK_B)


def _blockdiag_t_to_state(st):
    b = st.shape[0]
    s5 = st.reshape(b, N_HEADS, DV_B, N_HEADS, DK_B)
    diag = jnp.stack([s5[:, h, :, h, :] for h in range(N_HEADS)], axis=1)
    return jnp.swapaxes(diag, 2, 3)


SKIP_LOG2 = 160.0
NORM_SLACK = 1.01


def _tile_max(x, factor):
    return jnp.max(x.reshape(x.shape[:-1] + (x.shape[-1] // factor, factor)), axis=-1)


def _first_tiles(qn, kn, gap, tq, tk, group):
    b, h, nq = qn.shape
    r = tq // tk
    kn_diag = _tile_max(kn, r)
    bound = NORM_SLACK * (qn[..., None] * kn[:, :, None, :] + (qn * kn_diag)[..., None]) + gap
    dead = (bound < -SKIP_LOG2).reshape(b, h // group, group, nq, kn.shape[-1]).all(axis=2)
    lead = jnp.sum(jnp.cumprod(dead.astype(jnp.int32), axis=-1), axis=-1)
    return jnp.minimum(lead, jnp.arange(nq, dtype=jnp.int32) * r).astype(jnp.int32)


def _tile(n, pref):
    t = min(n, pref)
    assert n % t == 0, (n, t)
    return t


def _trunk(x, mods, params, ffn_w, g_norm, lam_params, mixer, tm, tiles_per_mod, with_qa):
    b, t, _ = x.shape
    n = b * t
    depth = len(params)
    xf = x.reshape(n, D_MODEL)
    stacked = None
    gla_states = []
    for l in range(depth):
        p = params[l]
        gn = lambda k: g_norm[l, k][None, :]
        xf = _ffn_call(xf, (mods[l], 0), gn(0), gn(1), *ffn_w, l, 0, 0.5, tm, tiles_per_mod)
        pr = _proj_call(xf, (mods[l], 1), gn(2), p["w_ext"], p["w_kvt"], p["bf"], p["wup"], p["bup"], p["vone"], tm,
                        tiles_per_mod, l, depth, stacked, with_qa, t)
        stacked = {nm: pr[nm] for nm, _, _ in _STACKED_OUTS}
        lam_init = 0.8 - 0.6 * math.exp(-0.3 * l)
        oa, gla, dif, s_fin, w_oa = mixer(l, pr, p, lam_params[l], lam_init, b, t)
        gla_states.append(s_fin)
        xf = _out_call(xf, oa, gla, dif, (mods[l], 1), gn(3), w_oa, p["w_og"], p["w_od"], tm, tiles_per_mod)
        xf = _ffn_call(xf, (mods[l], 2), gn(4), gn(5), *ffn_w, l, 1, 0.5, tm, tiles_per_mod)
    shp = lambda nm, *tail: stacked[nm].reshape((depth, b, t) + tail)
    if stacked["ka"].ndim == 4:
        shp_a = lambda nm: jnp.transpose(stacked[nm].reshape(depth, b, N_HEADS, HD_A, t), (0, 1, 4, 2, 3))
    else:
        shp_a = lambda nm: shp(nm, N_HEADS, HD_A)
    states = [shp_a("ka"), shp_a("va"), shp("logf", N_HEADS), jnp.stack(gla_states),
              shp("kc", N_HEADS, 2 * HD_C), shp("vc", N_HEADS, 2 * HD_C)]
    return xf.reshape(b, t, D_MODEL), states


def kernel(x_prompt, x_sample, c_prompt, c_sample, cache_fox_k, cache_fox_v, cache_fox_logf, state_gla,
           cache_diff_k, cache_diff_v, w_ada, b_ada, g_norm, w_ffn_in, w_ffn_out, w_in, b_f, w_gla_up,
           b_gla_up, g_gla, g_diff, lam_params, w_out):
    depth = w_in.shape[0]
    bp, tp, _ = x_prompt.shape
    bs, ts, _ = x_sample.shape
    past = cache_fox_k.shape[2]

    params = [_layer_params(l, w_in, b_f, w_gla_up, b_gla_up, g_gla, g_diff, w_out) for l in range(depth)]
    ffn_w = (w_ffn_in.astype(BF16), w_ffn_out.astype(BF16))

    slopes = 2.0 ** (-8.0 * jnp.arange(1, N_HEADS + 1, dtype=F32) / N_HEADS)
    slopes = jnp.broadcast_to(slopes[:, None, None], (N_HEADS, 1, LANES))

    rows = bp + bs
    rows_pad = -(-rows // 8) * 8
    c_all = jnp.pad(jnp.concatenate([c_prompt, c_sample], axis=0), ((0, rows_pad - rows), (0, 0)))
    mods_p, mods_s = [], []
    for l in range(depth):
        mod = _ada_call(c_all, w_ada[l].astype(BF16), b_ada[l][None, :])
        mods_p.append(mod[:bp, None, :])
        mods_s.append(jnp.repeat(mod[bp:rows], ts, axis=0)[None])

    tm_p = _tile(tp, 512)
    tq = _tile(tp, 1024)
    tk = _tile(tp, 512)
    tg = _tile(tp, 512)

    def prompt_mixer(l, pr, p, lam_p, lam_init, b, t):
        r3 = lambda a: a.reshape(b, t, a.shape[-1])
        logf_t = jnp.swapaxes(r3(pr["logf"][l]), 1, 2).reshape(b * N_HEADS, t // LANES, LANES)
        c = _cumsum_call(logf_t).reshape(b, N_HEADS, t)
        nrm = jnp.sqrt(pr["norms"][:, 0, :N_HEADS * len(_NORM_ROWS)])
        nrm = dict(zip(_NORM_ROWS, jnp.moveaxis(nrm.reshape(b, t // tm_p, len(_NORM_ROWS), N_HEADS), (2, 3), (0, 2))))
        q_tiles = lambda a: _tile_max(a, tq // tm_p)
        k_tiles = lambda a: _tile_max(a, tk // tm_p)
        gap_a = (c[:, :, ::tq][..., None] - c[:, :, tk - 1::tk][:, :, None, :]) * LOG2E
        first_a = _first_tiles(q_tiles(nrm["qa_p"]), k_tiles(nrm["ka_p"]), gap_a, tq, tk, GROUP_HEADS)
        oa = _fox_call(first_a, r3(pr["qa_p"]), r3(pr["ka_p"]), r3(pr["va_p"]), c, tq, tk)
        s0_t = jnp.zeros((b, W_B, N_HEADS * DK_B), F32)
        gla, s_t = _gla_call(r3(pr["qb"]), r3(pr["kb"]), r3(pr["vb"]), r3(pr["loga"]), r3(pr["rb"]), s0_t,
                             p["g_gla"], tg, min(CHUNK, t), math.gcd(b, 2))
        last_key = jnp.arange(tk - 1, t, tk, dtype=F32)[None, :] - jnp.arange(0, t, tq, dtype=F32)[:, None]
        gap_c = slopes[:, 0, :1, None] * LOG2E * last_key[None]
        first_c = _first_tiles(q_tiles(jnp.maximum(nrm["qc1_p"], nrm["qc2_p"])), k_tiles(nrm["kc_b"]), gap_c, tq, tk,
                                 DIFF_GROUP)
        dif = _diff_call(first_c, r3(pr["qc1_p"]), r3(pr["qc2_p"]), r3(pr["kc_b"]), r3(pr["vc_b"]), slopes, lam_p,
                         p["g_diff"], lam_init, tq, tk)
        flat = lambda a: a.reshape(b * t, a.shape[-1])
        return flat(oa), flat(gla), flat(dif), _blockdiag_t_to_state(s_t), p["w_oa_p"]

    feature_time = lambda a: jnp.transpose(a, (0, 1, 3, 4, 2)).reshape(depth, bs, W_A, past)
    cache_kt, cache_vt = feature_time(cache_fox_k), feature_time(cache_fox_v)
    cache_dk = cache_diff_k.reshape(depth, bs, past * N_HEADS, HEAD_PAD)
    cache_dv = cache_diff_v.reshape(depth, bs, past * N_HEADS, HEAD_PAD)

    def sample_mixer(l, pr, p, lam_p, lam_init, b, t):
        r3 = lambda a: a.reshape(b, t, a.shape[-1])
        total = past + t
        nb = -(-total // (8 * LANES)) * 8
        lf = jnp.concatenate([cache_fox_logf[l].astype(F32), r3(pr["logf"][l])], axis=1)
        lf = jnp.pad(jnp.swapaxes(lf, 1, 2), ((0, 0), (0, 0), (0, nb * LANES - total)))
        c = _cumsum_call(lf.reshape(b * N_HEADS, nb, LANES)).reshape(b, N_HEADS, nb * LANES)
        oa = _fox_s_call(l, r3(pr["qa"]), pr["ka"], pr["va"], cache_kt, cache_vt, c)
        gla, s_t = _gla_call(r3(pr["qb"]), r3(pr["kb"]), r3(pr["vb"]), r3(pr["loga"]), r3(pr["rb"]),
                             _state_to_blockdiag_t(state_gla[l].astype(F32)), p["g_gla"], t, min(CHUNK, t),
                             math.gcd(b, 8))
        dif = _diff_s_call(l, r3(pr["qc1_p"]), r3(pr["qc2_p"]), pr["kc"], pr["vc"], cache_dk, cache_dv,
                           slopes, lam_p, p["g_diff"], lam_init)
        flat = lambda a: a.reshape(b * t, a.shape[-1])
        return flat(oa), flat(gla), flat(dif), _blockdiag_t_to_state(s_t), p["w_oa"]

    y_p, sp = _trunk(x_prompt, mods_p, params, ffn_w, g_norm, lam_params, prompt_mixer, tm_p, tp // tm_p, False)
    y_s, ss = _trunk(x_sample, mods_s, params, ffn_w, g_norm, lam_params, sample_mixer, bs * ts, 1, True)
    return (y_p, y_s, sp[0], sp[1], sp[2], sp[3], sp[4], sp[5], ss[0], ss[1], ss[2], ss[3], ss[4], ss[5])
```

```python
import functools
import math

import jax
import jax.numpy as jnp
import numpy as np
from jax import lax
from jax.experimental import pallas as pl
from jax.experimental.pallas import tpu as pltpu

F32 = jnp.float32
BF16 = jnp.bfloat16

D_MODEL = 1024
N_HEADS = 4
HD_A = 64
DK_B = 32
DV_B = 64
GLA_RANK = 16
GLA_TAU = 16.0
HD_C = 64
CHUNK = 64
W_A = N_HEADS * HD_A
W_B = N_HEADS * DV_B
W_C = N_HEADS * 2 * HD_C
D_FF = ((8 * D_MODEL // 3 + 255) // 256) * 256
N_SUB = 3
PROJ_SIZES = (W_A, W_A, W_A, N_HEADS, N_HEADS * DK_B, N_HEADS * DK_B, W_B, GLA_RANK, W_B, W_C, W_C, W_C)
NEG = -1e30
HIDDEN = -1e34
EPS = 1e-6
LOG2E = math.log2(math.e)

LANES = 128
HEAD_PAD = 128
W_PAD = N_HEADS * HEAD_PAD
VMEM_LIMIT = 56 * 1024 * 1024

_SEG_WIDTHS = (
    ("ka", W_A), ("va", W_A), ("kc", W_C), ("vc", W_C), ("qa", W_A),
    ("qc", W_C),
    ("fa", LANES), ("qb", LANES), ("kb", LANES), ("vb", W_B), ("rb", W_B), ("gb", LANES),
)
_SEG = {}
_off = 0
for _name, _w in _SEG_WIDTHS:
    _SEG[_name] = (_off, _off + _w)
    _off += _w
N_EXT = _off


def _nt_dot(a, b):
    return lax.dot_general(a, b, (((1,), (1,)), ((), ())), preferred_element_type=F32)


def _dot(a, b):
    return jnp.dot(a, b, preferred_element_type=F32)


def _split3(x):
    hi = x.astype(BF16)
    r1 = x - hi.astype(F32)
    mid = r1.astype(BF16)
    lo = (r1 - mid.astype(F32)).astype(BF16)
    return hi, mid, lo


def _dot_exact_rhs(x, m):
    mb = m.astype(BF16)
    hi, mid, lo = _split3(x)
    return _dot(hi, mb) + _dot(mid, mb) + _dot(lo, mb)


def _dot_exact_lhs(m, x):
    mb = m.astype(BF16)
    hi, mid, lo = _split3(x)
    return _dot(mb, hi) + _dot(mb, mid) + _dot(mb, lo)


def _rms(x, g):
    return x * lax.rsqrt(jnp.mean(x * x, axis=-1, keepdims=True) + EPS) * g


def _log_sigmoid(x):
    return jnp.minimum(x, 0.0) - jnp.log1p(jnp.exp(-jnp.abs(x)))


def _const_spec(shape):
    nd = len(shape)
    return pl.BlockSpec(shape, lambda *_: (0,) * nd, pipeline_mode=pl.Buffered(1))


def _params(sem):
    return pltpu.CompilerParams(dimension_semantics=sem, vmem_limit_bytes=VMEM_LIMIT)


def _ada_kernel(c_ref, w_ref, b_ref, o_ref):
    c = c_ref[...]
    a = (c * jax.nn.sigmoid(c)).astype(BF16)
    o_ref[...] = _dot(a, w_ref[...]) + b_ref[...]


def _ada_call(c, w, b):
    m, n = c.shape[0], w.shape[1]
    tn = n // 8
    return pl.pallas_call(
        _ada_kernel,
        grid=(n // tn,),
        in_specs=[pl.BlockSpec((m, D_MODEL), lambda i: (0, 0)),
                  pl.BlockSpec((D_MODEL, tn), lambda i: (0, i)),
                  pl.BlockSpec((1, tn), lambda i: (0, i))],
        out_specs=pl.BlockSpec((m, tn), lambda i: (0, i)),
        out_shape=jax.ShapeDtypeStruct((m, n), F32),
        compiler_params=_params(("parallel",)),
        name="ada",
    )(c, w, b)


FF_CHUNK = 256


def _ffn_kernel(x_ref, shift_ref, scale_ref, gate_ref, g1_ref, g2_ref, win_ref, wout_ref, o_ref, *, res_w):
    x = x_ref[...]
    h = _rms(x, g1_ref[...]) * (1.0 + scale_ref[0]) + shift_ref[0]
    hb = h.astype(BF16)
    y = jnp.zeros(x.shape, F32)
    for c in range(D_FF // FF_CHUNK):
        lo, hi = c * FF_CHUNK, (c + 1) * FF_CHUNK
        g = _dot(hb, win_ref[:, lo:hi])
        u = _dot(hb, win_ref[:, D_FF + lo:D_FF + hi])
        a = (g * jax.nn.sigmoid(g) * u).astype(BF16)
        y = y + _dot(a, wout_ref[lo:hi, :])
    o_ref[...] = x + res_w * gate_ref[0] * _rms(y, g2_ref[...])


def _mod_spec(r, tiles_per_mod, col):
    return pl.BlockSpec((1, r, D_MODEL), lambda i: (i // tiles_per_mod, 0, col))


def _ffn_call(x, mod, g1, g2, w_in, w_out, layer, which, res_w, tm, tiles_per_mod):
    n = x.shape[0]
    mod_all, sub = mod
    r = mod_all.shape[1]
    tok = pl.BlockSpec((tm, D_MODEL), lambda i: (i, 0))
    pick = lambda rows, cols: pl.BlockSpec((None, None, rows, cols), lambda i: (layer, which, 0, 0),
                                           pipeline_mode=pl.Buffered(1))
    return pl.pallas_call(
        functools.partial(_ffn_kernel, res_w=res_w),
        grid=(n // tm,),
        in_specs=[tok] + [_mod_spec(r, tiles_per_mod, 3 * sub + k) for k in range(3)]
        + [_const_spec((1, D_MODEL)), _const_spec((1, D_MODEL)), pick(D_MODEL, 2 * D_FF), pick(D_FF, D_MODEL)],
        out_specs=tok,
        out_shape=jax.ShapeDtypeStruct((n, D_MODEL), F32),
        compiler_params=_params(("parallel",)),
        name="ffn",
    )(x, mod_all, mod_all, mod_all, g1, g2, w_in, w_out)


_STACKED_OUTS = (
    ("ka", 1, W_A), ("va", 1, W_A), ("kc", N_HEADS, HEAD_PAD), ("vc", N_HEADS, HEAD_PAD), ("logf", 1, N_HEADS),
)
_LAYER_OUTS = (
    ("qa_p", W_PAD, BF16), ("ka_p", W_PAD, BF16), ("va_p", W_PAD, BF16),
    ("qc1_p", W_PAD, BF16), ("qc2_p", W_PAD, BF16), ("kc_b", W_C, BF16), ("vc_b", W_C, BF16),
    ("qb", LANES, F32), ("kb", LANES, F32), ("vb", W_B, F32), ("loga", LANES, F32), ("rb", W_B, F32),
)
_NORM_ROWS = ("qa_p", "ka_p", "qc1_p", "qc2_p", "kc_b")


def _proj_kernel(x_ref, shift_ref, scale_ref, g1_ref, w_ref, wkvt_ref, bf_ref, wup_ref, bup_ref, vone_ref, *refs,
                 n_alias, with_qa, transposed_a):
    refs = refs[n_alias:]
    names = [n for n, _, _ in _STACKED_OUTS] + [n for n, _, _ in _LAYER_OUTS] + (["qa"] if with_qa else [])
    o = dict(zip(names + ["norms"], refs))
    tm = x_ref.shape[0]
    x = x_ref[...]
    hb = (_rms(x, g1_ref[...]) * (1.0 + scale_ref[0]) + shift_ref[0]).astype(BF16)

    def seg(name):
        lo, hi = _SEG[name]
        return _dot(hb, w_ref[:, lo:hi])

    def store_heads(ref, val):
        for d in range(ref.shape[0]):
            for h in range(N_HEADS):
                ref[d, pl.ds(h, tm, stride=N_HEADS), :] = val[:, h * HEAD_PAD:(h + 1) * HEAD_PAD]

    def store_slots(ref, val):
        for d in range(ref.shape[0]):
            ref[d] = val

    if transposed_a:
        kvt = _nt_dot(wkvt_ref[...], hb)
        store_slots(o["ka"], kvt[None, :W_A])
        store_slots(o["va"], kvt[None, W_A:])
    else:
        store_slots(o["ka"], seg("ka"))
        store_slots(o["va"], seg("va"))
    kc = seg("kc")
    store_heads(o["kc"], kc)
    rounded = {"kc_b": kc.astype(BF16)}
    o["kc_b"][...] = rounded["kc_b"]
    vc = seg("vc")
    store_heads(o["vc"], vc)
    o["vc_b"][...] = vc.astype(BF16)
    def pad_heads(val):
        zero = jnp.zeros((tm, HEAD_PAD - HD_A), val.dtype)
        pieces = []
        for h in range(N_HEADS):
            pieces += [val[:, h * HD_A:(h + 1) * HD_A], zero]
        return jnp.concatenate(pieces, axis=1)

    qa = seg("qa")
    if with_qa:
        o["qa"][...] = qa * (HD_A ** -0.5)
    rounded["qa_p"] = pad_heads((qa * (HD_A ** -0.5 * LOG2E)).astype(BF16))
    rounded["ka_p"] = pad_heads(seg("ka").astype(BF16))
    o["qa_p"][...] = rounded["qa_p"]
    o["ka_p"][...] = rounded["ka_p"]
    o["va_p"][...] = (pad_heads(seg("va")) + vone_ref[...]).astype(BF16)
    qc = seg("qc") * (HD_C ** -0.5 * LOG2E)
    first_map = lax.broadcasted_iota(jnp.int32, qc.shape, 1) % HEAD_PAD < HD_C
    rounded["qc1_p"] = jnp.where(first_map, qc, 0.0).astype(BF16)
    rounded["qc2_p"] = jnp.where(first_map, 0.0, qc).astype(BF16)
    o["qc1_p"][...] = rounded["qc1_p"]
    o["qc2_p"][...] = rounded["qc2_p"]
    fa = seg("fa") + bf_ref[...]
    store_slots(o["logf"], _log_sigmoid(fa)[:, :N_HEADS])
    o["qb"][...] = seg("qb")
    o["kb"][...] = seg("kb")
    o["vb"][...] = seg("vb")
    o["rb"][...] = seg("rb")
    gb = seg("gb").astype(BF16)
    o["loga"][...] = _log_sigmoid(_dot(gb, wup_ref[...]) + bup_ref[...]) * (1.0 / GLA_TAU)
    lane = lax.broadcasted_iota(jnp.int32, (8, LANES), 1)
    tile_norms = jnp.zeros((8, LANES), F32)
    for a, name in enumerate(_NORM_ROWS):
        sq = rounded[name].astype(F32)
        sq = sq * sq
        for h in range(N_HEADS):
            ss = jnp.sum(sq[:, h * HEAD_PAD:(h + 1) * HEAD_PAD], axis=1, keepdims=True)
            tile_norms = jnp.where(lane == N_HEADS * a + h, jnp.max(ss, axis=0, keepdims=True), tile_norms)
    o["norms"][0] = tile_norms


def _proj_call(x, mod, g1, w_ext, w_kvt, bf, wup, bup, vone, tm, tiles_per_mod, layer, depth, stacked, with_qa,
               seq_len):
    n = x.shape[0]
    mod_all, sub = mod
    r = mod_all.shape[1]
    tok = lambda w: pl.BlockSpec((tm, w), lambda i: (i, 0))
    layer_outs = _LAYER_OUTS + ((("qa", W_A, F32),) if with_qa else ())
    alias_in = [] if stacked is None else [stacked[nm] for nm, _, _ in _STACKED_OUTS]
    transposed_a = seq_len % tm == 0
    tps = max(seq_len // tm, 1)
    stacked_specs, stacked_shapes = [], []
    slots = depth if stacked is None else 1
    for nm, rows, w in _STACKED_OUTS:
        if transposed_a and nm in ("ka", "va"):
            stacked_specs.append(pl.BlockSpec((slots, 1, w, tm), lambda i: (layer, i // tps, 0, i % tps)))
            stacked_shapes.append(jax.ShapeDtypeStruct((depth, n // seq_len, w, seq_len), F32))
        else:
            stacked_specs.append(pl.BlockSpec((slots, tm * rows, w), lambda i: (layer, i, 0)))
            stacked_shapes.append(jax.ShapeDtypeStruct((depth, n * rows, w), F32))
    n_fixed = 10
    outs = pl.pallas_call(
        functools.partial(_proj_kernel, n_alias=len(alias_in), with_qa=with_qa, transposed_a=transposed_a),
        grid=(n // tm,),
        in_specs=[tok(D_MODEL), _mod_spec(r, tiles_per_mod, 3 * sub), _mod_spec(r, tiles_per_mod, 3 * sub + 1),
                  _const_spec((1, D_MODEL)), _const_spec((D_MODEL, N_EXT)), _const_spec((2 * W_A, D_MODEL)),
                  _const_spec((1, LANES)), _const_spec((LANES, LANES)), _const_spec((1, LANES)),
                  _const_spec((1, W_PAD))]
        + [pl.BlockSpec(memory_space=pl.ANY)] * len(alias_in),
        out_specs=stacked_specs + [tok(w) for _, w, _ in layer_outs]
        + [pl.BlockSpec((1, 8, LANES), lambda i: (i, 0, 0))],
        out_shape=stacked_shapes + [jax.ShapeDtypeStruct((n, w), dt) for _, w, dt in layer_outs]
        + [jax.ShapeDtypeStruct((n // tm, 8, LANES), F32)],
        input_output_aliases={n_fixed + k: k for k in range(len(alias_in))},
        compiler_params=_params(("parallel",)),
        name="proj",
    )(x, mod_all, mod_all, g1, w_ext, w_kvt, bf, wup, bup, vone, *alias_in)
    names = [nm for nm, _, _ in _STACKED_OUTS] + [nm for nm, _, _ in layer_outs] + ["norms"]
    return dict(zip(names, outs))


def _cumsum_kernel(x_ref, o_ref):
    x = x_ref[0]
    nb = x.shape[0]
    r = lax.broadcasted_iota(jnp.int32, (LANES, LANES), 0)
    c = lax.broadcasted_iota(jnp.int32, (LANES, LANES), 1)
    local = _dot_exact_rhs(x, (r <= c).astype(F32))
    tot = jnp.broadcast_to(local[:, LANES - 1:LANES], (nb, LANES))
    rr = lax.broadcasted_iota(jnp.int32, (nb, nb), 0)
    cc = lax.broadcasted_iota(jnp.int32, (nb, nb), 1)
    o_ref[0] = local + _dot_exact_lhs((cc < rr).astype(F32), tot)


def _cumsum_call(x):
    rows, nb, _ = x.shape
    spec = pl.BlockSpec((1, nb, LANES), lambda i: (i, 0, 0))
    return pl.pallas_call(
        _cumsum_kernel, grid=(rows,), in_specs=[spec], out_specs=spec,
        out_shape=jax.ShapeDtypeStruct(x.shape, F32),
        compiler_params=_params(("parallel",)), name="cumsum",
    )(x)


def _lane_tile(x, width):
    return jnp.concatenate([x] * (width // LANES), axis=1)


GROUP_HEADS = 2
W_GROUP = GROUP_HEADS * HEAD_PAD


def _causal_sweep(i, first, tq, tk, tiles_per_iter, tile):
    r = tq // tk

    def body(jj, carry):
        for u in range(tiles_per_iter):
            tile(jj * tiles_per_iter + u, slice(0, tq), None)
        return carry

    assert r % tiles_per_iter == 0 and tiles_per_iter in (1, 2)
    lead = 0
    if tiles_per_iter == 2:
        lead = lax.rem(i * r - first, 2)

        @pl.when(lead == 1)
        def _():
            tile(first, slice(0, tq), None)

    lax.fori_loop(lax.div(first + lead, tiles_per_iter), lax.div(i * r, tiles_per_iter), body, 0)
    for d in range(r):
        tile(i * r + d, slice(d * tk, (d + 1) * tk), d * tk)
        if d + 1 < r:
            tile(i * r + d, slice((d + 1) * tk, tq), None)


def _fox_kernel(first_ref, q_ref, k_ref, v_ref, c_ref, o_ref, m_sc, acc_sc, *, tk, tiles_per_iter):
    i = pl.program_id(2)
    first = first_ref[pl.program_id(0), pl.program_id(1), i]
    tq = q_ref.shape[1]
    m_sc[...] = jnp.full(m_sc.shape, NEG, F32)
    acc_sc[...] = jnp.zeros(acc_sc.shape, F32)
    row = lax.broadcasted_iota(jnp.int32, (tk, tk), 0)
    col = lax.broadcasted_iota(jnp.int32, (tk, tk), 1)
    causal = col <= row

    def tile(j, rows, local):
        ks = pl.ds(pl.multiple_of(j * tk, tk), tk)
        for h in range(GROUP_HEADS):
            hs = slice(h * HEAD_PAD, (h + 1) * HEAD_PAD)
            c_first = c_ref[0, h, pl.ds(i * (tq // tk), 1), :][:, 0:1]
            t = _nt_dot(q_ref[0, rows, hs], k_ref[0, ks, hs]) + (c_first - c_ref[0, h, pl.ds(j, 1), :]) * LOG2E
            if local is not None:
                t = jnp.where(causal, t, NEG)
            m_old = m_sc[h, rows, :]
            m_new = jnp.maximum(m_old, jnp.max(t, axis=1, keepdims=True))
            p = jnp.exp2(t - _lane_tile(m_new, tk)).astype(BF16)
            acc_sc[h, rows, :] = jnp.exp2(m_old - m_new) * acc_sc[h, rows, :] + _dot(p, v_ref[0, ks, hs])
            m_sc[h, rows, :] = m_new

    _causal_sweep(i, first, tq, tk, tiles_per_iter, tile)
    outs = []
    for h in range(GROUP_HEADS):
        acc = acc_sc[h]
        outs.append((acc[:, :HD_A] / acc[:, HD_A:HD_A + 1]).astype(BF16))
    o_ref[0] = jnp.concatenate(outs, axis=1)


def _resident_spec(t, width):
    return pl.BlockSpec((1, t, width), lambda b_, g, i, first: (b_, 0, g), pipeline_mode=pl.Buffered(1))


def _fox_call(first, qa_p, ka_p, va_p, c, tq, tk):
    b, t, _ = qa_p.shape
    qspec = pl.BlockSpec((1, tq, W_GROUP), lambda b_, g, i, first: (b_, i, g))
    grid_spec = pltpu.PrefetchScalarGridSpec(
        num_scalar_prefetch=1,
        grid=(b, N_HEADS // GROUP_HEADS, t // tq),
        in_specs=[qspec, _resident_spec(t, W_GROUP), _resident_spec(t, W_GROUP),
                  pl.BlockSpec((1, GROUP_HEADS, t // tk, tk), lambda b_, g, i, first: (b_, g, 0, 0))],
        out_specs=pl.BlockSpec((1, tq, GROUP_HEADS * HD_A), lambda b_, g, i, first: (b_, i, g)),
        scratch_shapes=[pltpu.VMEM((GROUP_HEADS, tq, LANES), F32), pltpu.VMEM((GROUP_HEADS, tq, HEAD_PAD), F32)])
    return pl.pallas_call(
        functools.partial(_fox_kernel, tk=tk, tiles_per_iter=1),
        grid_spec=grid_spec,
        out_shape=jax.ShapeDtypeStruct((b, t, W_A), BF16),
        compiler_params=_params(("parallel", "parallel", "arbitrary")),
        name="fox",
    )(first, qa_p, ka_p, va_p, c.reshape(b, N_HEADS, t // tk, tk))


def _diff_lambda(lp, lam_init):
    a = jnp.sum(lp[0:1] * lp[1:2], axis=1, keepdims=True)
    b = jnp.sum(lp[2:3] * lp[3:4], axis=1, keepdims=True)
    return jnp.exp(a) - jnp.exp(b) + lam_init


def _diag_distance(tk):
    q = np.arange(tk)[:, None]
    k = np.arange(tk)[None, :]
    return np.where(k // CHUNK <= q // CHUNK, 2 * np.minimum(q, k) - k, HIDDEN).astype(np.float32)


DIFF_GROUP = 1
W_DIFF_GROUP = DIFF_GROUP * HEAD_PAD


def _diff_kernel(first_ref, q1_ref, q2_ref, k_ref, v_ref, slope_ref, lam_ref, g_ref, rel_ref, o_ref,
                 m_sc, l_sc, a_sc, *, tk, tiles_per_iter, lam_init):
    g = pl.program_id(1)
    i = pl.program_id(2)
    first = first_ref[pl.program_id(0), g, i]
    tq = q1_ref.shape[1]
    m_sc[...] = jnp.full(m_sc.shape, NEG, F32)
    l_sc[...] = jnp.zeros(l_sc.shape, F32)
    a_sc[...] = jnp.zeros(a_sc.shape, F32)
    lane = lax.broadcasted_iota(jnp.int32, (1, tk), 1)

    def tile(j, rows, local):
        ks = pl.ds(pl.multiple_of(j * tk, tk), tk)
        if local is None:
            rel = (lane + (j * tk - i * tq)).astype(F32)
        else:
            rel = rel_ref[...] + float(local)
        for h in range(DIFF_GROUP):
            hs = slice(h * HEAD_PAD, (h + 1) * HEAD_PAD)
            bias = (slope_ref[g * DIFF_GROUP + h][:, 0:1] * LOG2E) * rel
            k = k_ref[0, ks, hs]
            v = v_ref[0, ks, hs]
            nrows = rows.stop - rows.start
            scores = _nt_dot(jnp.concatenate([q1_ref[0, rows, hs], q2_ref[0, rows, hs]], axis=0), k)
            probs, alphas = [], []
            for m in range(2):
                n = 2 * h + m
                t = scores[m * nrows:(m + 1) * nrows] + bias
                m_old = m_sc[n, rows, :]
                m_new = jnp.maximum(m_old, jnp.max(t, axis=1, keepdims=True))
                alpha = jnp.exp2(m_old - m_new)
                p = jnp.exp2(t - _lane_tile(m_new, tk))
                l_sc[n, rows, :] = alpha * l_sc[n, rows, :] + jnp.sum(p, axis=1, keepdims=True)
                m_sc[n, rows, :] = m_new
                probs.append(p.astype(BF16))
                alphas.append(alpha)
            pv = _dot(jnp.concatenate(probs, axis=0), v)
            for m in range(2):
                n = 2 * h + m
                a_sc[n, rows, :] = alphas[m] * a_sc[n, rows, :] + pv[m * nrows:(m + 1) * nrows]

    _causal_sweep(i, first, tq, tk, tiles_per_iter, tile)
    lam = _diff_lambda(lam_ref[...], lam_init)
    for h in range(DIFF_GROUP):
        out = a_sc[2 * h] / l_sc[2 * h] - lam * (a_sc[2 * h + 1] / l_sc[2 * h + 1])
        o_ref[0, :, h * HEAD_PAD:(h + 1) * HEAD_PAD] = (_rms(out, g_ref[...]) * (1.0 - lam_init)).astype(BF16)


def _diff_call(first, qc1_p, qc2_p, kc_b, vc_b, slopes, lam_p, g_diff, lam_init, tq, tk):
    assert CHUNK == 64 and tk % CHUNK == 0
    b, t, _ = qc1_p.shape
    qspec = pl.BlockSpec((1, tq, W_DIFF_GROUP), lambda b_, g, i, first: (b_, i, g))
    stat = pltpu.VMEM((2 * DIFF_GROUP, tq, LANES), F32)
    grid_spec = pltpu.PrefetchScalarGridSpec(
        num_scalar_prefetch=1,
        grid=(b, N_HEADS // DIFF_GROUP, t // tq),
        in_specs=[qspec, qspec, _resident_spec(t, W_DIFF_GROUP), _resident_spec(t, W_DIFF_GROUP),
                  pl.BlockSpec((N_HEADS, 1, LANES), lambda b_, g, i, first: (0, 0, 0)),
                  pl.BlockSpec((4, HD_C), lambda b_, g, i, first: (0, 0)),
                  pl.BlockSpec((1, HEAD_PAD), lambda b_, g, i, first: (0, 0)),
                  pl.BlockSpec((tk, tk), lambda b_, g, i, first: (0, 0), pipeline_mode=pl.Buffered(1))],
        out_specs=qspec,
        scratch_shapes=[stat, stat, pltpu.VMEM((2 * DIFF_GROUP, tq, HEAD_PAD), F32)])
    return pl.pallas_call(
        functools.partial(_diff_kernel, tk=tk, tiles_per_iter=2, lam_init=lam_init),
        grid_spec=grid_spec,
        out_shape=jax.ShapeDtypeStruct((b, t, W_C), BF16),
        compiler_params=_params(("parallel", "parallel", "arbitrary")),
        name="diff",
    )(first, qc1_p, qc2_p, kc_b, vc_b, slopes, lam_p, g_diff, jnp.asarray(_diag_distance(tk)))


def _gla_kernel(q_ref, k_ref, v_ref, la_ref, r_ref, s0_ref, g_ref, o_ref, s_ref, *, chunk):
    ti = pl.program_id(1)
    n_chunks = q_ref.shape[1] // chunk
    n_seq = q_ref.shape[0]

    @pl.when(ti == 0)
    def _():
        s_ref[...] = s0_ref[...]

    tri_r = lax.broadcasted_iota(jnp.int32, (chunk, chunk), 0)
    tri_c = lax.broadcasted_iota(jnp.int32, (chunk, chunk), 1)
    incl = (tri_c <= tri_r).astype(F32)
    srow = lax.broadcasted_iota(jnp.int32, (N_HEADS * chunk, LANES), 0) // chunk
    slane = lax.broadcasted_iota(jnp.int32, (N_HEADS * chunk, LANES), 1) // DK_B
    qmask = srow == slane
    arow = lax.broadcasted_iota(jnp.int32, (N_HEADS * chunk, chunk), 0) % chunk
    acol = lax.broadcasted_iota(jnp.int32, (N_HEADS * chunk, chunk), 1)
    causal = acol <= arow
    vlane = lax.broadcasted_iota(jnp.int32, (chunk, W_B), 1) // DV_B
    st_row = lax.broadcasted_iota(jnp.int32, (W_B, LANES), 0) // DV_B
    st_lane = lax.broadcasted_iota(jnp.int32, (W_B, LANES), 1) // DK_B
    diag_blocks = st_row == st_lane

    for c, sq in [(c, sq) for c in range(n_chunks) for sq in range(n_seq)]:
        sl = slice(c * chunk, (c + 1) * chunk)
        cb = _dot_exact_lhs(incl, la_ref[sq, sl, :])
        cb_last = cb[chunk - 1:chunk, :]
        q = q_ref[sq, sl, :] * (DK_B ** -0.5)
        k = k_ref[sq, sl, :]
        v = v_ref[sq, sl, :]
        vb = v.astype(BF16)
        qe = q * jnp.exp(cb)
        ke = (k * jnp.exp(-cb)).astype(BF16)
        kl = (k * jnp.exp(cb_last - cb)).astype(BF16)
        decay = jnp.exp(cb_last)
        qstack = jnp.where(qmask, jnp.concatenate([qe] * N_HEADS, axis=0), 0.0).astype(BF16)
        a = jnp.where(causal, _nt_dot(qstack, ke), 0.0).astype(BF16)
        oi = _dot(a, vb)
        o = jnp.zeros((chunk, W_B), F32)
        for h in range(N_HEADS):
            o = o + jnp.where(vlane == h, oi[h * chunk:(h + 1) * chunk], 0.0)
        st = s_ref[sq]
        o = o + _nt_dot(qe.astype(BF16), st.astype(BF16))
        upd = _dot(v.T.astype(BF16), kl)
        s_ref[sq] = decay * st + jnp.where(diag_blocks, upd, 0.0)
        o2 = o * o
        ms = jnp.zeros((chunk, W_B), F32)
        for h in range(N_HEADS):
            head_mean = jnp.sum(jnp.where(vlane == h, o2, 0.0), axis=1, keepdims=True) * (1.0 / DV_B)
            ms = jnp.where(vlane == h, head_mean, ms)
        r = r_ref[sq, sl, :]
        o_ref[sq, sl, :] = (o * lax.rsqrt(ms + EPS) * g_ref[...] * (r * jax.nn.sigmoid(r))).astype(BF16)


def _gla_call(qb, kb, vb, loga, rb, s0_t, g_gla, tg, chunk, group):
    b, t, _ = qb.shape
    tok = lambda w: pl.BlockSpec((group, tg, w), lambda b_, i: (b_, i, 0))
    sspec = pl.BlockSpec((group, W_B, LANES), lambda b_, i: (b_, 0, 0))
    return pl.pallas_call(
        functools.partial(_gla_kernel, chunk=chunk),
        grid=(b // group, t // tg),
        in_specs=[tok(LANES), tok(LANES), tok(W_B), tok(LANES), tok(W_B), sspec,
                  pl.BlockSpec((1, W_B), lambda b_, i: (0, 0))],
        out_specs=[tok(W_B), sspec],
        out_shape=[jax.ShapeDtypeStruct((b, t, W_B), BF16), jax.ShapeDtypeStruct((b, W_B, LANES), F32)],
        compiler_params=_params(("parallel", "arbitrary")),
        name="gla",
    )(qb, kb, vb, loga, rb, s0_t, g_gla)


def _softmax_two(t_past, t_new, exp_fn):
    m = jnp.maximum(jnp.max(t_past, axis=1, keepdims=True), jnp.max(t_new, axis=1, keepdims=True))
    p_past = exp_fn(t_past - m)
    p_new = exp_fn(t_new - m)
    l = jnp.sum(p_past, axis=1, keepdims=True) + jnp.sum(p_new, axis=1, keepdims=True)
    return p_past, p_new, l


def _fox_s_kernel(q_ref, kn_ref, vn_ref, kpt_ref, vpt_ref, c_ref, o_ref):
    t, past = q_ref.shape[1], kpt_ref.shape[1]
    q = q_ref[0]
    kn = kn_ref[...].astype(BF16)
    vn = vn_ref[...].astype(BF16)
    kpt = kpt_ref[...].astype(BF16)
    vpt = vpt_ref[...].astype(BF16)
    lane_head = lax.broadcasted_iota(jnp.int32, (t, W_A), 1) // HD_A
    row = lax.broadcasted_iota(jnp.int32, (t, t), 0)
    col = lax.broadcasted_iota(jnp.int32, (t, t), 1)
    o = jnp.zeros((t, W_A), F32)
    for h in range(N_HEADS):
        qh = jnp.where(lane_head == h, q, 0.0).astype(BF16)
        ch = c_ref[0, h:h + 1, :]
        cref = ch[:, past:past + 1]
        t_past = _dot(qh, kpt) + (cref - ch[:, :past])
        t_new = jnp.where(col <= row, _nt_dot(qh, kn) + (cref - ch[:, past:past + t]), NEG)
        p_past, p_new, l = _softmax_two(t_past, t_new, jnp.exp)
        oh = (_nt_dot(p_past.astype(BF16), vpt) + _dot(p_new.astype(BF16), vn)) / l
        o = o + jnp.where(lane_head == h, oh, 0.0)
    o_ref[0] = o.astype(BF16)


def _fox_s_call(layer, qa, ka, va, cache_kt, cache_vt, c):
    b, t, _ = qa.shape
    past = cache_kt.shape[3]
    new = pl.BlockSpec((1, t, W_A), lambda i: (i, 0, 0))
    new_l = lambda: pl.BlockSpec((None, t, W_A), lambda i: (layer, i, 0))
    old_l = lambda: pl.BlockSpec((None, None, W_A, past), lambda i: (layer, i, 0, 0))
    return pl.pallas_call(
        _fox_s_kernel,
        grid=(b,),
        in_specs=[new, new_l(), new_l(), old_l(), old_l(),
                  pl.BlockSpec((1, N_HEADS, c.shape[2]), lambda i: (i, 0, 0))],
        out_specs=new,
        out_shape=jax.ShapeDtypeStruct((b, t, W_A), BF16),
        compiler_params=_params(("parallel",)),
        name="fox_s",
    )(qa, ka, va, cache_kt, cache_vt, c)


def _diff_s_kernel(q1_ref, q2_ref, kn_ref, vn_ref, kp_ref, vp_ref, slope_ref, lam_ref, g_ref, o_ref, *, lam_init):
    t, past = q1_ref.shape[1], kp_ref.shape[0] // N_HEADS
    lam = _diff_lambda(lam_ref[...], lam_init)
    row = lax.broadcasted_iota(jnp.int32, (t, t), 0) + past
    col = lax.broadcasted_iota(jnp.int32, (t, t), 1) + past
    visible_new = lax.shift_right_logical(col, 6) <= lax.shift_right_logical(row, 6)
    dist_new = jnp.abs(row - col).astype(F32)
    prow = lax.broadcasted_iota(jnp.int32, (t, past), 0) + past
    pcol = lax.broadcasted_iota(jnp.int32, (t, past), 1)
    visible_past = lax.shift_right_logical(pcol, 6) <= lax.shift_right_logical(prow, 6)
    dist_past = (prow - pcol).astype(F32)
    for h in range(N_HEADS):
        hs = slice(h * HEAD_PAD, (h + 1) * HEAD_PAD)
        slope = slope_ref[h][:, 0:1] * LOG2E
        kn = kn_ref[pl.ds(h, t, stride=N_HEADS), :].astype(BF16)
        vn = vn_ref[pl.ds(h, t, stride=N_HEADS), :].astype(BF16)
        kp = kp_ref[pl.ds(h, past, stride=N_HEADS), :].astype(BF16)
        vp = vp_ref[pl.ds(h, past, stride=N_HEADS), :].astype(BF16)
        outs = []
        for q_ref in (q1_ref, q2_ref):
            q = q_ref[0, :, hs]
            t_past = jnp.where(visible_past, _nt_dot(q, kp) - slope * dist_past, NEG)
            t_new = jnp.where(visible_new, _nt_dot(q, kn) - slope * dist_new, NEG)
            p_past, p_new, l = _softmax_two(t_past, t_new, jnp.exp2)
            outs.append((_dot(p_past.astype(BF16), vp) + _dot(p_new.astype(BF16), vn)) / l)
        out = outs[0] - lam * outs[1]
        o_ref[0, :, hs] = (_rms(out, g_ref[...]) * (1.0 - lam_init)).astype(BF16)


def _diff_s_call(layer, qc1_p, qc2_p, kc, vc, cache_k, cache_v, slopes, lam_p, g_diff, lam_init):
    assert CHUNK == 64
    b, t, _ = qc1_p.shape
    rows_past = cache_k.shape[2]
    new = lambda: pl.BlockSpec((1, t, W_C), lambda i: (i, 0, 0))
    new_l = lambda: pl.BlockSpec((None, t * N_HEADS, HEAD_PAD), lambda i: (layer, i, 0))
    old_l = lambda: pl.BlockSpec((None, None, rows_past, HEAD_PAD), lambda i: (layer, i, 0, 0))
    return pl.pallas_call(
        functools.partial(_diff_s_kernel, lam_init=lam_init),
        grid=(b,),
        in_specs=[new(), new(), new_l(), new_l(), old_l(), old_l(),
                  pl.BlockSpec((N_HEADS, 1, LANES), lambda i: (0, 0, 0)),
                  pl.BlockSpec((4, HD_C), lambda i: (0, 0)),
                  pl.BlockSpec((1, HEAD_PAD), lambda i: (0, 0))],
        out_specs=new(),
        out_shape=jax.ShapeDtypeStruct((b, t, W_C), BF16),
        compiler_params=_params(("parallel",)),
        name="diff_s",
    )(qc1_p, qc2_p, kc, vc, cache_k, cache_v, slopes, lam_p, g_diff)


def _out_kernel(x_ref, oa_ref, gla_ref, dif_ref, gate_ref, g2_ref, woa_ref, wog_ref, wod_ref, o_ref):
    y = _dot(oa_ref[...], woa_ref[...]) + _dot(gla_ref[...], wog_ref[...]) + _dot(dif_ref[...], wod_ref[...])
    o_ref[...] = x_ref[...] + gate_ref[0] * _rms(y, g2_ref[...])


def _out_call(x, oa, gla, dif, mod, g2, w_oa, w_og, w_od, tm, tiles_per_mod):
    n = x.shape[0]
    mod_all, sub = mod
    r = mod_all.shape[1]
    tok = lambda w: pl.BlockSpec((tm, w), lambda i: (i, 0))
    return pl.pallas_call(
        _out_kernel,
        grid=(n // tm,),
        in_specs=[tok(D_MODEL), tok(oa.shape[1]), tok(W_B), tok(W_C), _mod_spec(r, tiles_per_mod, 3 * sub + 2),
                  _const_spec((1, D_MODEL)), _const_spec(w_oa.shape), _const_spec(w_og.shape),
                  _const_spec(w_od.shape)],
        out_specs=tok(D_MODEL),
        out_shape=jax.ShapeDtypeStruct((n, D_MODEL), F32),
        compiler_params=_params(("parallel",)),
        name="out_proj",
    )(x, oa, gla, dif, mod_all, g2, w_oa, w_og, w_od)


def _pad_lanes(w, width=LANES):
    return jnp.pad(w, ((0, 0), (0, width - w.shape[1])))


def _layer_params(l, w_in, b_f, w_gla_up, b_gla_up, g_gla, g_diff, w_out):
    offs = [int(o) for o in np.cumsum(PROJ_SIZES)[:-1]]
    qa, ka, va, fa, qb, kb, vb, gb, rb, qc, kc, vc = jnp.split(w_in[l], offs, axis=1)
    segs = {
        "ka": ka, "va": va, "kc": kc, "vc": vc, "qa": qa, "qc": qc,
        "fa": _pad_lanes(fa), "qb": qb, "kb": kb, "vb": vb, "rb": rb, "gb": _pad_lanes(gb),
    }
    w_ext = jnp.concatenate([segs[n] for n, _ in _SEG_WIDTHS], axis=1).astype(BF16)
    wo = w_out[l]
    vone = np.zeros((1, W_PAD), np.float32)
    vone[0, HD_A::HEAD_PAD] = 1.0
    return dict(
        w_ext=w_ext, w_kvt=jnp.concatenate([ka, va], axis=1).T.astype(BF16),
        bf=_pad_lanes(b_f[l][None, :]),
        wup=jnp.pad(w_gla_up[l], ((0, LANES - GLA_RANK), (0, 0))).astype(BF16),
        bup=b_gla_up[l][None, :],
        vone=jnp.asarray(vone),
        g_gla=jnp.tile(g_gla[l], N_HEADS)[None, :], g_diff=g_diff[l][None, :],
        w_oa=wo[:W_A].astype(BF16),
        w_og=wo[W_A:W_A + W_B].astype(BF16), w_od=wo[W_A + W_B:].astype(BF16),
    )


def _state_to_blockdiag_t(s):
    eye = jnp.eye(N_HEADS, dtype=s.dtype)
    return jnp.einsum("bhkv,hg->bhvgk", s, eye).reshape(s.shape[0], W_B, N_HEADS * DK_B)


def _blockdiag_t_to_state(st):
    b = st.shape[0]
    s5 = st.reshape(b, N_HEADS, DV_B, N_HEADS, DK_B)
    diag = jnp.stack([s5[:, h, :, h, :] for h in range(N_HEADS)], axis=1)
    return jnp.swapaxes(diag, 2, 3)


SKIP_LOG2 = 160.0
NORM_SLACK = 1.01


def _tile_max(x, factor):
    return jnp.max(x.reshape(x.shape[:-1] + (x.shape[-1] // factor, factor)), axis=-1)


def _first_tiles(qn, kn, gap, tq, tk, group):
    b, h, nq = qn.shape
    r = tq // tk
    kn_diag = _tile_max(kn, r)
    bound = NORM_SLACK * (qn[..., None] * kn[:, :, None, :] + (qn * kn_diag)[..., None]) + gap
    dead = (bound < -SKIP_LOG2).reshape(b, h // group, group, nq, kn.shape[-1]).all(axis=2)
    lead = jnp.sum(jnp.cumprod(dead.astype(jnp.int32), axis=-1), axis=-1)
    return jnp.minimum(lead, jnp.arange(nq, dtype=jnp.int32) * r).astype(jnp.int32)


def _tile(n, pref):
    t = min(n, pref)
    assert n % t == 0, (n, t)
    return t


def _trunk(x, mods, params, ffn_w, g_norm, lam_params, mixer, tm, tiles_per_mod, with_qa):
    b, t, _ = x.shape
    n = b * t
    depth = len(params)
    xf = x.reshape(n, D_MODEL)
    stacked = None
    gla_states = []
    for l in range(depth):
        p = params[l]
        gn = lambda k: g_norm[l, k][None, :]
        xf = _ffn_call(xf, (mods[l], 0), gn(0), gn(1), *ffn_w, l, 0, 0.5, tm, tiles_per_mod)
        pr = _proj_call(xf, (mods[l], 1), gn(2), p["w_ext"], p["w_kvt"], p["bf"], p["wup"], p["bup"], p["vone"], tm,
                        tiles_per_mod, l, depth, stacked, with_qa, t)
        stacked = {nm: pr[nm] for nm, _, _ in _STACKED_OUTS}
        lam_init = 0.8 - 0.6 * math.exp(-0.3 * l)
        oa, gla, dif, s_fin, w_oa = mixer(l, pr, p, lam_params[l], lam_init, b, t)
        gla_states.append(s_fin)
        xf = _out_call(xf, oa, gla, dif, (mods[l], 1), gn(3), w_oa, p["w_og"], p["w_od"], tm, tiles_per_mod)
        xf = _ffn_call(xf, (mods[l], 2), gn(4), gn(5), *ffn_w, l, 1, 0.5, tm, tiles_per_mod)
    shp = lambda nm, *tail: stacked[nm].reshape((depth, b, t) + tail)
    if stacked["ka"].ndim == 4:
        shp_a = lambda nm: jnp.transpose(stacked[nm].reshape(depth, b, N_HEADS, HD_A, t), (0, 1, 4, 2, 3))
    else:
        shp_a = lambda nm: shp(nm, N_HEADS, HD_A)
    states = [shp_a("ka"), shp_a("va"), shp("logf", N_HEADS), jnp.stack(gla_states),
              shp("kc", N_HEADS, 2 * HD_C), shp("vc", N_HEADS, 2 * HD_C)]
    return xf.reshape(b, t, D_MODEL), states


def kernel(x_prompt, x_sample, c_prompt, c_sample, cache_fox_k, cache_fox_v, cache_fox_logf, state_gla,
           cache_diff_k, cache_diff_v, w_ada, b_ada, g_norm, w_ffn_in, w_ffn_out, w_in, b_f, w_gla_up,
           b_gla_up, g_gla, g_diff, lam_params, w_out):
    depth = w_in.shape[0]
    bp, tp, _ = x_prompt.shape
    bs, ts, _ = x_sample.shape
    past = cache_fox_k.shape[2]

    params = [_layer_params(l, w_in, b_f, w_gla_up, b_gla_up, g_gla, g_diff, w_out) for l in range(depth)]
    ffn_w = (w_ffn_in.astype(BF16), w_ffn_out.astype(BF16))

    slopes = 2.0 ** (-8.0 * jnp.arange(1, N_HEADS + 1, dtype=F32) / N_HEADS)
    slopes = jnp.broadcast_to(slopes[:, None, None], (N_HEADS, 1, LANES))

    rows = bp + bs
    rows_pad = -(-rows // 8) * 8
    c_all = jnp.pad(jnp.concatenate([c_prompt, c_sample], axis=0), ((0, rows_pad - rows), (0, 0)))
    mods_p, mods_s = [], []
    for l in range(depth):
        mod = _ada_call(c_all, w_ada[l].astype(BF16), b_ada[l][None, :])
        mods_p.append(mod[:bp, None, :])
        mods_s.append(jnp.repeat(mod[bp:rows], ts, axis=0)[None])

    tm_p = _tile(tp, 512)
    tq = _tile(tp, 1024)
    tk = _tile(tp, 512)
    tg = _tile(tp, 512)

    def prompt_mixer(l, pr, p, lam_p, lam_init, b, t):
        r3 = lambda a: a.reshape(b, t, a.shape[-1])
        logf_t = jnp.swapaxes(r3(pr["logf"][l]), 1, 2).reshape(b * N_HEADS, t // LANES, LANES)
        c = _cumsum_call(logf_t).reshape(b, N_HEADS, t)
        nrm = jnp.sqrt(pr["norms"][:, 0, :N_HEADS * len(_NORM_ROWS)])
        nrm = dict(zip(_NORM_ROWS, jnp.moveaxis(nrm.reshape(b, t // tm_p, len(_NORM_ROWS), N_HEADS), (2, 3), (0, 2))))
        q_tiles = lambda a: _tile_max(a, tq // tm_p)
        k_tiles = lambda a: _tile_max(a, tk // tm_p)
        gap_a = (c[:, :, ::tq][..., None] - c[:, :, tk - 1::tk][:, :, None, :]) * LOG2E
        first_a = _first_tiles(q_tiles(nrm["qa_p"]), k_tiles(nrm["ka_p"]), gap_a, tq, tk, GROUP_HEADS)
        oa = _fox_call(first_a, r3(pr["qa_p"]), r3(pr["ka_p"]), r3(pr["va_p"]), c, tq, tk)
        s0_t = jnp.zeros((b, W_B, N_HEADS * DK_B), F32)
        gla, s_t = _gla_call(r3(pr["qb"]), r3(pr["kb"]), r3(pr["vb"]), r3(pr["loga"]), r3(pr["rb"]), s0_t,
                             p["g_gla"], tg, min(CHUNK, t), math.gcd(b, 2))
        last_key = jnp.arange(tk - 1, t, tk, dtype=F32)[None, :] - jnp.arange(0, t, tq, dtype=F32)[:, None]
        gap_c = slopes[:, 0, :1, None] * LOG2E * last_key[None]
        first_c = _first_tiles(q_tiles(jnp.maximum(nrm["qc1_p"], nrm["qc2_p"])), k_tiles(nrm["kc_b"]), gap_c, tq, tk,
                                 DIFF_GROUP)
        dif = _diff_call(first_c, r3(pr["qc1_p"]), r3(pr["qc2_p"]), r3(pr["kc_b"]), r3(pr["vc_b"]), slopes, lam_p,
                         p["g_diff"], lam_init, tq, tk)
        flat = lambda a: a.reshape(b * t, a.shape[-1])
        return flat(oa), flat(gla), flat(dif), _blockdiag_t_to_state(s_t), p["w_oa"]

    feature_time = lambda a: jnp.transpose(a, (0, 1, 3, 4, 2)).reshape(depth, bs, W_A, past)
    cache_kt, cache_vt = feature_time(cache_fox_k), feature_time(cache_fox_v)
    cache_dk = cache_diff_k.reshape(depth, bs, past * N_HEADS, HEAD_PAD)
    cache_dv = cache_diff_v.reshape(depth, bs, past * N_HEADS, HEAD_PAD)

    def sample_mixer(l, pr, p, lam_p, lam_init, b, t):
        r3 = lambda a: a.reshape(b, t, a.shape[-1])
        total = past + t
        nb = -(-total // (8 * LANES)) * 8
        lf = jnp.concatenate([cache_fox_logf[l].astype(F32), r3(pr["logf"][l])], axis=1)
        lf = jnp.pad(jnp.swapaxes(lf, 1, 2), ((0, 0), (0, 0), (0, nb * LANES - total)))
        c = _cumsum_call(lf.reshape(b * N_HEADS, nb, LANES)).reshape(b, N_HEADS, nb * LANES)
        oa = _fox_s_call(l, r3(pr["qa"]), pr["ka"], pr["va"], cache_kt, cache_vt, c)
        gla, s_t = _gla_call(r3(pr["qb"]), r3(pr["kb"]), r3(pr["vb"]), r3(pr["loga"]), r3(pr["rb"]),
                             _state_to_blockdiag_t(state_gla[l].astype(F32)), p["g_gla"], t, min(CHUNK, t),
                             math.gcd(b, 8))
        dif = _diff_s_call(l, r3(pr["qc1_p"]), r3(pr["qc2_p"]), pr["kc"], pr["vc"], cache_dk, cache_dv,
                           slopes, lam_p, p["g_diff"], lam_init)
        flat = lambda a: a.reshape(b * t, a.shape[-1])
        return flat(oa), flat(gla), flat(dif), _blockdiag_t_to_state(s_t), p["w_oa"]

    y_p, sp = _trunk(x_prompt, mods_p, params, ffn_w, g_norm, lam_params, prompt_mixer, tm_p, tp // tm_p, False)
    y_s, ss = _trunk(x_sample, mods_s, params, ffn_w, g_norm, lam_params, sample_mixer, bs * ts, 1, True)
    return (y_p, y_s, sp[0], sp[1], sp[2], sp[3], sp[4], sp[5], ss[0], ss[1], ss[2], ss[3], ss[4], ss[5])
```

```python
import functools
import math

import jax
import jax.numpy as jnp
import numpy as np
from jax import lax
from jax.experimental import pallas as pl
from jax.experimental.pallas import tpu as pltpu

F32 = jnp.float32
BF16 = jnp.bfloat16

D_MODEL = 1024
N_HEADS = 4
HD_A = 64
DK_B = 32
DV_B = 64
GLA_RANK = 16
GLA_TAU = 16.0
HD_C = 64
CHUNK = 64
W_A = N_HEADS * HD_A
W_B = N_HEADS * DV_B
W_C = N_HEADS * 2 * HD_C
D_FF = ((8 * D_MODEL // 3 + 255) // 256) * 256
N_SUB = 3
PROJ_SIZES = (W_A, W_A, W_A, N_HEADS, N_HEADS * DK_B, N_HEADS * DK_B, W_B, GLA_RANK, W_B, W_C, W_C, W_C)
NEG = -1e30
HIDDEN = -1e34
EPS = 1e-6
LOG2E = math.log2(math.e)

LANES = 128
HEAD_PAD = 128
W_PAD = N_HEADS * HEAD_PAD
VMEM_LIMIT = 56 * 1024 * 1024

_SEG_WIDTHS = (
    ("ka", W_A), ("va", W_A), ("kc", W_C), ("vc", W_C), ("qa", W_A),
    ("qc", W_C),
    ("fa", LANES), ("qb", LANES), ("kb", LANES), ("vb", W_B), ("rb", W_B), ("gb", LANES),
)
_SEG = {}
_off = 0
for _name, _w in _SEG_WIDTHS:
    _SEG[_name] = (_off, _off + _w)
    _off += _w
N_EXT = _off


def _nt_dot(a, b):
    return lax.dot_general(a, b, (((1,), (1,)), ((), ())), preferred_element_type=F32)


def _dot(a, b):
    return jnp.dot(a, b, preferred_element_type=F32)


def _split3(x):
    hi = x.astype(BF16)
    r1 = x - hi.astype(F32)
    mid = r1.astype(BF16)
    lo = (r1 - mid.astype(F32)).astype(BF16)
    return hi, mid, lo


def _dot_exact_rhs(x, m):
    mb = m.astype(BF16)
    hi, mid, lo = _split3(x)
    return _dot(hi, mb) + _dot(mid, mb) + _dot(lo, mb)


def _dot_exact_lhs(m, x):
    mb = m.astype(BF16)
    hi, mid, lo = _split3(x)
    return _dot(mb, hi) + _dot(mb, mid) + _dot(mb, lo)


def _rms(x, g):
    return x * lax.rsqrt(jnp.mean(x * x, axis=-1, keepdims=True) + EPS) * g


def _log_sigmoid(x):
    return jnp.minimum(x, 0.0) - jnp.log1p(jnp.exp(-jnp.abs(x)))


def _const_spec(shape):
    nd = len(shape)
    return pl.BlockSpec(shape, lambda *_: (0,) * nd, pipeline_mode=pl.Buffered(1))


def _params(sem):
    return pltpu.CompilerParams(dimension_semantics=sem, vmem_limit_bytes=VMEM_LIMIT)


def _ada_kernel(c_ref, w_ref, b_ref, o_ref):
    c = c_ref[...]
    a = (c * jax.nn.sigmoid(c)).astype(BF16)
    o_ref[...] = _dot(a, w_ref[...]) + b_ref[...]


def _ada_call(c, w, b):
    m, n = c.shape[0], w.shape[1]
    tn = n // 8
    return pl.pallas_call(
        _ada_kernel,
        grid=(n // tn,),
        in_specs=[pl.BlockSpec((m, D_MODEL), lambda i: (0, 0)),
                  pl.BlockSpec((D_MODEL, tn), lambda i: (0, i)),
                  pl.BlockSpec((1, tn), lambda i: (0, i))],
        out_specs=pl.BlockSpec((m, tn), lambda i: (0, i)),
        out_shape=jax.ShapeDtypeStruct((m, n), F32),
        compiler_params=_params(("parallel",)),
        name="ada",
    )(c, w, b)


FF_CHUNK = 256


def _ffn_kernel(x_ref, *refs, res_w, merge):
    x = x_ref[...]
    if merge:
        oa_ref, gla_ref, dif_ref, mgate_ref, mg_ref, woa_ref, wog_ref, wod_ref = refs[:8]
        refs = refs[8:]
        y = _dot(oa_ref[...], woa_ref[...]) + _dot(gla_ref[...], wog_ref[...]) + _dot(dif_ref[...], wod_ref[...])
        x = x + mgate_ref[0] * _rms(y, mg_ref[...])
    shift_ref, scale_ref, gate_ref, g1_ref, g2_ref, win_ref, wout_ref, o_ref = refs
    h = _rms(x, g1_ref[...]) * (1.0 + scale_ref[0]) + shift_ref[0]
    hb = h.astype(BF16)
    y = jnp.zeros(x.shape, F32)
    for c in range(D_FF // FF_CHUNK):
        lo, hi = c * FF_CHUNK, (c + 1) * FF_CHUNK
        g = _dot(hb, win_ref[:, lo:hi])
        u = _dot(hb, win_ref[:, D_FF + lo:D_FF + hi])
        a = (g * jax.nn.sigmoid(g) * u).astype(BF16)
        y = y + _dot(a, wout_ref[lo:hi, :])
    o_ref[...] = x + res_w * gate_ref[0] * _rms(y, g2_ref[...])


def _mod_spec(r, tiles_per_mod, col):
    return pl.BlockSpec((1, r, D_MODEL), lambda i: (i // tiles_per_mod, 0, col))


def _ffn_call(x, mod, g1, g2, w_in, w_out, layer, which, res_w, tm, tiles_per_mod, merge=None):
    n = x.shape[0]
    mod_all, sub = mod
    r = mod_all.shape[1]
    tok = pl.BlockSpec((tm, D_MODEL), lambda i: (i, 0))
    merge_specs, merge_args = [], []
    if merge is not None:
        oa, gla, dif, (mmod, msub), mg, w_oa, w_og, w_od = merge
        wide = lambda a: pl.BlockSpec((tm, a.shape[1]), lambda i: (i, 0))
        merge_specs = [wide(oa), wide(gla), wide(dif), _mod_spec(r, tiles_per_mod, 3 * msub + 2),
                       _const_spec((1, D_MODEL)), _const_spec(w_oa.shape), _const_spec(w_og.shape),
                       _const_spec(w_od.shape)]
        merge_args = [oa, gla, dif, mmod, mg, w_oa, w_og, w_od]
    pick = lambda rows, cols: pl.BlockSpec((None, None, rows, cols), lambda i: (layer, which, 0, 0),
                                           pipeline_mode=pl.Buffered(1))
    return pl.pallas_call(
        functools.partial(_ffn_kernel, res_w=res_w, merge=merge is not None),
        grid=(n // tm,),
        in_specs=[tok] + merge_specs + [_mod_spec(r, tiles_per_mod, 3 * sub + k) for k in range(3)]
        + [_const_spec((1, D_MODEL)), _const_spec((1, D_MODEL)), pick(D_MODEL, 2 * D_FF), pick(D_FF, D_MODEL)],
        out_specs=tok,
        out_shape=jax.ShapeDtypeStruct((n, D_MODEL), F32),
        compiler_params=_params(("parallel",)),
        name="ffn",
    )(x, *merge_args, mod_all, mod_all, mod_all, g1, g2, w_in, w_out)


_STACKED_OUTS = (
    ("ka", 1, W_A), ("va", 1, W_A), ("kc", N_HEADS, HEAD_PAD), ("vc", N_HEADS, HEAD_PAD), ("logf", 1, N_HEADS),
)
_LAYER_OUTS = (
    ("qa_p", W_PAD, BF16), ("ka_p", W_PAD, BF16), ("va_p", W_PAD, BF16),
    ("qc1_p", W_PAD, BF16), ("qc2_p", W_PAD, BF16), ("kc_b", W_C, BF16), ("vc_b", W_C, BF16),
    ("qb", LANES, F32), ("kb", LANES, F32), ("vb", W_B, F32), ("loga", LANES, F32), ("rb", W_B, F32),
)
_NORM_ROWS = ("qa_p", "ka_p", "qc1_p", "qc2_p", "kc_b")


def _proj_kernel(x_ref, shift_ref, scale_ref, g1_ref, w_ref, wkvt_ref, bf_ref, wup_ref, bup_ref, vone_ref, *refs,
                 n_alias, with_qa, transposed_a):
    refs = refs[n_alias:]
    names = [n for n, _, _ in _STACKED_OUTS] + [n for n, _, _ in _LAYER_OUTS] + (["qa"] if with_qa else [])
    o = dict(zip(names + ["norms"], refs))
    tm = x_ref.shape[0]
    x = x_ref[...]
    hb = (_rms(x, g1_ref[...]) * (1.0 + scale_ref[0]) + shift_ref[0]).astype(BF16)

    def seg(name):
        lo, hi = _SEG[name]
        return _dot(hb, w_ref[:, lo:hi])

    def store_heads(ref, val):
        for d in range(ref.shape[0]):
            for h in range(N_HEADS):
                ref[d, pl.ds(h, tm, stride=N_HEADS), :] = val[:, h * HEAD_PAD:(h + 1) * HEAD_PAD]

    def store_slots(ref, val):
        for d in range(ref.shape[0]):
            ref[d] = val

    if transposed_a:
        kvt = _nt_dot(wkvt_ref[...], hb)
        store_slots(o["ka"], kvt[None, :W_A])
        store_slots(o["va"], kvt[None, W_A:])
    else:
        store_slots(o["ka"], seg("ka"))
        store_slots(o["va"], seg("va"))
    kc = seg("kc")
    store_heads(o["kc"], kc)
    rounded = {"kc_b": kc.astype(BF16)}
    o["kc_b"][...] = rounded["kc_b"]
    vc = seg("vc")
    store_heads(o["vc"], vc)
    o["vc_b"][...] = vc.astype(BF16)
    def pad_heads(val):
        zero = jnp.zeros((tm, HEAD_PAD - HD_A), val.dtype)
        pieces = []
        for h in range(N_HEADS):
            pieces += [val[:, h * HD_A:(h + 1) * HD_A], zero]
        return jnp.concatenate(pieces, axis=1)

    qa = seg("qa")
    if with_qa:
        o["qa"][...] = qa * (HD_A ** -0.5)
    rounded["qa_p"] = pad_heads((qa * (HD_A ** -0.5 * LOG2E)).astype(BF16))
    rounded["ka_p"] = pad_heads(seg("ka").astype(BF16))
    o["qa_p"][...] = rounded["qa_p"]
    o["ka_p"][...] = rounded["ka_p"]
    o["va_p"][...] = (pad_heads(seg("va")) + vone_ref[...]).astype(BF16)
    qc = seg("qc") * (HD_C ** -0.5 * LOG2E)
    first_map = lax.broadcasted_iota(jnp.int32, qc.shape, 1) % HEAD_PAD < HD_C
    rounded["qc1_p"] = jnp.where(first_map, qc, 0.0).astype(BF16)
    rounded["qc2_p"] = jnp.where(first_map, 0.0, qc).astype(BF16)
    o["qc1_p"][...] = rounded["qc1_p"]
    o["qc2_p"][...] = rounded["qc2_p"]
    fa = seg("fa") + bf_ref[...]
    store_slots(o["logf"], _log_sigmoid(fa)[:, :N_HEADS])
    o["qb"][...] = seg("qb")
    o["kb"][...] = seg("kb")
    o["vb"][...] = seg("vb")
    o["rb"][...] = seg("rb")
    gb = seg("gb").astype(BF16)
    o["loga"][...] = _log_sigmoid(_dot(gb, wup_ref[...]) + bup_ref[...]) * (1.0 / GLA_TAU)
    lane = lax.broadcasted_iota(jnp.int32, (8, LANES), 1)
    tile_norms = jnp.zeros((8, LANES), F32)
    for a, name in enumerate(_NORM_ROWS):
        sq = rounded[name].astype(F32)
        sq = sq * sq
        for h in range(N_HEADS):
            ss = jnp.sum(sq[:, h * HEAD_PAD:(h + 1) * HEAD_PAD], axis=1, keepdims=True)
            tile_norms = jnp.where(lane == N_HEADS * a + h, jnp.max(ss, axis=0, keepdims=True), tile_norms)
    o["norms"][0] = tile_norms


def _proj_call(x, mod, g1, w_ext, w_kvt, bf, wup, bup, vone, tm, tiles_per_mod, layer, depth, stacked, with_qa,
               seq_len):
    n = x.shape[0]
    mod_all, sub = mod
    r = mod_all.shape[1]
    tok = lambda w: pl.BlockSpec((tm, w), lambda i: (i, 0))
    layer_outs = _LAYER_OUTS + ((("qa", W_A, F32),) if with_qa else ())
    alias_in = [] if stacked is None else [stacked[nm] for nm, _, _ in _STACKED_OUTS]
    transposed_a = seq_len % tm == 0
    tps = max(seq_len // tm, 1)
    stacked_specs, stacked_shapes = [], []
    slots = depth if stacked is None else 1
    for nm, rows, w in _STACKED_OUTS:
        if transposed_a and nm in ("ka", "va"):
            stacked_specs.append(pl.BlockSpec((slots, 1, w, tm), lambda i: (layer, i // tps, 0, i % tps)))
            stacked_shapes.append(jax.ShapeDtypeStruct((depth, n // seq_len, w, seq_len), F32))
        else:
            stacked_specs.append(pl.BlockSpec((slots, tm * rows, w), lambda i: (layer, i, 0)))
            stacked_shapes.append(jax.ShapeDtypeStruct((depth, n * rows, w), F32))
    n_fixed = 10
    outs = pl.pallas_call(
        functools.partial(_proj_kernel, n_alias=len(alias_in), with_qa=with_qa, transposed_a=transposed_a),
        grid=(n // tm,),
        in_specs=[tok(D_MODEL), _mod_spec(r, tiles_per_mod, 3 * sub), _mod_spec(r, tiles_per_mod, 3 * sub + 1),
                  _const_spec((1, D_MODEL)), _const_spec((D_MODEL, N_EXT)), _const_spec((2 * W_A, D_MODEL)),
                  _const_spec((1, LANES)), _const_spec((LANES, LANES)), _const_spec((1, LANES)),
                  _const_spec((1, W_PAD))]
        + [pl.BlockSpec(memory_space=pl.ANY)] * len(alias_in),
        out_specs=stacked_specs + [tok(w) for _, w, _ in layer_outs]
        + [pl.BlockSpec((1, 8, LANES), lambda i: (i, 0, 0))],
        out_shape=stacked_shapes + [jax.ShapeDtypeStruct((n, w), dt) for _, w, dt in layer_outs]
        + [jax.ShapeDtypeStruct((n // tm, 8, LANES), F32)],
        input_output_aliases={n_fixed + k: k for k in range(len(alias_in))},
        compiler_params=_params(("parallel",)),
        name="proj",
    )(x, mod_all, mod_all, g1, w_ext, w_kvt, bf, wup, bup, vone, *alias_in)
    names = [nm for nm, _, _ in _STACKED_OUTS] + [nm for nm, _, _ in layer_outs] + ["norms"]
    return dict(zip(names, outs))


def _cumsum_kernel(x_ref, o_ref):
    nb = x_ref.shape[1]
    r = lax.broadcasted_iota(jnp.int32, (LANES, LANES), 0)
    c = lax.broadcasted_iota(jnp.int32, (LANES, LANES), 1)
    within = (r <= c).astype(F32)
    rr = lax.broadcasted_iota(jnp.int32, (nb, nb), 0)
    cc = lax.broadcasted_iota(jnp.int32, (nb, nb), 1)
    before = (cc < rr).astype(F32)
    for s in range(x_ref.shape[0]):
        x = x_ref[s]
        local = _dot_exact_rhs(x, within)
        tot = jnp.broadcast_to(local[:, LANES - 1:LANES], (nb, LANES))
        o_ref[s] = local + _dot_exact_lhs(before, tot)


def _cumsum_call(x):
    rows, nb, _ = x.shape
    group = math.gcd(rows, 8)
    spec = pl.BlockSpec((group, nb, LANES), lambda i: (i, 0, 0))
    return pl.pallas_call(
        _cumsum_kernel, grid=(rows // group,), in_specs=[spec], out_specs=spec,
        out_shape=jax.ShapeDtypeStruct(x.shape, F32),
        compiler_params=_params(("parallel",)), name="cumsum",
    )(x)


def _lane_tile(x, width):
    return jnp.concatenate([x] * (width // LANES), axis=1)


GROUP_HEADS = 2
W_GROUP = GROUP_HEADS * HEAD_PAD


def _causal_sweep(i, first, tq, tk, tiles_per_iter, tile):
    r = tq // tk

    def body(jj, carry):
        for u in range(tiles_per_iter):
            tile(jj * tiles_per_iter + u, slice(0, tq), None)
        return carry

    assert r % tiles_per_iter == 0 and tiles_per_iter in (1, 2)
    lead = 0
    if tiles_per_iter == 2:
        lead = lax.rem(i * r - first, 2)

        @pl.when(lead == 1)
        def _():
            tile(first, slice(0, tq), None)

    lax.fori_loop(lax.div(first + lead, tiles_per_iter), lax.div(i * r, tiles_per_iter), body, 0)
    for d in range(r):
        tile(i * r + d, slice(d * tk, (d + 1) * tk), d * tk)
        if d + 1 < r:
            tile(i * r + d, slice((d + 1) * tk, tq), None)


def _fox_kernel(first_ref, q_ref, k_ref, v_ref, c_ref, o_ref, m_sc, acc_sc, *, tk, tiles_per_iter):
    i = pl.program_id(2)
    first = first_ref[pl.program_id(0), pl.program_id(1), i]
    tq = q_ref.shape[1]
    m_sc[...] = jnp.full(m_sc.shape, NEG, F32)
    acc_sc[...] = jnp.zeros(acc_sc.shape, F32)
    row = lax.broadcasted_iota(jnp.int32, (tk, tk), 0)
    col = lax.broadcasted_iota(jnp.int32, (tk, tk), 1)
    causal = col <= row

    def tile(j, rows, local):
        ks = pl.ds(pl.multiple_of(j * tk, tk), tk)
        for h in range(GROUP_HEADS):
            hs = slice(h * HEAD_PAD, (h + 1) * HEAD_PAD)
            c_first = c_ref[0, h, pl.ds(i * (tq // tk), 1), :][:, 0:1]
            t = _nt_dot(q_ref[0, rows, hs], k_ref[0, ks, hs]) + (c_first - c_ref[0, h, pl.ds(j, 1), :]) * LOG2E
            if local is not None:
                t = jnp.where(causal, t, NEG)
            m_old = m_sc[h, rows, :]
            m_new = jnp.maximum(m_old, jnp.max(t, axis=1, keepdims=True))
            p = jnp.exp2(t - _lane_tile(m_new, tk)).astype(BF16)
            acc_sc[h, rows, :] = jnp.exp2(m_old - m_new) * acc_sc[h, rows, :] + _dot(p, v_ref[0, ks, hs])
            m_sc[h, rows, :] = m_new

    _causal_sweep(i, first, tq, tk, tiles_per_iter, tile)
    outs = []
    for h in range(GROUP_HEADS):
        acc = acc_sc[h]
        outs.append((acc[:, :HD_A] / acc[:, HD_A:HD_A + 1]).astype(BF16))
    o_ref[0] = jnp.concatenate(outs, axis=1)


def _resident_spec(t, width):
    return pl.BlockSpec((1, t, width), lambda b_, g, i, first: (b_, 0, g), pipeline_mode=pl.Buffered(1))


def _fox_call(first, qa_p, ka_p, va_p, c, tq, tk):
    b, t, _ = qa_p.shape
    qspec = pl.BlockSpec((1, tq, W_GROUP), lambda b_, g, i, first: (b_, i, g))
    grid_spec = pltpu.PrefetchScalarGridSpec(
        num_scalar_prefetch=1,
        grid=(b, N_HEADS // GROUP_HEADS, t // tq),
        in_specs=[qspec, _resident_spec(t, W_GROUP), _resident_spec(t, W_GROUP),
                  pl.BlockSpec((1, GROUP_HEADS, t // tk, tk), lambda b_, g, i, first: (b_, g, 0, 0))],
        out_specs=pl.BlockSpec((1, tq, GROUP_HEADS * HD_A), lambda b_, g, i, first: (b_, i, g)),
        scratch_shapes=[pltpu.VMEM((GROUP_HEADS, tq, LANES), F32), pltpu.VMEM((GROUP_HEADS, tq, HEAD_PAD), F32)])
    return pl.pallas_call(
        functools.partial(_fox_kernel, tk=tk, tiles_per_iter=1),
        grid_spec=grid_spec,
        out_shape=jax.ShapeDtypeStruct((b, t, W_A), BF16),
        compiler_params=_params(("parallel", "parallel", "arbitrary")),
        name="fox",
    )(first, qa_p, ka_p, va_p, c.reshape(b, N_HEADS, t // tk, tk))


def _diff_lambda(lp, lam_init):
    a = jnp.sum(lp[0:1] * lp[1:2], axis=1, keepdims=True)
    b = jnp.sum(lp[2:3] * lp[3:4], axis=1, keepdims=True)
    return jnp.exp(a) - jnp.exp(b) + lam_init


def _diag_distance(tk):
    q = np.arange(tk)[:, None]
    k = np.arange(tk)[None, :]
    return np.where(k // CHUNK <= q // CHUNK, 2 * np.minimum(q, k) - k, HIDDEN).astype(np.float32)


DIFF_GROUP = 1
W_DIFF_GROUP = DIFF_GROUP * HEAD_PAD


def _diff_kernel(first_ref, q1_ref, q2_ref, k_ref, v_ref, slope_ref, lam_ref, g_ref, rel_ref, o_ref,
                 m_sc, l_sc, a_sc, *, tk, tiles_per_iter, lam_init):
    g = pl.program_id(1)
    i = pl.program_id(2)
    first = first_ref[pl.program_id(0), g, i]
    tq = q1_ref.shape[1]
    m_sc[...] = jnp.full(m_sc.shape, NEG, F32)
    l_sc[...] = jnp.zeros(l_sc.shape, F32)
    a_sc[...] = jnp.zeros(a_sc.shape, F32)
    lane = lax.broadcasted_iota(jnp.int32, (1, tk), 1)

    def tile(j, rows, local):
        ks = pl.ds(pl.multiple_of(j * tk, tk), tk)
        if local is None:
            rel = (lane + (j * tk - i * tq)).astype(F32)
        else:
            rel = rel_ref[...] + float(local)
        for h in range(DIFF_GROUP):
            hs = slice(h * HEAD_PAD, (h + 1) * HEAD_PAD)
            bias = (slope_ref[g * DIFF_GROUP + h][:, 0:1] * LOG2E) * rel
            k = k_ref[0, ks, hs]
            v = v_ref[0, ks, hs]
            nrows = rows.stop - rows.start
            scores = _nt_dot(jnp.concatenate([q1_ref[0, rows, hs], q2_ref[0, rows, hs]], axis=0), k)
            probs, alphas = [], []
            for m in range(2):
                n = 2 * h + m
                t = scores[m * nrows:(m + 1) * nrows] + bias
                m_old = m_sc[n, rows, :]
                m_new = jnp.maximum(m_old, jnp.max(t, axis=1, keepdims=True))
                alpha = jnp.exp2(m_old - m_new)
                p = jnp.exp2(t - _lane_tile(m_new, tk))
                l_sc[n, rows, :] = alpha * l_sc[n, rows, :] + jnp.sum(p, axis=1, keepdims=True)
                m_sc[n, rows, :] = m_new
                probs.append(p.astype(BF16))
                alphas.append(alpha)
            pv = _dot(jnp.concatenate(probs, axis=0), v)
            for m in range(2):
                n = 2 * h + m
                a_sc[n, rows, :] = alphas[m] * a_sc[n, rows, :] + pv[m * nrows:(m + 1) * nrows]

    _causal_sweep(i, first, tq, tk, tiles_per_iter, tile)
    lam = _diff_lambda(lam_ref[...], lam_init)
    for h in range(DIFF_GROUP):
        out = a_sc[2 * h] / l_sc[2 * h] - lam * (a_sc[2 * h + 1] / l_sc[2 * h + 1])
        o_ref[0, :, h * HEAD_PAD:(h + 1) * HEAD_PAD] = (_rms(out, g_ref[...]) * (1.0 - lam_init)).astype(BF16)


def _diff_call(first, qc1_p, qc2_p, kc_b, vc_b, slopes, lam_p, g_diff, lam_init, tq, tk):
    assert CHUNK == 64 and tk % CHUNK == 0
    b, t, _ = qc1_p.shape
    qspec = pl.BlockSpec((1, tq, W_DIFF_GROUP), lambda b_, g, i, first: (b_, i, g))
    stat = pltpu.VMEM((2 * DIFF_GROUP, tq, LANES), F32)
    grid_spec = pltpu.PrefetchScalarGridSpec(
        num_scalar_prefetch=1,
        grid=(b, N_HEADS // DIFF_GROUP, t // tq),
        in_specs=[qspec, qspec, _resident_spec(t, W_DIFF_GROUP), _resident_spec(t, W_DIFF_GROUP),
                  pl.BlockSpec((N_HEADS, 1, LANES), lambda b_, g, i, first: (0, 0, 0)),
                  pl.BlockSpec((4, HD_C), lambda b_, g, i, first: (0, 0)),
                  pl.BlockSpec((1, HEAD_PAD), lambda b_, g, i, first: (0, 0)),
                  pl.BlockSpec((tk, tk), lambda b_, g, i, first: (0, 0), pipeline_mode=pl.Buffered(1))],
        out_specs=qspec,
        scratch_shapes=[stat, stat, pltpu.VMEM((2 * DIFF_GROUP, tq, HEAD_PAD), F32)])
    return pl.pallas_call(
        functools.partial(_diff_kernel, tk=tk, tiles_per_iter=2, lam_init=lam_init),
        grid_spec=grid_spec,
        out_shape=jax.ShapeDtypeStruct((b, t, W_C), BF16),
        compiler_params=_params(("parallel", "parallel", "arbitrary")),
        name="diff",
    )(first, qc1_p, qc2_p, kc_b, vc_b, slopes, lam_p, g_diff, jnp.asarray(_diag_distance(tk)))


def _gla_kernel(q_ref, k_ref, v_ref, la_ref, r_ref, s0_ref, g_ref, o_ref, s_ref, *, chunk):
    ti = pl.program_id(1)
    n_chunks = q_ref.shape[1] // chunk
    n_seq = q_ref.shape[0]

    @pl.when(ti == 0)
    def _():
        s_ref[...] = s0_ref[...]

    tri_r = lax.broadcasted_iota(jnp.int32, (chunk, chunk), 0)
    tri_c = lax.broadcasted_iota(jnp.int32, (chunk, chunk), 1)
    incl = (tri_c <= tri_r).astype(F32)
    srow = lax.broadcasted_iota(jnp.int32, (N_HEADS * chunk, LANES), 0) // chunk
    slane = lax.broadcasted_iota(jnp.int32, (N_HEADS * chunk, LANES), 1) // DK_B
    qmask = srow == slane
    arow = lax.broadcasted_iota(jnp.int32, (N_HEADS * chunk, chunk), 0) % chunk
    acol = lax.broadcasted_iota(jnp.int32, (N_HEADS * chunk, chunk), 1)
    causal = acol <= arow
    vlane = lax.broadcasted_iota(jnp.int32, (chunk, W_B), 1) // DV_B
    st_row = lax.broadcasted_iota(jnp.int32, (W_B, LANES), 0) // DV_B
    st_lane = lax.broadcasted_iota(jnp.int32, (W_B, LANES), 1) // DK_B
    diag_blocks = st_row == st_lane

    for c, sq in [(c, sq) for c in range(n_chunks) for sq in range(n_seq)]:
        sl = slice(c * chunk, (c + 1) * chunk)
        cb = _dot_exact_lhs(incl, la_ref[sq, sl, :])
        cb_last = cb[chunk - 1:chunk, :]
        q = q_ref[sq, sl, :] * (DK_B ** -0.5)
        k = k_ref[sq, sl, :]
        v = v_ref[sq, sl, :]
        vb = v.astype(BF16)
        qe = q * jnp.exp(cb)
        ke = (k * jnp.exp(-cb)).astype(BF16)
        kl = (k * jnp.exp(cb_last - cb)).astype(BF16)
        decay = jnp.exp(cb_last)
        qstack = jnp.where(qmask, jnp.concatenate([qe] * N_HEADS, axis=0), 0.0).astype(BF16)
        a = jnp.where(causal, _nt_dot(qstack, ke), 0.0).astype(BF16)
        oi = _dot(a, vb)
        o = jnp.zeros((chunk, W_B), F32)
        for h in range(N_HEADS):
            o = o + jnp.where(vlane == h, oi[h * chunk:(h + 1) * chunk], 0.0)
        st = s_ref[sq]
        o = o + _nt_dot(qe.astype(BF16), st.astype(BF16))
        upd = _dot(v.T.astype(BF16), kl)
        s_ref[sq] = decay * st + jnp.where(diag_blocks, upd, 0.0)
        o2 = o * o
        ms = jnp.zeros((chunk, W_B), F32)
        for h in range(N_HEADS):
            head_mean = jnp.sum(jnp.where(vlane == h, o2, 0.0), axis=1, keepdims=True) * (1.0 / DV_B)
            ms = jnp.where(vlane == h, head_mean, ms)
        r = r_ref[sq, sl, :]
        o_ref[sq, sl, :] = (o * lax.rsqrt(ms + EPS) * g_ref[...] * (r * jax.nn.sigmoid(r))).astype(BF16)


def _gla_call(qb, kb, vb, loga, rb, s0_t, g_gla, tg, chunk, group):
    b, t, _ = qb.shape
    tok = lambda w: pl.BlockSpec((group, tg, w), lambda b_, i: (b_, i, 0))
    sspec = pl.BlockSpec((group, W_B, LANES), lambda b_, i: (b_, 0, 0))
    return pl.pallas_call(
        functools.partial(_gla_kernel, chunk=chunk),
        grid=(b // group, t // tg),
        in_specs=[tok(LANES), tok(LANES), tok(W_B), tok(LANES), tok(W_B), sspec,
                  pl.BlockSpec((1, W_B), lambda b_, i: (0, 0))],
        out_specs=[tok(W_B), sspec],
        out_shape=[jax.ShapeDtypeStruct((b, t, W_B), BF16), jax.ShapeDtypeStruct((b, W_B, LANES), F32)],
        compiler_params=_params(("parallel", "arbitrary")),
        name="gla",
    )(qb, kb, vb, loga, rb, s0_t, g_gla)


def _softmax_two(t_past, t_new, exp_fn):
    m = jnp.maximum(jnp.max(t_past, axis=1, keepdims=True), jnp.max(t_new, axis=1, keepdims=True))
    p_past = exp_fn(t_past - m)
    p_new = exp_fn(t_new - m)
    l = jnp.sum(p_past, axis=1, keepdims=True) + jnp.sum(p_new, axis=1, keepdims=True)
    return p_past, p_new, l


def _fox_s_kernel(q_ref, kn_ref, vn_ref, kpt_ref, vpt_ref, c_ref, o_ref):
    t, past = q_ref.shape[1], kpt_ref.shape[1]
    q = q_ref[0]
    kn = kn_ref[...].astype(BF16)
    vn = vn_ref[...].astype(BF16)
    kpt = kpt_ref[...].astype(BF16)
    vpt = vpt_ref[...].astype(BF16)
    lane_head = lax.broadcasted_iota(jnp.int32, (t, W_A), 1) // HD_A
    row = lax.broadcasted_iota(jnp.int32, (t, t), 0)
    col = lax.broadcasted_iota(jnp.int32, (t, t), 1)
    o = jnp.zeros((t, W_A), F32)
    for h in range(N_HEADS):
        qh = jnp.where(lane_head == h, q, 0.0).astype(BF16)
        ch = c_ref[0, h:h + 1, :]
        cref = ch[:, past:past + 1]
        t_past = _dot(qh, kpt) + (cref - ch[:, :past])
        t_new = jnp.where(col <= row, _nt_dot(qh, kn) + (cref - ch[:, past:past + t]), NEG)
        p_past, p_new, l = _softmax_two(t_past, t_new, jnp.exp)
        oh = (_nt_dot(p_past.astype(BF16), vpt) + _dot(p_new.astype(BF16), vn)) / l
        o = o + jnp.where(lane_head == h, oh, 0.0)
    o_ref[0] = o.astype(BF16)


def _fox_s_call(layer, qa, ka, va, cache_kt, cache_vt, c):
    b, t, _ = qa.shape
    past = cache_kt.shape[3]
    new = pl.BlockSpec((1, t, W_A), lambda i: (i, 0, 0))
    new_l = lambda: pl.BlockSpec((None, t, W_A), lambda i: (layer, i, 0))
    old_l = lambda: pl.BlockSpec((None, None, W_A, past), lambda i: (layer, i, 0, 0))
    return pl.pallas_call(
        _fox_s_kernel,
        grid=(b,),
        in_specs=[new, new_l(), new_l(), old_l(), old_l(),
                  pl.BlockSpec((1, N_HEADS, c.shape[2]), lambda i: (i, 0, 0))],
        out_specs=new,
        out_shape=jax.ShapeDtypeStruct((b, t, W_A), BF16),
        compiler_params=_params(("parallel",)),
        name="fox_s",
    )(qa, ka, va, cache_kt, cache_vt, c)


def _diff_s_kernel(q1_ref, q2_ref, kn_ref, vn_ref, kp_ref, vp_ref, slope_ref, lam_ref, g_ref, o_ref, *, lam_init):
    t, past = q1_ref.shape[1], kp_ref.shape[0] // N_HEADS
    lam = _diff_lambda(lam_ref[...], lam_init)
    row = lax.broadcasted_iota(jnp.int32, (t, t), 0) + past
    col = lax.broadcasted_iota(jnp.int32, (t, t), 1) + past
    visible_new = lax.shift_right_logical(col, 6) <= lax.shift_right_logical(row, 6)
    dist_new = jnp.abs(row - col).astype(F32)
    prow = lax.broadcasted_iota(jnp.int32, (t, past), 0) + past
    pcol = lax.broadcasted_iota(jnp.int32, (t, past), 1)
    visible_past = lax.shift_right_logical(pcol, 6) <= lax.shift_right_logical(prow, 6)
    dist_past = (prow - pcol).astype(F32)
    for h in range(N_HEADS):
        hs = slice(h * HEAD_PAD, (h + 1) * HEAD_PAD)
        slope = slope_ref[h][:, 0:1] * LOG2E
        kn = kn_ref[pl.ds(h, t, stride=N_HEADS), :].astype(BF16)
        vn = vn_ref[pl.ds(h, t, stride=N_HEADS), :].astype(BF16)
        kp = kp_ref[pl.ds(h, past, stride=N_HEADS), :].astype(BF16)
        vp = vp_ref[pl.ds(h, past, stride=N_HEADS), :].astype(BF16)
        outs = []
        for q_ref in (q1_ref, q2_ref):
            q = q_ref[0, :, hs]
            t_past = jnp.where(visible_past, _nt_dot(q, kp) - slope * dist_past, NEG)
            t_new = jnp.where(visible_new, _nt_dot(q, kn) - slope * dist_new, NEG)
            p_past, p_new, l = _softmax_two(t_past, t_new, jnp.exp2)
            outs.append((_dot(p_past.astype(BF16), vp) + _dot(p_new.astype(BF16), vn)) / l)
        out = outs[0] - lam * outs[1]
        o_ref[0, :, hs] = (_rms(out, g_ref[...]) * (1.0 - lam_init)).astype(BF16)


def _diff_s_call(layer, qc1_p, qc2_p, kc, vc, cache_k, cache_v, slopes, lam_p, g_diff, lam_init):
    assert CHUNK == 64
    b, t, _ = qc1_p.shape
    rows_past = cache_k.shape[2]
    new = lambda: pl.BlockSpec((1, t, W_C), lambda i: (i, 0, 0))
    new_l = lambda: pl.BlockSpec((None, t * N_HEADS, HEAD_PAD), lambda i: (layer, i, 0))
    old_l = lambda: pl.BlockSpec((None, None, rows_past, HEAD_PAD), lambda i: (layer, i, 0, 0))
    return pl.pallas_call(
        functools.partial(_diff_s_kernel, lam_init=lam_init),
        grid=(b,),
        in_specs=[new(), new(), new_l(), new_l(), old_l(), old_l(),
                  pl.BlockSpec((N_HEADS, 1, LANES), lambda i: (0, 0, 0)),
                  pl.BlockSpec((4, HD_C), lambda i: (0, 0)),
                  pl.BlockSpec((1, HEAD_PAD), lambda i: (0, 0))],
        out_specs=new(),
        out_shape=jax.ShapeDtypeStruct((b, t, W_C), BF16),
        compiler_params=_params(("parallel",)),
        name="diff_s",
    )(qc1_p, qc2_p, kc, vc, cache_k, cache_v, slopes, lam_p, g_diff)


def _pad_lanes(w, width=LANES):
    return jnp.pad(w, ((0, 0), (0, width - w.shape[1])))


def _layer_params(l, w_in, b_f, w_gla_up, b_gla_up, g_gla, g_diff, w_out):
    offs = [int(o) for o in np.cumsum(PROJ_SIZES)[:-1]]
    qa, ka, va, fa, qb, kb, vb, gb, rb, qc, kc, vc = jnp.split(w_in[l], offs, axis=1)
    segs = {
        "ka": ka, "va": va, "kc": kc, "vc": vc, "qa": qa, "qc": qc,
        "fa": _pad_lanes(fa), "qb": qb, "kb": kb, "vb": vb, "rb": rb, "gb": _pad_lanes(gb),
    }
    w_ext = jnp.concatenate([segs[n] for n, _ in _SEG_WIDTHS], axis=1).astype(BF16)
    wo = w_out[l]
    vone = np.zeros((1, W_PAD), np.float32)
    vone[0, HD_A::HEAD_PAD] = 1.0
    return dict(
        w_ext=w_ext, w_kvt=jnp.concatenate([ka, va], axis=1).T.astype(BF16),
        bf=_pad_lanes(b_f[l][None, :]),
        wup=jnp.pad(w_gla_up[l], ((0, LANES - GLA_RANK), (0, 0))).astype(BF16),
        bup=b_gla_up[l][None, :],
        vone=jnp.asarray(vone),
        g_gla=jnp.tile(g_gla[l], N_HEADS)[None, :], g_diff=g_diff[l][None, :],
        w_oa=wo[:W_A].astype(BF16),
        w_og=wo[W_A:W_A + W_B].astype(BF16), w_od=wo[W_A + W_B:].astype(BF16),
    )


def _state_to_blockdiag_t(s):
    eye = jnp.eye(N_HEADS, dtype=s.dtype)
    return jnp.einsum("bhkv,hg->bhvgk", s, eye).reshape(s.shape[0], W_B, N_HEADS * DK_B)


def _blockdiag_t_to_state(st):
    b = st.shape[0]
    s5 = st.reshape(b, N_HEADS, DV_B, N_HEADS, DK_B)
    diag = jnp.stack([s5[:, h, :, h, :] for h in range(N_HEADS)], axis=1)
    return jnp.swapaxes(diag, 2, 3)


SKIP_LOG2 = 160.0
NORM_SLACK = 1.01


def _tile_max(x, factor):
    return jnp.max(x.reshape(x.shape[:-1] + (x.shape[-1] // factor, factor)), axis=-1)


def _first_tiles(qn, kn, gap, tq, tk, group):
    b, h, nq = qn.shape
    r = tq // tk
    kn_diag = _tile_max(kn, r)
    bound = NORM_SLACK * (qn[..., None] * kn[:, :, None, :] + (qn * kn_diag)[..., None]) + gap
    dead = (bound < -SKIP_LOG2).reshape(b, h // group, group, nq, kn.shape[-1]).all(axis=2)
    lead = jnp.sum(jnp.cumprod(dead.astype(jnp.int32), axis=-1), axis=-1)
    return jnp.minimum(lead, jnp.arange(nq, dtype=jnp.int32) * r).astype(jnp.int32)


def _tile(n, pref):
    t = min(n, pref)
    assert n % t == 0, (n, t)
    return t


def _trunk(x, mods, params, ffn_w, g_norm, lam_params, mixer, tm, tiles_per_mod, with_qa):
    b, t, _ = x.shape
    n = b * t
    depth = len(params)
    xf = x.reshape(n, D_MODEL)
    stacked = None
    gla_states = []
    for l in range(depth):
        p = params[l]
        gn = lambda k: g_norm[l, k][None, :]
        xf = _ffn_call(xf, (mods[l], 0), gn(0), gn(1), *ffn_w, l, 0, 0.5, tm, tiles_per_mod)
        pr = _proj_call(xf, (mods[l], 1), gn(2), p["w_ext"], p["w_kvt"], p["bf"], p["wup"], p["bup"], p["vone"], tm,
                        tiles_per_mod, l, depth, stacked, with_qa, t)
        stacked = {nm: pr[nm] for nm, _, _ in _STACKED_OUTS}
        lam_init = 0.8 - 0.6 * math.exp(-0.3 * l)
        oa, gla, dif, s_fin, w_oa = mixer(l, pr, p, lam_params[l], lam_init, b, t)
        gla_states.append(s_fin)
        xf = _ffn_call(xf, (mods[l], 2), gn(4), gn(5), *ffn_w, l, 1, 0.5, tm, tiles_per_mod,
                       merge=(oa, gla, dif, (mods[l], 1), gn(3), w_oa, p["w_og"], p["w_od"]))
    shp = lambda nm, *tail: stacked[nm].reshape((depth, b, t) + tail)
    if stacked["ka"].ndim == 4:
        shp_a = lambda nm: jnp.transpose(stacked[nm].reshape(depth, b, N_HEADS, HD_A, t), (0, 1, 4, 2, 3))
    else:
        shp_a = lambda nm: shp(nm, N_HEADS, HD_A)
    states = [shp_a("ka"), shp_a("va"), shp("logf", N_HEADS), jnp.stack(gla_states),
              shp("kc", N_HEADS, 2 * HD_C), shp("vc", N_HEADS, 2 * HD_C)]
    return xf.reshape(b, t, D_MODEL), states


def kernel(x_prompt, x_sample, c_prompt, c_sample, cache_fox_k, cache_fox_v, cache_fox_logf, state_gla,
           cache_diff_k, cache_diff_v, w_ada, b_ada, g_norm, w_ffn_in, w_ffn_out, w_in, b_f, w_gla_up,
           b_gla_up, g_gla, g_diff, lam_params, w_out):
    depth = w_in.shape[0]
    bp, tp, _ = x_prompt.shape
    bs, ts, _ = x_sample.shape
    past = cache_fox_k.shape[2]

    params = [_layer_params(l, w_in, b_f, w_gla_up, b_gla_up, g_gla, g_diff, w_out) for l in range(depth)]
    ffn_w = (w_ffn_in.astype(BF16), w_ffn_out.astype(BF16))

    slopes = 2.0 ** (-8.0 * jnp.arange(1, N_HEADS + 1, dtype=F32) / N_HEADS)
    slopes = jnp.broadcast_to(slopes[:, None, None], (N_HEADS, 1, LANES))

    rows = bp + bs
    rows_pad = -(-rows // 8) * 8
    c_all = jnp.pad(jnp.concatenate([c_prompt, c_sample], axis=0), ((0, rows_pad - rows), (0, 0)))
    mods_p, mods_s = [], []
    for l in range(depth):
        mod = _ada_call(c_all, w_ada[l].astype(BF16), b_ada[l][None, :])
        mods_p.append(mod[:bp, None, :])
        mods_s.append(jnp.repeat(mod[bp:rows], ts, axis=0)[None])

    tm_p = _tile(tp, 512)
    tq = _tile(tp, 1024)
    tk = _tile(tp, 512)
    tg = _tile(tp, 512)

    def prompt_mixer(l, pr, p, lam_p, lam_init, b, t):
        r3 = lambda a: a.reshape(b, t, a.shape[-1])
        logf_t = jnp.swapaxes(r3(pr["logf"][l]), 1, 2).reshape(b * N_HEADS, t // LANES, LANES)
        c = _cumsum_call(logf_t).reshape(b, N_HEADS, t)
        nrm = jnp.sqrt(pr["norms"][:, 0, :N_HEADS * len(_NORM_ROWS)])
        nrm = dict(zip(_NORM_ROWS, jnp.moveaxis(nrm.reshape(b, t // tm_p, len(_NORM_ROWS), N_HEADS), (2, 3), (0, 2))))
        q_tiles = lambda a: _tile_max(a, tq // tm_p)
        k_tiles = lambda a: _tile_max(a, tk // tm_p)
        gap_a = (c[:, :, ::tq][..., None] - c[:, :, tk - 1::tk][:, :, None, :]) * LOG2E
        first_a = _first_tiles(q_tiles(nrm["qa_p"]), k_tiles(nrm["ka_p"]), gap_a, tq, tk, GROUP_HEADS)
        oa = _fox_call(first_a, r3(pr["qa_p"]), r3(pr["ka_p"]), r3(pr["va_p"]), c, tq, tk)
        s0_t = jnp.zeros((b, W_B, N_HEADS * DK_B), F32)
        gla, s_t = _gla_call(r3(pr["qb"]), r3(pr["kb"]), r3(pr["vb"]), r3(pr["loga"]), r3(pr["rb"]), s0_t,
                             p["g_gla"], tg, min(CHUNK, t), math.gcd(b, 2))
        last_key = jnp.arange(tk - 1, t, tk, dtype=F32)[None, :] - jnp.arange(0, t, tq, dtype=F32)[:, None]
        gap_c = slopes[:, 0, :1, None] * LOG2E * last_key[None]
        first_c = _first_tiles(q_tiles(jnp.maximum(nrm["qc1_p"], nrm["qc2_p"])), k_tiles(nrm["kc_b"]), gap_c, tq, tk,
                                 DIFF_GROUP)
        dif = _diff_call(first_c, r3(pr["qc1_p"]), r3(pr["qc2_p"]), r3(pr["kc_b"]), r3(pr["vc_b"]), slopes, lam_p,
                         p["g_diff"], lam_init, tq, tk)
        flat = lambda a: a.reshape(b * t, a.shape[-1])
        return flat(oa), flat(gla), flat(dif), _blockdiag_t_to_state(s_t), p["w_oa"]

    feature_time = lambda a: jnp.transpose(a, (0, 1, 3, 4, 2)).reshape(depth, bs, W_A, past)
    cache_kt, cache_vt = feature_time(cache_fox_k), feature_time(cache_fox_v)
    cache_dk = cache_diff_k.reshape(depth, bs, past * N_HEADS, HEAD_PAD)
    cache_dv = cache_diff_v.reshape(depth, bs, past * N_HEADS, HEAD_PAD)

    def sample_mixer(l, pr, p, lam_p, lam_init, b, t):
        r3 = lambda a: a.reshape(b, t, a.shape[-1])
        total = past + t
        nb = -(-total // (8 * LANES)) * 8
        lf = jnp.concatenate([cache_fox_logf[l].astype(F32), r3(pr["logf"][l])], axis=1)
        lf = jnp.pad(jnp.swapaxes(lf, 1, 2), ((0, 0), (0, 0), (0, nb * LANES - total)))
        c = _cumsum_call(lf.reshape(b * N_HEADS, nb, LANES)).reshape(b, N_HEADS, nb * LANES)
        oa = _fox_s_call(l, r3(pr["qa"]), pr["ka"], pr["va"], cache_kt, cache_vt, c)
        gla, s_t = _gla_call(r3(pr["qb"]), r3(pr["kb"]), r3(pr["vb"]), r3(pr["loga"]), r3(pr["rb"]),
                             _state_to_blockdiag_t(state_gla[l].astype(F32)), p["g_gla"], t, min(CHUNK, t),
                             math.gcd(b, 8))
        dif = _diff_s_call(l, r3(pr["qc1_p"]), r3(pr["qc2_p"]), pr["kc"], pr["vc"], cache_dk, cache_dv,
                           slopes, lam_p, p["g_diff"], lam_init)
        flat = lambda a: a.reshape(b * t, a.shape[-1])
        return flat(oa), flat(gla), flat(dif), _blockdiag_t_to_state(s_t), p["w_oa"]

    y_p, sp = _trunk(x_prompt, mods_p, params, ffn_w, g_norm, lam_params, prompt_mixer, tm_p, tp // tm_p, False)
    y_s, ss = _trunk(x_sample, mods_s, params, ffn_w, g_norm, lam_params, sample_mixer, bs * ts, 1, True)
    return (y_p, y_s, sp[0], sp[1], sp[2], sp[3], sp[4], sp[5], ss[0], ss[1], ss[2], ss[3], ss[4], ss[5])
```

```python
import functools
import math

import jax
import jax.numpy as jnp
import numpy as np
from jax import lax
from jax.experimental import pallas as pl
from jax.experimental.pallas import tpu as pltpu

F32 = jnp.float32
BF16 = jnp.bfloat16

D_MODEL = 1024
N_HEADS = 4
HD_A = 64
DK_B = 32
DV_B = 64
GLA_RANK = 16
GLA_TAU = 16.0
HD_C = 64
CHUNK = 64
W_A = N_HEADS * HD_A
W_B = N_HEADS * DV_B
W_C = N_HEADS * 2 * HD_C
D_FF = ((8 * D_MODEL // 3 + 255) // 256) * 256
N_SUB = 3
PROJ_SIZES = (W_A, W_A, W_A, N_HEADS, N_HEADS * DK_B, N_HEADS * DK_B, W_B, GLA_RANK, W_B, W_C, W_C, W_C)
NEG = -1e30
HIDDEN = -1e34
EPS = 1e-6
LOG2E = math.log2(math.e)

LANES = 128
HEAD_PAD = 128
W_PAD = N_HEADS * HEAD_PAD
VMEM_LIMIT = 56 * 1024 * 1024

_SEG_WIDTHS = (
    ("ka", W_A), ("va", W_A), ("kc", W_C), ("vc", W_C), ("qa", W_A),
    ("qc", W_C),
    ("fa", LANES), ("qb", LANES), ("kb", LANES), ("vb", W_B), ("rb", W_B), ("gb", LANES),
)
_SEG = {}
_off = 0
for _name, _w in _SEG_WIDTHS:
    _SEG[_name] = (_off, _off + _w)
    _off += _w
N_EXT = _off


def _nt_dot(a, b):
    return lax.dot_general(a, b, (((1,), (1,)), ((), ())), preferred_element_type=F32)


def _dot(a, b):
    return jnp.dot(a, b, preferred_element_type=F32)


def _split3(x):
    hi = x.astype(BF16)
    r1 = x - hi.astype(F32)
    mid = r1.astype(BF16)
    lo = (r1 - mid.astype(F32)).astype(BF16)
    return hi, mid, lo


def _dot_exact_rhs(x, m):
    mb = m.astype(BF16)
    hi, mid, lo = _split3(x)
    return _dot(hi, mb) + _dot(mid, mb) + _dot(lo, mb)


def _dot_exact_lhs(m, x):
    mb = m.astype(BF16)
    hi, mid, lo = _split3(x)
    return _dot(mb, hi) + _dot(mb, mid) + _dot(mb, lo)


def _rms(x, g):
    return x * lax.rsqrt(jnp.mean(x * x, axis=-1, keepdims=True) + EPS) * g


def _log_sigmoid(x):
    return jnp.minimum(x, 0.0) - jnp.log1p(jnp.exp(-jnp.abs(x)))


def _const_spec(shape):
    nd = len(shape)
    return pl.BlockSpec(shape, lambda *_: (0,) * nd, pipeline_mode=pl.Buffered(1))


def _params(sem):
    return pltpu.CompilerParams(dimension_semantics=sem, vmem_limit_bytes=VMEM_LIMIT)


def _ada_kernel(c_ref, w_ref, b_ref, o_ref):
    c = c_ref[...]
    a = (c * jax.nn.sigmoid(c)).astype(BF16)
    o_ref[...] = _dot(a, w_ref[...].astype(BF16)) + b_ref[...]


def _ada_call(c, w, b, layer):
    m, n = c.shape[0], w.shape[2]
    assert n == N_SUB * 3 * D_MODEL
    tn = n // 8
    return pl.pallas_call(
        _ada_kernel,
        grid=(n // tn,),
        in_specs=[pl.BlockSpec((m, D_MODEL), lambda i: (0, 0)),
                  pl.BlockSpec((None, D_MODEL, tn), lambda i: (layer, 0, i)),
                  pl.BlockSpec((None, 1, tn), lambda i: (layer, 0, i))],
        out_specs=pl.BlockSpec((m, tn), lambda i: (0, i)),
        out_shape=jax.ShapeDtypeStruct((m, n), F32),
        compiler_params=_params(("parallel",)),
        name="ada",
    )(c, w, b)


FF_CHUNK = 256


def _ffn_kernel(x_ref, *refs, res_w, merge):
    x = x_ref[...]
    if merge:
        oa_ref, gla_ref, dif_ref, mgate_ref, mg_ref, woa_ref, wog_ref, wod_ref = refs[:8]
        refs = refs[8:]
        y = _dot(oa_ref[...], woa_ref[...]) + _dot(gla_ref[...], wog_ref[...]) + _dot(dif_ref[...], wod_ref[...])
        x = x + mgate_ref[0] * _rms(y, mg_ref[...])
    shift_ref, scale_ref, gate_ref, g1_ref, g2_ref, win_ref, wout_ref, o_ref = refs
    h = _rms(x, g1_ref[...]) * (1.0 + scale_ref[0]) + shift_ref[0]
    hb = h.astype(BF16)
    y = jnp.zeros(x.shape, F32)
    for c in range(D_FF // FF_CHUNK):
        lo, hi = c * FF_CHUNK, (c + 1) * FF_CHUNK
        g = _dot(hb, win_ref[:, lo:hi])
        u = _dot(hb, win_ref[:, D_FF + lo:D_FF + hi])
        a = (g * jax.nn.sigmoid(g) * u).astype(BF16)
        y = y + _dot(a, wout_ref[lo:hi, :])
    o_ref[...] = x + res_w * gate_ref[0] * _rms(y, g2_ref[...])


def _mod_spec(r, tiles_per_mod, col):
    return pl.BlockSpec((1, r, D_MODEL), lambda i: (i // tiles_per_mod, 0, col))


def _ffn_call(x, mod, g1, g2, w_in, w_out, layer, which, res_w, tm, tiles_per_mod, merge=None):
    n = x.shape[0]
    mod_all, sub = mod
    r = mod_all.shape[1]
    tok = pl.BlockSpec((tm, D_MODEL), lambda i: (i, 0))
    merge_specs, merge_args = [], []
    if merge is not None:
        oa, gla, dif, (mmod, msub), mg, w_oa, w_og, w_od = merge
        wide = lambda a: pl.BlockSpec((tm, a.shape[1]), lambda i: (i, 0))
        merge_specs = [wide(oa), wide(gla), wide(dif), _mod_spec(r, tiles_per_mod, 3 * msub + 2),
                       _const_spec((1, D_MODEL)), _const_spec(w_oa.shape), _const_spec(w_og.shape),
                       _const_spec(w_od.shape)]
        merge_args = [oa, gla, dif, mmod, mg, w_oa, w_og, w_od]
    pick = lambda rows, cols: pl.BlockSpec((None, None, rows, cols), lambda i: (layer, which, 0, 0),
                                           pipeline_mode=pl.Buffered(1))
    return pl.pallas_call(
        functools.partial(_ffn_kernel, res_w=res_w, merge=merge is not None),
        grid=(n // tm,),
        in_specs=[tok] + merge_specs + [_mod_spec(r, tiles_per_mod, 3 * sub + k) for k in range(3)]
        + [_const_spec((1, D_MODEL)), _const_spec((1, D_MODEL)), pick(D_MODEL, 2 * D_FF), pick(D_FF, D_MODEL)],
        out_specs=tok,
        out_shape=jax.ShapeDtypeStruct((n, D_MODEL), F32),
        compiler_params=_params(("parallel",)),
        name="ffn",
    )(x, *merge_args, mod_all, mod_all, mod_all, g1, g2, w_in, w_out)


_STACKED_OUTS = (
    ("ka", 1, W_A), ("va", 1, W_A), ("kc", N_HEADS, HEAD_PAD), ("vc", N_HEADS, HEAD_PAD), ("logf", 1, N_HEADS),
)
_LAYER_OUTS = (
    ("qa_p", W_PAD, BF16), ("ka_p", W_PAD, BF16), ("va_p", W_PAD, BF16),
    ("qc1_p", W_PAD, BF16), ("qc2_p", W_PAD, BF16), ("kc_b", W_C, BF16), ("vc_b", W_C, BF16),
    ("qb", LANES, F32), ("kb", LANES, F32), ("vb", W_B, F32), ("loga", LANES, F32), ("rb", W_B, F32),
)
_NORM_ROWS = ("qa_p", "ka_p", "qc1_p", "qc2_p", "kc_b")


def _proj_kernel(x_ref, shift_ref, scale_ref, g1_ref, w_ref, wkvt_ref, bf_ref, wup_ref, bup_ref, vone_ref, *refs,
                 n_alias, with_qa, transposed_a):
    refs = refs[n_alias:]
    names = [n for n, _, _ in _STACKED_OUTS] + [n for n, _, _ in _LAYER_OUTS] + (["qa"] if with_qa else [])
    o = dict(zip(names + ["norms"], refs))
    tm = x_ref.shape[0]
    x = x_ref[...]
    hb = (_rms(x, g1_ref[...]) * (1.0 + scale_ref[0]) + shift_ref[0]).astype(BF16)

    def seg(name):
        lo, hi = _SEG[name]
        return _dot(hb, w_ref[:, lo:hi])

    def store_heads(ref, val):
        for d in range(ref.shape[0]):
            for h in range(N_HEADS):
                ref[d, pl.ds(h, tm, stride=N_HEADS), :] = val[:, h * HEAD_PAD:(h + 1) * HEAD_PAD]

    def store_slots(ref, val):
        for d in range(ref.shape[0]):
            ref[d] = val

    if transposed_a:
        kvt = _nt_dot(wkvt_ref[...], hb)
        store_slots(o["ka"], kvt[None, :W_A])
        store_slots(o["va"], kvt[None, W_A:])
    else:
        store_slots(o["ka"], seg("ka"))
        store_slots(o["va"], seg("va"))
    kc = seg("kc")
    store_heads(o["kc"], kc)
    rounded = {"kc_b": kc.astype(BF16)}
    o["kc_b"][...] = rounded["kc_b"]
    vc = seg("vc")
    store_heads(o["vc"], vc)
    o["vc_b"][...] = vc.astype(BF16)
    def pad_heads(val):
        zero = jnp.zeros((tm, HEAD_PAD - HD_A), val.dtype)
        pieces = []
        for h in range(N_HEADS):
            pieces += [val[:, h * HD_A:(h + 1) * HD_A], zero]
        return jnp.concatenate(pieces, axis=1)

    qa = seg("qa")
    if with_qa:
        o["qa"][...] = qa * (HD_A ** -0.5)
    rounded["qa_p"] = pad_heads((qa * (HD_A ** -0.5 * LOG2E)).astype(BF16))
    rounded["ka_p"] = pad_heads(seg("ka").astype(BF16))
    o["qa_p"][...] = rounded["qa_p"]
    o["ka_p"][...] = rounded["ka_p"]
    o["va_p"][...] = (pad_heads(seg("va")) + vone_ref[...]).astype(BF16)
    qc = seg("qc") * (HD_C ** -0.5 * LOG2E)
    first_map = lax.broadcasted_iota(jnp.int32, qc.shape, 1) % HEAD_PAD < HD_C
    rounded["qc1_p"] = jnp.where(first_map, qc, 0.0).astype(BF16)
    rounded["qc2_p"] = jnp.where(first_map, 0.0, qc).astype(BF16)
    o["qc1_p"][...] = rounded["qc1_p"]
    o["qc2_p"][...] = rounded["qc2_p"]
    fa = seg("fa") + bf_ref[...]
    store_slots(o["logf"], _log_sigmoid(fa)[:, :N_HEADS])
    o["qb"][...] = seg("qb")
    o["kb"][...] = seg("kb")
    o["vb"][...] = seg("vb")
    o["rb"][...] = seg("rb")
    gb = seg("gb").astype(BF16)
    o["loga"][...] = _log_sigmoid(_dot(gb, wup_ref[...]) + bup_ref[...]) * (1.0 / GLA_TAU)
    lane = lax.broadcasted_iota(jnp.int32, (8, LANES), 1)
    tile_norms = jnp.zeros((8, LANES), F32)
    for a, name in enumerate(_NORM_ROWS):
        sq = rounded[name].astype(F32)
        sq = sq * sq
        for h in range(N_HEADS):
            ss = jnp.sum(sq[:, h * HEAD_PAD:(h + 1) * HEAD_PAD], axis=1, keepdims=True)
            tile_norms = jnp.where(lane == N_HEADS * a + h, jnp.max(ss, axis=0, keepdims=True), tile_norms)
    o["norms"][0] = tile_norms


def _proj_call(x, mod, g1, w_ext, w_kvt, bf, wup, bup, vone, tm, tiles_per_mod, layer, depth, stacked, with_qa,
               seq_len):
    n = x.shape[0]
    mod_all, sub = mod
    r = mod_all.shape[1]
    tok = lambda w: pl.BlockSpec((tm, w), lambda i: (i, 0))
    layer_outs = _LAYER_OUTS + ((("qa", W_A, F32),) if with_qa else ())
    alias_in = [] if stacked is None else [stacked[nm] for nm, _, _ in _STACKED_OUTS]
    transposed_a = seq_len % tm == 0
    tps = max(seq_len // tm, 1)
    stacked_specs, stacked_shapes = [], []
    slots = depth if stacked is None else 1
    for nm, rows, w in _STACKED_OUTS:
        if transposed_a and nm in ("ka", "va"):
            stacked_specs.append(pl.BlockSpec((slots, 1, w, tm), lambda i: (layer, i // tps, 0, i % tps)))
            stacked_shapes.append(jax.ShapeDtypeStruct((depth, n // seq_len, w, seq_len), F32))
        else:
            stacked_specs.append(pl.BlockSpec((slots, tm * rows, w), lambda i: (layer, i, 0)))
            stacked_shapes.append(jax.ShapeDtypeStruct((depth, n * rows, w), F32))
    n_fixed = 10
    outs = pl.pallas_call(
        functools.partial(_proj_kernel, n_alias=len(alias_in), with_qa=with_qa, transposed_a=transposed_a),
        grid=(n // tm,),
        in_specs=[tok(D_MODEL), _mod_spec(r, tiles_per_mod, 3 * sub), _mod_spec(r, tiles_per_mod, 3 * sub + 1),
                  _const_spec((1, D_MODEL)), _const_spec((D_MODEL, N_EXT)), _const_spec((2 * W_A, D_MODEL)),
                  _const_spec((1, LANES)), _const_spec((LANES, LANES)), _const_spec((1, LANES)),
                  _const_spec((1, W_PAD))]
        + [pl.BlockSpec(memory_space=pl.ANY)] * len(alias_in),
        out_specs=stacked_specs + [tok(w) for _, w, _ in layer_outs]
        + [pl.BlockSpec((1, 8, LANES), lambda i: (i, 0, 0))],
        out_shape=stacked_shapes + [jax.ShapeDtypeStruct((n, w), dt) for _, w, dt in layer_outs]
        + [jax.ShapeDtypeStruct((n // tm, 8, LANES), F32)],
        input_output_aliases={n_fixed + k: k for k in range(len(alias_in))},
        compiler_params=_params(("parallel",)),
        name="proj",
    )(x, mod_all, mod_all, g1, w_ext, w_kvt, bf, wup, bup, vone, *alias_in)
    names = [nm for nm, _, _ in _STACKED_OUTS] + [nm for nm, _, _ in layer_outs] + ["norms"]
    return dict(zip(names, outs))


def _cumsum_kernel(x_ref, o_ref):
    nb = x_ref.shape[1]
    r = lax.broadcasted_iota(jnp.int32, (LANES, LANES), 0)
    c = lax.broadcasted_iota(jnp.int32, (LANES, LANES), 1)
    within = (r <= c).astype(F32)
    rr = lax.broadcasted_iota(jnp.int32, (nb, nb), 0)
    cc = lax.broadcasted_iota(jnp.int32, (nb, nb), 1)
    before = (cc < rr).astype(F32)
    for s in range(x_ref.shape[0]):
        x = x_ref[s]
        local = _dot_exact_rhs(x, within)
        tot = jnp.broadcast_to(local[:, LANES - 1:LANES], (nb, LANES))
        o_ref[s] = local + _dot_exact_lhs(before, tot)


def _cumsum_call(x):
    rows, nb, _ = x.shape
    group = math.gcd(rows, 8)
    spec = pl.BlockSpec((group, nb, LANES), lambda i: (i, 0, 0))
    return pl.pallas_call(
        _cumsum_kernel, grid=(rows // group,), in_specs=[spec], out_specs=spec,
        out_shape=jax.ShapeDtypeStruct(x.shape, F32),
        compiler_params=_params(("parallel",)), name="cumsum",
    )(x)


def _lane_tile(x, width):
    return jnp.concatenate([x] * (width // LANES), axis=1)


GROUP_HEADS = 2
W_GROUP = GROUP_HEADS * HEAD_PAD


def _causal_sweep(i, first, tq, tk, tiles_per_iter, tile):
    r = tq // tk

    def body(jj, carry):
        for u in range(tiles_per_iter):
            tile(jj * tiles_per_iter + u, slice(0, tq), None)
        return carry

    assert r % tiles_per_iter == 0 and tiles_per_iter in (1, 2)
    lead = 0
    if tiles_per_iter == 2:
        lead = lax.rem(i * r - first, 2)

        @pl.when(lead == 1)
        def _():
            tile(first, slice(0, tq), None)

    lax.fori_loop(lax.div(first + lead, tiles_per_iter), lax.div(i * r, tiles_per_iter), body, 0)
    for d in range(r):
        tile(i * r + d, slice(d * tk, (d + 1) * tk), d * tk)
        if d + 1 < r:
            tile(i * r + d, slice((d + 1) * tk, tq), None)


def _fox_kernel(first_ref, q_ref, k_ref, v_ref, c_ref, o_ref, m_sc, acc_sc, *, tk, tiles_per_iter):
    i = pl.program_id(2)
    first = first_ref[pl.program_id(0), pl.program_id(1), i]
    tq = q_ref.shape[1]
    m_sc[...] = jnp.full(m_sc.shape, NEG, F32)
    acc_sc[...] = jnp.zeros(acc_sc.shape, F32)
    row = lax.broadcasted_iota(jnp.int32, (tk, tk), 0)
    col = lax.broadcasted_iota(jnp.int32, (tk, tk), 1)
    causal = col <= row

    def tile(j, rows, local):
        ks = pl.ds(pl.multiple_of(j * tk, tk), tk)
        for h in range(GROUP_HEADS):
            hs = slice(h * HEAD_PAD, (h + 1) * HEAD_PAD)
            c_first = c_ref[0, h, pl.ds(i * (tq // tk), 1), :][:, 0:1]
            t = _nt_dot(q_ref[0, rows, hs], k_ref[0, ks, hs]) + (c_first - c_ref[0, h, pl.ds(j, 1), :]) * LOG2E
            if local is not None:
                t = jnp.where(causal, t, NEG)
            m_old = m_sc[h, rows, :]
            m_new = jnp.maximum(m_old, jnp.max(t, axis=1, keepdims=True))
            p = jnp.exp2(t - _lane_tile(m_new, tk)).astype(BF16)
            acc_sc[h, rows, :] = jnp.exp2(m_old - m_new) * acc_sc[h, rows, :] + _dot(p, v_ref[0, ks, hs])
            m_sc[h, rows, :] = m_new

    _causal_sweep(i, first, tq, tk, tiles_per_iter, tile)
    outs = []
    for h in range(GROUP_HEADS):
        acc = acc_sc[h]
        outs.append((acc[:, :HD_A] / acc[:, HD_A:HD_A + 1]).astype(BF16))
    o_ref[0] = jnp.concatenate(outs, axis=1)


def _resident_spec(t, width):
    return pl.BlockSpec((1, t, width), lambda b_, g, i, first: (b_, 0, g), pipeline_mode=pl.Buffered(1))


def _fox_call(first, qa_p, ka_p, va_p, c, tq, tk):
    b, t, _ = qa_p.shape
    qspec = pl.BlockSpec((1, tq, W_GROUP), lambda b_, g, i, first: (b_, i, g))
    grid_spec = pltpu.PrefetchScalarGridSpec(
        num_scalar_prefetch=1,
        grid=(b, N_HEADS // GROUP_HEADS, t // tq),
        in_specs=[qspec, _resident_spec(t, W_GROUP), _resident_spec(t, W_GROUP),
                  pl.BlockSpec((1, GROUP_HEADS, t // tk, tk), lambda b_, g, i, first: (b_, g, 0, 0))],
        out_specs=pl.BlockSpec((1, tq, GROUP_HEADS * HD_A), lambda b_, g, i, first: (b_, i, g)),
        scratch_shapes=[pltpu.VMEM((GROUP_HEADS, tq, LANES), F32), pltpu.VMEM((GROUP_HEADS, tq, HEAD_PAD), F32)])
    return pl.pallas_call(
        functools.partial(_fox_kernel, tk=tk, tiles_per_iter=1),
        grid_spec=grid_spec,
        out_shape=jax.ShapeDtypeStruct((b, t, W_A), BF16),
        compiler_params=_params(("parallel", "parallel", "arbitrary")),
        name="fox",
    )(first, qa_p, ka_p, va_p, c.reshape(b, N_HEADS, t // tk, tk))


def _diff_lambda(lp, lam_init):
    a = jnp.sum(lp[0:1] * lp[1:2], axis=1, keepdims=True)
    b = jnp.sum(lp[2:3] * lp[3:4], axis=1, keepdims=True)
    return jnp.exp(a) - jnp.exp(b) + lam_init


def _diag_distance(tk):
    q = np.arange(tk)[:, None]
    k = np.arange(tk)[None, :]
    return np.where(k // CHUNK <= q // CHUNK, 2 * np.minimum(q, k) - k, HIDDEN).astype(np.float32)


DIFF_GROUP = 1
W_DIFF_GROUP = DIFF_GROUP * HEAD_PAD


def _diff_kernel(first_ref, q1_ref, q2_ref, k_ref, v_ref, slope_ref, lam_ref, g_ref, rel_ref, o_ref,
                 m_sc, l_sc, a_sc, *, tk, tiles_per_iter, lam_init):
    g = pl.program_id(1)
    i = pl.program_id(2)
    first = first_ref[pl.program_id(0), g, i]
    tq = q1_ref.shape[1]
    m_sc[...] = jnp.full(m_sc.shape, NEG, F32)
    l_sc[...] = jnp.zeros(l_sc.shape, F32)
    a_sc[...] = jnp.zeros(a_sc.shape, F32)
    lane = lax.broadcasted_iota(jnp.int32, (1, tk), 1)

    def tile(j, rows, local):
        ks = pl.ds(pl.multiple_of(j * tk, tk), tk)
        if local is None:
            rel = (lane + (j * tk - i * tq)).astype(F32)
        else:
            rel = rel_ref[...] + float(local)
        for h in range(DIFF_GROUP):
            hs = slice(h * HEAD_PAD, (h + 1) * HEAD_PAD)
            bias = (slope_ref[g * DIFF_GROUP + h][:, 0:1] * LOG2E) * rel
            k = k_ref[0, ks, hs]
            v = v_ref[0, ks, hs]
            nrows = rows.stop - rows.start
            scores = _nt_dot(jnp.concatenate([q1_ref[0, rows, hs], q2_ref[0, rows, hs]], axis=0), k)
            probs, alphas = [], []
            for m in range(2):
                n = 2 * h + m
                t = scores[m * nrows:(m + 1) * nrows] + bias
                m_old = m_sc[n, rows, :]
                m_new = jnp.maximum(m_old, jnp.max(t, axis=1, keepdims=True))
                alpha = jnp.exp2(m_old - m_new)
                p = jnp.exp2(t - _lane_tile(m_new, tk))
                l_sc[n, rows, :] = alpha * l_sc[n, rows, :] + jnp.sum(p, axis=1, keepdims=True)
                m_sc[n, rows, :] = m_new
                probs.append(p.astype(BF16))
                alphas.append(alpha)
            pv = _dot(jnp.concatenate(probs, axis=0), v)
            for m in range(2):
                n = 2 * h + m
                a_sc[n, rows, :] = alphas[m] * a_sc[n, rows, :] + pv[m * nrows:(m + 1) * nrows]

    _causal_sweep(i, first, tq, tk, tiles_per_iter, tile)
    lam = _diff_lambda(lam_ref[...], lam_init)
    for h in range(DIFF_GROUP):
        out = a_sc[2 * h] / l_sc[2 * h] - lam * (a_sc[2 * h + 1] / l_sc[2 * h + 1])
        o_ref[0, :, h * HEAD_PAD:(h + 1) * HEAD_PAD] = (_rms(out, g_ref[...]) * (1.0 - lam_init)).astype(BF16)


def _diff_call(first, qc1_p, qc2_p, kc_b, vc_b, slopes, lam_p, g_diff, lam_init, tq, tk):
    assert CHUNK == 64 and tk % CHUNK == 0
    b, t, _ = qc1_p.shape
    qspec = pl.BlockSpec((1, tq, W_DIFF_GROUP), lambda b_, g, i, first: (b_, i, g))
    stat = pltpu.VMEM((2 * DIFF_GROUP, tq, LANES), F32)
    grid_spec = pltpu.PrefetchScalarGridSpec(
        num_scalar_prefetch=1,
        grid=(b, N_HEADS // DIFF_GROUP, t // tq),
        in_specs=[qspec, qspec, _resident_spec(t, W_DIFF_GROUP), _resident_spec(t, W_DIFF_GROUP),
                  pl.BlockSpec((N_HEADS, 1, LANES), lambda b_, g, i, first: (0, 0, 0)),
                  pl.BlockSpec((4, HD_C), lambda b_, g, i, first: (0, 0)),
                  pl.BlockSpec((1, HEAD_PAD), lambda b_, g, i, first: (0, 0)),
                  pl.BlockSpec((tk, tk), lambda b_, g, i, first: (0, 0), pipeline_mode=pl.Buffered(1))],
        out_specs=qspec,
        scratch_shapes=[stat, stat, pltpu.VMEM((2 * DIFF_GROUP, tq, HEAD_PAD), F32)])
    return pl.pallas_call(
        functools.partial(_diff_kernel, tk=tk, tiles_per_iter=2, lam_init=lam_init),
        grid_spec=grid_spec,
        out_shape=jax.ShapeDtypeStruct((b, t, W_C), BF16),
        compiler_params=_params(("parallel", "parallel", "arbitrary")),
        name="diff",
    )(first, qc1_p, qc2_p, kc_b, vc_b, slopes, lam_p, g_diff, jnp.asarray(_diag_distance(tk)))


def _gla_kernel(q_ref, k_ref, v_ref, la_ref, r_ref, s0_ref, g_ref, o_ref, s_ref, *, chunk):
    ti = pl.program_id(1)
    n_chunks = q_ref.shape[1] // chunk
    n_seq = q_ref.shape[0]

    @pl.when(ti == 0)
    def _():
        s_ref[...] = s0_ref[...]

    tri_r = lax.broadcasted_iota(jnp.int32, (chunk, chunk), 0)
    tri_c = lax.broadcasted_iota(jnp.int32, (chunk, chunk), 1)
    incl = (tri_c <= tri_r).astype(F32)
    srow = lax.broadcasted_iota(jnp.int32, (N_HEADS * chunk, LANES), 0) // chunk
    slane = lax.broadcasted_iota(jnp.int32, (N_HEADS * chunk, LANES), 1) // DK_B
    qmask = srow == slane
    arow = lax.broadcasted_iota(jnp.int32, (N_HEADS * chunk, chunk), 0) % chunk
    acol = lax.broadcasted_iota(jnp.int32, (N_HEADS * chunk, chunk), 1)
    causal = acol <= arow
    vlane = lax.broadcasted_iota(jnp.int32, (chunk, W_B), 1) // DV_B
    st_row = lax.broadcasted_iota(jnp.int32, (W_B, LANES), 0) // DV_B
    st_lane = lax.broadcasted_iota(jnp.int32, (W_B, LANES), 1) // DK_B
    diag_blocks = st_row == st_lane

    for c, sq in [(c, sq) for c in range(n_chunks) for sq in range(n_seq)]:
        sl = slice(c * chunk, (c + 1) * chunk)
        cb = _dot_exact_lhs(incl, la_ref[sq, sl, :])
        cb_last = cb[chunk - 1:chunk, :]
        q = q_ref[sq, sl, :] * (DK_B ** -0.5)
        k = k_ref[sq, sl, :]
        v = v_ref[sq, sl, :]
        vb = v.astype(BF16)
        qe = q * jnp.exp(cb)
        ke = (k * jnp.exp(-cb)).astype(BF16)
        kl = (k * jnp.exp(cb_last - cb)).astype(BF16)
        decay = jnp.exp(cb_last)
        qstack = jnp.where(qmask, jnp.concatenate([qe] * N_HEADS, axis=0), 0.0).astype(BF16)
        a = jnp.where(causal, _nt_dot(qstack, ke), 0.0).astype(BF16)
        oi = _dot(a, vb)
        o = jnp.zeros((chunk, W_B), F32)
        for h in range(N_HEADS):
            o = o + jnp.where(vlane == h, oi[h * chunk:(h + 1) * chunk], 0.0)
        st = s_ref[sq]
        o = o + _nt_dot(qe.astype(BF16), st.astype(BF16))
        upd = _dot(v.T.astype(BF16), kl)
        s_ref[sq] = decay * st + jnp.where(diag_blocks, upd, 0.0)
        o2 = o * o
        ms = jnp.zeros((chunk, W_B), F32)
        for h in range(N_HEADS):
            head_mean = jnp.sum(jnp.where(vlane == h, o2, 0.0), axis=1, keepdims=True) * (1.0 / DV_B)
            ms = jnp.where(vlane == h, head_mean, ms)
        r = r_ref[sq, sl, :]
        o_ref[sq, sl, :] = (o * lax.rsqrt(ms + EPS) * g_ref[...] * (r * jax.nn.sigmoid(r))).astype(BF16)


def _gla_call(qb, kb, vb, loga, rb, s0_t, g_gla, tg, chunk, group):
    b, t, _ = qb.shape
    tok = lambda w: pl.BlockSpec((group, tg, w), lambda b_, i: (b_, i, 0))
    sspec = pl.BlockSpec((group, W_B, LANES), lambda b_, i: (b_, 0, 0))
    return pl.pallas_call(
        functools.partial(_gla_kernel, chunk=chunk),
        grid=(b // group, t // tg),
        in_specs=[tok(LANES), tok(LANES), tok(W_B), tok(LANES), tok(W_B), sspec,
                  pl.BlockSpec((1, W_B), lambda b_, i: (0, 0))],
        out_specs=[tok(W_B), sspec],
        out_shape=[jax.ShapeDtypeStruct((b, t, W_B), BF16), jax.ShapeDtypeStruct((b, W_B, LANES), F32)],
        compiler_params=_params(("parallel", "arbitrary")),
        name="gla",
    )(qb, kb, vb, loga, rb, s0_t, g_gla)


def _softmax_two(t_past, t_new, exp_fn):
    m = jnp.maximum(jnp.max(t_past, axis=1, keepdims=True), jnp.max(t_new, axis=1, keepdims=True))
    p_past = exp_fn(t_past - m)
    p_new = exp_fn(t_new - m)
    l = jnp.sum(p_past, axis=1, keepdims=True) + jnp.sum(p_new, axis=1, keepdims=True)
    return p_past, p_new, l


def _fox_s_kernel(q_ref, kn_ref, vn_ref, kpt_ref, vpt_ref, c_ref, o_ref):
    t, past = q_ref.shape[1], kpt_ref.shape[1]
    q = q_ref[0]
    kn = kn_ref[...].astype(BF16)
    vn = vn_ref[...].astype(BF16)
    kpt = kpt_ref[...].astype(BF16)
    vpt = vpt_ref[...].astype(BF16)
    lane_head = lax.broadcasted_iota(jnp.int32, (t, W_A), 1) // HD_A
    row = lax.broadcasted_iota(jnp.int32, (t, t), 0)
    col = lax.broadcasted_iota(jnp.int32, (t, t), 1)
    o = jnp.zeros((t, W_A), F32)
    for h in range(N_HEADS):
        qh = jnp.where(lane_head == h, q, 0.0).astype(BF16)
        ch = c_ref[0, h:h + 1, :]
        cref = ch[:, past:past + 1]
        t_past = _dot(qh, kpt) + (cref - ch[:, :past])
        t_new = jnp.where(col <= row, _nt_dot(qh, kn) + (cref - ch[:, past:past + t]), NEG)
        p_past, p_new, l = _softmax_two(t_past, t_new, jnp.exp)
        oh = (_nt_dot(p_past.astype(BF16), vpt) + _dot(p_new.astype(BF16), vn)) / l
        o = o + jnp.where(lane_head == h, oh, 0.0)
    o_ref[0] = o.astype(BF16)


def _fox_s_call(layer, qa, ka, va, cache_kt, cache_vt, c):
    b, t, _ = qa.shape
    past = cache_kt.shape[3]
    new = pl.BlockSpec((1, t, W_A), lambda i: (i, 0, 0))
    new_l = lambda: pl.BlockSpec((None, t, W_A), lambda i: (layer, i, 0))
    old_l = lambda: pl.BlockSpec((None, None, W_A, past), lambda i: (layer, i, 0, 0))
    return pl.pallas_call(
        _fox_s_kernel,
        grid=(b,),
        in_specs=[new, new_l(), new_l(), old_l(), old_l(),
                  pl.BlockSpec((1, N_HEADS, c.shape[2]), lambda i: (i, 0, 0))],
        out_specs=new,
        out_shape=jax.ShapeDtypeStruct((b, t, W_A), BF16),
        compiler_params=_params(("parallel",)),
        name="fox_s",
    )(qa, ka, va, cache_kt, cache_vt, c)


def _diff_s_kernel(q1_ref, q2_ref, kn_ref, vn_ref, kp_ref, vp_ref, slope_ref, lam_ref, g_ref, o_ref, *, lam_init):
    t, past = q1_ref.shape[1], kp_ref.shape[0] // N_HEADS
    lam = _diff_lambda(lam_ref[...], lam_init)
    row = lax.broadcasted_iota(jnp.int32, (t, t), 0) + past
    col = lax.broadcasted_iota(jnp.int32, (t, t), 1) + past
    visible_new = lax.shift_right_logical(col, 6) <= lax.shift_right_logical(row, 6)
    dist_new = jnp.abs(row - col).astype(F32)
    prow = lax.broadcasted_iota(jnp.int32, (t, past), 0) + past
    pcol = lax.broadcasted_iota(jnp.int32, (t, past), 1)
    visible_past = lax.shift_right_logical(pcol, 6) <= lax.shift_right_logical(prow, 6)
    dist_past = (prow - pcol).astype(F32)
    for h in range(N_HEADS):
        hs = slice(h * HEAD_PAD, (h + 1) * HEAD_PAD)
        slope = slope_ref[h][:, 0:1] * LOG2E
        kn = kn_ref[pl.ds(h, t, stride=N_HEADS), :].astype(BF16)
        vn = vn_ref[pl.ds(h, t, stride=N_HEADS), :].astype(BF16)
        kp = kp_ref[pl.ds(h, past, stride=N_HEADS), :].astype(BF16)
        vp = vp_ref[pl.ds(h, past, stride=N_HEADS), :].astype(BF16)
        outs = []
        for q_ref in (q1_ref, q2_ref):
            q = q_ref[0, :, hs]
            t_past = jnp.where(visible_past, _nt_dot(q, kp) - slope * dist_past, NEG)
            t_new = jnp.where(visible_new, _nt_dot(q, kn) - slope * dist_new, NEG)
            p_past, p_new, l = _softmax_two(t_past, t_new, jnp.exp2)
            outs.append((_dot(p_past.astype(BF16), vp) + _dot(p_new.astype(BF16), vn)) / l)
        out = outs[0] - lam * outs[1]
        o_ref[0, :, hs] = (_rms(out, g_ref[...]) * (1.0 - lam_init)).astype(BF16)


def _diff_s_call(layer, qc1_p, qc2_p, kc, vc, cache_k, cache_v, slopes, lam_p, g_diff, lam_init):
    assert CHUNK == 64
    b, t, _ = qc1_p.shape
    rows_past = cache_k.shape[2]
    new = lambda: pl.BlockSpec((1, t, W_C), lambda i: (i, 0, 0))
    new_l = lambda: pl.BlockSpec((None, t * N_HEADS, HEAD_PAD), lambda i: (layer, i, 0))
    old_l = lambda: pl.BlockSpec((None, None, rows_past, HEAD_PAD), lambda i: (layer, i, 0, 0))
    return pl.pallas_call(
        functools.partial(_diff_s_kernel, lam_init=lam_init),
        grid=(b,),
        in_specs=[new(), new(), new_l(), new_l(), old_l(), old_l(),
                  pl.BlockSpec((N_HEADS, 1, LANES), lambda i: (0, 0, 0)),
                  pl.BlockSpec((4, HD_C), lambda i: (0, 0)),
                  pl.BlockSpec((1, HEAD_PAD), lambda i: (0, 0))],
        out_specs=new(),
        out_shape=jax.ShapeDtypeStruct((b, t, W_C), BF16),
        compiler_params=_params(("parallel",)),
        name="diff_s",
    )(qc1_p, qc2_p, kc, vc, cache_k, cache_v, slopes, lam_p, g_diff)


def _pad_lanes(w, width=LANES):
    return jnp.pad(w, ((0, 0), (0, width - w.shape[1])))


def _layer_params(l, w_in, b_f, w_gla_up, b_gla_up, g_gla, g_diff, w_out):
    offs = [int(o) for o in np.cumsum(PROJ_SIZES)[:-1]]
    qa, ka, va, fa, qb, kb, vb, gb, rb, qc, kc, vc = jnp.split(w_in[l], offs, axis=1)
    segs = {
        "ka": ka, "va": va, "kc": kc, "vc": vc, "qa": qa, "qc": qc,
        "fa": _pad_lanes(fa), "qb": qb, "kb": kb, "vb": vb, "rb": rb, "gb": _pad_lanes(gb),
    }
    w_ext = jnp.concatenate([segs[n] for n, _ in _SEG_WIDTHS], axis=1).astype(BF16)
    wo = w_out[l]
    vone = np.zeros((1, W_PAD), np.float32)
    vone[0, HD_A::HEAD_PAD] = 1.0
    return dict(
        w_ext=w_ext, w_kvt=jnp.concatenate([ka, va], axis=1).T.astype(BF16),
        bf=_pad_lanes(b_f[l][None, :]),
        wup=jnp.pad(w_gla_up[l], ((0, LANES - GLA_RANK), (0, 0))).astype(BF16),
        bup=b_gla_up[l][None, :],
        vone=jnp.asarray(vone),
        g_gla=jnp.tile(g_gla[l], N_HEADS)[None, :], g_diff=g_diff[l][None, :],
        w_oa=wo[:W_A].astype(BF16),
        w_og=wo[W_A:W_A + W_B].astype(BF16), w_od=wo[W_A + W_B:].astype(BF16),
    )


def _state_to_blockdiag_t(s):
    eye = jnp.eye(N_HEADS, dtype=s.dtype)
    return jnp.einsum("bhkv,hg->bhvgk", s, eye).reshape(s.shape[0], W_B, N_HEADS * DK_B)


def _blockdiag_t_to_state(st):
    b = st.shape[0]
    s5 = st.reshape(b, N_HEADS, DV_B, N_HEADS, DK_B)
    diag = jnp.stack([s5[:, h, :, h, :] for h in range(N_HEADS)], axis=1)
    return jnp.swapaxes(diag, 2, 3)


SKIP_LOG2 = 160.0
NORM_SLACK = 1.01


def _tile_max(x, factor):
    return jnp.max(x.reshape(x.shape[:-1] + (x.shape[-1] // factor, factor)), axis=-1)


def _first_tiles(qn, kn, gap, tq, tk, group):
    b, h, nq = qn.shape
    r = tq // tk
    kn_diag = _tile_max(kn, r)
    bound = NORM_SLACK * (qn[..., None] * kn[:, :, None, :] + (qn * kn_diag)[..., None]) + gap
    dead = (bound < -SKIP_LOG2).reshape(b, h // group, group, nq, kn.shape[-1]).all(axis=2)
    lead = jnp.sum(jnp.cumprod(dead.astype(jnp.int32), axis=-1), axis=-1)
    return jnp.minimum(lead, jnp.arange(nq, dtype=jnp.int32) * r).astype(jnp.int32)


def _tile(n, pref):
    t = min(n, pref)
    assert n % t == 0, (n, t)
    return t


def _trunk(x, mods, params, ffn_w, g_norm, lam_params, mixer, tm, tiles_per_mod, with_qa):
    b, t, _ = x.shape
    n = b * t
    depth = len(params)
    xf = x.reshape(n, D_MODEL)
    stacked = None
    gla_states = []
    for l in range(depth):
        p = params[l]
        gn = lambda k: g_norm[l, k][None, :]
        xf = _ffn_call(xf, (mods[l], 0), gn(0), gn(1), *ffn_w, l, 0, 0.5, tm, tiles_per_mod)
        pr = _proj_call(xf, (mods[l], 1), gn(2), p["w_ext"], p["w_kvt"], p["bf"], p["wup"], p["bup"], p["vone"], tm,
                        tiles_per_mod, l, depth, stacked, with_qa, t)
        stacked = {nm: pr[nm] for nm, _, _ in _STACKED_OUTS}
        lam_init = 0.8 - 0.6 * math.exp(-0.3 * l)
        oa, gla, dif, s_fin, w_oa = mixer(l, pr, p, lam_params[l], lam_init, b, t)
        gla_states.append(s_fin)
        xf = _ffn_call(xf, (mods[l], 2), gn(4), gn(5), *ffn_w, l, 1, 0.5, tm, tiles_per_mod,
                       merge=(oa, gla, dif, (mods[l], 1), gn(3), w_oa, p["w_og"], p["w_od"]))
    shp = lambda nm, *tail: stacked[nm].reshape((depth, b, t) + tail)
    if stacked["ka"].ndim == 4:
        shp_a = lambda nm: jnp.transpose(stacked[nm].reshape(depth, b, N_HEADS, HD_A, t), (0, 1, 4, 2, 3))
    else:
        shp_a = lambda nm: shp(nm, N_HEADS, HD_A)
    states = [shp_a("ka"), shp_a("va"), shp("logf", N_HEADS), jnp.stack(gla_states),
              shp("kc", N_HEADS, 2 * HD_C), shp("vc", N_HEADS, 2 * HD_C)]
    return xf.reshape(b, t, D_MODEL), states


def kernel(x_prompt, x_sample, c_prompt, c_sample, cache_fox_k, cache_fox_v, cache_fox_logf, state_gla,
           cache_diff_k, cache_diff_v, w_ada, b_ada, g_norm, w_ffn_in, w_ffn_out, w_in, b_f, w_gla_up,
           b_gla_up, g_gla, g_diff, lam_params, w_out):
    depth = w_in.shape[0]
    bp, tp, _ = x_prompt.shape
    bs, ts, _ = x_sample.shape
    past = cache_fox_k.shape[2]

    params = [_layer_params(l, w_in, b_f, w_gla_up, b_gla_up, g_gla, g_diff, w_out) for l in range(depth)]
    ffn_w = (w_ffn_in.astype(BF16), w_ffn_out.astype(BF16))

    slopes = 2.0 ** (-8.0 * jnp.arange(1, N_HEADS + 1, dtype=F32) / N_HEADS)
    slopes = jnp.broadcast_to(slopes[:, None, None], (N_HEADS, 1, LANES))

    rows = bp + bs
    rows_pad = -(-rows // 8) * 8
    c_all = jnp.pad(jnp.concatenate([c_prompt, c_sample], axis=0), ((0, rows_pad - rows), (0, 0)))
    mods_p, mods_s = [], []
    for l in range(depth):
        mod = _ada_call(c_all, w_ada, b_ada[:, None, :], l)
        mods_p.append(mod[:bp, None, :])
        mods_s.append(jnp.repeat(mod[bp:rows], ts, axis=0)[None])

    tm_p = _tile(tp, 512)
    tq = _tile(tp, 1024)
    tk = _tile(tp, 512)
    tg = _tile(tp, 512)

    def prompt_mixer(l, pr, p, lam_p, lam_init, b, t):
        r3 = lambda a: a.reshape(b, t, a.shape[-1])
        logf_t = jnp.swapaxes(r3(pr["logf"][l]), 1, 2).reshape(b * N_HEADS, t // LANES, LANES)
        c = _cumsum_call(logf_t).reshape(b, N_HEADS, t)
        nrm = jnp.sqrt(pr["norms"][:, 0, :N_HEADS * len(_NORM_ROWS)])
        nrm = dict(zip(_NORM_ROWS, jnp.moveaxis(nrm.reshape(b, t // tm_p, len(_NORM_ROWS), N_HEADS), (2, 3), (0, 2))))
        q_tiles = lambda a: _tile_max(a, tq // tm_p)
        k_tiles = lambda a: _tile_max(a, tk // tm_p)
        gap_a = (c[:, :, ::tq][..., None] - c[:, :, tk - 1::tk][:, :, None, :]) * LOG2E
        first_a = _first_tiles(q_tiles(nrm["qa_p"]), k_tiles(nrm["ka_p"]), gap_a, tq, tk, GROUP_HEADS)
        oa = _fox_call(first_a, r3(pr["qa_p"]), r3(pr["ka_p"]), r3(pr["va_p"]), c, tq, tk)
        s0_t = jnp.zeros((b, W_B, N_HEADS * DK_B), F32)
        gla, s_t = _gla_call(r3(pr["qb"]), r3(pr["kb"]), r3(pr["vb"]), r3(pr["loga"]), r3(pr["rb"]), s0_t,
                             p["g_gla"], tg, min(CHUNK, t), math.gcd(b, 2))
        last_key = jnp.arange(tk - 1, t, tk, dtype=F32)[None, :] - jnp.arange(0, t, tq, dtype=F32)[:, None]
        gap_c = slopes[:, 0, :1, None] * LOG2E * last_key[None]
        first_c = _first_tiles(q_tiles(jnp.maximum(nrm["qc1_p"], nrm["qc2_p"])), k_tiles(nrm["kc_b"]), gap_c, tq, tk,
                                 DIFF_GROUP)
        dif = _diff_call(first_c, r3(pr["qc1_p"]), r3(pr["qc2_p"]), r3(pr["kc_b"]), r3(pr["vc_b"]), slopes, lam_p,
                         p["g_diff"], lam_init, tq, tk)
        flat = lambda a: a.reshape(b * t, a.shape[-1])
        return flat(oa), flat(gla), flat(dif), _blockdiag_t_to_state(s_t), p["w_oa"]

    feature_time = lambda a: jnp.transpose(a, (0, 1, 3, 4, 2)).reshape(depth, bs, W_A, past)
    cache_kt, cache_vt = feature_time(cache_fox_k), feature_time(cache_fox_v)
    cache_dk = cache_diff_k.reshape(depth, bs, past * N_HEADS, HEAD_PAD)
    cache_dv = cache_diff_v.reshape(depth, bs, past * N_HEADS, HEAD_PAD)

    def sample_mixer(l, pr, p, lam_p, lam_init, b, t):
        r3 = lambda a: a.reshape(b, t, a.shape[-1])
        total = past + t
        nb = -(-total // (8 * LANES)) * 8
        lf = jnp.concatenate([cache_fox_logf[l].astype(F32), r3(pr["logf"][l])], axis=1)
        lf = jnp.pad(jnp.swapaxes(lf, 1, 2), ((0, 0), (0, 0), (0, nb * LANES - total)))
        c = _cumsum_call(lf.reshape(b * N_HEADS, nb, LANES)).reshape(b, N_HEADS, nb * LANES)
        oa = _fox_s_call(l, r3(pr["qa"]), pr["ka"], pr["va"], cache_kt, cache_vt, c)
        gla, s_t = _gla_call(r3(pr["qb"]), r3(pr["kb"]), r3(pr["vb"]), r3(pr["loga"]), r3(pr["rb"]),
                             _state_to_blockdiag_t(state_gla[l].astype(F32)), p["g_gla"], t, min(CHUNK, t),
                             math.gcd(b, 8))
        dif = _diff_s_call(l, r3(pr["qc1_p"]), r3(pr["qc2_p"]), pr["kc"], pr["vc"], cache_dk, cache_dv,
                           slopes, lam_p, p["g_diff"], lam_init)
        flat = lambda a: a.reshape(b * t, a.shape[-1])
        return flat(oa), flat(gla), flat(dif), _blockdiag_t_to_state(s_t), p["w_oa"]

    y_p, sp = _trunk(x_prompt, mods_p, params, ffn_w, g_norm, lam_params, prompt_mixer, tm_p, tp // tm_p, False)
    y_s, ss = _trunk(x_sample, mods_s, params, ffn_w, g_norm, lam_params, sample_mixer, bs * ts, 1, True)
    return (y_p, y_s, sp[0], sp[1], sp[2], sp[3], sp[4], sp[5], ss[0], ss[1], ss[2], ss[3], ss[4], ss[5])
```

```python
import functools
import math

import jax
import jax.numpy as jnp
import numpy as np
from jax import lax
from jax.experimental import pallas as pl
from jax.experimental.pallas import tpu as pltpu

F32 = jnp.float32
BF16 = jnp.bfloat16

D_MODEL = 1024
N_HEADS = 4
HD_A = 64
DK_B = 32
DV_B = 64
GLA_RANK = 16
GLA_TAU = 16.0
HD_C = 64
CHUNK = 64
W_A = N_HEADS * HD_A
W_B = N_HEADS * DV_B
W_C = N_HEADS * 2 * HD_C
D_FF = ((8 * D_MODEL // 3 + 255) // 256) * 256
N_SUB = 3
PROJ_SIZES = (W_A, W_A, W_A, N_HEADS, N_HEADS * DK_B, N_HEADS * DK_B, W_B, GLA_RANK, W_B, W_C, W_C, W_C)
NEG = -1e30
HIDDEN = -1e34
EPS = 1e-6
LOG2E = math.log2(math.e)

LANES = 128
HEAD_PAD = 128
W_PAD = N_HEADS * HEAD_PAD
VMEM_LIMIT = 56 * 1024 * 1024

_SEG_WIDTHS = (
    ("ka", W_A), ("va", W_A), ("kc", W_C), ("vc", W_C), ("qa", W_A),
    ("qc", W_C),
    ("fa", LANES), ("qb", LANES), ("kb", LANES), ("vb", W_B), ("rb", W_B), ("gb", LANES),
)
_SEG = {}
_off = 0
for _name, _w in _SEG_WIDTHS:
    _SEG[_name] = (_off, _off + _w)
    _off += _w
N_EXT = _off


def _nt_dot(a, b):
    return lax.dot_general(a, b, (((1,), (1,)), ((), ())), preferred_element_type=F32)


def _dot(a, b):
    return jnp.dot(a, b, preferred_element_type=F32)


def _split3(x):
    hi = x.astype(BF16)
    r1 = x - hi.astype(F32)
    mid = r1.astype(BF16)
    lo = (r1 - mid.astype(F32)).astype(BF16)
    return hi, mid, lo


def _dot_exact_rhs(x, m):
    mb = m.astype(BF16)
    hi, mid, lo = _split3(x)
    return _dot(hi, mb) + _dot(mid, mb) + _dot(lo, mb)


def _dot_exact_lhs(m, x):
    mb = m.astype(BF16)
    hi, mid, lo = _split3(x)
    return _dot(mb, hi) + _dot(mb, mid) + _dot(mb, lo)


def _rms(x, g):
    return x * lax.rsqrt(jnp.mean(x * x, axis=-1, keepdims=True) + EPS) * g


def _log_sigmoid(x):
    return jnp.minimum(x, 0.0) - jnp.log1p(jnp.exp(-jnp.abs(x)))


def _const_spec(shape):
    nd = len(shape)
    return pl.BlockSpec(shape, lambda *_: (0,) * nd, pipeline_mode=pl.Buffered(1))


def _params(sem):
    return pltpu.CompilerParams(dimension_semantics=sem, vmem_limit_bytes=VMEM_LIMIT)


def _ada_kernel(c_ref, w_ref, b_ref, o_ref):
    c = c_ref[...]
    a = (c * jax.nn.sigmoid(c)).astype(BF16)
    o_ref[...] = _dot(a, w_ref[...].astype(BF16)) + b_ref[...]


def _ada_call(c, w, b, layer):
    m, n = c.shape[0], w.shape[2]
    assert n == N_SUB * 3 * D_MODEL
    tn = n // 8
    return pl.pallas_call(
        _ada_kernel,
        grid=(n // tn,),
        in_specs=[pl.BlockSpec((m, D_MODEL), lambda i: (0, 0)),
                  pl.BlockSpec((None, D_MODEL, tn), lambda i: (layer, 0, i)),
                  pl.BlockSpec((None, 1, tn), lambda i: (layer, 0, i))],
        out_specs=pl.BlockSpec((m, tn), lambda i: (0, i)),
        out_shape=jax.ShapeDtypeStruct((m, n), F32),
        compiler_params=_params(("parallel",)),
        name="ada",
    )(c, w, b)


FF_CHUNK = 256


def _ffn_kernel(x_ref, *refs, res_w, merge):
    x = x_ref[...]
    if merge:
        oa_ref, gla_ref, dif_ref, mgate_ref, mg_ref, woa_ref, wog_ref, wod_ref = refs[:8]
        refs = refs[8:]
        y = _dot(oa_ref[...], woa_ref[...]) + _dot(gla_ref[...], wog_ref[...]) + _dot(dif_ref[...], wod_ref[...])
        x = x + mgate_ref[0] * _rms(y, mg_ref[...])
    shift_ref, scale_ref, gate_ref, g1_ref, g2_ref, win_ref, wout_ref, o_ref = refs
    h = _rms(x, g1_ref[...]) * (1.0 + scale_ref[0]) + shift_ref[0]
    hb = h.astype(BF16)
    y = jnp.zeros(x.shape, F32)
    for c in range(D_FF // FF_CHUNK):
        lo, hi = c * FF_CHUNK, (c + 1) * FF_CHUNK
        g = _dot(hb, win_ref[:, lo:hi])
        u = _dot(hb, win_ref[:, D_FF + lo:D_FF + hi])
        a = (g * jax.nn.sigmoid(g) * u).astype(BF16)
        y = y + _dot(a, wout_ref[lo:hi, :])
    o_ref[...] = x + res_w * gate_ref[0] * _rms(y, g2_ref[...])


def _mod_spec(r, tiles_per_mod, col):
    return pl.BlockSpec((1, r, D_MODEL), lambda i: (i // tiles_per_mod, 0, col))


def _ffn_call(x, mod, g1, g2, w_in, w_out, layer, which, res_w, tm, tiles_per_mod, merge=None):
    n = x.shape[0]
    mod_all, sub = mod
    r = mod_all.shape[1]
    tok = pl.BlockSpec((tm, D_MODEL), lambda i: (i, 0))
    merge_specs, merge_args = [], []
    if merge is not None:
        oa, gla, dif, (mmod, msub), mg, w_oa, w_og, w_od = merge
        wide = lambda a: pl.BlockSpec((tm, a.shape[1]), lambda i: (i, 0))
        merge_specs = [wide(oa), wide(gla), wide(dif), _mod_spec(r, tiles_per_mod, 3 * msub + 2),
                       _const_spec((1, D_MODEL)), _const_spec(w_oa.shape), _const_spec(w_og.shape),
                       _const_spec(w_od.shape)]
        merge_args = [oa, gla, dif, mmod, mg, w_oa, w_og, w_od]
    pick = lambda rows, cols: pl.BlockSpec((None, None, rows, cols), lambda i: (layer, which, 0, 0),
                                           pipeline_mode=pl.Buffered(1))
    return pl.pallas_call(
        functools.partial(_ffn_kernel, res_w=res_w, merge=merge is not None),
        grid=(n // tm,),
        in_specs=[tok] + merge_specs + [_mod_spec(r, tiles_per_mod, 3 * sub + k) for k in range(3)]
        + [_const_spec((1, D_MODEL)), _const_spec((1, D_MODEL)), pick(D_MODEL, 2 * D_FF), pick(D_FF, D_MODEL)],
        out_specs=tok,
        out_shape=jax.ShapeDtypeStruct((n, D_MODEL), F32),
        compiler_params=_params(("parallel",)),
        name="ffn",
    )(x, *merge_args, mod_all, mod_all, mod_all, g1, g2, w_in, w_out)


_STACKED_OUTS = (
    ("ka", 1, W_A), ("va", 1, W_A), ("kc", N_HEADS, HEAD_PAD), ("vc", N_HEADS, HEAD_PAD), ("logf", 1, N_HEADS),
)
_LAYER_OUTS = (
    ("qa_p", W_PAD, BF16), ("ka_p", W_PAD, BF16), ("va_p", W_PAD, BF16),
    ("qc1_p", W_PAD, BF16), ("qc2_p", W_PAD, BF16), ("kc_b", W_C, BF16), ("vc_b", W_C, BF16),
    ("qb", LANES, F32), ("kb", LANES, F32), ("vb", W_B, F32), ("loga", LANES, F32), ("rb", W_B, F32),
)
_NORM_ROWS = ("qa_p", "ka_p", "qc1_p", "qc2_p", "kc_b")


def _proj_kernel(x_ref, shift_ref, scale_ref, g1_ref, w_ref, wkvt_ref, bf_ref, wup_ref, bup_ref, vone_ref, *refs,
                 n_alias, with_qa, transposed_a):
    refs = refs[n_alias:]
    names = [n for n, _, _ in _STACKED_OUTS] + [n for n, _, _ in _LAYER_OUTS] + (["qa"] if with_qa else [])
    o = dict(zip(names + ["norms"], refs))
    tm = x_ref.shape[0]
    x = x_ref[...]
    hb = (_rms(x, g1_ref[...]) * (1.0 + scale_ref[0]) + shift_ref[0]).astype(BF16)

    def seg(name):
        lo, hi = _SEG[name]
        return _dot(hb, w_ref[:, lo:hi])

    def store_heads(ref, val):
        for d in range(ref.shape[0]):
            for h in range(N_HEADS):
                ref[d, pl.ds(h, tm, stride=N_HEADS), :] = val[:, h * HEAD_PAD:(h + 1) * HEAD_PAD]

    def store_slots(ref, val):
        for d in range(ref.shape[0]):
            ref[d] = val

    if transposed_a:
        kvt = _nt_dot(wkvt_ref[...], hb)
        store_slots(o["ka"], kvt[None, :W_A])
        store_slots(o["va"], kvt[None, W_A:])
    else:
        store_slots(o["ka"], seg("ka"))
        store_slots(o["va"], seg("va"))
    kc = seg("kc")
    store_heads(o["kc"], kc)
    rounded = {"kc_b": kc.astype(BF16)}
    o["kc_b"][...] = rounded["kc_b"]
    vc = seg("vc")
    store_heads(o["vc"], vc)
    o["vc_b"][...] = vc.astype(BF16)
    def pad_heads(val):
        zero = jnp.zeros((tm, HEAD_PAD - HD_A), val.dtype)
        pieces = []
        for h in range(N_HEADS):
            pieces += [val[:, h * HD_A:(h + 1) * HD_A], zero]
        return jnp.concatenate(pieces, axis=1)

    qa = seg("qa")
    if with_qa:
        o["qa"][...] = qa * (HD_A ** -0.5)
    rounded["qa_p"] = pad_heads((qa * (HD_A ** -0.5 * LOG2E)).astype(BF16))
    rounded["ka_p"] = pad_heads(seg("ka").astype(BF16))
    o["qa_p"][...] = rounded["qa_p"]
    o["ka_p"][...] = rounded["ka_p"]
    o["va_p"][...] = (pad_heads(seg("va")) + vone_ref[...]).astype(BF16)
    qc = seg("qc") * (HD_C ** -0.5 * LOG2E)
    first_map = lax.broadcasted_iota(jnp.int32, qc.shape, 1) % HEAD_PAD < HD_C
    rounded["qc1_p"] = jnp.where(first_map, qc, 0.0).astype(BF16)
    rounded["qc2_p"] = jnp.where(first_map, 0.0, qc).astype(BF16)
    o["qc1_p"][...] = rounded["qc1_p"]
    o["qc2_p"][...] = rounded["qc2_p"]
    fa = seg("fa") + bf_ref[...]
    store_slots(o["logf"], _log_sigmoid(fa)[:, :N_HEADS])
    o["qb"][...] = seg("qb")
    o["kb"][...] = seg("kb")
    o["vb"][...] = seg("vb")
    o["rb"][...] = seg("rb")
    gb = seg("gb").astype(BF16)
    o["loga"][...] = _log_sigmoid(_dot(gb, wup_ref[...]) + bup_ref[...]) * (1.0 / GLA_TAU)
    lane = lax.broadcasted_iota(jnp.int32, (8, LANES), 1)
    tile_norms = jnp.zeros((8, LANES), F32)
    for a, name in enumerate(_NORM_ROWS):
        sq = rounded[name].astype(F32)
        sq = sq * sq
        for h in range(N_HEADS):
            ss = jnp.sum(sq[:, h * HEAD_PAD:(h + 1) * HEAD_PAD], axis=1, keepdims=True)
            tile_norms = jnp.where(lane == N_HEADS * a + h, jnp.max(ss, axis=0, keepdims=True), tile_norms)
    o["norms"][0] = tile_norms


def _proj_call(x, mod, g1, w_ext, w_kvt, bf, wup, bup, vone, tm, tiles_per_mod, layer, depth, stacked, with_qa,
               seq_len):
    n = x.shape[0]
    mod_all, sub = mod
    r = mod_all.shape[1]
    tok = lambda w: pl.BlockSpec((tm, w), lambda i: (i, 0))
    layer_outs = _LAYER_OUTS + ((("qa", W_A, F32),) if with_qa else ())
    alias_in = [] if stacked is None else [stacked[nm] for nm, _, _ in _STACKED_OUTS]
    transposed_a = seq_len % tm == 0
    tps = max(seq_len // tm, 1)
    stacked_specs, stacked_shapes = [], []
    slots = depth if stacked is None else 1
    for nm, rows, w in _STACKED_OUTS:
        if transposed_a and nm in ("ka", "va"):
            stacked_specs.append(pl.BlockSpec((slots, 1, w, tm), lambda i: (layer, i // tps, 0, i % tps)))
            stacked_shapes.append(jax.ShapeDtypeStruct((depth, n // seq_len, w, seq_len), F32))
        else:
            stacked_specs.append(pl.BlockSpec((slots, tm * rows, w), lambda i: (layer, i, 0)))
            stacked_shapes.append(jax.ShapeDtypeStruct((depth, n * rows, w), F32))
    n_fixed = 10
    outs = pl.pallas_call(
        functools.partial(_proj_kernel, n_alias=len(alias_in), with_qa=with_qa, transposed_a=transposed_a),
        grid=(n // tm,),
        in_specs=[tok(D_MODEL), _mod_spec(r, tiles_per_mod, 3 * sub), _mod_spec(r, tiles_per_mod, 3 * sub + 1),
                  _const_spec((1, D_MODEL)), _const_spec((D_MODEL, N_EXT)), _const_spec((2 * W_A, D_MODEL)),
                  _const_spec((1, LANES)), _const_spec((LANES, LANES)), _const_spec((1, LANES)),
                  _const_spec((1, W_PAD))]
        + [pl.BlockSpec(memory_space=pl.ANY)] * len(alias_in),
        out_specs=stacked_specs + [tok(w) for _, w, _ in layer_outs]
        + [pl.BlockSpec((1, 8, LANES), lambda i: (i, 0, 0))],
        out_shape=stacked_shapes + [jax.ShapeDtypeStruct((n, w), dt) for _, w, dt in layer_outs]
        + [jax.ShapeDtypeStruct((n // tm, 8, LANES), F32)],
        input_output_aliases={n_fixed + k: k for k in range(len(alias_in))},
        compiler_params=_params(("parallel",)),
        name="proj",
    )(x, mod_all, mod_all, g1, w_ext, w_kvt, bf, wup, bup, vone, *alias_in)
    names = [nm for nm, _, _ in _STACKED_OUTS] + [nm for nm, _, _ in layer_outs] + ["norms"]
    return dict(zip(names, outs))


def _cumsum_kernel(x_ref, o_ref):
    nb = x_ref.shape[1]
    r = lax.broadcasted_iota(jnp.int32, (LANES, LANES), 0)
    c = lax.broadcasted_iota(jnp.int32, (LANES, LANES), 1)
    within = (r <= c).astype(F32)
    rr = lax.broadcasted_iota(jnp.int32, (nb, nb), 0)
    cc = lax.broadcasted_iota(jnp.int32, (nb, nb), 1)
    before = (cc < rr).astype(F32)
    for s in range(x_ref.shape[0]):
        x = x_ref[s]
        local = _dot_exact_rhs(x, within)
        tot = jnp.broadcast_to(local[:, LANES - 1:LANES], (nb, LANES))
        o_ref[s] = local + _dot_exact_lhs(before, tot)


def _cumsum_call(x):
    rows, nb, _ = x.shape
    group = math.gcd(rows, 8)
    spec = pl.BlockSpec((group, nb, LANES), lambda i: (i, 0, 0))
    return pl.pallas_call(
        _cumsum_kernel, grid=(rows // group,), in_specs=[spec], out_specs=spec,
        out_shape=jax.ShapeDtypeStruct(x.shape, F32),
        compiler_params=_params(("parallel",)), name="cumsum",
    )(x)


def _lane_tile(x, width):
    return jnp.concatenate([x] * (width // LANES), axis=1)


GROUP_HEADS = 2
W_GROUP = GROUP_HEADS * HEAD_PAD


def _causal_sweep(i, first, tq, tk, tiles_per_iter, tile):
    r = tq // tk

    def body(jj, carry):
        for u in range(tiles_per_iter):
            tile(jj * tiles_per_iter + u, slice(0, tq), None)
        return carry

    assert r % tiles_per_iter == 0 and tiles_per_iter in (1, 2)
    lead = 0
    if tiles_per_iter == 2:
        lead = lax.rem(i * r - first, 2)

        @pl.when(lead == 1)
        def _():
            tile(first, slice(0, tq), None)

    lax.fori_loop(lax.div(first + lead, tiles_per_iter), lax.div(i * r, tiles_per_iter), body, 0)
    for d in range(r):
        tile(i * r + d, slice(d * tk, (d + 1) * tk), d * tk)
        if d + 1 < r:
            tile(i * r + d, slice((d + 1) * tk, tq), None)


def _fox_kernel(first_ref, q_ref, k_ref, v_ref, c_ref, o_ref, m_sc, acc_sc, *, tk, tiles_per_iter):
    i = pl.program_id(2)
    first = first_ref[pl.program_id(0), pl.program_id(1), i]
    tq = q_ref.shape[1]
    m_sc[...] = jnp.full(m_sc.shape, NEG, F32)
    acc_sc[...] = jnp.zeros(acc_sc.shape, F32)
    row = lax.broadcasted_iota(jnp.int32, (tk, tk), 0)
    col = lax.broadcasted_iota(jnp.int32, (tk, tk), 1)
    causal = col <= row

    def tile(j, rows, local):
        ks = pl.ds(pl.multiple_of(j * tk, tk), tk)
        for h in range(GROUP_HEADS):
            hs = slice(h * HEAD_PAD, (h + 1) * HEAD_PAD)
            c_first = c_ref[0, h, pl.ds(i * (tq // tk), 1), :][:, 0:1]
            t = _nt_dot(q_ref[0, rows, hs], k_ref[0, ks, hs]) + (c_first - c_ref[0, h, pl.ds(j, 1), :]) * LOG2E
            if local is not None:
                t = jnp.where(causal, t, NEG)
            m_old = m_sc[h, rows, :]
            m_new = jnp.maximum(m_old, jnp.max(t, axis=1, keepdims=True))
            p = jnp.exp2(t - _lane_tile(m_new, tk)).astype(BF16)
            acc_sc[h, rows, :] = jnp.exp2(m_old - m_new) * acc_sc[h, rows, :] + _dot(p, v_ref[0, ks, hs])
            m_sc[h, rows, :] = m_new

    _causal_sweep(i, first, tq, tk, tiles_per_iter, tile)
    outs = []
    for h in range(GROUP_HEADS):
        acc = acc_sc[h]
        outs.append((acc[:, :HD_A] / acc[:, HD_A:HD_A + 1]).astype(BF16))
    o_ref[0] = jnp.concatenate(outs, axis=1)


def _resident_spec(t, width):
    return pl.BlockSpec((1, t, width), lambda b_, g, i, first: (b_, 0, g), pipeline_mode=pl.Buffered(1))


def _fox_call(first, qa_p, ka_p, va_p, c, tq, tk):
    b, t, _ = qa_p.shape
    qspec = pl.BlockSpec((1, tq, W_GROUP), lambda b_, g, i, first: (b_, i, g))
    grid_spec = pltpu.PrefetchScalarGridSpec(
        num_scalar_prefetch=1,
        grid=(b, N_HEADS // GROUP_HEADS, t // tq),
        in_specs=[qspec, _resident_spec(t, W_GROUP), _resident_spec(t, W_GROUP),
                  pl.BlockSpec((1, GROUP_HEADS, t // tk, tk), lambda b_, g, i, first: (b_, g, 0, 0))],
        out_specs=pl.BlockSpec((1, tq, GROUP_HEADS * HD_A), lambda b_, g, i, first: (b_, i, g)),
        scratch_shapes=[pltpu.VMEM((GROUP_HEADS, tq, LANES), F32), pltpu.VMEM((GROUP_HEADS, tq, HEAD_PAD), F32)])
    return pl.pallas_call(
        functools.partial(_fox_kernel, tk=tk, tiles_per_iter=1),
        grid_spec=grid_spec,
        out_shape=jax.ShapeDtypeStruct((b, t, W_A), BF16),
        compiler_params=_params(("parallel", "parallel", "arbitrary")),
        name="fox",
    )(first, qa_p, ka_p, va_p, c.reshape(b, N_HEADS, t // tk, tk))


def _diff_lambda(lp, lam_init):
    a = jnp.sum(lp[0:1] * lp[1:2], axis=1, keepdims=True)
    b = jnp.sum(lp[2:3] * lp[3:4], axis=1, keepdims=True)
    return jnp.exp(a) - jnp.exp(b) + lam_init


def _diag_distance(tk):
    q = np.arange(tk)[:, None]
    k = np.arange(tk)[None, :]
    return np.where(k // CHUNK <= q // CHUNK, 2 * np.minimum(q, k) - k, HIDDEN).astype(np.float32)


DIFF_GROUP = 1
W_DIFF_GROUP = DIFF_GROUP * HEAD_PAD


def _diff_kernel(first_ref, q1_ref, q2_ref, k_ref, v_ref, slope_ref, lam_ref, g_ref, rel_ref, o_ref,
                 m_sc, l_sc, a_sc, *, tk, tiles_per_iter, lam_init):
    g = pl.program_id(1)
    i = pl.program_id(2)
    first = first_ref[pl.program_id(0), g, i]
    tq = q1_ref.shape[1]
    m_sc[...] = jnp.full(m_sc.shape, NEG, F32)
    l_sc[...] = jnp.zeros(l_sc.shape, F32)
    a_sc[...] = jnp.zeros(a_sc.shape, F32)
    lane = lax.broadcasted_iota(jnp.int32, (1, tk), 1)

    def tile(j, rows, local):
        ks = pl.ds(pl.multiple_of(j * tk, tk), tk)
        if local is None:
            rel = (lane + (j * tk - i * tq)).astype(F32)
        else:
            rel = rel_ref[...] + float(local)
        for h in range(DIFF_GROUP):
            hs = slice(h * HEAD_PAD, (h + 1) * HEAD_PAD)
            bias = (slope_ref[g * DIFF_GROUP + h][:, 0:1] * LOG2E) * rel
            k = k_ref[0, ks, hs]
            v = v_ref[0, ks, hs]
            nrows = rows.stop - rows.start
            scores = _nt_dot(jnp.concatenate([q1_ref[0, rows, hs], q2_ref[0, rows, hs]], axis=0), k)
            probs, alphas = [], []
            for m in range(2):
                n = 2 * h + m
                t = scores[m * nrows:(m + 1) * nrows] + bias
                m_old = m_sc[n, rows, :]
                m_new = jnp.maximum(m_old, jnp.max(t, axis=1, keepdims=True))
                alpha = jnp.exp2(m_old - m_new)
                p = jnp.exp2(t - _lane_tile(m_new, tk))
                l_sc[n, rows, :] = alpha * l_sc[n, rows, :] + jnp.sum(p, axis=1, keepdims=True)
                m_sc[n, rows, :] = m_new
                probs.append(p.astype(BF16))
                alphas.append(alpha)
            pv = _dot(jnp.concatenate(probs, axis=0), v)
            for m in range(2):
                n = 2 * h + m
                a_sc[n, rows, :] = alphas[m] * a_sc[n, rows, :] + pv[m * nrows:(m + 1) * nrows]

    _causal_sweep(i, first, tq, tk, tiles_per_iter, tile)
    lam = _diff_lambda(lam_ref[...], lam_init)
    for h in range(DIFF_GROUP):
        out = a_sc[2 * h] / l_sc[2 * h] - lam * (a_sc[2 * h + 1] / l_sc[2 * h + 1])
        o_ref[0, :, h * HEAD_PAD:(h + 1) * HEAD_PAD] = (_rms(out, g_ref[...]) * (1.0 - lam_init)).astype(BF16)


def _diff_call(first, qc1_p, qc2_p, kc_b, vc_b, slopes, lam_p, g_diff, lam_init, tq, tk):
    assert CHUNK == 64 and tk % CHUNK == 0
    b, t, _ = qc1_p.shape
    qspec = pl.BlockSpec((1, tq, W_DIFF_GROUP), lambda b_, g, i, first: (b_, i, g))
    stat = pltpu.VMEM((2 * DIFF_GROUP, tq, LANES), F32)
    grid_spec = pltpu.PrefetchScalarGridSpec(
        num_scalar_prefetch=1,
        grid=(b, N_HEADS // DIFF_GROUP, t // tq),
        in_specs=[qspec, qspec, _resident_spec(t, W_DIFF_GROUP), _resident_spec(t, W_DIFF_GROUP),
                  pl.BlockSpec((N_HEADS, 1, LANES), lambda b_, g, i, first: (0, 0, 0)),
                  pl.BlockSpec((4, HD_C), lambda b_, g, i, first: (0, 0)),
                  pl.BlockSpec((1, HEAD_PAD), lambda b_, g, i, first: (0, 0)),
                  pl.BlockSpec((tk, tk), lambda b_, g, i, first: (0, 0), pipeline_mode=pl.Buffered(1))],
        out_specs=qspec,
        scratch_shapes=[stat, stat, pltpu.VMEM((2 * DIFF_GROUP, tq, HEAD_PAD), F32)])
    return pl.pallas_call(
        functools.partial(_diff_kernel, tk=tk, tiles_per_iter=2, lam_init=lam_init),
        grid_spec=grid_spec,
        out_shape=jax.ShapeDtypeStruct((b, t, W_C), BF16),
        compiler_params=_params(("parallel", "parallel", "arbitrary")),
        name="diff",
    )(first, qc1_p, qc2_p, kc_b, vc_b, slopes, lam_p, g_diff, jnp.asarray(_diag_distance(tk)))


def _gla_kernel(q_ref, k_ref, v_ref, la_ref, r_ref, s0_ref, g_ref, o_ref, s_ref, *, chunk):
    ti = pl.program_id(1)
    n_chunks = q_ref.shape[1] // chunk
    n_seq = q_ref.shape[0]

    @pl.when(ti == 0)
    def _():
        s_ref[...] = s0_ref[...]

    tri_r = lax.broadcasted_iota(jnp.int32, (chunk, chunk), 0)
    tri_c = lax.broadcasted_iota(jnp.int32, (chunk, chunk), 1)
    incl = (tri_c <= tri_r).astype(F32)
    srow = lax.broadcasted_iota(jnp.int32, (N_HEADS * chunk, LANES), 0) // chunk
    slane = lax.broadcasted_iota(jnp.int32, (N_HEADS * chunk, LANES), 1) // DK_B
    qmask = srow == slane
    arow = lax.broadcasted_iota(jnp.int32, (N_HEADS * chunk, chunk), 0) % chunk
    acol = lax.broadcasted_iota(jnp.int32, (N_HEADS * chunk, chunk), 1)
    causal = acol <= arow
    vlane = lax.broadcasted_iota(jnp.int32, (chunk, W_B), 1) // DV_B
    st_row = lax.broadcasted_iota(jnp.int32, (W_B, LANES), 0) // DV_B
    st_lane = lax.broadcasted_iota(jnp.int32, (W_B, LANES), 1) // DK_B
    diag_blocks = st_row == st_lane

    for c, sq in [(c, sq) for c in range(n_chunks) for sq in range(n_seq)]:
        sl = slice(c * chunk, (c + 1) * chunk)
        cb = _dot_exact_lhs(incl, la_ref[sq, sl, :])
        cb_last = cb[chunk - 1:chunk, :]
        q = q_ref[sq, sl, :] * (DK_B ** -0.5)
        k = k_ref[sq, sl, :]
        v = v_ref[sq, sl, :]
        vb = v.astype(BF16)
        qe = q * jnp.exp(cb)
        ke = (k * jnp.exp(-cb)).astype(BF16)
        kl = (k * jnp.exp(cb_last - cb)).astype(BF16)
        decay = jnp.exp(cb_last)
        qstack = jnp.where(qmask, jnp.concatenate([qe] * N_HEADS, axis=0), 0.0).astype(BF16)
        a = jnp.where(causal, _nt_dot(qstack, ke), 0.0).astype(BF16)
        oi = _dot(a, vb)
        o = jnp.zeros((chunk, W_B), F32)
        for h in range(N_HEADS):
            o = o + jnp.where(vlane == h, oi[h * chunk:(h + 1) * chunk], 0.0)
        st = s_ref[sq]
        o = o + _nt_dot(qe.astype(BF16), st.astype(BF16))
        upd = _dot(v.T.astype(BF16), kl)
        s_ref[sq] = decay * st + jnp.where(diag_blocks, upd, 0.0)
        o2 = o * o
        ms = jnp.zeros((chunk, W_B), F32)
        for h in range(N_HEADS):
            head_mean = jnp.sum(jnp.where(vlane == h, o2, 0.0), axis=1, keepdims=True) * (1.0 / DV_B)
            ms = jnp.where(vlane == h, head_mean, ms)
        r = r_ref[sq, sl, :]
        o_ref[sq, sl, :] = (o * lax.rsqrt(ms + EPS) * g_ref[...] * (r * jax.nn.sigmoid(r))).astype(BF16)


def _gla_call(qb, kb, vb, loga, rb, s0_t, g_gla, tg, chunk, group):
    b, t, _ = qb.shape
    tok = lambda w: pl.BlockSpec((group, tg, w), lambda b_, i: (b_, i, 0))
    sspec = pl.BlockSpec((group, W_B, LANES), lambda b_, i: (b_, 0, 0))
    return pl.pallas_call(
        functools.partial(_gla_kernel, chunk=chunk),
        grid=(b // group, t // tg),
        in_specs=[tok(LANES), tok(LANES), tok(W_B), tok(LANES), tok(W_B), sspec,
                  pl.BlockSpec((1, W_B), lambda b_, i: (0, 0))],
        out_specs=[tok(W_B), sspec],
        out_shape=[jax.ShapeDtypeStruct((b, t, W_B), BF16), jax.ShapeDtypeStruct((b, W_B, LANES), F32)],
        compiler_params=_params(("parallel", "arbitrary")),
        name="gla",
    )(qb, kb, vb, loga, rb, s0_t, g_gla)


SAMPLE_GROUP = 4

def _softmax_two(t_past, t_new, exp_fn):
    m = jnp.maximum(jnp.max(t_past, axis=1, keepdims=True), jnp.max(t_new, axis=1, keepdims=True))
    p_past = exp_fn(t_past - m)
    p_new = exp_fn(t_new - m)
    l = jnp.sum(p_past, axis=1, keepdims=True) + jnp.sum(p_new, axis=1, keepdims=True)
    return p_past, p_new, l


def _fox_s_kernel(q_ref, kn_ref, vn_ref, kpt_ref, vpt_ref, c_ref, o_ref):
    n_seq, t, past = q_ref.shape[0], q_ref.shape[1], kpt_ref.shape[2]
    lane_head = lax.broadcasted_iota(jnp.int32, (t, W_A), 1) // HD_A
    row = lax.broadcasted_iota(jnp.int32, (t, t), 0)
    col = lax.broadcasted_iota(jnp.int32, (t, t), 1)
    for s in range(n_seq):
        q = q_ref[s]
        kn = kn_ref[s * t:(s + 1) * t, :].astype(BF16)
        vn = vn_ref[s * t:(s + 1) * t, :].astype(BF16)
        kpt = kpt_ref[s].astype(BF16)
        vpt = vpt_ref[s].astype(BF16)
        o = jnp.zeros((t, W_A), F32)
        for h in range(N_HEADS):
            qh = jnp.where(lane_head == h, q, 0.0).astype(BF16)
            ch = c_ref[s, h:h + 1, :]
            cref = ch[:, past:past + 1]
            t_past = _dot(qh, kpt) + (cref - ch[:, :past])
            t_new = jnp.where(col <= row, _nt_dot(qh, kn) + (cref - ch[:, past:past + t]), NEG)
            p_past, p_new, l = _softmax_two(t_past, t_new, jnp.exp)
            oh = (_nt_dot(p_past.astype(BF16), vpt) + _dot(p_new.astype(BF16), vn)) / l
            o = o + jnp.where(lane_head == h, oh, 0.0)
        o_ref[s] = o.astype(BF16)


def _fox_s_call(layer, qa, ka, va, cache_kt, cache_vt, c):
    b, t, _ = qa.shape
    past = cache_kt.shape[3]
    sg = math.gcd(b, SAMPLE_GROUP)
    new = pl.BlockSpec((sg, t, W_A), lambda i: (i, 0, 0))
    new_l = lambda: pl.BlockSpec((None, sg * t, W_A), lambda i: (layer, i, 0))
    old_l = lambda: pl.BlockSpec((None, sg, W_A, past), lambda i: (layer, i, 0, 0))
    return pl.pallas_call(
        _fox_s_kernel,
        grid=(b // sg,),
        in_specs=[new, new_l(), new_l(), old_l(), old_l(),
                  pl.BlockSpec((sg, N_HEADS, c.shape[2]), lambda i: (i, 0, 0))],
        out_specs=new,
        out_shape=jax.ShapeDtypeStruct((b, t, W_A), BF16),
        compiler_params=_params(("parallel",)),
        name="fox_s",
    )(qa, ka, va, cache_kt, cache_vt, c)


def _diff_s_kernel(q1_ref, q2_ref, kn_ref, vn_ref, kp_ref, vp_ref, slope_ref, lam_ref, g_ref, o_ref, *, lam_init):
    n_seq, t, past = q1_ref.shape[0], q1_ref.shape[1], kp_ref.shape[1] // N_HEADS
    lam = _diff_lambda(lam_ref[...], lam_init)
    row = lax.broadcasted_iota(jnp.int32, (t, t), 0) + past
    col = lax.broadcasted_iota(jnp.int32, (t, t), 1) + past
    visible_new = lax.shift_right_logical(col, 6) <= lax.shift_right_logical(row, 6)
    dist_new = jnp.abs(row - col).astype(F32)
    prow = lax.broadcasted_iota(jnp.int32, (t, past), 0) + past
    pcol = lax.broadcasted_iota(jnp.int32, (t, past), 1)
    visible_past = lax.shift_right_logical(pcol, 6) <= lax.shift_right_logical(prow, 6)
    dist_past = (prow - pcol).astype(F32)
    for s, h in [(s, h) for s in range(n_seq) for h in range(N_HEADS)]:
        hs = slice(h * HEAD_PAD, (h + 1) * HEAD_PAD)
        slope = slope_ref[h][:, 0:1] * LOG2E
        kn = kn_ref[pl.ds(s * t * N_HEADS + h, t, stride=N_HEADS), :].astype(BF16)
        vn = vn_ref[pl.ds(s * t * N_HEADS + h, t, stride=N_HEADS), :].astype(BF16)
        kp = kp_ref[s, pl.ds(h, past, stride=N_HEADS), :].astype(BF16)
        vp = vp_ref[s, pl.ds(h, past, stride=N_HEADS), :].astype(BF16)
        outs = []
        for q_ref in (q1_ref, q2_ref):
            q = q_ref[s, :, hs]
            t_past = jnp.where(visible_past, _nt_dot(q, kp) - slope * dist_past, NEG)
            t_new = jnp.where(visible_new, _nt_dot(q, kn) - slope * dist_new, NEG)
            p_past, p_new, l = _softmax_two(t_past, t_new, jnp.exp2)
            outs.append((_dot(p_past.astype(BF16), vp) + _dot(p_new.astype(BF16), vn)) / l)
        out = outs[0] - lam * outs[1]
        o_ref[s, :, hs] = (_rms(out, g_ref[...]) * (1.0 - lam_init)).astype(BF16)


def _diff_s_call(layer, qc1_p, qc2_p, kc, vc, cache_k, cache_v, slopes, lam_p, g_diff, lam_init):
    assert CHUNK == 64
    b, t, _ = qc1_p.shape
    rows_past = cache_k.shape[2]
    sg = math.gcd(b, SAMPLE_GROUP)
    new = lambda: pl.BlockSpec((sg, t, W_C), lambda i: (i, 0, 0))
    new_l = lambda: pl.BlockSpec((None, sg * t * N_HEADS, HEAD_PAD), lambda i: (layer, i, 0))
    old_l = lambda: pl.BlockSpec((None, sg, rows_past, HEAD_PAD), lambda i: (layer, i, 0, 0))
    return pl.pallas_call(
        functools.partial(_diff_s_kernel, lam_init=lam_init),
        grid=(b // sg,),
        in_specs=[new(), new(), new_l(), new_l(), old_l(), old_l(),
                  pl.BlockSpec((N_HEADS, 1, LANES), lambda i: (0, 0, 0)),
                  pl.BlockSpec((4, HD_C), lambda i: (0, 0)),
                  pl.BlockSpec((1, HEAD_PAD), lambda i: (0, 0))],
        out_specs=new(),
        out_shape=jax.ShapeDtypeStruct((b, t, W_C), BF16),
        compiler_params=_params(("parallel",)),
        name="diff_s",
    )(qc1_p, qc2_p, kc, vc, cache_k, cache_v, slopes, lam_p, g_diff)


def _pad_lanes(w, width=LANES):
    return jnp.pad(w, ((0, 0), (0, width - w.shape[1])))


def _layer_params(l, w_in, b_f, w_gla_up, b_gla_up, g_gla, g_diff, w_out):
    offs = [int(o) for o in np.cumsum(PROJ_SIZES)[:-1]]
    qa, ka, va, fa, qb, kb, vb, gb, rb, qc, kc, vc = jnp.split(w_in[l], offs, axis=1)
    segs = {
        "ka": ka, "va": va, "kc": kc, "vc": vc, "qa": qa, "qc": qc,
        "fa": _pad_lanes(fa), "qb": qb, "kb": kb, "vb": vb, "rb": rb, "gb": _pad_lanes(gb),
    }
    w_ext = jnp.concatenate([segs[n] for n, _ in _SEG_WIDTHS], axis=1).astype(BF16)
    wo = w_out[l]
    vone = np.zeros((1, W_PAD), np.float32)
    vone[0, HD_A::HEAD_PAD] = 1.0
    return dict(
        w_ext=w_ext, w_kvt=jnp.concatenate([ka, va], axis=1).T.astype(BF16),
        bf=_pad_lanes(b_f[l][None, :]),
        wup=jnp.pad(w_gla_up[l], ((0, LANES - GLA_RANK), (0, 0))).astype(BF16),
        bup=b_gla_up[l][None, :],
        vone=jnp.asarray(vone),
        g_gla=jnp.tile(g_gla[l], N_HEADS)[None, :], g_diff=g_diff[l][None, :],
        w_oa=wo[:W_A].astype(BF16),
        w_og=wo[W_A:W_A + W_B].astype(BF16), w_od=wo[W_A + W_B:].astype(BF16),
    )


def _state_to_blockdiag_t(s):
    eye = jnp.eye(N_HEADS, dtype=s.dtype)
    return jnp.einsum("bhkv,hg->bhvgk", s, eye).reshape(s.shape[0], W_B, N_HEADS * DK_B)


def _blockdiag_t_to_state(st):
    b = st.shape[0]
    s5 = st.reshape(b, N_HEADS, DV_B, N_HEADS, DK_B)
    diag = jnp.stack([s5[:, h, :, h, :] for h in range(N_HEADS)], axis=1)
    return jnp.swapaxes(diag, 2, 3)


SKIP_LOG2 = 160.0
NORM_SLACK = 1.01


def _tile_max(x, factor):
    return jnp.max(x.reshape(x.shape[:-1] + (x.shape[-1] // factor, factor)), axis=-1)


def _first_tiles(qn, kn, gap, tq, tk, group):
    b, h, nq = qn.shape
    r = tq // tk
    kn_diag = _tile_max(kn, r)
    bound = NORM_SLACK * (qn[..., None] * kn[:, :, None, :] + (qn * kn_diag)[..., None]) + gap
    dead = (bound < -SKIP_LOG2).reshape(b, h // group, group, nq, kn.shape[-1]).all(axis=2)
    lead = jnp.sum(jnp.cumprod(dead.astype(jnp.int32), axis=-1), axis=-1)
    return jnp.minimum(lead, jnp.arange(nq, dtype=jnp.int32) * r).astype(jnp.int32)


def _tile(n, pref):
    t = min(n, pref)
    assert n % t == 0, (n, t)
    return t


def _trunk(x, mods, params, ffn_w, g_norm, lam_params, mixer, tm, tiles_per_mod, with_qa):
    b, t, _ = x.shape
    n = b * t
    depth = len(params)
    xf = x.reshape(n, D_MODEL)
    stacked = None
    gla_states = []
    for l in range(depth):
        p = params[l]
        gn = lambda k: g_norm[l, k][None, :]
        xf = _ffn_call(xf, (mods[l], 0), gn(0), gn(1), *ffn_w, l, 0, 0.5, tm, tiles_per_mod)
        pr = _proj_call(xf, (mods[l], 1), gn(2), p["w_ext"], p["w_kvt"], p["bf"], p["wup"], p["bup"], p["vone"], tm,
                        tiles_per_mod, l, depth, stacked, with_qa, t)
        stacked = {nm: pr[nm] for nm, _, _ in _STACKED_OUTS}
        lam_init = 0.8 - 0.6 * math.exp(-0.3 * l)
        oa, gla, dif, s_fin, w_oa = mixer(l, pr, p, lam_params[l], lam_init, b, t)
        gla_states.append(s_fin)
        xf = _ffn_call(xf, (mods[l], 2), gn(4), gn(5), *ffn_w, l, 1, 0.5, tm, tiles_per_mod,
                       merge=(oa, gla, dif, (mods[l], 1), gn(3), w_oa, p["w_og"], p["w_od"]))
    shp = lambda nm, *tail: stacked[nm].reshape((depth, b, t) + tail)
    if stacked["ka"].ndim == 4:
        shp_a = lambda nm: jnp.transpose(stacked[nm].reshape(depth, b, N_HEADS, HD_A, t), (0, 1, 4, 2, 3))
    else:
        shp_a = lambda nm: shp(nm, N_HEADS, HD_A)
    states = [shp_a("ka"), shp_a("va"), shp("logf", N_HEADS), jnp.stack(gla_states),
              shp("kc", N_HEADS, 2 * HD_C), shp("vc", N_HEADS, 2 * HD_C)]
    return xf.reshape(b, t, D_MODEL), states


def kernel(x_prompt, x_sample, c_prompt, c_sample, cache_fox_k, cache_fox_v, cache_fox_logf, state_gla,
           cache_diff_k, cache_diff_v, w_ada, b_ada, g_norm, w_ffn_in, w_ffn_out, w_in, b_f, w_gla_up,
           b_gla_up, g_gla, g_diff, lam_params, w_out):
    depth = w_in.shape[0]
    bp, tp, _ = x_prompt.shape
    bs, ts, _ = x_sample.shape
    past = cache_fox_k.shape[2]

    params = [_layer_params(l, w_in, b_f, w_gla_up, b_gla_up, g_gla, g_diff, w_out) for l in range(depth)]
    ffn_w = (w_ffn_in.astype(BF16), w_ffn_out.astype(BF16))

    slopes = 2.0 ** (-8.0 * jnp.arange(1, N_HEADS + 1, dtype=F32) / N_HEADS)
    slopes = jnp.broadcast_to(slopes[:, None, None], (N_HEADS, 1, LANES))

    rows = bp + bs
    rows_pad = -(-rows // 8) * 8
    c_all = jnp.pad(jnp.concatenate([c_prompt, c_sample], axis=0), ((0, rows_pad - rows), (0, 0)))
    mods_p, mods_s = [], []
    for l in range(depth):
        mod = _ada_call(c_all, w_ada, b_ada[:, None, :], l)
        mods_p.append(mod[:bp, None, :])
        mods_s.append(jnp.repeat(mod[bp:rows], ts, axis=0)[None])

    tm_p = _tile(tp, 512)
    tq = _tile(tp, 1024)
    tk = _tile(tp, 512)
    tg = _tile(tp, 512)

    def prompt_mixer(l, pr, p, lam_p, lam_init, b, t):
        r3 = lambda a: a.reshape(b, t, a.shape[-1])
        logf_t = jnp.swapaxes(r3(pr["logf"][l]), 1, 2).reshape(b * N_HEADS, t // LANES, LANES)
        c = _cumsum_call(logf_t).reshape(b, N_HEADS, t)
        nrm = jnp.sqrt(pr["norms"][:, 0, :N_HEADS * len(_NORM_ROWS)])
        nrm = dict(zip(_NORM_ROWS, jnp.moveaxis(nrm.reshape(b, t // tm_p, len(_NORM_ROWS), N_HEADS), (2, 3), (0, 2))))
        q_tiles = lambda a: _tile_max(a, tq // tm_p)
        k_tiles = lambda a: _tile_max(a, tk // tm_p)
        gap_a = (c[:, :, ::tq][..., None] - c[:, :, tk - 1::tk][:, :, None, :]) * LOG2E
        first_a = _first_tiles(q_tiles(nrm["qa_p"]), k_tiles(nrm["ka_p"]), gap_a, tq, tk, GROUP_HEADS)
        oa = _fox_call(first_a, r3(pr["qa_p"]), r3(pr["ka_p"]), r3(pr["va_p"]), c, tq, tk)
        s0_t = jnp.zeros((b, W_B, N_HEADS * DK_B), F32)
        gla, s_t = _gla_call(r3(pr["qb"]), r3(pr["kb"]), r3(pr["vb"]), r3(pr["loga"]), r3(pr["rb"]), s0_t,
                             p["g_gla"], tg, min(CHUNK, t), math.gcd(b, 2))
        last_key = jnp.arange(tk - 1, t, tk, dtype=F32)[None, :] - jnp.arange(0, t, tq, dtype=F32)[:, None]
        gap_c = slopes[:, 0, :1, None] * LOG2E * last_key[None]
        first_c = _first_tiles(q_tiles(jnp.maximum(nrm["qc1_p"], nrm["qc2_p"])), k_tiles(nrm["kc_b"]), gap_c, tq, tk,
                                 DIFF_GROUP)
        dif = _diff_call(first_c, r3(pr["qc1_p"]), r3(pr["qc2_p"]), r3(pr["kc_b"]), r3(pr["vc_b"]), slopes, lam_p,
                         p["g_diff"], lam_init, tq, tk)
        flat = lambda a: a.reshape(b * t, a.shape[-1])
        return flat(oa), flat(gla), flat(dif), _blockdiag_t_to_state(s_t), p["w_oa"]

    feature_time = lambda a: jnp.transpose(a, (0, 1, 3, 4, 2)).reshape(depth, bs, W_A, past)
    cache_kt, cache_vt = feature_time(cache_fox_k), feature_time(cache_fox_v)
    cache_dk = cache_diff_k.reshape(depth, bs, past * N_HEADS, HEAD_PAD)
    cache_dv = cache_diff_v.reshape(depth, bs, past * N_HEADS, HEAD_PAD)

    def sample_mixer(l, pr, p, lam_p, lam_init, b, t):
        r3 = lambda a: a.reshape(b, t, a.shape[-1])
        total = past + t
        nb = -(-total // (8 * LANES)) * 8
        lf = jnp.concatenate([cache_fox_logf[l].astype(F32), r3(pr["logf"][l])], axis=1)
        lf = jnp.pad(jnp.swapaxes(lf, 1, 2), ((0, 0), (0, 0), (0, nb * LANES - total)))
        c = _cumsum_call(lf.reshape(b * N_HEADS, nb, LANES)).reshape(b, N_HEADS, nb * LANES)
        oa = _fox_s_call(l, r3(pr["qa"]), pr["ka"], pr["va"], cache_kt, cache_vt, c)
        gla, s_t = _gla_call(r3(pr["qb"]), r3(pr["kb"]), r3(pr["vb"]), r3(pr["loga"]), r3(pr["rb"]),
                             _state_to_blockdiag_t(state_gla[l].astype(F32)), p["g_gla"], t, min(CHUNK, t),
                             math.gcd(b, 8))
        dif = _diff_s_call(l, r3(pr["qc1_p"]), r3(pr["qc2_p"]), pr["kc"], pr["vc"], cache_dk, cache_dv,
                           slopes, lam_p, p["g_diff"], lam_init)
        flat = lambda a: a.reshape(b * t, a.shape[-1])
        return flat(oa), flat(gla), flat(dif), _blockdiag_t_to_state(s_t), p["w_oa"]

    y_p, sp = _trunk(x_prompt, mods_p, params, ffn_w, g_norm, lam_params, prompt_mixer, tm_p, tp // tm_p, False)
    y_s, ss = _trunk(x_sample, mods_s, params, ffn_w, g_norm, lam_params, sample_mixer, bs * ts, 1, True)
    return (y_p, y_s, sp[0], sp[1], sp[2], sp[3], sp[4], sp[5], ss[0], ss[1], ss[2], ss[3], ss[4], ss[5])
```

```python
import functools
import math

import jax
import jax.numpy as jnp
import numpy as np
from jax import lax
from jax.experimental import pallas as pl
from jax.experimental.pallas import tpu as pltpu

F32 = jnp.float32
BF16 = jnp.bfloat16

D_MODEL = 1024
N_HEADS = 4
HD_A = 64
DK_B = 32
DV_B = 64
GLA_RANK = 16
GLA_TAU = 16.0
HD_C = 64
CHUNK = 64
W_A = N_HEADS * HD_A
W_B = N_HEADS * DV_B
W_C = N_HEADS * 2 * HD_C
D_FF = ((8 * D_MODEL // 3 + 255) // 256) * 256
N_SUB = 3
PROJ_SIZES = (W_A, W_A, W_A, N_HEADS, N_HEADS * DK_B, N_HEADS * DK_B, W_B, GLA_RANK, W_B, W_C, W_C, W_C)
NEG = -1e30
HIDDEN = -1e34
EPS = 1e-6
LOG2E = math.log2(math.e)

LANES = 128
HEAD_PAD = 128
W_PAD = N_HEADS * HEAD_PAD
VMEM_LIMIT = 56 * 1024 * 1024

_SEG_WIDTHS = (
    ("ka", W_A), ("va", W_A), ("kc", W_C), ("vc", W_C), ("qa", W_A),
    ("qc", W_C),
    ("fa", LANES), ("qb", LANES), ("kb", LANES), ("vb", W_B), ("rb", W_B), ("gb", LANES),
)
_SEG = {}
_off = 0
for _name, _w in _SEG_WIDTHS:
    _SEG[_name] = (_off, _off + _w)
    _off += _w
N_EXT = _off


def _nt_dot(a, b):
    return lax.dot_general(a, b, (((1,), (1,)), ((), ())), preferred_element_type=F32)


def _dot(a, b):
    return jnp.dot(a, b, preferred_element_type=F32)


def _split3(x):
    hi = x.astype(BF16)
    r1 = x - hi.astype(F32)
    mid = r1.astype(BF16)
    lo = (r1 - mid.astype(F32)).astype(BF16)
    return hi, mid, lo


def _dot_exact_rhs(x, m):
    mb = m.astype(BF16)
    hi, mid, lo = _split3(x)
    return _dot(hi, mb) + _dot(mid, mb) + _dot(lo, mb)


def _dot_exact_lhs(m, x):
    mb = m.astype(BF16)
    hi, mid, lo = _split3(x)
    return _dot(mb, hi) + _dot(mb, mid) + _dot(mb, lo)


def _rms(x, g):
    return x * lax.rsqrt(jnp.mean(x * x, axis=-1, keepdims=True) + EPS) * g


def _log_sigmoid(x):
    return jnp.minimum(x, 0.0) - jnp.log1p(jnp.exp(-jnp.abs(x)))


def _const_spec(shape):
    nd = len(shape)
    return pl.BlockSpec(shape, lambda *_: (0,) * nd, pipeline_mode=pl.Buffered(1))


def _params(sem):
    return pltpu.CompilerParams(dimension_semantics=sem, vmem_limit_bytes=VMEM_LIMIT)


def _ada_kernel(c_ref, w_ref, b_ref, o_ref):
    c = c_ref[...]
    a = (c * jax.nn.sigmoid(c)).astype(BF16)
    o_ref[...] = _dot(a, w_ref[...].astype(BF16)) + b_ref[...]


def _ada_call(c, w, b, layer):
    m, n = c.shape[0], w.shape[2]
    assert n == N_SUB * 3 * D_MODEL
    tn = n // 8
    return pl.pallas_call(
        _ada_kernel,
        grid=(n // tn,),
        in_specs=[pl.BlockSpec((m, D_MODEL), lambda i: (0, 0)),
                  pl.BlockSpec((None, D_MODEL, tn), lambda i: (layer, 0, i)),
                  pl.BlockSpec((None, 1, tn), lambda i: (layer, 0, i))],
        out_specs=pl.BlockSpec((m, tn), lambda i: (0, i)),
        out_shape=jax.ShapeDtypeStruct((m, n), F32),
        compiler_params=_params(("parallel",)),
        name="ada",
    )(c, w, b)


FF_CHUNK = 256


def _ffn_kernel(x_ref, *refs, res_w, merge):
    x = x_ref[...]
    if merge:
        oa_ref, gla_ref, dif_ref, mgate_ref, mg_ref, woa_ref, wog_ref, wod_ref = refs[:8]
        refs = refs[8:]
        y = _dot(oa_ref[...], woa_ref[...]) + _dot(gla_ref[...], wog_ref[...]) + _dot(dif_ref[...], wod_ref[...])
        x = x + mgate_ref[0] * _rms(y, mg_ref[...])
    shift_ref, scale_ref, gate_ref, g1_ref, g2_ref, win_ref, wout_ref, o_ref = refs
    h = _rms(x, g1_ref[...]) * (1.0 + scale_ref[0]) + shift_ref[0]
    hb = h.astype(BF16)
    y = jnp.zeros(x.shape, F32)
    for c in range(D_FF // FF_CHUNK):
        lo, hi = c * FF_CHUNK, (c + 1) * FF_CHUNK
        g = _dot(hb, win_ref[:, lo:hi])
        u = _dot(hb, win_ref[:, D_FF + lo:D_FF + hi])
        a = (g * jax.nn.sigmoid(g) * u).astype(BF16)
        y = y + _dot(a, wout_ref[lo:hi, :])
    o_ref[...] = x + res_w * gate_ref[0] * _rms(y, g2_ref[...])


def _mod_spec(r, tiles_per_mod, col):
    return pl.BlockSpec((1, r, D_MODEL), lambda i: (i // tiles_per_mod, 0, col))


def _ffn_call(x, mod, g1, g2, w_in, w_out, layer, which, res_w, tm, tiles_per_mod, merge=None):
    n = x.shape[0]
    mod_all, sub = mod
    r = mod_all.shape[1]
    tok = pl.BlockSpec((tm, D_MODEL), lambda i: (i, 0))
    merge_specs, merge_args = [], []
    if merge is not None:
        oa, gla, dif, (mmod, msub), mg, w_oa, w_og, w_od = merge
        wide = lambda a: pl.BlockSpec((tm, a.shape[1]), lambda i: (i, 0))
        merge_specs = [wide(oa), wide(gla), wide(dif), _mod_spec(r, tiles_per_mod, 3 * msub + 2),
                       _const_spec((1, D_MODEL)), _const_spec(w_oa.shape), _const_spec(w_og.shape),
                       _const_spec(w_od.shape)]
        merge_args = [oa, gla, dif, mmod, mg, w_oa, w_og, w_od]
    pick = lambda rows, cols: pl.BlockSpec((None, None, rows, cols), lambda i: (layer, which, 0, 0),
                                           pipeline_mode=pl.Buffered(1))
    return pl.pallas_call(
        functools.partial(_ffn_kernel, res_w=res_w, merge=merge is not None),
        grid=(n // tm,),
        in_specs=[tok] + merge_specs + [_mod_spec(r, tiles_per_mod, 3 * sub + k) for k in range(3)]
        + [_const_spec((1, D_MODEL)), _const_spec((1, D_MODEL)), pick(D_MODEL, 2 * D_FF), pick(D_FF, D_MODEL)],
        out_specs=tok,
        out_shape=jax.ShapeDtypeStruct((n, D_MODEL), F32),
        compiler_params=_params(("parallel",)),
        name="ffn",
    )(x, *merge_args, mod_all, mod_all, mod_all, g1, g2, w_in, w_out)


_STACKED_OUTS = (
    ("ka", 1, W_A), ("va", 1, W_A), ("kc", N_HEADS, HEAD_PAD), ("vc", N_HEADS, HEAD_PAD), ("logf", 1, N_HEADS),
)
_LAYER_OUTS = (
    ("qa_p", W_PAD, BF16), ("ka_p", W_PAD, BF16), ("va_p", W_PAD, BF16),
    ("qc1_p", W_PAD, BF16), ("qc2_p", W_PAD, BF16), ("kc_b", W_C, BF16), ("vc_b", W_C, BF16),
    ("qb", LANES, F32), ("kb", LANES, F32), ("vb", W_B, F32), ("loga", LANES, F32), ("rb", W_B, F32),
)
_NORM_ROWS = ("qa_p", "ka_p", "qc1_p", "qc2_p", "kc_b")
_FLOOR_LANE = N_HEADS * len(_NORM_ROWS)


def _proj_kernel(x_ref, shift_ref, scale_ref, g1_ref, w_ref, wkvt_ref, bf_ref, wup_ref, bup_ref, vone_ref, *refs,
                 n_alias, with_qa, transposed_a):
    refs = refs[n_alias:]
    names = [n for n, _, _ in _STACKED_OUTS] + [n for n, _, _ in _LAYER_OUTS] + (["qa"] if with_qa else [])
    o = dict(zip(names + ["norms"], refs))
    tm = x_ref.shape[0]
    x = x_ref[...]
    hb = (_rms(x, g1_ref[...]) * (1.0 + scale_ref[0]) + shift_ref[0]).astype(BF16)

    def seg(name):
        lo, hi = _SEG[name]
        return _dot(hb, w_ref[:, lo:hi])

    def store_heads(ref, val):
        for d in range(ref.shape[0]):
            for h in range(N_HEADS):
                ref[d, pl.ds(h, tm, stride=N_HEADS), :] = val[:, h * HEAD_PAD:(h + 1) * HEAD_PAD]

    def store_slots(ref, val):
        for d in range(ref.shape[0]):
            ref[d] = val

    if transposed_a:
        kvt = _nt_dot(wkvt_ref[...], hb)
        store_slots(o["ka"], kvt[None, :W_A])
        store_slots(o["va"], kvt[None, W_A:])
    else:
        store_slots(o["ka"], seg("ka"))
        store_slots(o["va"], seg("va"))
    kc = seg("kc")
    store_heads(o["kc"], kc)
    rounded = {"kc_b": kc.astype(BF16)}
    o["kc_b"][...] = rounded["kc_b"]
    vc = seg("vc")
    store_heads(o["vc"], vc)
    o["vc_b"][...] = vc.astype(BF16)
    def pad_heads(val):
        zero = jnp.zeros((tm, HEAD_PAD - HD_A), val.dtype)
        pieces = []
        for h in range(N_HEADS):
            pieces += [val[:, h * HD_A:(h + 1) * HD_A], zero]
        return jnp.concatenate(pieces, axis=1)

    qa = seg("qa")
    if with_qa:
        o["qa"][...] = qa * (HD_A ** -0.5)
    rounded["qa_p"] = pad_heads((qa * (HD_A ** -0.5 * LOG2E)).astype(BF16))
    rounded["ka_p"] = pad_heads(seg("ka").astype(BF16))
    o["qa_p"][...] = rounded["qa_p"]
    o["ka_p"][...] = rounded["ka_p"]
    o["va_p"][...] = (pad_heads(seg("va")) + vone_ref[...]).astype(BF16)
    qc = seg("qc") * (HD_C ** -0.5 * LOG2E)
    first_map = lax.broadcasted_iota(jnp.int32, qc.shape, 1) % HEAD_PAD < HD_C
    rounded["qc1_p"] = jnp.where(first_map, qc, 0.0).astype(BF16)
    rounded["qc2_p"] = jnp.where(first_map, 0.0, qc).astype(BF16)
    o["qc1_p"][...] = rounded["qc1_p"]
    o["qc2_p"][...] = rounded["qc2_p"]
    fa = seg("fa") + bf_ref[...]
    store_slots(o["logf"], _log_sigmoid(fa)[:, :N_HEADS])
    o["qb"][...] = seg("qb")
    o["kb"][...] = seg("kb")
    o["vb"][...] = seg("vb")
    o["rb"][...] = seg("rb")
    gb = seg("gb").astype(BF16)
    o["loga"][...] = _log_sigmoid(_dot(gb, wup_ref[...]) + bup_ref[...]) * (1.0 / GLA_TAU)
    lane = lax.broadcasted_iota(jnp.int32, (8, LANES), 1)
    tile_norms = jnp.zeros((8, LANES), F32)
    for a, name in enumerate(_NORM_ROWS):
        sq = rounded[name].astype(F32)
        sq = sq * sq
        for h in range(N_HEADS):
            ss = jnp.sum(sq[:, h * HEAD_PAD:(h + 1) * HEAD_PAD], axis=1, keepdims=True)
            tile_norms = jnp.where(lane == N_HEADS * a + h, jnp.max(ss, axis=0, keepdims=True), tile_norms)

    def own_scores(q_name, k_name):
        prod = rounded[q_name].astype(F32) * rounded[k_name].astype(F32)
        return [jnp.min(jnp.sum(prod[:, h * HEAD_PAD:(h + 1) * HEAD_PAD], axis=1, keepdims=True), axis=0, keepdims=True)
                for h in range(N_HEADS)]

    floor_a = own_scores("qa_p", "ka_p")
    floor_c = [jnp.minimum(u, v) for u, v in zip(own_scores("qc1_p", "kc_b"), own_scores("qc2_p", "kc_b"))]
    for h in range(N_HEADS):
        tile_norms = jnp.where(lane == _FLOOR_LANE + h, floor_a[h], tile_norms)
        tile_norms = jnp.where(lane == _FLOOR_LANE + N_HEADS + h, floor_c[h], tile_norms)
    o["norms"][0] = tile_norms


def _proj_call(x, mod, g1, w_ext, w_kvt, bf, wup, bup, vone, tm, tiles_per_mod, layer, depth, stacked, with_qa,
               seq_len):
    n = x.shape[0]
    mod_all, sub = mod
    r = mod_all.shape[1]
    tok = lambda w: pl.BlockSpec((tm, w), lambda i: (i, 0))
    layer_outs = _LAYER_OUTS + ((("qa", W_A, F32),) if with_qa else ())
    alias_in = [] if stacked is None else [stacked[nm] for nm, _, _ in _STACKED_OUTS]
    transposed_a = seq_len % tm == 0
    tps = max(seq_len // tm, 1)
    stacked_specs, stacked_shapes = [], []
    slots = depth if stacked is None else 1
    for nm, rows, w in _STACKED_OUTS:
        if transposed_a and nm in ("ka", "va"):
            stacked_specs.append(pl.BlockSpec((slots, 1, w, tm), lambda i: (layer, i // tps, 0, i % tps)))
            stacked_shapes.append(jax.ShapeDtypeStruct((depth, n // seq_len, w, seq_len), F32))
        else:
            stacked_specs.append(pl.BlockSpec((slots, tm * rows, w), lambda i: (layer, i, 0)))
            stacked_shapes.append(jax.ShapeDtypeStruct((depth, n * rows, w), F32))
    n_fixed = 10
    outs = pl.pallas_call(
        functools.partial(_proj_kernel, n_alias=len(alias_in), with_qa=with_qa, transposed_a=transposed_a),
        grid=(n // tm,),
        in_specs=[tok(D_MODEL), _mod_spec(r, tiles_per_mod, 3 * sub), _mod_spec(r, tiles_per_mod, 3 * sub + 1),
                  _const_spec((1, D_MODEL)), _const_spec((D_MODEL, N_EXT)), _const_spec((2 * W_A, D_MODEL)),
                  _const_spec((1, LANES)), _const_spec((LANES, LANES)), _const_spec((1, LANES)),
                  _const_spec((1, W_PAD))]
        + [pl.BlockSpec(memory_space=pl.ANY)] * len(alias_in),
        out_specs=stacked_specs + [tok(w) for _, w, _ in layer_outs]
        + [pl.BlockSpec((1, 8, LANES), lambda i: (i, 0, 0))],
        out_shape=stacked_shapes + [jax.ShapeDtypeStruct((n, w), dt) for _, w, dt in layer_outs]
        + [jax.ShapeDtypeStruct((n // tm, 8, LANES), F32)],
        input_output_aliases={n_fixed + k: k for k in range(len(alias_in))},
        compiler_params=_params(("parallel",)),
        name="proj",
    )(x, mod_all, mod_all, g1, w_ext, w_kvt, bf, wup, bup, vone, *alias_in)
    names = [nm for nm, _, _ in _STACKED_OUTS] + [nm for nm, _, _ in layer_outs] + ["norms"]
    return dict(zip(names, outs))


def _cumsum_kernel(x_ref, o_ref):
    nb = x_ref.shape[1]
    r = lax.broadcasted_iota(jnp.int32, (LANES, LANES), 0)
    c = lax.broadcasted_iota(jnp.int32, (LANES, LANES), 1)
    within = (r <= c).astype(F32)
    rr = lax.broadcasted_iota(jnp.int32, (nb, nb), 0)
    cc = lax.broadcasted_iota(jnp.int32, (nb, nb), 1)
    before = (cc < rr).astype(F32)
    for s in range(x_ref.shape[0]):
        x = x_ref[s]
        local = _dot_exact_rhs(x, within)
        tot = jnp.broadcast_to(local[:, LANES - 1:LANES], (nb, LANES))
        o_ref[s] = local + _dot_exact_lhs(before, tot)


def _cumsum_call(x):
    rows, nb, _ = x.shape
    group = math.gcd(rows, 8)
    spec = pl.BlockSpec((group, nb, LANES), lambda i: (i, 0, 0))
    return pl.pallas_call(
        _cumsum_kernel, grid=(rows // group,), in_specs=[spec], out_specs=spec,
        out_shape=jax.ShapeDtypeStruct(x.shape, F32),
        compiler_params=_params(("parallel",)), name="cumsum",
    )(x)


def _lane_tile(x, width):
    return jnp.concatenate([x] * (width // LANES), axis=1)


GROUP_HEADS = 2
W_GROUP = GROUP_HEADS * HEAD_PAD


def _causal_sweep(i, first, tq, tk, tiles_per_iter, tile):
    r = tq // tk

    def body(jj, carry):
        for u in range(tiles_per_iter):
            tile(jj * tiles_per_iter + u, slice(0, tq), None)
        return carry

    assert r % tiles_per_iter == 0 and tiles_per_iter in (1, 2)
    lead = 0
    if tiles_per_iter == 2:
        lead = lax.rem(i * r - first, 2)

        @pl.when(lead == 1)
        def _():
            tile(first, slice(0, tq), None)

    lax.fori_loop(lax.div(first + lead, tiles_per_iter), lax.div(i * r, tiles_per_iter), body, 0)
    for d in range(r):
        tile(i * r + d, slice(d * tk, (d + 1) * tk), d * tk)
        if d + 1 < r:
            tile(i * r + d, slice((d + 1) * tk, tq), None)


def _fox_kernel(first_ref, q_ref, k_ref, v_ref, c_ref, o_ref, m_sc, acc_sc, *, tk, tiles_per_iter):
    i = pl.program_id(2)
    first = first_ref[pl.program_id(0), pl.program_id(1), i]
    tq = q_ref.shape[1]
    m_sc[...] = jnp.full(m_sc.shape, NEG, F32)
    acc_sc[...] = jnp.zeros(acc_sc.shape, F32)
    row = lax.broadcasted_iota(jnp.int32, (tk, tk), 0)
    col = lax.broadcasted_iota(jnp.int32, (tk, tk), 1)
    causal = col <= row

    def tile(j, rows, local):
        ks = pl.ds(pl.multiple_of(j * tk, tk), tk)
        for h in range(GROUP_HEADS):
            hs = slice(h * HEAD_PAD, (h + 1) * HEAD_PAD)
            c_first = c_ref[0, h, pl.ds(i * (tq // tk), 1), :][:, 0:1]
            t = _nt_dot(q_ref[0, rows, hs], k_ref[0, ks, hs]) + (c_first - c_ref[0, h, pl.ds(j, 1), :]) * LOG2E
            if local is not None:
                t = jnp.where(causal, t, NEG)
            m_old = m_sc[h, rows, :]
            m_new = jnp.maximum(m_old, jnp.max(t, axis=1, keepdims=True))
            p = jnp.exp2(t - _lane_tile(m_new, tk)).astype(BF16)
            acc_sc[h, rows, :] = jnp.exp2(m_old - m_new) * acc_sc[h, rows, :] + _dot(p, v_ref[0, ks, hs])
            m_sc[h, rows, :] = m_new

    _causal_sweep(i, first, tq, tk, tiles_per_iter, tile)
    outs = []
    for h in range(GROUP_HEADS):
        acc = acc_sc[h]
        outs.append((acc[:, :HD_A] / acc[:, HD_A:HD_A + 1]).astype(BF16))
    o_ref[0] = jnp.concatenate(outs, axis=1)


def _resident_spec(t, width):
    return pl.BlockSpec((1, t, width), lambda b_, g, i, first: (b_, 0, g), pipeline_mode=pl.Buffered(1))


def _fox_call(first, qa_p, ka_p, va_p, c, tq, tk):
    b, t, _ = qa_p.shape
    qspec = pl.BlockSpec((1, tq, W_GROUP), lambda b_, g, i, first: (b_, i, g))
    grid_spec = pltpu.PrefetchScalarGridSpec(
        num_scalar_prefetch=1,
        grid=(b, N_HEADS // GROUP_HEADS, t // tq),
        in_specs=[qspec, _resident_spec(t, W_GROUP), _resident_spec(t, W_GROUP),
                  pl.BlockSpec((1, GROUP_HEADS, t // tk, tk), lambda b_, g, i, first: (b_, g, 0, 0))],
        out_specs=pl.BlockSpec((1, tq, GROUP_HEADS * HD_A), lambda b_, g, i, first: (b_, i, g)),
        scratch_shapes=[pltpu.VMEM((GROUP_HEADS, tq, LANES), F32), pltpu.VMEM((GROUP_HEADS, tq, HEAD_PAD), F32)])
    return pl.pallas_call(
        functools.partial(_fox_kernel, tk=tk, tiles_per_iter=1),
        grid_spec=grid_spec,
        out_shape=jax.ShapeDtypeStruct((b, t, W_A), BF16),
        compiler_params=_params(("parallel", "parallel", "arbitrary")),
        name="fox",
    )(first, qa_p, ka_p, va_p, c.reshape(b, N_HEADS, t // tk, tk))


def _diff_lambda(lp, lam_init):
    a = jnp.sum(lp[0:1] * lp[1:2], axis=1, keepdims=True)
    b = jnp.sum(lp[2:3] * lp[3:4], axis=1, keepdims=True)
    return jnp.exp(a) - jnp.exp(b) + lam_init


def _diag_distance(tk):
    q = np.arange(tk)[:, None]
    k = np.arange(tk)[None, :]
    return np.where(k // CHUNK <= q // CHUNK, 2 * np.minimum(q, k) - k, HIDDEN).astype(np.float32)


DIFF_GROUP = 1
W_DIFF_GROUP = DIFF_GROUP * HEAD_PAD


def _diff_kernel(first_ref, q1_ref, q2_ref, k_ref, v_ref, slope_ref, lam_ref, g_ref, rel_ref, o_ref,
                 m_sc, l_sc, a_sc, *, tk, tiles_per_iter, lam_init):
    g = pl.program_id(1)
    i = pl.program_id(2)
    first = first_ref[pl.program_id(0), g, i]
    tq = q1_ref.shape[1]
    m_sc[...] = jnp.full(m_sc.shape, NEG, F32)
    l_sc[...] = jnp.zeros(l_sc.shape, F32)
    a_sc[...] = jnp.zeros(a_sc.shape, F32)
    lane = lax.broadcasted_iota(jnp.int32, (1, tk), 1)

    def tile(j, rows, local):
        ks = pl.ds(pl.multiple_of(j * tk, tk), tk)
        if local is None:
            rel = (lane + (j * tk - i * tq)).astype(F32)
        else:
            rel = rel_ref[...] + float(local)
        for h in range(DIFF_GROUP):
            hs = slice(h * HEAD_PAD, (h + 1) * HEAD_PAD)
            bias = (slope_ref[g * DIFF_GROUP + h][:, 0:1] * LOG2E) * rel
            k = k_ref[0, ks, hs]
            v = v_ref[0, ks, hs]
            nrows = rows.stop - rows.start
            scores = _nt_dot(jnp.concatenate([q1_ref[0, rows, hs], q2_ref[0, rows, hs]], axis=0), k)
            probs, alphas = [], []
            for m in range(2):
                n = 2 * h + m
                t = scores[m * nrows:(m + 1) * nrows] + bias
                m_old = m_sc[n, rows, :]
                m_new = jnp.maximum(m_old, jnp.max(t, axis=1, keepdims=True))
                alpha = jnp.exp2(m_old - m_new)
                p = jnp.exp2(t - _lane_tile(m_new, tk))
                l_sc[n, rows, :] = alpha * l_sc[n, rows, :] + jnp.sum(p, axis=1, keepdims=True)
                m_sc[n, rows, :] = m_new
                probs.append(p.astype(BF16))
                alphas.append(alpha)
            pv = _dot(jnp.concatenate(probs, axis=0), v)
            for m in range(2):
                n = 2 * h + m
                a_sc[n, rows, :] = alphas[m] * a_sc[n, rows, :] + pv[m * nrows:(m + 1) * nrows]

    _causal_sweep(i, first, tq, tk, tiles_per_iter, tile)
    lam = _diff_lambda(lam_ref[...], lam_init)
    for h in range(DIFF_GROUP):
        out = a_sc[2 * h] / l_sc[2 * h] - lam * (a_sc[2 * h + 1] / l_sc[2 * h + 1])
        o_ref[0, :, h * HEAD_PAD:(h + 1) * HEAD_PAD] = (_rms(out, g_ref[...]) * (1.0 - lam_init)).astype(BF16)


def _diff_call(first, qc1_p, qc2_p, kc_b, vc_b, slopes, lam_p, g_diff, lam_init, tq, tk):
    assert CHUNK == 64 and tk % CHUNK == 0
    b, t, _ = qc1_p.shape
    qspec = pl.BlockSpec((1, tq, W_DIFF_GROUP), lambda b_, g, i, first: (b_, i, g))
    stat = pltpu.VMEM((2 * DIFF_GROUP, tq, LANES), F32)
    grid_spec = pltpu.PrefetchScalarGridSpec(
        num_scalar_prefetch=1,
        grid=(b, N_HEADS // DIFF_GROUP, t // tq),
        in_specs=[qspec, qspec, _resident_spec(t, W_DIFF_GROUP), _resident_spec(t, W_DIFF_GROUP),
                  pl.BlockSpec((N_HEADS, 1, LANES), lambda b_, g, i, first: (0, 0, 0)),
                  pl.BlockSpec((4, HD_C), lambda b_, g, i, first: (0, 0)),
                  pl.BlockSpec((1, HEAD_PAD), lambda b_, g, i, first: (0, 0)),
                  pl.BlockSpec((tk, tk), lambda b_, g, i, first: (0, 0), pipeline_mode=pl.Buffered(1))],
        out_specs=qspec,
        scratch_shapes=[stat, stat, pltpu.VMEM((2 * DIFF_GROUP, tq, HEAD_PAD), F32)])
    return pl.pallas_call(
        functools.partial(_diff_kernel, tk=tk, tiles_per_iter=2, lam_init=lam_init),
        grid_spec=grid_spec,
        out_shape=jax.ShapeDtypeStruct((b, t, W_C), BF16),
        compiler_params=_params(("parallel", "parallel", "arbitrary")),
        name="diff",
    )(first, qc1_p, qc2_p, kc_b, vc_b, slopes, lam_p, g_diff, jnp.asarray(_diag_distance(tk)))


def _gla_kernel(q_ref, k_ref, v_ref, la_ref, r_ref, s0_ref, g_ref, o_ref, s_ref, *, chunk):
    ti = pl.program_id(1)
    n_chunks = q_ref.shape[1] // chunk
    n_seq = q_ref.shape[0]

    @pl.when(ti == 0)
    def _():
        s_ref[...] = s0_ref[...]

    tri_r = lax.broadcasted_iota(jnp.int32, (chunk, chunk), 0)
    tri_c = lax.broadcasted_iota(jnp.int32, (chunk, chunk), 1)
    incl = (tri_c <= tri_r).astype(F32)
    srow = lax.broadcasted_iota(jnp.int32, (N_HEADS * chunk, LANES), 0) // chunk
    slane = lax.broadcasted_iota(jnp.int32, (N_HEADS * chunk, LANES), 1) // DK_B
    qmask = srow == slane
    arow = lax.broadcasted_iota(jnp.int32, (N_HEADS * chunk, chunk), 0) % chunk
    acol = lax.broadcasted_iota(jnp.int32, (N_HEADS * chunk, chunk), 1)
    causal = acol <= arow
    vlane = lax.broadcasted_iota(jnp.int32, (chunk, W_B), 1) // DV_B
    st_row = lax.broadcasted_iota(jnp.int32, (W_B, LANES), 0) // DV_B
    st_lane = lax.broadcasted_iota(jnp.int32, (W_B, LANES), 1) // DK_B
    diag_blocks = st_row == st_lane

    for c, sq in [(c, sq) for c in range(n_chunks) for sq in range(n_seq)]:
        sl = slice(c * chunk, (c + 1) * chunk)
        cb = _dot_exact_lhs(incl, la_ref[sq, sl, :])
        cb_last = cb[chunk - 1:chunk, :]
        q = q_ref[sq, sl, :] * (DK_B ** -0.5)
        k = k_ref[sq, sl, :]
        v = v_ref[sq, sl, :]
        vb = v.astype(BF16)
        qe = q * jnp.exp(cb)
        ke = (k * jnp.exp(-cb)).astype(BF16)
        kl = (k * jnp.exp(cb_last - cb)).astype(BF16)
        decay = jnp.exp(cb_last)
        qstack = jnp.where(qmask, jnp.concatenate([qe] * N_HEADS, axis=0), 0.0).astype(BF16)
        a = jnp.where(causal, _nt_dot(qstack, ke), 0.0).astype(BF16)
        oi = _dot(a, vb)
        o = jnp.zeros((chunk, W_B), F32)
        for h in range(N_HEADS):
            o = o + jnp.where(vlane == h, oi[h * chunk:(h + 1) * chunk], 0.0)
        st = s_ref[sq]
        o = o + _nt_dot(qe.astype(BF16), st.astype(BF16))
        upd = _dot(v.T.astype(BF16), kl)
        s_ref[sq] = decay * st + jnp.where(diag_blocks, upd, 0.0)
        o2 = o * o
        ms = jnp.zeros((chunk, W_B), F32)
        for h in range(N_HEADS):
            head_mean = jnp.sum(jnp.where(vlane == h, o2, 0.0), axis=1, keepdims=True) * (1.0 / DV_B)
            ms = jnp.where(vlane == h, head_mean, ms)
        r = r_ref[sq, sl, :]
        o_ref[sq, sl, :] = (o * lax.rsqrt(ms + EPS) * g_ref[...] * (r * jax.nn.sigmoid(r))).astype(BF16)


def _gla_call(qb, kb, vb, loga, rb, s0_t, g_gla, tg, chunk, group):
    b, t, _ = qb.shape
    tok = lambda w: pl.BlockSpec((group, tg, w), lambda b_, i: (b_, i, 0))
    sspec = pl.BlockSpec((group, W_B, LANES), lambda b_, i: (b_, 0, 0))
    return pl.pallas_call(
        functools.partial(_gla_kernel, chunk=chunk),
        grid=(b // group, t // tg),
        in_specs=[tok(LANES), tok(LANES), tok(W_B), tok(LANES), tok(W_B), sspec,
                  pl.BlockSpec((1, W_B), lambda b_, i: (0, 0))],
        out_specs=[tok(W_B), sspec],
        out_shape=[jax.ShapeDtypeStruct((b, t, W_B), BF16), jax.ShapeDtypeStruct((b, W_B, LANES), F32)],
        compiler_params=_params(("parallel", "arbitrary")),
        name="gla",
    )(qb, kb, vb, loga, rb, s0_t, g_gla)


def _softmax_two(t_past, t_new, exp_fn):
    m = jnp.maximum(jnp.max(t_past, axis=1, keepdims=True), jnp.max(t_new, axis=1, keepdims=True))
    p_past = exp_fn(t_past - m)
    p_new = exp_fn(t_new - m)
    l = jnp.sum(p_past, axis=1, keepdims=True) + jnp.sum(p_new, axis=1, keepdims=True)
    return p_past, p_new, l


def _fox_s_kernel(q_ref, kn_ref, vn_ref, kpt_ref, vpt_ref, c_ref, o_ref):
    t, past = q_ref.shape[1], kpt_ref.shape[1]
    q = q_ref[0]
    kn = kn_ref[...].astype(BF16)
    vn = vn_ref[...].astype(BF16)
    kpt = kpt_ref[...].astype(BF16)
    vpt = vpt_ref[...].astype(BF16)
    lane_head = lax.broadcasted_iota(jnp.int32, (t, W_A), 1) // HD_A
    row = lax.broadcasted_iota(jnp.int32, (t, t), 0)
    col = lax.broadcasted_iota(jnp.int32, (t, t), 1)
    o = jnp.zeros((t, W_A), F32)
    for h in range(N_HEADS):
        qh = jnp.where(lane_head == h, q, 0.0).astype(BF16)
        ch = c_ref[0, h:h + 1, :]
        cref = ch[:, past:past + 1]
        t_past = _dot(qh, kpt) + (cref - ch[:, :past])
        t_new = jnp.where(col <= row, _nt_dot(qh, kn) + (cref - ch[:, past:past + t]), NEG)
        p_past, p_new, l = _softmax_two(t_past, t_new, jnp.exp)
        oh = (_nt_dot(p_past.astype(BF16), vpt) + _dot(p_new.astype(BF16), vn)) / l
        o = o + jnp.where(lane_head == h, oh, 0.0)
    o_ref[0] = o.astype(BF16)


def _fox_s_call(layer, qa, ka, va, cache_kt, cache_vt, c):
    b, t, _ = qa.shape
    past = cache_kt.shape[3]
    new = pl.BlockSpec((1, t, W_A), lambda i: (i, 0, 0))
    new_l = lambda: pl.BlockSpec((None, t, W_A), lambda i: (layer, i, 0))
    old_l = lambda: pl.BlockSpec((None, None, W_A, past), lambda i: (layer, i, 0, 0))
    return pl.pallas_call(
        _fox_s_kernel,
        grid=(b,),
        in_specs=[new, new_l(), new_l(), old_l(), old_l(),
                  pl.BlockSpec((1, N_HEADS, c.shape[2]), lambda i: (i, 0, 0))],
        out_specs=new,
        out_shape=jax.ShapeDtypeStruct((b, t, W_A), BF16),
        compiler_params=_params(("parallel",)),
        name="fox_s",
    )(qa, ka, va, cache_kt, cache_vt, c)


def _diff_s_kernel(q1_ref, q2_ref, kn_ref, vn_ref, kp_ref, vp_ref, slope_ref, lam_ref, g_ref, o_ref, *, lam_init):
    t, past = q1_ref.shape[1], kp_ref.shape[0] // N_HEADS
    lam = _diff_lambda(lam_ref[...], lam_init)
    row = lax.broadcasted_iota(jnp.int32, (t, t), 0) + past
    col = lax.broadcasted_iota(jnp.int32, (t, t), 1) + past
    visible_new = lax.shift_right_logical(col, 6) <= lax.shift_right_logical(row, 6)
    dist_new = jnp.abs(row - col).astype(F32)
    prow = lax.broadcasted_iota(jnp.int32, (t, past), 0) + past
    pcol = lax.broadcasted_iota(jnp.int32, (t, past), 1)
    visible_past = lax.shift_right_logical(pcol, 6) <= lax.shift_right_logical(prow, 6)
    dist_past = (prow - pcol).astype(F32)
    for h in range(N_HEADS):
        hs = slice(h * HEAD_PAD, (h + 1) * HEAD_PAD)
        slope = slope_ref[h][:, 0:1] * LOG2E
        kn = kn_ref[pl.ds(h, t, stride=N_HEADS), :].astype(BF16)
        vn = vn_ref[pl.ds(h, t, stride=N_HEADS), :].astype(BF16)
        kp = kp_ref[pl.ds(h, past, stride=N_HEADS), :].astype(BF16)
        vp = vp_ref[pl.ds(h, past, stride=N_HEADS), :].astype(BF16)
        outs = []
        for q_ref in (q1_ref, q2_ref):
            q = q_ref[0, :, hs]
            t_past = jnp.where(visible_past, _nt_dot(q, kp) - slope * dist_past, NEG)
            t_new = jnp.where(visible_new, _nt_dot(q, kn) - slope * dist_new, NEG)
            p_past, p_new, l = _softmax_two(t_past, t_new, jnp.exp2)
            outs.append((_dot(p_past.astype(BF16), vp) + _dot(p_new.astype(BF16), vn)) / l)
        out = outs[0] - lam * outs[1]
        o_ref[0, :, hs] = (_rms(out, g_ref[...]) * (1.0 - lam_init)).astype(BF16)


def _diff_s_call(layer, qc1_p, qc2_p, kc, vc, cache_k, cache_v, slopes, lam_p, g_diff, lam_init):
    assert CHUNK == 64
    b, t, _ = qc1_p.shape
    rows_past = cache_k.shape[2]
    new = lambda: pl.BlockSpec((1, t, W_C), lambda i: (i, 0, 0))
    new_l = lambda: pl.BlockSpec((None, t * N_HEADS, HEAD_PAD), lambda i: (layer, i, 0))
    old_l = lambda: pl.BlockSpec((None, None, rows_past, HEAD_PAD), lambda i: (layer, i, 0, 0))
    return pl.pallas_call(
        functools.partial(_diff_s_kernel, lam_init=lam_init),
        grid=(b,),
        in_specs=[new(), new(), new_l(), new_l(), old_l(), old_l(),
                  pl.BlockSpec((N_HEADS, 1, LANES), lambda i: (0, 0, 0)),
                  pl.BlockSpec((4, HD_C), lambda i: (0, 0)),
                  pl.BlockSpec((1, HEAD_PAD), lambda i: (0, 0))],
        out_specs=new(),
        out_shape=jax.ShapeDtypeStruct((b, t, W_C), BF16),
        compiler_params=_params(("parallel",)),
        name="diff_s",
    )(qc1_p, qc2_p, kc, vc, cache_k, cache_v, slopes, lam_p, g_diff)


def _pad_lanes(w, width=LANES):
    return jnp.pad(w, ((0, 0), (0, width - w.shape[1])))


def _layer_params(l, w_in, b_f, w_gla_up, b_gla_up, g_gla, g_diff, w_out):
    offs = [int(o) for o in np.cumsum(PROJ_SIZES)[:-1]]
    qa, ka, va, fa, qb, kb, vb, gb, rb, qc, kc, vc = jnp.split(w_in[l], offs, axis=1)
    segs = {
        "ka": ka, "va": va, "kc": kc, "vc": vc, "qa": qa, "qc": qc,
        "fa": _pad_lanes(fa), "qb": qb, "kb": kb, "vb": vb, "rb": rb, "gb": _pad_lanes(gb),
    }
    w_ext = jnp.concatenate([segs[n] for n, _ in _SEG_WIDTHS], axis=1).astype(BF16)
    wo = w_out[l]
    vone = np.zeros((1, W_PAD), np.float32)
    vone[0, HD_A::HEAD_PAD] = 1.0
    return dict(
        w_ext=w_ext, w_kvt=jnp.concatenate([ka, va], axis=1).T.astype(BF16),
        bf=_pad_lanes(b_f[l][None, :]),
        wup=jnp.pad(w_gla_up[l], ((0, LANES - GLA_RANK), (0, 0))).astype(BF16),
        bup=b_gla_up[l][None, :],
        vone=jnp.asarray(vone),
        g_gla=jnp.tile(g_gla[l], N_HEADS)[None, :], g_diff=g_diff[l][None, :],
        w_oa=wo[:W_A].astype(BF16),
        w_og=wo[W_A:W_A + W_B].astype(BF16), w_od=wo[W_A + W_B:].astype(BF16),
    )


def _state_to_blockdiag_t(s):
    eye = jnp.eye(N_HEADS, dtype=s.dtype)
    return jnp.einsum("bhkv,hg->bhvgk", s, eye).reshape(s.shape[0], W_B, N_HEADS * DK_B)


def _blockdiag_t_to_state(st):
    b = st.shape[0]
    s5 = st.reshape(b, N_HEADS, DV_B, N_HEADS, DK_B)
    diag = jnp.stack([s5[:, h, :, h, :] for h in range(N_HEADS)], axis=1)
    return jnp.swapaxes(diag, 2, 3)


SKIP_LOG2 = 160.0
FLOOR_MARGIN = 0.25
NORM_SLACK = 1.01


def _tile_max(x, factor):
    return jnp.max(x.reshape(x.shape[:-1] + (x.shape[-1] // factor, factor)), axis=-1)


def _first_tiles(qn, kn, row_floor, gap, tq, tk, group):
    b, h, nq = qn.shape
    r = tq // tk
    floor = row_floor - (FLOOR_MARGIN + (NORM_SLACK - 1.0) * jnp.abs(row_floor))
    bound = NORM_SLACK * qn[..., None] * kn[:, :, None, :] + gap - floor[..., None]
    dead = (bound < -SKIP_LOG2).reshape(b, h // group, group, nq, kn.shape[-1]).all(axis=2)
    lead = jnp.sum(jnp.cumprod(dead.astype(jnp.int32), axis=-1), axis=-1)
    return jnp.minimum(lead, jnp.arange(nq, dtype=jnp.int32) * r).astype(jnp.int32)


def _tile(n, pref):
    t = min(n, pref)
    assert n % t == 0, (n, t)
    return t


def _trunk(x, mods, params, ffn_w, g_norm, lam_params, mixer, tm, tiles_per_mod, with_qa):
    b, t, _ = x.shape
    n = b * t
    depth = len(params)
    xf = x.reshape(n, D_MODEL)
    stacked = None
    gla_states = []
    for l in range(depth):
        p = params[l]
        gn = lambda k: g_norm[l, k][None, :]
        xf = _ffn_call(xf, (mods[l], 0), gn(0), gn(1), *ffn_w, l, 0, 0.5, tm, tiles_per_mod)
        pr = _proj_call(xf, (mods[l], 1), gn(2), p["w_ext"], p["w_kvt"], p["bf"], p["wup"], p["bup"], p["vone"], tm,
                        tiles_per_mod, l, depth, stacked, with_qa, t)
        stacked = {nm: pr[nm] for nm, _, _ in _STACKED_OUTS}
        lam_init = 0.8 - 0.6 * math.exp(-0.3 * l)
        oa, gla, dif, s_fin, w_oa = mixer(l, pr, p, lam_params[l], lam_init, b, t)
        gla_states.append(s_fin)
        xf = _ffn_call(xf, (mods[l], 2), gn(4), gn(5), *ffn_w, l, 1, 0.5, tm, tiles_per_mod,
                       merge=(oa, gla, dif, (mods[l], 1), gn(3), w_oa, p["w_og"], p["w_od"]))
    shp = lambda nm, *tail: stacked[nm].reshape((depth, b, t) + tail)
    if stacked["ka"].ndim == 4:
        shp_a = lambda nm: jnp.transpose(stacked[nm].reshape(depth, b, N_HEADS, HD_A, t), (0, 1, 4, 2, 3))
    else:
        shp_a = lambda nm: shp(nm, N_HEADS, HD_A)
    states = [shp_a("ka"), shp_a("va"), shp("logf", N_HEADS), jnp.stack(gla_states),
              shp("kc", N_HEADS, 2 * HD_C), shp("vc", N_HEADS, 2 * HD_C)]
    return xf.reshape(b, t, D_MODEL), states


def kernel(x_prompt, x_sample, c_prompt, c_sample, cache_fox_k, cache_fox_v, cache_fox_logf, state_gla,
           cache_diff_k, cache_diff_v, w_ada, b_ada, g_norm, w_ffn_in, w_ffn_out, w_in, b_f, w_gla_up,
           b_gla_up, g_gla, g_diff, lam_params, w_out):
    depth = w_in.shape[0]
    bp, tp, _ = x_prompt.shape
    bs, ts, _ = x_sample.shape
    past = cache_fox_k.shape[2]

    params = [_layer_params(l, w_in, b_f, w_gla_up, b_gla_up, g_gla, g_diff, w_out) for l in range(depth)]
    ffn_w = (w_ffn_in.astype(BF16), w_ffn_out.astype(BF16))

    slopes = 2.0 ** (-8.0 * jnp.arange(1, N_HEADS + 1, dtype=F32) / N_HEADS)
    slopes = jnp.broadcast_to(slopes[:, None, None], (N_HEADS, 1, LANES))

    rows = bp + bs
    rows_pad = -(-rows // 8) * 8
    c_all = jnp.pad(jnp.concatenate([c_prompt, c_sample], axis=0), ((0, rows_pad - rows), (0, 0)))
    mods_p, mods_s = [], []
    for l in range(depth):
        mod = _ada_call(c_all, w_ada, b_ada[:, None, :], l)
        mods_p.append(mod[:bp, None, :])
        mods_s.append(jnp.repeat(mod[bp:rows], ts, axis=0)[None])

    tm_p = _tile(tp, 512)
    tq = _tile(tp, 1024)
    tk = _tile(tp, 512)
    tg = _tile(tp, 512)

    def prompt_mixer(l, pr, p, lam_p, lam_init, b, t):
        r3 = lambda a: a.reshape(b, t, a.shape[-1])
        logf_t = jnp.swapaxes(r3(pr["logf"][l]), 1, 2).reshape(b * N_HEADS, t // LANES, LANES)
        c = _cumsum_call(logf_t).reshape(b, N_HEADS, t)
        nrm = jnp.sqrt(pr["norms"][:, 0, :_FLOOR_LANE])
        nrm = dict(zip(_NORM_ROWS, jnp.moveaxis(nrm.reshape(b, t // tm_p, len(_NORM_ROWS), N_HEADS), (2, 3), (0, 2))))
        floors = pr["norms"][:, 0, _FLOOR_LANE:_FLOOR_LANE + 2 * N_HEADS].reshape(b, t // tm_p, 2, N_HEADS)
        floor_a, floor_c = [-_tile_max(-f, tq // tm_p) for f in jnp.moveaxis(floors, (2, 3), (0, 2))]
        q_tiles = lambda a: _tile_max(a, tq // tm_p)
        k_tiles = lambda a: _tile_max(a, tk // tm_p)
        gap_a = (c[:, :, ::tq][..., None] - c[:, :, tk - 1::tk][:, :, None, :]) * LOG2E
        first_a = _first_tiles(q_tiles(nrm["qa_p"]), k_tiles(nrm["ka_p"]), floor_a, gap_a, tq, tk, GROUP_HEADS)
        oa = _fox_call(first_a, r3(pr["qa_p"]), r3(pr["ka_p"]), r3(pr["va_p"]), c, tq, tk)
        s0_t = jnp.zeros((b, W_B, N_HEADS * DK_B), F32)
        gla, s_t = _gla_call(r3(pr["qb"]), r3(pr["kb"]), r3(pr["vb"]), r3(pr["loga"]), r3(pr["rb"]), s0_t,
                             p["g_gla"], tg, min(CHUNK, t), math.gcd(b, 2))
        last_key = jnp.arange(tk - 1, t, tk, dtype=F32)[None, :] - jnp.arange(0, t, tq, dtype=F32)[:, None]
        gap_c = slopes[:, 0, :1, None] * LOG2E * last_key[None]
        first_c = _first_tiles(q_tiles(jnp.maximum(nrm["qc1_p"], nrm["qc2_p"])), k_tiles(nrm["kc_b"]), floor_c, gap_c,
                               tq, tk, DIFF_GROUP)
        dif = _diff_call(first_c, r3(pr["qc1_p"]), r3(pr["qc2_p"]), r3(pr["kc_b"]), r3(pr["vc_b"]), slopes, lam_p,
                         p["g_diff"], lam_init, tq, tk)
        flat = lambda a: a.reshape(b * t, a.shape[-1])
        return flat(oa), flat(gla), flat(dif), _blockdiag_t_to_state(s_t), p["w_oa"]

    feature_time = lambda a: jnp.transpose(a, (0, 1, 3, 4, 2)).reshape(depth, bs, W_A, past)
    cache_kt, cache_vt = feature_time(cache_fox_k), feature_time(cache_fox_v)
    cache_dk = cache_diff_k.reshape(depth, bs, past * N_HEADS, HEAD_PAD)
    cache_dv = cache_diff_v.reshape(depth, bs, past * N_HEADS, HEAD_PAD)

    def sample_mixer(l, pr, p, lam_p, lam_init, b, t):
        r3 = lambda a: a.reshape(b, t, a.shape[-1])
        total = past + t
        nb = -(-total // (8 * LANES)) * 8
        lf = jnp.concatenate([cache_fox_logf[l].astype(F32), r3(pr["logf"][l])], axis=1)
        lf = jnp.pad(jnp.swapaxes(lf, 1, 2), ((0, 0), (0, 0), (0, nb * LANES - total)))
        c = _cumsum_call(lf.reshape(b * N_HEADS, nb, LANES)).reshape(b, N_HEADS, nb * LANES)
        oa = _fox_s_call(l, r3(pr["qa"]), pr["ka"], pr["va"], cache_kt, cache_vt, c)
        gla, s_t = _gla_call(r3(pr["qb"]), r3(pr["kb"]), r3(pr["vb"]), r3(pr["loga"]), r3(pr["rb"]),
                             _state_to_blockdiag_t(state_gla[l].astype(F32)), p["g_gla"], t, min(CHUNK, t),
                             math.gcd(b, 8))
        dif = _diff_s_call(l, r3(pr["qc1_p"]), r3(pr["qc2_p"]), pr["kc"], pr["vc"], cache_dk, cache_dv,
                           slopes, lam_p, p["g_diff"], lam_init)
        flat = lambda a: a.reshape(b * t, a.shape[-1])
        return flat(oa), flat(gla), flat(dif), _blockdiag_t_to_state(s_t), p["w_oa"]

    y_p, sp = _trunk(x_prompt, mods_p, params, ffn_w, g_norm, lam_params, prompt_mixer, tm_p, tp // tm_p, False)
    y_s, ss = _trunk(x_sample, mods_s, params, ffn_w, g_norm, lam_params, sample_mixer, bs * ts, 1, True)
    return (y_p, y_s, sp[0], sp[1], sp[2], sp[3], sp[4], sp[5], ss[0], ss[1], ss[2], ss[3], ss[4], ss[5])
```

```python
import functools
import math

import jax
import jax.numpy as jnp
import numpy as np
from jax import lax
from jax.experimental import pallas as pl
from jax.experimental.pallas import tpu as pltpu

F32 = jnp.float32
BF16 = jnp.bfloat16

D_MODEL = 1024
N_HEADS = 4
HD_A = 64
DK_B = 32
DV_B = 64
GLA_RANK = 16
GLA_TAU = 16.0
HD_C = 64
CHUNK = 64
W_A = N_HEADS * HD_A
W_B = N_HEADS * DV_B
W_C = N_HEADS * 2 * HD_C
D_FF = ((8 * D_MODEL // 3 + 255) // 256) * 256
N_SUB = 3
PROJ_SIZES = (W_A, W_A, W_A, N_HEADS, N_HEADS * DK_B, N_HEADS * DK_B, W_B, GLA_RANK, W_B, W_C, W_C, W_C)
NEG = -1e30
HIDDEN = -1e34
EPS = 1e-6
LOG2E = math.log2(math.e)

LANES = 128
HEAD_PAD = 128
W_PAD = N_HEADS * HEAD_PAD
VMEM_LIMIT = 56 * 1024 * 1024

_SEG_WIDTHS = (
    ("ka", W_A), ("va", W_A), ("kc", W_C), ("vc", W_C), ("qa", W_A),
    ("qc", W_C),
    ("fa", LANES), ("qb", LANES), ("kb", LANES), ("vb", W_B), ("rb", W_B), ("gb", LANES),
)
_SEG = {}
_off = 0
for _name, _w in _SEG_WIDTHS:
    _SEG[_name] = (_off, _off + _w)
    _off += _w
N_EXT = _off


def _nt_dot(a, b):
    return lax.dot_general(a, b, (((1,), (1,)), ((), ())), preferred_element_type=F32)


def _dot(a, b):
    return jnp.dot(a, b, preferred_element_type=F32)


def _split3(x):
    hi = x.astype(BF16)
    r1 = x - hi.astype(F32)
    mid = r1.astype(BF16)
    lo = (r1 - mid.astype(F32)).astype(BF16)
    return hi, mid, lo


def _dot_exact_rhs(x, m):
    mb = m.astype(BF16)
    hi, mid, lo = _split3(x)
    return _dot(hi, mb) + _dot(mid, mb) + _dot(lo, mb)


def _dot_exact_lhs(m, x):
    mb = m.astype(BF16)
    hi, mid, lo = _split3(x)
    return _dot(mb, hi) + _dot(mb, mid) + _dot(mb, lo)


def _rms(x, g):
    return x * lax.rsqrt(jnp.mean(x * x, axis=-1, keepdims=True) + EPS) * g


def _log_sigmoid(x):
    return jnp.minimum(x, 0.0) - jnp.log1p(jnp.exp(-jnp.abs(x)))


def _const_spec(shape):
    nd = len(shape)
    return pl.BlockSpec(shape, lambda *_: (0,) * nd, pipeline_mode=pl.Buffered(1))


def _params(sem):
    return pltpu.CompilerParams(dimension_semantics=sem, vmem_limit_bytes=VMEM_LIMIT)


def _ada_kernel(c_ref, w_ref, b_ref, o_ref):
    c = c_ref[...]
    a = (c * jax.nn.sigmoid(c)).astype(BF16)
    o_ref[...] = _dot(a, w_ref[...].astype(BF16)) + b_ref[...]


def _ada_call(c, w, b, layer):
    m, n = c.shape[0], w.shape[2]
    assert n == N_SUB * 3 * D_MODEL
    tn = n // 8
    return pl.pallas_call(
        _ada_kernel,
        grid=(n // tn,),
        in_specs=[pl.BlockSpec((m, D_MODEL), lambda i: (0, 0)),
                  pl.BlockSpec((None, D_MODEL, tn), lambda i: (layer, 0, i)),
                  pl.BlockSpec((None, 1, tn), lambda i: (layer, 0, i))],
        out_specs=pl.BlockSpec((m, tn), lambda i: (0, i)),
        out_shape=jax.ShapeDtypeStruct((m, n), F32),
        compiler_params=_params(("parallel",)),
        name="ada",
    )(c, w, b)


FF_CHUNK = 256


def _ffn_kernel(x_ref, *refs, res_w, merge):
    x = x_ref[...]
    if merge:
        oa_ref, gla_ref, dif_ref, mgate_ref, mg_ref, woa_ref, wog_ref, wod_ref = refs[:8]
        refs = refs[8:]
        y = _dot(oa_ref[...], woa_ref[...]) + _dot(gla_ref[...], wog_ref[...]) + _dot(dif_ref[...], wod_ref[...])
        x = x + mgate_ref[0] * _rms(y, mg_ref[...])
    shift_ref, scale_ref, gate_ref, g1_ref, g2_ref, win_ref, wout_ref, o_ref = refs
    h = _rms(x, g1_ref[...]) * (1.0 + scale_ref[0]) + shift_ref[0]
    hb = h.astype(BF16)
    y = jnp.zeros(x.shape, F32)
    for c in range(D_FF // FF_CHUNK):
        lo, hi = c * FF_CHUNK, (c + 1) * FF_CHUNK
        g = _dot(hb, win_ref[:, lo:hi])
        u = _dot(hb, win_ref[:, D_FF + lo:D_FF + hi])
        a = (g * jax.nn.sigmoid(g) * u).astype(BF16)
        y = y + _dot(a, wout_ref[lo:hi, :])
    o_ref[...] = x + res_w * gate_ref[0] * _rms(y, g2_ref[...])


def _mod_spec(r, tiles_per_mod, col):
    return pl.BlockSpec((1, r, D_MODEL), lambda i: (i // tiles_per_mod, 0, col))


def _ffn_call(x, mod, g1, g2, w_in, w_out, layer, which, res_w, tm, tiles_per_mod, merge=None):
    n = x.shape[0]
    mod_all, sub = mod
    r = mod_all.shape[1]
    tok = pl.BlockSpec((tm, D_MODEL), lambda i: (i, 0))
    merge_specs, merge_args = [], []
    if merge is not None:
        oa, gla, dif, (mmod, msub), mg, w_oa, w_og, w_od = merge
        wide = lambda a: pl.BlockSpec((tm, a.shape[1]), lambda i: (i, 0))
        merge_specs = [wide(oa), wide(gla), wide(dif), _mod_spec(r, tiles_per_mod, 3 * msub + 2),
                       _const_spec((1, D_MODEL)), _const_spec(w_oa.shape), _const_spec(w_og.shape),
                       _const_spec(w_od.shape)]
        merge_args = [oa, gla, dif, mmod, mg, w_oa, w_og, w_od]
    pick = lambda rows, cols: pl.BlockSpec((None, None, rows, cols), lambda i: (layer, which, 0, 0),
                                           pipeline_mode=pl.Buffered(1))
    return pl.pallas_call(
        functools.partial(_ffn_kernel, res_w=res_w, merge=merge is not None),
        grid=(n // tm,),
        in_specs=[tok] + merge_specs + [_mod_spec(r, tiles_per_mod, 3 * sub + k) for k in range(3)]
        + [_const_spec((1, D_MODEL)), _const_spec((1, D_MODEL)), pick(D_MODEL, 2 * D_FF), pick(D_FF, D_MODEL)],
        out_specs=tok,
        out_shape=jax.ShapeDtypeStruct((n, D_MODEL), F32),
        compiler_params=_params(("parallel",)),
        name="ffn",
    )(x, *merge_args, mod_all, mod_all, mod_all, g1, g2, w_in, w_out)


_STACKED_OUTS = (
    ("ka", 1, W_A), ("va", 1, W_A), ("kc", N_HEADS, HEAD_PAD), ("vc", N_HEADS, HEAD_PAD), ("logf", 1, N_HEADS),
)
_LAYER_OUTS = (
    ("qa_p", W_PAD, BF16), ("ka_p", W_PAD, BF16), ("va_p", W_PAD, BF16),
    ("qc1_p", W_PAD, BF16), ("qc2_p", W_PAD, BF16), ("kc_b", W_C, BF16), ("vc_b", W_C, BF16),
    ("qb", LANES, F32), ("kb", LANES, F32), ("vb", W_B, F32), ("loga", LANES, F32), ("rb", W_B, F32),
)
_NORM_ROWS = ("qa_p", "ka_p", "qc1_p", "qc2_p", "kc_b")


def _proj_kernel(x_ref, shift_ref, scale_ref, g1_ref, w_ref, wkvt_ref, bf_ref, wup_ref, bup_ref, vone_ref, *refs,
                 n_alias, with_qa, transposed_a):
    refs = refs[n_alias:]
    names = [n for n, _, _ in _STACKED_OUTS] + [n for n, _, _ in _LAYER_OUTS] + (["qa"] if with_qa else [])
    o = dict(zip(names + ["norms"], refs))
    tm = x_ref.shape[0]
    x = x_ref[...]
    hb = (_rms(x, g1_ref[...]) * (1.0 + scale_ref[0]) + shift_ref[0]).astype(BF16)

    def seg(name):
        lo, hi = _SEG[name]
        return _dot(hb, w_ref[:, lo:hi])

    def store_heads(ref, val):
        for d in range(ref.shape[0]):
            for h in range(N_HEADS):
                ref[d, pl.ds(h, tm, stride=N_HEADS), :] = val[:, h * HEAD_PAD:(h + 1) * HEAD_PAD]

    def store_slots(ref, val):
        for d in range(ref.shape[0]):
            ref[d] = val

    if transposed_a:
        kvt = _nt_dot(wkvt_ref[...], hb)
        store_slots(o["ka"], kvt[None, :W_A])
        store_slots(o["va"], kvt[None, W_A:])
    else:
        store_slots(o["ka"], seg("ka"))
        store_slots(o["va"], seg("va"))
    kc = seg("kc")
    store_heads(o["kc"], kc)
    rounded = {"kc_b": kc.astype(BF16)}
    o["kc_b"][...] = rounded["kc_b"]
    vc = seg("vc")
    store_heads(o["vc"], vc)
    o["vc_b"][...] = vc.astype(BF16)
    def pad_heads(val):
        zero = jnp.zeros((tm, HEAD_PAD - HD_A), val.dtype)
        pieces = []
        for h in range(N_HEADS):
            pieces += [val[:, h * HD_A:(h + 1) * HD_A], zero]
        return jnp.concatenate(pieces, axis=1)

    qa = seg("qa")
    if with_qa:
        o["qa"][...] = qa * (HD_A ** -0.5)
    rounded["qa_p"] = pad_heads((qa * (HD_A ** -0.5 * LOG2E)).astype(BF16))
    rounded["ka_p"] = pad_heads(seg("ka").astype(BF16))
    o["qa_p"][...] = rounded["qa_p"]
    o["ka_p"][...] = rounded["ka_p"]
    o["va_p"][...] = (pad_heads(seg("va")) + vone_ref[...]).astype(BF16)
    qc = seg("qc") * (HD_C ** -0.5 * LOG2E)
    first_map = lax.broadcasted_iota(jnp.int32, qc.shape, 1) % HEAD_PAD < HD_C
    rounded["qc1_p"] = jnp.where(first_map, qc, 0.0).astype(BF16)
    rounded["qc2_p"] = jnp.where(first_map, 0.0, qc).astype(BF16)
    o["qc1_p"][...] = rounded["qc1_p"]
    o["qc2_p"][...] = rounded["qc2_p"]
    fa = seg("fa") + bf_ref[...]
    store_slots(o["logf"], _log_sigmoid(fa)[:, :N_HEADS])
    o["qb"][...] = seg("qb")
    o["kb"][...] = seg("kb")
    o["vb"][...] = seg("vb")
    o["rb"][...] = seg("rb")
    gb = seg("gb").astype(BF16)
    o["loga"][...] = _log_sigmoid(_dot(gb, wup_ref[...]) + bup_ref[...]) * (1.0 / GLA_TAU)
    lane = lax.broadcasted_iota(jnp.int32, (8, LANES), 1)
    tile_norms = jnp.zeros((8, LANES), F32)
    for a, name in enumerate(_NORM_ROWS):
        sq = rounded[name].astype(F32)
        sq = sq * sq
        for h in range(N_HEADS):
            ss = jnp.sum(sq[:, h * HEAD_PAD:(h + 1) * HEAD_PAD], axis=1, keepdims=True)
            tile_norms = jnp.where(lane == N_HEADS * a + h, jnp.max(ss, axis=0, keepdims=True), tile_norms)
    o["norms"][0] = tile_norms


def _proj_call(x, mod, g1, w_ext, w_kvt, bf, wup, bup, vone, tm, tiles_per_mod, layer, depth, stacked, with_qa,
               seq_len):
    n = x.shape[0]
    mod_all, sub = mod
    r = mod_all.shape[1]
    tok = lambda w: pl.BlockSpec((tm, w), lambda i: (i, 0))
    layer_outs = _LAYER_OUTS + ((("qa", W_A, F32),) if with_qa else ())
    alias_in = [] if stacked is None else [stacked[nm] for nm, _, _ in _STACKED_OUTS]
    transposed_a = seq_len % tm == 0
    tps = max(seq_len // tm, 1)
    stacked_specs, stacked_shapes = [], []
    slots = depth if stacked is None else 1
    for nm, rows, w in _STACKED_OUTS:
        if transposed_a and nm in ("ka", "va"):
            stacked_specs.append(pl.BlockSpec((slots, 1, w, tm), lambda i: (layer, i // tps, 0, i % tps)))
            stacked_shapes.append(jax.ShapeDtypeStruct((depth, n // seq_len, w, seq_len), F32))
        else:
            stacked_specs.append(pl.BlockSpec((slots, tm * rows, w), lambda i: (layer, i, 0)))
            stacked_shapes.append(jax.ShapeDtypeStruct((depth, n * rows, w), F32))
    n_fixed = 10
    outs = pl.pallas_call(
        functools.partial(_proj_kernel, n_alias=len(alias_in), with_qa=with_qa, transposed_a=transposed_a),
        grid=(n // tm,),
        in_specs=[tok(D_MODEL), _mod_spec(r, tiles_per_mod, 3 * sub), _mod_spec(r, tiles_per_mod, 3 * sub + 1),
                  _const_spec((1, D_MODEL)), _const_spec((D_MODEL, N_EXT)), _const_spec((2 * W_A, D_MODEL)),
                  _const_spec((1, LANES)), _const_spec((LANES, LANES)), _const_spec((1, LANES)),
                  _const_spec((1, W_PAD))]
        + [pl.BlockSpec(memory_space=pl.ANY)] * len(alias_in),
        out_specs=stacked_specs + [tok(w) for _, w, _ in layer_outs]
        + [pl.BlockSpec((1, 8, LANES), lambda i: (i, 0, 0))],
        out_shape=stacked_shapes + [jax.ShapeDtypeStruct((n, w), dt) for _, w, dt in layer_outs]
        + [jax.ShapeDtypeStruct((n // tm, 8, LANES), F32)],
        input_output_aliases={n_fixed + k: k for k in range(len(alias_in))},
        compiler_params=_params(("parallel",)),
        name="proj",
    )(x, mod_all, mod_all, g1, w_ext, w_kvt, bf, wup, bup, vone, *alias_in)
    names = [nm for nm, _, _ in _STACKED_OUTS] + [nm for nm, _, _ in layer_outs] + ["norms"]
    return dict(zip(names, outs))


def _cumsum_kernel(x_ref, o_ref):
    nb = x_ref.shape[1]
    r = lax.broadcasted_iota(jnp.int32, (LANES, LANES), 0)
    c = lax.broadcasted_iota(jnp.int32, (LANES, LANES), 1)
    within = (r <= c).astype(F32)
    rr = lax.broadcasted_iota(jnp.int32, (nb, nb), 0)
    cc = lax.broadcasted_iota(jnp.int32, (nb, nb), 1)
    before = (cc < rr).astype(F32)
    for s in range(x_ref.shape[0]):
        x = x_ref[s]
        local = _dot_exact_rhs(x, within)
        tot = jnp.broadcast_to(local[:, LANES - 1:LANES], (nb, LANES))
        o_ref[s] = local + _dot_exact_lhs(before, tot)


def _cumsum_call(x):
    rows, nb, _ = x.shape
    group = math.gcd(rows, 8)
    spec = pl.BlockSpec((group, nb, LANES), lambda i: (i, 0, 0))
    return pl.pallas_call(
        _cumsum_kernel, grid=(rows // group,), in_specs=[spec], out_specs=spec,
        out_shape=jax.ShapeDtypeStruct(x.shape, F32),
        compiler_params=_params(("parallel",)), name="cumsum",
    )(x)


def _lane_tile(x, width):
    return jnp.concatenate([x] * (width // LANES), axis=1)


GROUP_HEADS = 4
W_GROUP = GROUP_HEADS * HEAD_PAD


def _causal_sweep(i, first, tq, tk, tiles_per_iter, tile):
    r = tq // tk

    def body(jj, carry):
        for u in range(tiles_per_iter):
            tile(jj * tiles_per_iter + u, slice(0, tq), None)
        return carry

    assert r % tiles_per_iter == 0 and tiles_per_iter in (1, 2)
    lead = 0
    if tiles_per_iter == 2:
        lead = lax.rem(i * r - first, 2)

        @pl.when(lead == 1)
        def _():
            tile(first, slice(0, tq), None)

    lax.fori_loop(lax.div(first + lead, tiles_per_iter), lax.div(i * r, tiles_per_iter), body, 0)
    for d in range(r):
        tile(i * r + d, slice(d * tk, (d + 1) * tk), d * tk)
        if d + 1 < r:
            tile(i * r + d, slice((d + 1) * tk, tq), None)


def _fox_kernel(first_ref, q_ref, k_ref, v_ref, c_ref, o_ref, m_sc, acc_sc, *, tk, tiles_per_iter):
    i = pl.program_id(2)
    first = first_ref[pl.program_id(0), pl.program_id(1), i]
    tq = q_ref.shape[1]
    m_sc[...] = jnp.full(m_sc.shape, NEG, F32)
    acc_sc[...] = jnp.zeros(acc_sc.shape, F32)
    row = lax.broadcasted_iota(jnp.int32, (tk, tk), 0)
    col = lax.broadcasted_iota(jnp.int32, (tk, tk), 1)
    causal = col <= row

    def tile(j, rows, local):
        ks = pl.ds(pl.multiple_of(j * tk, tk), tk)
        for h in range(GROUP_HEADS):
            hs = slice(h * HEAD_PAD, (h + 1) * HEAD_PAD)
            c_first = c_ref[0, h, pl.ds(i * (tq // tk), 1), :][:, 0:1]
            t = _nt_dot(q_ref[0, rows, hs], k_ref[0, ks, hs]) + (c_first - c_ref[0, h, pl.ds(j, 1), :]) * LOG2E
            if local is not None:
                t = jnp.where(causal, t, NEG)
            m_old = m_sc[h, rows, :]
            m_new = jnp.maximum(m_old, jnp.max(t, axis=1, keepdims=True))
            p = jnp.exp2(t - _lane_tile(m_new, tk)).astype(BF16)
            acc_sc[h, rows, :] = jnp.exp2(m_old - m_new) * acc_sc[h, rows, :] + _dot(p, v_ref[0, ks, hs])
            m_sc[h, rows, :] = m_new

    _causal_sweep(i, first, tq, tk, tiles_per_iter, tile)
    outs = []
    for h in range(GROUP_HEADS):
        acc = acc_sc[h]
        outs.append((acc[:, :HD_A] / acc[:, HD_A:HD_A + 1]).astype(BF16))
    o_ref[0] = jnp.concatenate(outs, axis=1)


def _resident_spec(t, width):
    return pl.BlockSpec((1, t, width), lambda b_, g, i, first: (b_, 0, g), pipeline_mode=pl.Buffered(1))


def _fox_call(first, qa_p, ka_p, va_p, c, tq, tk):
    b, t, _ = qa_p.shape
    qspec = pl.BlockSpec((1, tq, W_GROUP), lambda b_, g, i, first: (b_, i, g))
    grid_spec = pltpu.PrefetchScalarGridSpec(
        num_scalar_prefetch=1,
        grid=(b, N_HEADS // GROUP_HEADS, t // tq),
        in_specs=[qspec, _resident_spec(t, W_GROUP), _resident_spec(t, W_GROUP),
                  pl.BlockSpec((1, GROUP_HEADS, t // tk, tk), lambda b_, g, i, first: (b_, g, 0, 0))],
        out_specs=pl.BlockSpec((1, tq, GROUP_HEADS * HD_A), lambda b_, g, i, first: (b_, i, g)),
        scratch_shapes=[pltpu.VMEM((GROUP_HEADS, tq, LANES), F32), pltpu.VMEM((GROUP_HEADS, tq, HEAD_PAD), F32)])
    return pl.pallas_call(
        functools.partial(_fox_kernel, tk=tk, tiles_per_iter=1),
        grid_spec=grid_spec,
        out_shape=jax.ShapeDtypeStruct((b, t, W_A), BF16),
        compiler_params=_params(("parallel", "parallel", "arbitrary")),
        name="fox",
    )(first, qa_p, ka_p, va_p, c.reshape(b, N_HEADS, t // tk, tk))


def _diff_lambda(lp, lam_init):
    a = jnp.sum(lp[0:1] * lp[1:2], axis=1, keepdims=True)
    b = jnp.sum(lp[2:3] * lp[3:4], axis=1, keepdims=True)
    return jnp.exp(a) - jnp.exp(b) + lam_init


def _diag_distance(tk):
    q = np.arange(tk)[:, None]
    k = np.arange(tk)[None, :]
    return np.where(k // CHUNK <= q // CHUNK, 2 * np.minimum(q, k) - k, HIDDEN).astype(np.float32)


DIFF_GROUP = 1
W_DIFF_GROUP = DIFF_GROUP * HEAD_PAD


def _diff_kernel(first_ref, q1_ref, q2_ref, k_ref, v_ref, slope_ref, lam_ref, g_ref, rel_ref, o_ref,
                 m_sc, l_sc, a_sc, *, tk, tiles_per_iter, lam_init):
    g = pl.program_id(1)
    i = pl.program_id(2)
    first = first_ref[pl.program_id(0), g, i]
    tq = q1_ref.shape[1]
    m_sc[...] = jnp.full(m_sc.shape, NEG, F32)
    l_sc[...] = jnp.zeros(l_sc.shape, F32)
    a_sc[...] = jnp.zeros(a_sc.shape, F32)
    lane = lax.broadcasted_iota(jnp.int32, (1, tk), 1)

    def tile(j, rows, local):
        ks = pl.ds(pl.multiple_of(j * tk, tk), tk)
        if local is None:
            rel = (lane + (j * tk - i * tq)).astype(F32)
        else:
            rel = rel_ref[...] + float(local)
        for h in range(DIFF_GROUP):
            hs = slice(h * HEAD_PAD, (h + 1) * HEAD_PAD)
            bias = (slope_ref[g * DIFF_GROUP + h][:, 0:1] * LOG2E) * rel
            k = k_ref[0, ks, hs]
            v = v_ref[0, ks, hs]
            nrows = rows.stop - rows.start
            scores = _nt_dot(jnp.concatenate([q1_ref[0, rows, hs], q2_ref[0, rows, hs]], axis=0), k)
            probs, alphas = [], []
            for m in range(2):
                n = 2 * h + m
                t = scores[m * nrows:(m + 1) * nrows] + bias
                m_old = m_sc[n, rows, :]
                m_new = jnp.maximum(m_old, jnp.max(t, axis=1, keepdims=True))
                alpha = jnp.exp2(m_old - m_new)
                p = jnp.exp2(t - _lane_tile(m_new, tk))
                l_sc[n, rows, :] = alpha * l_sc[n, rows, :] + jnp.sum(p, axis=1, keepdims=True)
                m_sc[n, rows, :] = m_new
                probs.append(p.astype(BF16))
                alphas.append(alpha)
            pv = _dot(jnp.concatenate(probs, axis=0), v)
            for m in range(2):
                n = 2 * h + m
                a_sc[n, rows, :] = alphas[m] * a_sc[n, rows, :] + pv[m * nrows:(m + 1) * nrows]

    _causal_sweep(i, first, tq, tk, tiles_per_iter, tile)
    lam = _diff_lambda(lam_ref[...], lam_init)
    for h in range(DIFF_GROUP):
        out = a_sc[2 * h] / l_sc[2 * h] - lam * (a_sc[2 * h + 1] / l_sc[2 * h + 1])
        o_ref[0, :, h * HEAD_PAD:(h + 1) * HEAD_PAD] = (_rms(out, g_ref[...]) * (1.0 - lam_init)).astype(BF16)


def _diff_call(first, qc1_p, qc2_p, kc_b, vc_b, slopes, lam_p, g_diff, lam_init, tq, tk):
    assert CHUNK == 64 and tk % CHUNK == 0
    b, t, _ = qc1_p.shape
    qspec = pl.BlockSpec((1, tq, W_DIFF_GROUP), lambda b_, g, i, first: (b_, i, g))
    stat = pltpu.VMEM((2 * DIFF_GROUP, tq, LANES), F32)
    grid_spec = pltpu.PrefetchScalarGridSpec(
        num_scalar_prefetch=1,
        grid=(b, N_HEADS // DIFF_GROUP, t // tq),
        in_specs=[qspec, qspec, _resident_spec(t, W_DIFF_GROUP), _resident_spec(t, W_DIFF_GROUP),
                  pl.BlockSpec((N_HEADS, 1, LANES), lambda b_, g, i, first: (0, 0, 0)),
                  pl.BlockSpec((4, HD_C), lambda b_, g, i, first: (0, 0)),
                  pl.BlockSpec((1, HEAD_PAD), lambda b_, g, i, first: (0, 0)),
                  pl.BlockSpec((tk, tk), lambda b_, g, i, first: (0, 0), pipeline_mode=pl.Buffered(1))],
        out_specs=qspec,
        scratch_shapes=[stat, stat, pltpu.VMEM((2 * DIFF_GROUP, tq, HEAD_PAD), F32)])
    return pl.pallas_call(
        functools.partial(_diff_kernel, tk=tk, tiles_per_iter=2, lam_init=lam_init),
        grid_spec=grid_spec,
        out_shape=jax.ShapeDtypeStruct((b, t, W_C), BF16),
        compiler_params=_params(("parallel", "parallel", "arbitrary")),
        name="diff",
    )(first, qc1_p, qc2_p, kc_b, vc_b, slopes, lam_p, g_diff, jnp.asarray(_diag_distance(tk)))


def _gla_kernel(q_ref, k_ref, v_ref, la_ref, r_ref, s0_ref, g_ref, o_ref, s_ref, *, chunk):
    ti = pl.program_id(1)
    n_chunks = q_ref.shape[1] // chunk
    n_seq = q_ref.shape[0]

    @pl.when(ti == 0)
    def _():
        s_ref[...] = s0_ref[...]

    tri_r = lax.broadcasted_iota(jnp.int32, (chunk, chunk), 0)
    tri_c = lax.broadcasted_iota(jnp.int32, (chunk, chunk), 1)
    incl = (tri_c <= tri_r).astype(F32)
    srow = lax.broadcasted_iota(jnp.int32, (N_HEADS * chunk, LANES), 0) // chunk
    slane = lax.broadcasted_iota(jnp.int32, (N_HEADS * chunk, LANES), 1) // DK_B
    qmask = srow == slane
    arow = lax.broadcasted_iota(jnp.int32, (N_HEADS * chunk, chunk), 0) % chunk
    acol = lax.broadcasted_iota(jnp.int32, (N_HEADS * chunk, chunk), 1)
    causal = acol <= arow
    vlane = lax.broadcasted_iota(jnp.int32, (chunk, W_B), 1) // DV_B
    st_row = lax.broadcasted_iota(jnp.int32, (W_B, LANES), 0) // DV_B
    st_lane = lax.broadcasted_iota(jnp.int32, (W_B, LANES), 1) // DK_B
    diag_blocks = st_row == st_lane

    for c, sq in [(c, sq) for c in range(n_chunks) for sq in range(n_seq)]:
        sl = slice(c * chunk, (c + 1) * chunk)
        cb = _dot_exact_lhs(incl, la_ref[sq, sl, :])
        cb_last = cb[chunk - 1:chunk, :]
        q = q_ref[sq, sl, :] * (DK_B ** -0.5)
        k = k_ref[sq, sl, :]
        v = v_ref[sq, sl, :]
        vb = v.astype(BF16)
        qe = q * jnp.exp(cb)
        ke = (k * jnp.exp(-cb)).astype(BF16)
        kl = (k * jnp.exp(cb_last - cb)).astype(BF16)
        decay = jnp.exp(cb_last)
        qstack = jnp.where(qmask, jnp.concatenate([qe] * N_HEADS, axis=0), 0.0).astype(BF16)
        a = jnp.where(causal, _nt_dot(qstack, ke), 0.0).astype(BF16)
        oi = _dot(a, vb)
        o = jnp.zeros((chunk, W_B), F32)
        for h in range(N_HEADS):
            o = o + jnp.where(vlane == h, oi[h * chunk:(h + 1) * chunk], 0.0)
        st = s_ref[sq]
        o = o + _nt_dot(qe.astype(BF16), st.astype(BF16))
        upd = _dot(v.T.astype(BF16), kl)
        s_ref[sq] = decay * st + jnp.where(diag_blocks, upd, 0.0)
        o2 = o * o
        ms = jnp.zeros((chunk, W_B), F32)
        for h in range(N_HEADS):
            head_mean = jnp.sum(jnp.where(vlane == h, o2, 0.0), axis=1, keepdims=True) * (1.0 / DV_B)
            ms = jnp.where(vlane == h, head_mean, ms)
        r = r_ref[sq, sl, :]
        o_ref[sq, sl, :] = (o * lax.rsqrt(ms + EPS) * g_ref[...] * (r * jax.nn.sigmoid(r))).astype(BF16)


def _gla_call(qb, kb, vb, loga, rb, s0_t, g_gla, tg, chunk, group):
    b, t, _ = qb.shape
    tok = lambda w: pl.BlockSpec((group, tg, w), lambda b_, i: (b_, i, 0))
    sspec = pl.BlockSpec((group, W_B, LANES), lambda b_, i: (b_, 0, 0))
    return pl.pallas_call(
        functools.partial(_gla_kernel, chunk=chunk),
        grid=(b // group, t // tg),
        in_specs=[tok(LANES), tok(LANES), tok(W_B), tok(LANES), tok(W_B), sspec,
                  pl.BlockSpec((1, W_B), lambda b_, i: (0, 0))],
        out_specs=[tok(W_B), sspec],
        out_shape=[jax.ShapeDtypeStruct((b, t, W_B), BF16), jax.ShapeDtypeStruct((b, W_B, LANES), F32)],
        compiler_params=_params(("parallel", "arbitrary")),
        name="gla",
    )(qb, kb, vb, loga, rb, s0_t, g_gla)


def _softmax_two(t_past, t_new, exp_fn):
    m = jnp.maximum(jnp.max(t_past, axis=1, keepdims=True), jnp.max(t_new, axis=1, keepdims=True))
    p_past = exp_fn(t_past - m)
    p_new = exp_fn(t_new - m)
    l = jnp.sum(p_past, axis=1, keepdims=True) + jnp.sum(p_new, axis=1, keepdims=True)
    return p_past, p_new, l


def _fox_s_kernel(q_ref, kn_ref, vn_ref, kpt_ref, vpt_ref, c_ref, o_ref):
    t, past = q_ref.shape[1], kpt_ref.shape[1]
    q = q_ref[0]
    kn = kn_ref[...].astype(BF16)
    vn = vn_ref[...].astype(BF16)
    kpt = kpt_ref[...].astype(BF16)
    vpt = vpt_ref[...].astype(BF16)
    lane_head = lax.broadcasted_iota(jnp.int32, (t, W_A), 1) // HD_A
    row = lax.broadcasted_iota(jnp.int32, (t, t), 0)
    col = lax.broadcasted_iota(jnp.int32, (t, t), 1)
    o = jnp.zeros((t, W_A), F32)
    for h in range(N_HEADS):
        qh = jnp.where(lane_head == h, q, 0.0).astype(BF16)
        ch = c_ref[0, h:h + 1, :]
        cref = ch[:, past:past + 1]
        t_past = _dot(qh, kpt) + (cref - ch[:, :past])
        t_new = jnp.where(col <= row, _nt_dot(qh, kn) + (cref - ch[:, past:past + t]), NEG)
        p_past, p_new, l = _softmax_two(t_past, t_new, jnp.exp)
        oh = (_nt_dot(p_past.astype(BF16), vpt) + _dot(p_new.astype(BF16), vn)) / l
        o = o + jnp.where(lane_head == h, oh, 0.0)
    o_ref[0] = o.astype(BF16)


def _fox_s_call(layer, qa, ka, va, cache_kt, cache_vt, c):
    b, t, _ = qa.shape
    past = cache_kt.shape[3]
    new = pl.BlockSpec((1, t, W_A), lambda i: (i, 0, 0))
    new_l = lambda: pl.BlockSpec((None, t, W_A), lambda i: (layer, i, 0))
    old_l = lambda: pl.BlockSpec((None, None, W_A, past), lambda i: (layer, i, 0, 0))
    return pl.pallas_call(
        _fox_s_kernel,
        grid=(b,),
        in_specs=[new, new_l(), new_l(), old_l(), old_l(),
                  pl.BlockSpec((1, N_HEADS, c.shape[2]), lambda i: (i, 0, 0))],
        out_specs=new,
        out_shape=jax.ShapeDtypeStruct((b, t, W_A), BF16),
        compiler_params=_params(("parallel",)),
        name="fox_s",
    )(qa, ka, va, cache_kt, cache_vt, c)


def _diff_s_kernel(q1_ref, q2_ref, kn_ref, vn_ref, kp_ref, vp_ref, slope_ref, lam_ref, g_ref, o_ref, *, lam_init):
    t, past = q1_ref.shape[1], kp_ref.shape[0] // N_HEADS
    lam = _diff_lambda(lam_ref[...], lam_init)
    row = lax.broadcasted_iota(jnp.int32, (t, t), 0) + past
    col = lax.broadcasted_iota(jnp.int32, (t, t), 1) + past
    visible_new = lax.shift_right_logical(col, 6) <= lax.shift_right_logical(row, 6)
    dist_new = jnp.abs(row - col).astype(F32)
    prow = lax.broadcasted_iota(jnp.int32, (t, past), 0) + past
    pcol = lax.broadcasted_iota(jnp.int32, (t, past), 1)
    visible_past = lax.shift_right_logical(pcol, 6) <= lax.shift_right_logical(prow, 6)
    dist_past = (prow - pcol).astype(F32)
    for h in range(N_HEADS):
        hs = slice(h * HEAD_PAD, (h + 1) * HEAD_PAD)
        slope = slope_ref[h][:, 0:1] * LOG2E
        kn = kn_ref[pl.ds(h, t, stride=N_HEADS), :].astype(BF16)
        vn = vn_ref[pl.ds(h, t, stride=N_HEADS), :].astype(BF16)
        kp = kp_ref[pl.ds(h, past, stride=N_HEADS), :].astype(BF16)
        vp = vp_ref[pl.ds(h, past, stride=N_HEADS), :].astype(BF16)
        outs = []
        for q_ref in (q1_ref, q2_ref):
            q = q_ref[0, :, hs]
            t_past = jnp.where(visible_past, _nt_dot(q, kp) - slope * dist_past, NEG)
            t_new = jnp.where(visible_new, _nt_dot(q, kn) - slope * dist_new, NEG)
            p_past, p_new, l = _softmax_two(t_past, t_new, jnp.exp2)
            outs.append((_dot(p_past.astype(BF16), vp) + _dot(p_new.astype(BF16), vn)) / l)
        out = outs[0] - lam * outs[1]
        o_ref[0, :, hs] = (_rms(out, g_ref[...]) * (1.0 - lam_init)).astype(BF16)


def _diff_s_call(layer, qc1_p, qc2_p, kc, vc, cache_k, cache_v, slopes, lam_p, g_diff, lam_init):
    assert CHUNK == 64
    b, t, _ = qc1_p.shape
    rows_past = cache_k.shape[2]
    new = lambda: pl.BlockSpec((1, t, W_C), lambda i: (i, 0, 0))
    new_l = lambda: pl.BlockSpec((None, t * N_HEADS, HEAD_PAD), lambda i: (layer, i, 0))
    old_l = lambda: pl.BlockSpec((None, None, rows_past, HEAD_PAD), lambda i: (layer, i, 0, 0))
    return pl.pallas_call(
        functools.partial(_diff_s_kernel, lam_init=lam_init),
        grid=(b,),
        in_specs=[new(), new(), new_l(), new_l(), old_l(), old_l(),
                  pl.BlockSpec((N_HEADS, 1, LANES), lambda i: (0, 0, 0)),
                  pl.BlockSpec((4, HD_C), lambda i: (0, 0)),
                  pl.BlockSpec((1, HEAD_PAD), lambda i: (0, 0))],
        out_specs=new(),
        out_shape=jax.ShapeDtypeStruct((b, t, W_C), BF16),
        compiler_params=_params(("parallel",)),
        name="diff_s",
    )(qc1_p, qc2_p, kc, vc, cache_k, cache_v, slopes, lam_p, g_diff)


def _pad_lanes(w, width=LANES):
    return jnp.pad(w, ((0, 0), (0, width - w.shape[1])))


def _layer_params(l, w_in, b_f, w_gla_up, b_gla_up, g_gla, g_diff, w_out):
    offs = [int(o) for o in np.cumsum(PROJ_SIZES)[:-1]]
    qa, ka, va, fa, qb, kb, vb, gb, rb, qc, kc, vc = jnp.split(w_in[l], offs, axis=1)
    segs = {
        "ka": ka, "va": va, "kc": kc, "vc": vc, "qa": qa, "qc": qc,
        "fa": _pad_lanes(fa), "qb": qb, "kb": kb, "vb": vb, "rb": rb, "gb": _pad_lanes(gb),
    }
    w_ext = jnp.concatenate([segs[n] for n, _ in _SEG_WIDTHS], axis=1).astype(BF16)
    wo = w_out[l]
    vone = np.zeros((1, W_PAD), np.float32)
    vone[0, HD_A::HEAD_PAD] = 1.0
    return dict(
        w_ext=w_ext, w_kvt=jnp.concatenate([ka, va], axis=1).T.astype(BF16),
        bf=_pad_lanes(b_f[l][None, :]),
        wup=jnp.pad(w_gla_up[l], ((0, LANES - GLA_RANK), (0, 0))).astype(BF16),
        bup=b_gla_up[l][None, :],
        vone=jnp.asarray(vone),
        g_gla=jnp.tile(g_gla[l], N_HEADS)[None, :], g_diff=g_diff[l][None, :],
        w_oa=wo[:W_A].astype(BF16),
        w_og=wo[W_A:W_A + W_B].astype(BF16), w_od=wo[W_A + W_B:].astype(BF16),
    )


def _state_to_blockdiag_t(s):
    eye = jnp.eye(N_HEADS, dtype=s.dtype)
    return jnp.einsum("bhkv,hg->bhvgk", s, eye).reshape(s.shape[0], W_B, N_HEADS * DK_B)


def _blockdiag_t_to_state(st):
    b = st.shape[0]
    s5 = st.reshape(b, N_HEADS, DV_B, N_HEADS, DK_B)
    diag = jnp.stack([s5[:, h, :, h, :] for h in range(N_HEADS)], axis=1)
    return jnp.swapaxes(diag, 2, 3)


SKIP_LOG2 = 160.0
NORM_SLACK = 1.01


def _tile_max(x, factor):
    return jnp.max(x.reshape(x.shape[:-1] + (x.shape[-1] // factor, factor)), axis=-1)


def _first_tiles(qn, kn, gap, tq, tk, group):
    b, h, nq = qn.shape
    r = tq // tk
    kn_diag = _tile_max(kn, r)
    bound = NORM_SLACK * (qn[..., None] * kn[:, :, None, :] + (qn * kn_diag)[..., None]) + gap
    dead = (bound < -SKIP_LOG2).reshape(b, h // group, group, nq, kn.shape[-1]).all(axis=2)
    lead = jnp.sum(jnp.cumprod(dead.astype(jnp.int32), axis=-1), axis=-1)
    return jnp.minimum(lead, jnp.arange(nq, dtype=jnp.int32) * r).astype(jnp.int32)


def _tile(n, pref):
    t = min(n, pref)
    assert n % t == 0, (n, t)
    return t


def _trunk(x, mods, params, ffn_w, g_norm, lam_params, mixer, tm, tiles_per_mod, with_qa):
    b, t, _ = x.shape
    n = b * t
    depth = len(params)
    xf = x.reshape(n, D_MODEL)
    stacked = None
    gla_states = []
    for l in range(depth):
        p = params[l]
        gn = lambda k: g_norm[l, k][None, :]
        xf = _ffn_call(xf, (mods[l], 0), gn(0), gn(1), *ffn_w, l, 0, 0.5, tm, tiles_per_mod)
        pr = _proj_call(xf, (mods[l], 1), gn(2), p["w_ext"], p["w_kvt"], p["bf"], p["wup"], p["bup"], p["vone"], tm,
                        tiles_per_mod, l, depth, stacked, with_qa, t)
        stacked = {nm: pr[nm] for nm, _, _ in _STACKED_OUTS}
        lam_init = 0.8 - 0.6 * math.exp(-0.3 * l)
        oa, gla, dif, s_fin, w_oa = mixer(l, pr, p, lam_params[l], lam_init, b, t)
        gla_states.append(s_fin)
        xf = _ffn_call(xf, (mods[l], 2), gn(4), gn(5), *ffn_w, l, 1, 0.5, tm, tiles_per_mod,
                       merge=(oa, gla, dif, (mods[l], 1), gn(3), w_oa, p["w_og"], p["w_od"]))
    shp = lambda nm, *tail: stacked[nm].reshape((depth, b, t) + tail)
    if stacked["ka"].ndim == 4:
        shp_a = lambda nm: jnp.transpose(stacked[nm].reshape(depth, b, N_HEADS, HD_A, t), (0, 1, 4, 2, 3))
    else:
        shp_a = lambda nm: shp(nm, N_HEADS, HD_A)
    states = [shp_a("ka"), shp_a("va"), shp("logf", N_HEADS), jnp.stack(gla_states),
              shp("kc", N_HEADS, 2 * HD_C), shp("vc", N_HEADS, 2 * HD_C)]
    return xf.reshape(b, t, D_MODEL), states


def kernel(x_prompt, x_sample, c_prompt, c_sample, cache_fox_k, cache_fox_v, cache_fox_logf, state_gla,
           cache_diff_k, cache_diff_v, w_ada, b_ada, g_norm, w_ffn_in, w_ffn_out, w_in, b_f, w_gla_up,
           b_gla_up, g_gla, g_diff, lam_params, w_out):
    depth = w_in.shape[0]
    bp, tp, _ = x_prompt.shape
    bs, ts, _ = x_sample.shape
    past = cache_fox_k.shape[2]

    params = [_layer_params(l, w_in, b_f, w_gla_up, b_gla_up, g_gla, g_diff, w_out) for l in range(depth)]
    ffn_w = (w_ffn_in.astype(BF16), w_ffn_out.astype(BF16))

    slopes = 2.0 ** (-8.0 * jnp.arange(1, N_HEADS + 1, dtype=F32) / N_HEADS)
    slopes = jnp.broadcast_to(slopes[:, None, None], (N_HEADS, 1, LANES))

    rows = bp + bs
    rows_pad = -(-rows // 8) * 8
    c_all = jnp.pad(jnp.concatenate([c_prompt, c_sample], axis=0), ((0, rows_pad - rows), (0, 0)))
    mods_p, mods_s = [], []
    for l in range(depth):
        mod = _ada_call(c_all, w_ada, b_ada[:, None, :], l)
        mods_p.append(mod[:bp, None, :])
        mods_s.append(jnp.repeat(mod[bp:rows], ts, axis=0)[None])

    tm_p = _tile(tp, 512)
    tq = _tile(tp, 1024)
    tk = _tile(tp, 512)
    tg = _tile(tp, 512)

    def prompt_mixer(l, pr, p, lam_p, lam_init, b, t):
        r3 = lambda a: a.reshape(b, t, a.shape[-1])
        logf_t = jnp.swapaxes(r3(pr["logf"][l]), 1, 2).reshape(b * N_HEADS, t // LANES, LANES)
        c = _cumsum_call(logf_t).reshape(b, N_HEADS, t)
        nrm = jnp.sqrt(pr["norms"][:, 0, :N_HEADS * len(_NORM_ROWS)])
        nrm = dict(zip(_NORM_ROWS, jnp.moveaxis(nrm.reshape(b, t // tm_p, len(_NORM_ROWS), N_HEADS), (2, 3), (0, 2))))
        q_tiles = lambda a: _tile_max(a, tq // tm_p)
        k_tiles = lambda a: _tile_max(a, tk // tm_p)
        gap_a = (c[:, :, ::tq][..., None] - c[:, :, tk - 1::tk][:, :, None, :]) * LOG2E
        first_a = _first_tiles(q_tiles(nrm["qa_p"]), k_tiles(nrm["ka_p"]), gap_a, tq, tk, GROUP_HEADS)
        oa = _fox_call(first_a, r3(pr["qa_p"]), r3(pr["ka_p"]), r3(pr["va_p"]), c, tq, tk)
        s0_t = jnp.zeros((b, W_B, N_HEADS * DK_B), F32)
        gla, s_t = _gla_call(r3(pr["qb"]), r3(pr["kb"]), r3(pr["vb"]), r3(pr["loga"]), r3(pr["rb"]), s0_t,
                             p["g_gla"], tg, min(CHUNK, t), math.gcd(b, 2))
        last_key = jnp.arange(tk - 1, t, tk, dtype=F32)[None, :] - jnp.arange(0, t, tq, dtype=F32)[:, None]
        gap_c = slopes[:, 0, :1, None] * LOG2E * last_key[None]
        first_c = _first_tiles(q_tiles(jnp.maximum(nrm["qc1_p"], nrm["qc2_p"])), k_tiles(nrm["kc_b"]), gap_c, tq, tk,
                                 DIFF_GROUP)
        dif = _diff_call(first_c, r3(pr["qc1_p"]), r3(pr["qc2_p"]), r3(pr["kc_b"]), r3(pr["vc_b"]), slopes, lam_p,
                         p["g_diff"], lam_init, tq, tk)
        flat = lambda a: a.reshape(b * t, a.shape[-1])
        return flat(oa), flat(gla), flat(dif), _blockdiag_t_to_state(s_t), p["w_oa"]

    feature_time = lambda a: jnp.transpose(a, (0, 1, 3, 4, 2)).reshape(depth, bs, W_A, past)
    cache_kt, cache_vt = feature_time(cache_fox_k), feature_time(cache_fox_v)
    cache_dk = cache_diff_k.reshape(depth, bs, past * N_HEADS, HEAD_PAD)
    cache_dv = cache_diff_v.reshape(depth, bs, past * N_HEADS, HEAD_PAD)

    def sample_mixer(l, pr, p, lam_p, lam_init, b, t):
        r3 = lambda a: a.reshape(b, t, a.shape[-1])
        total = past + t
        nb = -(-total // (8 * LANES)) * 8
        lf = jnp.concatenate([cache_fox_logf[l].astype(F32), r3(pr["logf"][l])], axis=1)
        lf = jnp.pad(jnp.swapaxes(lf, 1, 2), ((0, 0), (0, 0), (0, nb * LANES - total)))
        c = _cumsum_call(lf.reshape(b * N_HEADS, nb, LANES)).reshape(b, N_HEADS, nb * LANES)
        oa = _fox_s_call(l, r3(pr["qa"]), pr["ka"], pr["va"], cache_kt, cache_vt, c)
        gla, s_t = _gla_call(r3(pr["qb"]), r3(pr["kb"]), r3(pr["vb"]), r3(pr["loga"]), r3(pr["rb"]),
                             _state_to_blockdiag_t(state_gla[l].astype(F32)), p["g_gla"], t, min(CHUNK, t),
                             math.gcd(b, 8))
        dif = _diff_s_call(l, r3(pr["qc1_p"]), r3(pr["qc2_p"]), pr["kc"], pr["vc"], cache_dk, cache_dv,
                           slopes, lam_p, p["g_diff"], lam_init)
        flat = lambda a: a.reshape(b * t, a.shape[-1])
        return flat(oa), flat(gla), flat(dif), _blockdiag_t_to_state(s_t), p["w_oa"]

    y_p, sp = _trunk(x_prompt, mods_p, params, ffn_w, g_norm, lam_params, prompt_mixer, tm_p, tp // tm_p, False)
    y_s, ss = _trunk(x_sample, mods_s, params, ffn_w, g_norm, lam_params, sample_mixer, bs * ts, 1, True)
    return (y_p, y_s, sp[0], sp[1], sp[2], sp[3], sp[4], sp[5], ss[0], ss[1], ss[2], ss[3], ss[4], ss[5])
```
